```python
import jax, jax.numpy as jnp
from jax import lax
import numpy as np

D_MODEL = 1024
BATCH = 8
SEQ = 2048
DEPTH = 1
DEC_BATCH = 128
DEC_SEQ = 8
PAST_LEN = 16384
PAGE_SIZE = 128

MIX_WIDTH = D_MODEL
POOL_WIDTH = MIX_WIDTH // 2
GMLP_WIDTH = MIX_WIDTH - POOL_WIDTH
POOL_WINDOWS = (2, 4, 8, 16)
N_POOL_GROUPS = len(POOL_WINDOWS)
POOL_GROUP = POOL_WIDTH // N_POOL_GROUPS
POOL_HIST = max(POOL_WINDOWS) - 1
CHUNK = 128
N_GMLP_HEADS = 4
GMLP_HEAD = GMLP_WIDTH // N_GMLP_HEADS
IN_WIDTH = POOL_WIDTH + 2 * GMLP_WIDTH
N_EXPERT_GROUPS = 4
EXPERTS_PER_GROUP = 8
N_EXPERTS = N_EXPERT_GROUPS * EXPERTS_PER_GROUP
TOP_K_INNER = 2
D_EXPERT = D_MODEL // 2
PLE_DIM = 256
EPS = 1e-6

kernel_name = 'hybrid_pool_gmlp_hmoe_decode_step'


def rmsnorm(x, g):
    xf = x.astype(jnp.float32)
    y = xf * lax.rsqrt(jnp.mean(xf * xf, axis=-1, keepdims=True) + EPS) * g.astype(jnp.float32)
    return y.astype(x.dtype)


def layernorm(x, g, b):
    xf = x.astype(jnp.float32)
    mu = jnp.mean(xf, axis=-1, keepdims=True)
    xc = xf - mu
    y = xc * lax.rsqrt(jnp.mean(xc * xc, axis=-1, keepdims=True) + EPS)
    return (y * g.astype(jnp.float32) + b.astype(jnp.float32)).astype(x.dtype)


def pool_mix(a, hist, pos0, w_pool, pool_scale):
    B, L, _ = a.shape
    ext = jnp.concatenate([hist.astype(jnp.float32), a.astype(jnp.float32)], axis=1)
    csum = jnp.concatenate([jnp.zeros((B, 1, POOL_WIDTH), jnp.float32),
                            jnp.cumsum(ext, axis=1)], axis=1)
    pos = pos0 + jnp.arange(L)
    cur = ext[:, POOL_HIST:]
    outs = []
    for gi, w in enumerate(POOL_WINDOWS):
        sl = slice(gi * POOL_GROUP, (gi + 1) * POOL_GROUP)
        win_sum = (csum[:, POOL_HIST + 1:POOL_HIST + 1 + L, sl]
                   - csum[:, POOL_HIST + 1 - w:POOL_HIST + 1 - w + L, sl])
        cnt = jnp.minimum(w, pos + 1).astype(jnp.float32)
        outs.append(win_sum / cnt[None, :, None] - cur[..., sl])
    pooled = jnp.stack(outs, axis=2)
    y = jnp.einsum('blgc,gcd->blgd', pooled, w_pool.astype(jnp.float32)).reshape(B, L, POOL_WIDTH)
    y = y * pool_scale.astype(jnp.float32)
    new_hist = ext[:, -POOL_HIST:]
    return y.astype(a.dtype), new_hist.astype(a.dtype)


def spatial_gate(u, v, w_spatial, b_spatial):
    B, L, _ = u.shape
    Lc = -(-L // CHUNK) * CHUNK
    nc = Lc // CHUNK
    vp = jnp.pad(v, ((0, 0), (0, Lc - L), (0, 0))).reshape(B, nc, CHUNK, N_GMLP_HEADS, GMLP_HEAD)
    mask = jnp.tril(jnp.ones((CHUNK, CHUNK), dtype=bool))
    w = jnp.where(mask[None], w_spatial, jnp.zeros_like(w_spatial))
    mixed = jnp.einsum('hts,bcshd->bcthd', w, vp) + jnp.transpose(b_spatial)[None, None, :, :, None]
    mixed = mixed.reshape(B, Lc, GMLP_WIDTH)[:, :L]
    return u * mixed


def hier_moe(x, w_rg, b_rg, w_re, b_re, w_eg, w_eu, w_ed):
    N = x.shape[0]
    g_logits = (x @ w_rg + b_rg).astype(jnp.float32)
    g_prob = jax.nn.softmax(g_logits, axis=-1)
    _, g_idx = lax.top_k(g_logits, 1)
    e_logits = (x @ w_re + b_re).astype(jnp.float32).reshape(N, N_EXPERT_GROUPS, EXPERTS_PER_GROUP)
    e_sel = jnp.take_along_axis(e_logits, g_idx[:, :, None], axis=1)[:, 0]
    e_val, e_idx = lax.top_k(e_sel, TOP_K_INNER)
    e_w = jax.nn.softmax(e_val, axis=-1) * jnp.take_along_axis(g_prob, g_idx, axis=1)
    flat = g_idx * EXPERTS_PER_GROUP + e_idx
    combine = jnp.sum(jax.nn.one_hot(flat, N_EXPERTS, dtype=jnp.float32) * e_w[..., None], axis=1)
    out = jnp.zeros((N, x.shape[1]), jnp.float32)
    for e in range(N_EXPERTS):
        h = jax.nn.silu(x @ w_eg[e]) * (x @ w_eu[e])
        out = out + combine[:, e:e + 1] * (h @ w_ed[e]).astype(jnp.float32)
    return out.astype(x.dtype)


def decoder_layer(x, p, hist, pos0, g_mix, w_in, w_pool, pool_scale, g_v, b_v, w_spatial,
                  b_spatial, w_out, g_ffn, w_rg, b_rg, w_re, b_re, w_eg, w_eu, w_ed,
                  w_ple, g_ple, w_ple_gate):
    B, L, D = x.shape
    xn = rmsnorm(x, g_mix)
    z = xn @ w_in
    a = z[..., :POOL_WIDTH]
    uv = jax.nn.gelu(z[..., POOL_WIDTH:])
    u = uv[..., :GMLP_WIDTH]
    v = layernorm(uv[..., GMLP_WIDTH:], g_v, b_v)
    pool_out, new_hist = pool_mix(a, hist, pos0, w_pool, pool_scale)
    gmlp_out = spatial_gate(u, v, w_spatial, b_spatial)
    h = x + jnp.concatenate([pool_out, gmlp_out.astype(pool_out.dtype)], axis=-1) @ w_out
    hn = rmsnorm(h, g_ffn).reshape(B * L, D)
    h = h + hier_moe(hn, w_rg, b_rg, w_re, b_re, w_eg, w_eu, w_ed).reshape(B, L, D)
    gate = jax.nn.sigmoid((rmsnorm(h, g_ple) @ w_ple_gate).astype(jnp.float32))
    h = h + ((p @ w_ple).astype(jnp.float32) * gate).astype(h.dtype)
    v_open = v[:, ((L - 1) // CHUNK) * CHUNK:]
    return h, new_hist, v_open


def setup_inputs(seed: int = 0) -> dict:
    key = jax.random.key(seed)
    ks = jax.random.split(key, 32)
    f = jnp.float32
    nrm = lambda k, shape, s: jax.random.normal(k, shape, f) * s
    gain = lambda k, shape: 1.0 + 0.05 * jax.random.normal(k, shape, f)
    return {
        'x_prompt': nrm(ks[0], (BATCH, SEQ, D_MODEL), 1.0),
        'x_sample': nrm(ks[1], (DEC_BATCH, DEC_SEQ, D_MODEL), 1.0),
        'state_pool': nrm(ks[2], (DEPTH, DEC_BATCH, POOL_HIST, POOL_WIDTH), 0.5),
        'p_prompt': nrm(ks[3], (DEPTH, BATCH, SEQ, PLE_DIM), 1.0),
        'p_sample': nrm(ks[4], (DEPTH, DEC_BATCH, DEC_SEQ, PLE_DIM), 1.0),
        'g_mix': gain(ks[5], (DEPTH, D_MODEL)),
        'w_in': nrm(ks[6], (DEPTH, D_MODEL, IN_WIDTH), D_MODEL ** -0.5),
        'w_pool': nrm(ks[7], (DEPTH, N_POOL_GROUPS, POOL_GROUP, POOL_GROUP), POOL_GROUP ** -0.5),
        'pool_scale': gain(ks[8], (DEPTH, POOL_WIDTH)),
        'g_v': gain(ks[9], (DEPTH, GMLP_WIDTH)),
        'b_v': nrm(ks[10], (DEPTH, GMLP_WIDTH), 0.02),
        'w_spatial': nrm(ks[11], (DEPTH, N_GMLP_HEADS, CHUNK, CHUNK), CHUNK ** -0.5),
        'b_spatial': gain(ks[12], (DEPTH, N_GMLP_HEADS, CHUNK)),
        'w_out': nrm(ks[13], (DEPTH, MIX_WIDTH, D_MODEL), MIX_WIDTH ** -0.5),
        'g_ffn': gain(ks[14], (DEPTH, D_MODEL)),
        'w_rg': nrm(ks[15], (DEPTH, D_MODEL, N_EXPERT_GROUPS), D_MODEL ** -0.5),
        'b_rg': nrm(ks[16], (DEPTH, N_EXPERT_GROUPS), 0.01),
        'w_re': nrm(ks[17], (DEPTH, D_MODEL, N_EXPERTS), D_MODEL ** -0.5),
        'b_re': nrm(ks[18], (DEPTH, N_EXPERTS), 0.01),
        'w_eg': nrm(ks[19], (DEPTH, N_EXPERTS, D_MODEL, D_EXPERT), D_MODEL ** -0.5),
        'w_eu': nrm(ks[20], (DEPTH, N_EXPERTS, D_MODEL, D_EXPERT), D_MODEL ** -0.5),
        'w_ed': nrm(ks[21], (DEPTH, N_EXPERTS, D_EXPERT, D_MODEL), D_EXPERT ** -0.5),
        'w_ple': nrm(ks[22], (DEPTH, PLE_DIM, D_MODEL), PLE_DIM ** -0.5),
        'g_ple': gain(ks[23], (DEPTH, D_MODEL)),
        'w_ple_gate': nrm(ks[24], (DEPTH, D_MODEL, D_MODEL), D_MODEL ** -0.5),
        'g_final': gain(ks[25], (D_MODEL,)),
    }


def reference(x_prompt, x_sample, state_pool, p_prompt, p_sample, g_mix, w_in, w_pool,
              pool_scale, g_v, b_v, w_spatial, b_spatial, w_out, g_ffn, w_rg, b_rg, w_re,
              b_re, w_eg, w_eu, w_ed, w_ple, g_ple, w_ple_gate, g_final):
    hp = x_prompt
    hs = x_sample
    zero_hist = jnp.zeros((x_prompt.shape[0], POOL_HIST, POOL_WIDTH), x_prompt.dtype)
    pool_p, pool_s, vchunk_s = [], [], []
    for i in range(DEPTH):
        lw = (g_mix[i], w_in[i], w_pool[i], pool_scale[i], g_v[i], b_v[i], w_spatial[i],
              b_spatial[i], w_out[i], g_ffn[i], w_rg[i], b_rg[i], w_re[i], b_re[i],
              w_eg[i], w_eu[i], w_ed[i], w_ple[i], g_ple[i], w_ple_gate[i])
        hp, hist_p, _ = decoder_layer(hp, p_prompt[i], zero_hist, 0, *lw)
        hs, hist_s, v_s = decoder_layer(hs, p_sample[i], state_pool[i], PAST_LEN, *lw)
        pool_p.append(hist_p)
        pool_s.append(hist_s)
        vchunk_s.append(v_s)
    y_prompt = rmsnorm(hp, g_final)
    y_sample = rmsnorm(hs, g_final)
    new_pool_prompt = jnp.stack(pool_p, axis=0)
    new_pool_sample = jnp.stack(pool_s, axis=0)
    new_vchunk_sample = jnp.stack(vchunk_s, axis=0)
    return (y_prompt, y_sample, new_pool_prompt, new_pool_sample, new_vchunk_sample)
```

```python
import functools

import jax
import jax.numpy as jnp
from jax import lax
from jax.experimental import pallas as pl
from jax.experimental.pallas import tpu as pltpu

D_MODEL = 1024
POOL_WIDTH = 512
GMLP_WIDTH = 512
IN_WIDTH = POOL_WIDTH + 2 * GMLP_WIDTH
POOL_WINDOWS = (2, 4, 8, 16)
POOL_GROUP = 128
POOL_HIST = 15
CHUNK = 128
N_GMLP_HEADS = 4
GMLP_HEAD = 128
N_EXPERT_GROUPS = 4
EXPERTS_PER_GROUP = 8
N_EXPERTS = 32
D_EXPERT = 512
PLE_DIM = 256
EPS = 1e-6

LANES = 128
SUBLANES = 8
ROW_CHUNKS = D_MODEL // LANES

MIX_TILE = 256
SAMPLE_SEQS = 32
MOE_TILE = 256
FIN_TILE = 256
DISPATCH_CHUNK = 1024
VMEM_LIMIT = 56 * 1024 * 1024

F32 = jnp.float32
BF16 = jnp.bfloat16


def _rms(x, g):
    return x * lax.rsqrt(jnp.mean(x * x, axis=-1, keepdims=True) + EPS) * g


def _gelu_tanh(x):
    return x * (0.5 * (1.0 + jnp.tanh(0.7978845608028654 * (x + 0.044715 * (x * x * x)))))


def _store_rows(ref, val, n_rows):
    for k in range(ROW_CHUNKS):
        ref[pl.ds(k, n_rows, stride=ROW_CHUNKS), :] = val[:, k * LANES:(k + 1) * LANES]


def _load_rows(ref, n_rows, row0=0):
    return jnp.concatenate(
        [ref[pl.ds(row0 * ROW_CHUNKS + k, n_rows, stride=ROW_CHUNKS), :] for k in range(ROW_CHUNKS)],
        axis=-1)


def _route(hn, wr_hl_ref, wr_hi_ref, br_ref, cnt_scr, n_tok):
    hn_hi = hn.astype(BF16)
    hn_lo = (hn - hn_hi.astype(F32)).astype(BF16)
    l1 = jnp.dot(hn_hi, wr_hl_ref[...], preferred_element_type=F32)
    l2 = jnp.dot(hn_lo, wr_hi_ref[...], preferred_element_type=F32)
    logits = l1[:, :LANES] + (l1[:, LANES:] + l2) + br_ref[...]

    lane = lax.broadcasted_iota(jnp.int32, (n_tok, LANES), 1).astype(F32)
    neg = jnp.float32(-jnp.inf)
    big = jnp.float32(1e9)
    gmask = (lane >= N_EXPERTS) & (lane < N_EXPERTS + N_EXPERT_GROUPS)
    gl = jnp.where(gmask, logits, neg)
    gmax = jnp.max(gl, axis=-1, keepdims=True)
    gidx = jnp.min(jnp.where(gl == gmax, lane, big), axis=-1, keepdims=True) - N_EXPERTS
    gden = jnp.sum(jnp.exp(gl - gmax), axis=-1, keepdims=True)
    gprob = 1.0 / gden

    lo = gidx * EXPERTS_PER_GROUP
    emask = (lane >= lo) & (lane < lo + EXPERTS_PER_GROUP)
    el = jnp.where(emask, logits, neg)
    m1 = jnp.max(el, axis=-1, keepdims=True)
    i1 = jnp.min(jnp.where(el == m1, lane, big), axis=-1, keepdims=True)
    el2 = jnp.where(lane == i1, neg, el)
    m2 = jnp.max(el2, axis=-1, keepdims=True)
    i2 = jnp.min(jnp.where(el2 == m2, lane, big), axis=-1, keepdims=True)
    t = jnp.exp(m2 - m1)
    w1 = gprob / (1.0 + t)
    w2 = gprob * t / (1.0 + t)

    sel1 = lane == i1
    sel2 = lane == i2
    oh = jnp.where(sel1 | sel2, 1.0, 0.0)
    r = lax.broadcasted_iota(jnp.int32, (n_tok, n_tok), 0)
    c = lax.broadcasted_iota(jnp.int32, (n_tok, n_tok), 1)
    tri = jnp.where(c < r, 1.0, 0.0).astype(BF16)
    before = jnp.dot(tri, oh.astype(BF16), preferred_element_type=F32) + cnt_scr[...]
    rank1 = jnp.sum(jnp.where(sel1, before, 0.0), axis=-1, keepdims=True)
    rank2 = jnp.sum(jnp.where(sel2, before, 0.0), axis=-1, keepdims=True)
    cnt_scr[...] = cnt_scr[...] + jnp.sum(oh, axis=0, keepdims=True)

    out = jnp.where(lane == 0, i1, 0.0)
    out = jnp.where(lane == 1, i2, out)
    out = jnp.where(lane == 2, rank1, out)
    out = jnp.where(lane == 3, rank2, out)
    out = jnp.where(lane == 4, w1, out)
    out = jnp.where(lane == 5, w2, out)
    return out


def _mixer_kernel(*refs, n_tok, sample, pos0):
    if sample:
        (x_ref, hist_ref, gmix_ref, win_ref, wpool_ref, pscale_ref, gv_ref, bv_ref, wsp_ref,
         bsp_ref, wout_ref, gffn_ref, wr_hl_ref, wr_hi_ref, br_ref, cnt0_ref,
         h_ref, hn_ref, rt_ref, hist_out_ref, cnt_ref, v_ref, ext_scr, cnt_scr) = refs
        first = pl.program_id(0) == 0
    else:
        (x_ref, gmix_ref, win_ref, wpool_ref, pscale_ref, gv_ref, bv_ref, wsp_ref,
         bsp_ref, wout_ref, gffn_ref, wr_hl_ref, wr_hi_ref, br_ref, cnt0_ref,
         h_ref, hn_ref, rt_ref, hist_out_ref, cnt_ref, ext_scr, cnt_scr) = refs
        first = (pl.program_id(0) == 0) & (pl.program_id(1) == 0)

    @pl.when(first)
    def _():
        cnt_scr[...] = cnt0_ref[...]

    x = x_ref[...]
    xn = _rms(x, gmix_ref[...])
    z = jnp.dot(xn.astype(BF16), win_ref[...], preferred_element_type=F32)
    a = z[:, :POOL_WIDTH]
    uv = _gelu_tanh(z[:, POOL_WIDTH:])
    u = uv[:, :GMLP_WIDTH]
    v = uv[:, GMLP_WIDTH:]
    mu = jnp.mean(v, axis=-1, keepdims=True)
    vc = v - mu
    v = vc * lax.rsqrt(jnp.mean(vc * vc, axis=-1, keepdims=True) + EPS) * gv_ref[...] + bv_ref[...]

    pooled = []
    if sample:
        n_seq = n_tok // SUBLANES
        ext_scr[:, 1:1 + POOL_HIST, :] = hist_ref[...]
        ext_scr[:, 1 + POOL_HIST:, :] = a.reshape(n_seq, SUBLANES, POOL_WIDTH)
        hist_out_ref[...] = ext_scr[:, 1 + SUBLANES:, :]
        pos = pos0 + lax.broadcasted_iota(jnp.int32, (n_seq, SUBLANES, 1), 1)
        for gi, w in enumerate(POOL_WINDOWS):
            cols = slice(gi * POOL_GROUP, (gi + 1) * POOL_GROUP)
            win = ext_scr[:, 16:16 + SUBLANES, cols]
            for j in range(1, w):
                win = win + ext_scr[:, 16 - j:16 - j + SUBLANES, cols]
            cnt = jnp.minimum(w, pos + 1).astype(F32)
            pooled.append((win / cnt).reshape(n_tok, POOL_GROUP) - a[:, cols])
    else:
        l = pl.program_id(1)

        @pl.when(l == 0)
        def _():
            ext_scr[0:16, :] = jnp.zeros((16, POOL_WIDTH), F32)

        ext_scr[16:16 + n_tok, :] = a
        pos = pos0 + l * n_tok + lax.broadcasted_iota(jnp.int32, (n_tok, 1), 0)
        for gi, w in enumerate(POOL_WINDOWS):
            cols = slice(gi * POOL_GROUP, (gi + 1) * POOL_GROUP)
            win = a[:, cols]
            for j in range(1, w):
                win = win + ext_scr[16 - j:16 - j + n_tok, cols]
            cnt = jnp.minimum(w, pos + 1).astype(F32)
            pooled.append(win / cnt - a[:, cols])
        tail = ext_scr[n_tok + 1:n_tok + 16, :]
        hist_out_ref[...] = tail
        ext_scr[1:16, :] = tail

    pool_out = jnp.concatenate(
        [jnp.dot(pooled[gi].astype(BF16), wpool_ref[gi], preferred_element_type=F32)
         for gi in range(len(POOL_WINDOWS))], axis=-1) * pscale_ref[...]

    blk = SUBLANES if sample else CHUNK
    tr = lax.broadcasted_iota(jnp.int32, (CHUNK, CHUNK), 0)
    sc = lax.broadcasted_iota(jnp.int32, (CHUNK, CHUNK), 1)
    mask = (sc <= tr) & ((sc // blk) == (tr // blk))
    wsp = [jnp.where(mask, wsp_ref[hh], jnp.zeros((), BF16)) for hh in range(N_GMLP_HEADS)]
    v_bf = v.astype(BF16)
    mixed_chunks = []
    for ci in range(n_tok // CHUNK):
        rows = slice(ci * CHUNK, (ci + 1) * CHUNK)
        heads = [jnp.dot(wsp[hh], v_bf[rows, hh * GMLP_HEAD:(hh + 1) * GMLP_HEAD],
                         preferred_element_type=F32) for hh in range(N_GMLP_HEADS)]
        mixed_chunks.append(jnp.concatenate(heads, axis=-1) + bsp_ref[...])
    mixed = jnp.concatenate(mixed_chunks, axis=0)
    gmlp_out = u * mixed

    cat = jnp.concatenate([pool_out, gmlp_out], axis=-1)
    h = x + jnp.dot(cat.astype(BF16), wout_ref[...], preferred_element_type=F32)
    hn = _rms(h, gffn_ref[...])

    h_ref[...] = h
    _store_rows(hn_ref, hn, n_tok)
    if sample:
        v_ref[...] = v
    rt = _route(hn, wr_hl_ref, wr_hi_ref, br_ref, cnt_scr, n_tok)
    rt_ref[...] = rt[:, :SUBLANES]
    cnt_ref[...] = cnt_scr[...]


def _full(shape):
    return pl.BlockSpec(shape, lambda *_: (0,) * len(shape))


def _mixer_call(x, hist, cnt0, weights, *, sample, pos0):
    (gmix, win, wpool, pscale, gv, bv, wsp, bsp, wout, gffn, wr_hl, wr_hi, br) = weights
    w_specs = [_full(w.shape) for w in weights] + [_full((1, LANES))]
    if sample:
        n_rows = x.shape[0]
        n_tok = SAMPLE_SEQS * SUBLANES
        n_steps = n_rows // n_tok
        grid = (n_steps,)
        tok_map = lambda i: (i, 0)
        in_specs = [pl.BlockSpec((n_tok, D_MODEL), tok_map),
                    pl.BlockSpec((SAMPLE_SEQS, POOL_HIST, POOL_WIDTH), lambda i: (i, 0, 0))] + w_specs
        hist_shape = jax.ShapeDtypeStruct(hist.shape, F32)
        hist_spec = pl.BlockSpec((SAMPLE_SEQS, POOL_HIST, POOL_WIDTH), lambda i: (i, 0, 0))
        ext = pltpu.VMEM((SAMPLE_SEQS, 16 + SUBLANES, POOL_WIDTH), F32)
        args = (x, hist)
    else:
        b, seq, _ = x.shape
        n_rows = b * seq
        n_tok = MIX_TILE
        n_l = seq // n_tok
        grid = (b, n_l)
        tok_map = lambda bi, li: (bi * n_l + li, 0)
        in_specs = [pl.BlockSpec((None, n_tok, D_MODEL), lambda bi, li: (bi, li, 0))] + w_specs
        hist_shape = jax.ShapeDtypeStruct((b, POOL_HIST, POOL_WIDTH), F32)
        hist_spec = pl.BlockSpec((None, POOL_HIST, POOL_WIDTH), lambda bi, li: (bi, 0, 0))
        ext = pltpu.VMEM((16 + n_tok, POOL_WIDTH), F32)
        args = (x,)
    out_shape = [jax.ShapeDtypeStruct((n_rows, D_MODEL), F32),
                 jax.ShapeDtypeStruct((n_rows * ROW_CHUNKS, LANES), F32),
                 jax.ShapeDtypeStruct((n_rows, SUBLANES), F32),
                 hist_shape,
                 jax.ShapeDtypeStruct((1, LANES), F32)]
    out_specs = [pl.BlockSpec((n_tok, D_MODEL), tok_map),
                 pl.BlockSpec((n_tok * ROW_CHUNKS, LANES), tok_map),
                 pl.BlockSpec((n_tok, SUBLANES), tok_map),
                 hist_spec,
                 _full((1, LANES))]
    if sample:
        out_shape.append(jax.ShapeDtypeStruct((n_rows, GMLP_WIDTH), F32))
        out_specs.append(pl.BlockSpec((n_tok, GMLP_WIDTH), tok_map))
    return pl.pallas_call(
        functools.partial(_mixer_kernel, n_tok=n_tok, sample=sample, pos0=pos0),
        grid=grid, in_specs=in_specs, out_specs=out_specs, out_shape=out_shape,
        scratch_shapes=[ext, pltpu.VMEM((1, LANES), F32)],
        compiler_params=pltpu.CompilerParams(
            dimension_semantics=("arbitrary",) * len(grid), vmem_limit_bytes=VMEM_LIMIT),
        name="mixer_sample" if sample else "mixer_prompt",
    )(*args, *weights, cnt0)


def _dispatch_kernel(pos_ref, hnp_ref, hns_ref, xs_in_ref, xs_ref, sem, *, n_prompt_steps):
    del xs_in_ref
    i = pl.program_id(0)

    def run(src_ref, local0):
        tok0 = i * DISPATCH_CHUNK

        def body(t, carry):
            src = src_ref.at[pl.ds(pl.multiple_of((local0 + t) * ROW_CHUNKS, ROW_CHUNKS), ROW_CHUNKS)]
            for k in range(2):
                p = pos_ref[2 * (tok0 + t) + k]
                dst = xs_ref.at[pl.ds(pl.multiple_of(p * ROW_CHUNKS, ROW_CHUNKS), ROW_CHUNKS)]
                pltpu.make_async_copy(src, dst, sem).start()
            return carry

        lax.fori_loop(0, DISPATCH_CHUNK, body, 0)
        n = DISPATCH_CHUNK * ROW_CHUNKS
        for _ in range(2):
            pltpu.make_async_copy(src_ref.at[pl.ds(0, n)], xs_ref.at[pl.ds(0, n)], sem).wait()

    @pl.when(i < n_prompt_steps)
    def _():
        run(hnp_ref, i * DISPATCH_CHUNK)

    @pl.when(i >= n_prompt_steps)
    def _():
        run(hns_ref, (i - n_prompt_steps) * DISPATCH_CHUNK)


def _dispatch_call(pos_flat, hn_p, hn_s, xs_init):
    n_p = hn_p.shape[0] // ROW_CHUNKS
    n_s = hn_s.shape[0] // ROW_CHUNKS
    n_prompt_steps = n_p // DISPATCH_CHUNK
    n_steps = n_prompt_steps + n_s // DISPATCH_CHUNK
    any_spec = pl.BlockSpec(memory_space=pl.ANY)
    return pl.pallas_call(
        functools.partial(_dispatch_kernel, n_prompt_steps=n_prompt_steps),
        grid_spec=pltpu.PrefetchScalarGridSpec(
            num_scalar_prefetch=1, grid=(n_steps,),
            in_specs=[any_spec, any_spec, any_spec], out_specs=any_spec,
            scratch_shapes=[pltpu.SemaphoreType.DMA(())]),
        out_shape=jax.ShapeDtypeStruct(xs_init.shape, F32),
        input_output_aliases={3: 0},
        compiler_params=pltpu.CompilerParams(dimension_semantics=("arbitrary",)),
        name="dispatch",
    )(pos_flat, hn_p, hn_s, xs_init)


def _moe_kernel(te_ref, nu_ref, xs_ref, wg_ref, wu_ref, wd_ref, ys_ref, wg_bf, wu_bf, wd_bf):
    i = pl.program_id(0)
    used = i < nu_ref[0]

    @pl.when(used)
    def _():
        prev = te_ref[jnp.maximum(i - 1, 0)]

        @pl.when((i == 0) | (te_ref[i] != prev))
        def _():
            wg_bf[...] = wg_ref[...].astype(BF16)
            wu_bf[...] = wu_ref[...].astype(BF16)
            wd_bf[...] = wd_ref[...].astype(BF16)

        x = _load_rows(xs_ref, MOE_TILE).astype(BF16)
        g = jnp.dot(x, wg_bf[...], preferred_element_type=F32)
        u = jnp.dot(x, wu_bf[...], preferred_element_type=F32)
        hh = (g * jax.nn.sigmoid(g)) * u
        y = jnp.dot(hh.astype(BF16), wd_bf[...], preferred_element_type=F32)
        _store_rows(ys_ref, y, MOE_TILE)

    @pl.when(jnp.logical_not(used))
    def _():
        ys_ref[...] = jnp.zeros(ys_ref.shape, F32)


def _moe_call(tile_expert, n_used, xs, w_eg, w_eu, w_ed):
    n_tiles = xs.shape[0] // (MOE_TILE * ROW_CHUNKS)
    blk = MOE_TILE * ROW_CHUNKS
    return pl.pallas_call(
        _moe_kernel,
        grid_spec=pltpu.PrefetchScalarGridSpec(
            num_scalar_prefetch=2, grid=(n_tiles,),
            in_specs=[
                pl.BlockSpec((blk, LANES), lambda i, te, nu: (jnp.minimum(i, nu[0] - 1), 0)),
                pl.BlockSpec((None, D_MODEL, D_EXPERT), lambda i, te, nu: (te[i], 0, 0)),
                pl.BlockSpec((None, D_MODEL, D_EXPERT), lambda i, te, nu: (te[i], 0, 0)),
                pl.BlockSpec((None, D_EXPERT, D_MODEL), lambda i, te, nu: (te[i], 0, 0)),
            ],
            out_specs=pl.BlockSpec((blk, LANES), lambda i, te, nu: (i, 0)),
            scratch_shapes=[pltpu.VMEM((D_MODEL, D_EXPERT), BF16),
                            pltpu.VMEM((D_MODEL, D_EXPERT), BF16),
                            pltpu.VMEM((D_EXPERT, D_MODEL), BF16)]),
        out_shape=jax.ShapeDtypeStruct(xs.shape, F32),
        compiler_params=pltpu.CompilerParams(
            dimension_semantics=("arbitrary",), vmem_limit_bytes=VMEM_LIMIT),
        name="experts",
    )(tile_expert, n_used, xs, w_eg, w_eu, w_ed)


def _final_kernel(pos_ref, hp_ref, hs_ref, rt_ref, pp_ref, ps_ref, ys_ref, wple_ref, wgate_ref,
                  gple_ref, gfin_ref, outp_ref, outs_ref, ybuf, sem, *, n_prompt_steps):
    i = pl.program_id(0)
    n = pl.num_programs(0)
    rows = FIN_TILE * ROW_CHUNKS

    def issue(tile, slot):
        def body(t, carry):
            for k in range(2):
                p = pos_ref[2 * (tile * FIN_TILE + t) + k]
                src = ys_ref.at[pl.ds(pl.multiple_of(p * ROW_CHUNKS, ROW_CHUNKS), ROW_CHUNKS)]
                dst = ybuf.at[slot, pl.ds(pl.multiple_of((k * FIN_TILE + t) * ROW_CHUNKS, ROW_CHUNKS),
                                          ROW_CHUNKS)]
                pltpu.make_async_copy(src, dst, sem.at[slot]).start()
            return carry

        lax.fori_loop(0, FIN_TILE, body, 0)

    @pl.when(i == 0)
    def _():
        issue(0, 0)

    @pl.when(i + 1 < n)
    def _():
        issue(i + 1, (i + 1) % 2)

    slot = i % 2
    pltpu.make_async_copy(ys_ref.at[pl.ds(0, 2 * rows)], ybuf.at[slot], sem.at[slot]).wait()

    is_prompt = i < n_prompt_steps
    h = jnp.where(is_prompt, hp_ref[...], hs_ref[...])
    p = jnp.where(is_prompt, pp_ref[...], ps_ref[...])
    rt = rt_ref[...]
    yb = ybuf.at[slot]
    y1 = _load_rows(yb, FIN_TILE, 0)
    y2 = _load_rows(yb, FIN_TILE, FIN_TILE)
    h = h + (rt[:, 4:5] * y1 + rt[:, 5:6] * y2)
    r = _rms(h, gple_ref[...])
    gate = jax.nn.sigmoid(jnp.dot(r.astype(BF16), wgate_ref[...], preferred_element_type=F32))
    h = h + jnp.dot(p.astype(BF16), wple_ref[...], preferred_element_type=F32) * gate
    y = _rms(h, gfin_ref[...])

    @pl.when(is_prompt)
    def _():
        outp_ref[...] = y

    @pl.when(jnp.logical_not(is_prompt))
    def _():
        outs_ref[...] = y


def _final_call(pos_flat, h_p, h_s, rt, p_p, p_s, ys, wple, wgate, gple, gfin):
    n_p, n_s = h_p.shape[0], h_s.shape[0]
    np_steps = n_p // FIN_TILE
    n_steps = np_steps + n_s // FIN_TILE
    pmap = lambda i, pos: (jnp.minimum(i, np_steps - 1), 0)
    smap = lambda i, pos: (jnp.maximum(i - np_steps, 0), 0)
    cmap = lambda i, pos: (0, 0)
    return pl.pallas_call(
        functools.partial(_final_kernel, n_prompt_steps=np_steps),
        grid_spec=pltpu.PrefetchScalarGridSpec(
            num_scalar_prefetch=1, grid=(n_steps,),
            in_specs=[
                pl.BlockSpec((FIN_TILE, D_MODEL), pmap),
                pl.BlockSpec((FIN_TILE, D_MODEL), smap),
                pl.BlockSpec((FIN_TILE, SUBLANES), lambda i, pos: (i, 0)),
                pl.BlockSpec((FIN_TILE, PLE_DIM), pmap),
                pl.BlockSpec((FIN_TILE, PLE_DIM), smap),
                pl.BlockSpec(memory_space=pl.ANY),
                pl.BlockSpec(wple.shape, cmap),
                pl.BlockSpec(wgate.shape, cmap),
                pl.BlockSpec(gple.shape, cmap),
                pl.BlockSpec(gfin.shape, cmap),
            ],
            out_specs=[pl.BlockSpec((FIN_TILE, D_MODEL), pmap),
                       pl.BlockSpec((FIN_TILE, D_MODEL), smap)],
            scratch_shapes=[pltpu.VMEM((2, 2 * FIN_TILE * ROW_CHUNKS, LANES), F32),
                            pltpu.SemaphoreType.DMA((2,))]),
        out_shape=[jax.ShapeDtypeStruct((n_p, D_MODEL), F32),
                   jax.ShapeDtypeStruct((n_s, D_MODEL), F32)],
        compiler_params=pltpu.CompilerParams(
            dimension_semantics=("arbitrary",), vmem_limit_bytes=VMEM_LIMIT),
        name="final",
    )(pos_flat, h_p, h_s, rt, p_p, p_s, ys, wple, wgate, gple, gfin)


def kernel(x_prompt, x_sample, state_pool, p_prompt, p_sample, g_mix, w_in, w_pool, pool_scale,
           g_v, b_v, w_spatial, b_spatial, w_out, g_ffn, w_rg, b_rg, w_re, b_re, w_eg, w_eu, w_ed,
           w_ple, g_ple, w_ple_gate, g_final):
    batch, seq, _ = x_prompt.shape
    dec_batch, dec_seq, _ = x_sample.shape
    assert dec_seq == SUBLANES and g_mix.shape[0] == 1
    n_p = batch * seq
    n_s = dec_batch * dec_seq
    n_tok = n_p + n_s
    past_len = 16384

    row = lambda a: a.reshape(1, -1).astype(F32)
    wr = jnp.zeros((D_MODEL, LANES), F32)
    wr = wr.at[:, :N_EXPERTS].set(w_re[0]).at[:, N_EXPERTS:N_EXPERTS + N_EXPERT_GROUPS].set(w_rg[0])
    wr_hi = wr.astype(BF16)
    wr_lo = (wr - wr_hi.astype(F32)).astype(BF16)
    br = jnp.zeros((1, LANES), F32)
    br = br.at[0, :N_EXPERTS].set(b_re[0]).at[0, N_EXPERTS:N_EXPERTS + N_EXPERT_GROUPS].set(b_rg[0])
    bsp_p = jnp.repeat(jnp.transpose(b_spatial[0]), GMLP_HEAD, axis=1)
    bsp_s = jnp.tile(bsp_p[:dec_seq], (CHUNK // dec_seq, 1))
    wsp_p = w_spatial[0].astype(BF16)
    wsp_s = jnp.tile(w_spatial[0][:, :dec_seq, :dec_seq], (1, CHUNK // dec_seq, CHUNK // dec_seq)).astype(BF16)

    def mixer_weights(wsp, bsp):
        return (row(g_mix[0]), w_in[0].astype(BF16), w_pool[0].astype(BF16), row(pool_scale[0]),
                row(g_v[0]), row(b_v[0]), wsp, bsp, w_out[0].astype(BF16), row(g_ffn[0]),
                jnp.concatenate([wr_hi, wr_lo], axis=1), wr_hi, br)

    cnt0 = jnp.zeros((1, LANES), F32)
    h_p, hn_p, rt_p, hist_p, cnt_p = _mixer_call(
        x_prompt, None, cnt0, mixer_weights(wsp_p, bsp_p), sample=False, pos0=0)
    h_s, hn_s, rt_s, hist_s, cnt_all, v_s = _mixer_call(
        x_sample.reshape(n_s, D_MODEL), state_pool[0], cnt_p, mixer_weights(wsp_s, bsp_s),
        sample=True, pos0=past_len)

    rt = jnp.concatenate([rt_p, rt_s], axis=0)
    counts = cnt_all[0, :N_EXPERTS].astype(jnp.int32)
    padded = ((counts + MOE_TILE - 1) // MOE_TILE) * MOE_TILE
    ends = jnp.cumsum(padded)
    offs = ends - padded
    eid = rt[:, 0:2].astype(jnp.int32)
    rank = rt[:, 2:4].astype(jnp.int32)
    onehot = eid[:, :, None] == jnp.arange(N_EXPERTS, dtype=jnp.int32)[None, None, :]
    pos = jnp.sum(jnp.where(onehot, offs[None, None, :], 0), axis=-1) + rank
    pos_flat = pos.reshape(-1).astype(jnp.int32)
    n_tiles = (2 * n_tok) // MOE_TILE + N_EXPERTS
    n_used = (ends[-1] // MOE_TILE).astype(jnp.int32)
    tile_id = jnp.minimum(jnp.arange(n_tiles, dtype=jnp.int32), n_used - 1)
    tile_expert = jnp.sum((tile_id[:, None] * MOE_TILE) >= ends[None, :], axis=1).astype(jnp.int32)
    tile_expert = jnp.minimum(tile_expert, N_EXPERTS - 1)

    xs_init = jnp.zeros((n_tiles * MOE_TILE * ROW_CHUNKS, LANES), F32)
    xs = _dispatch_call(pos_flat, hn_p, hn_s, xs_init)
    ys = _moe_call(tile_expert, n_used.reshape(1), xs, w_eg[0], w_eu[0], w_ed[0])
    y_p, y_s = _final_call(pos_flat, h_p, h_s, rt, p_prompt[0].reshape(n_p, PLE_DIM),
                           p_sample[0].reshape(n_s, PLE_DIM), ys,
                           w_ple[0].astype(BF16), w_ple_gate[0].astype(BF16), row(g_ple[0]), row(g_final))

    return (y_p.reshape(batch, seq, D_MODEL),
            y_s.reshape(dec_batch, dec_seq, D_MODEL),
            hist_p[None],
            hist_s[None],
            v_s.reshape(1, dec_batch, dec_seq, GMLP_WIDTH))
```

```python
import functools

import jax
import jax.numpy as jnp
from jax import lax
from jax.experimental import pallas as pl
from jax.experimental.pallas import tpu as pltpu

D_MODEL = 1024
POOL_WIDTH = 512
GMLP_WIDTH = 512
IN_WIDTH = POOL_WIDTH + 2 * GMLP_WIDTH
POOL_WINDOWS = (2, 4, 8, 16)
POOL_GROUP = 128
POOL_HIST = 15
CHUNK = 128
N_GMLP_HEADS = 4
GMLP_HEAD = 128
N_EXPERT_GROUPS = 4
EXPERTS_PER_GROUP = 8
N_EXPERTS = 32
D_EXPERT = 512
PLE_DIM = 256
EPS = 1e-6

LANES = 128
SUBLANES = 8
ROW_CHUNKS = D_MODEL // LANES

MIX_TILE = 256
SAMPLE_SEQS = 32
MOE_TILE = 256
FIN_TILE = 256
DISPATCH_CHUNK = 1024
VMEM_LIMIT = 56 * 1024 * 1024

F32 = jnp.float32
BF16 = jnp.bfloat16


def _rms(x, g):
    return x * lax.rsqrt(jnp.mean(x * x, axis=-1, keepdims=True) + EPS) * g


def _gelu_tanh(x):
    return x * (0.5 * (1.0 + jnp.tanh(0.7978845608028654 * (x + 0.044715 * (x * x * x)))))


def _store_rows(ref, val, n_rows):
    for k in range(ROW_CHUNKS):
        ref[pl.ds(k, n_rows, stride=ROW_CHUNKS), :] = val[:, k * LANES:(k + 1) * LANES]


def _load_rows(ref, n_rows, row0=0):
    return jnp.concatenate(
        [ref[pl.ds(row0 * ROW_CHUNKS + k, n_rows, stride=ROW_CHUNKS), :] for k in range(ROW_CHUNKS)],
        axis=-1)


def _route(hn, wr_hl_ref, wr_hi_ref, br_ref, cnt_scr, n_tok):
    hn_hi = hn.astype(BF16)
    hn_lo = (hn - hn_hi.astype(F32)).astype(BF16)
    l1 = jnp.dot(hn_hi, wr_hl_ref[...], preferred_element_type=F32)
    l2 = jnp.dot(hn_lo, wr_hi_ref[...], preferred_element_type=F32)
    logits = l1[:, :LANES] + (l1[:, LANES:] + l2) + br_ref[...]

    lane = lax.broadcasted_iota(jnp.int32, (n_tok, LANES), 1).astype(F32)
    neg = jnp.float32(-jnp.inf)
    big = jnp.float32(1e9)
    gmask = (lane >= N_EXPERTS) & (lane < N_EXPERTS + N_EXPERT_GROUPS)
    gl = jnp.where(gmask, logits, neg)
    gmax = jnp.max(gl, axis=-1, keepdims=True)
    gidx = jnp.min(jnp.where(gl == gmax, lane, big), axis=-1, keepdims=True) - N_EXPERTS
    gden = jnp.sum(jnp.exp(gl - gmax), axis=-1, keepdims=True)
    gprob = 1.0 / gden

    lo = gidx * EXPERTS_PER_GROUP
    emask = (lane >= lo) & (lane < lo + EXPERTS_PER_GROUP)
    el = jnp.where(emask, logits, neg)
    m1 = jnp.max(el, axis=-1, keepdims=True)
    i1 = jnp.min(jnp.where(el == m1, lane, big), axis=-1, keepdims=True)
    el2 = jnp.where(lane == i1, neg, el)
    m2 = jnp.max(el2, axis=-1, keepdims=True)
    i2 = jnp.min(jnp.where(el2 == m2, lane, big), axis=-1, keepdims=True)
    t = jnp.exp(m2 - m1)
    w1 = gprob / (1.0 + t)
    w2 = gprob * t / (1.0 + t)

    sel1 = lane == i1
    sel2 = lane == i2
    oh = jnp.where(sel1 | sel2, 1.0, 0.0)
    r = lax.broadcasted_iota(jnp.int32, (n_tok, n_tok), 0)
    c = lax.broadcasted_iota(jnp.int32, (n_tok, n_tok), 1)
    tri = jnp.where(c < r, 1.0, 0.0).astype(BF16)
    before = jnp.dot(tri, oh.astype(BF16), preferred_element_type=F32) + cnt_scr[...]
    rank1 = jnp.sum(jnp.where(sel1, before, 0.0), axis=-1, keepdims=True)
    rank2 = jnp.sum(jnp.where(sel2, before, 0.0), axis=-1, keepdims=True)
    cnt_scr[...] = cnt_scr[...] + jnp.sum(oh, axis=0, keepdims=True)

    out = jnp.where(lane == 0, i1, 0.0)
    out = jnp.where(lane == 1, i2, out)
    out = jnp.where(lane == 2, rank1, out)
    out = jnp.where(lane == 3, rank2, out)
    out = jnp.where(lane == 4, w1, out)
    out = jnp.where(lane == 5, w2, out)
    return out


def _mixer_kernel(*refs, n_tok, sample, pos0):
    if sample:
        (x_ref, hist_ref, gmix_ref, win_ref, wpool_ref, pscale_ref, gv_ref, bv_ref, wsp_ref,
         bsp_ref, wout_ref, gffn_ref, wr_hl_ref, wr_hi_ref, br_ref, cnt0_ref,
         h_ref, hn_ref, rt_ref, hist_out_ref, cnt_ref, v_ref, ext_scr, cnt_scr) = refs
        first = pl.program_id(0) == 0
    else:
        (x_ref, gmix_ref, win_ref, wpool_ref, pscale_ref, gv_ref, bv_ref, wsp_ref,
         bsp_ref, wout_ref, gffn_ref, wr_hl_ref, wr_hi_ref, br_ref, cnt0_ref,
         h_ref, hn_ref, rt_ref, hist_out_ref, cnt_ref, ext_scr, cnt_scr) = refs
        first = (pl.program_id(0) == 0) & (pl.program_id(1) == 0)

    @pl.when(first)
    def _():
        cnt_scr[...] = cnt0_ref[...]

    x = x_ref[...]
    xn = _rms(x, gmix_ref[...])
    z = jnp.dot(xn.astype(BF16), win_ref[...], preferred_element_type=F32)
    a = z[:, :POOL_WIDTH]
    uv = _gelu_tanh(z[:, POOL_WIDTH:])
    u = uv[:, :GMLP_WIDTH]
    v = uv[:, GMLP_WIDTH:]
    mu = jnp.mean(v, axis=-1, keepdims=True)
    vc = v - mu
    v = vc * lax.rsqrt(jnp.mean(vc * vc, axis=-1, keepdims=True) + EPS) * gv_ref[...] + bv_ref[...]

    pooled = []
    if sample:
        n_seq = n_tok // SUBLANES
        ext_scr[:, 1:1 + POOL_HIST, :] = hist_ref[...]
        ext_scr[:, 1 + POOL_HIST:, :] = a.reshape(n_seq, SUBLANES, POOL_WIDTH)
        hist_out_ref[...] = ext_scr[:, 1 + SUBLANES:, :]
        pos = pos0 + lax.broadcasted_iota(jnp.int32, (n_seq, SUBLANES, 1), 1)
        for gi, w in enumerate(POOL_WINDOWS):
            cols = slice(gi * POOL_GROUP, (gi + 1) * POOL_GROUP)
            win = ext_scr[:, 16:16 + SUBLANES, cols]
            for j in range(1, w):
                win = win + ext_scr[:, 16 - j:16 - j + SUBLANES, cols]
            cnt = jnp.minimum(w, pos + 1).astype(F32)
            pooled.append((win / cnt).reshape(n_tok, POOL_GROUP) - a[:, cols])
    else:
        l = pl.program_id(1)

        @pl.when(l == 0)
        def _():
            ext_scr[0:16, :] = jnp.zeros((16, POOL_WIDTH), F32)

        ext_scr[16:16 + n_tok, :] = a
        pos = pos0 + l * n_tok + lax.broadcasted_iota(jnp.int32, (n_tok, 1), 0)
        for gi, w in enumerate(POOL_WINDOWS):
            cols = slice(gi * POOL_GROUP, (gi + 1) * POOL_GROUP)
            win = a[:, cols]
            for j in range(1, w):
                win = win + ext_scr[16 - j:16 - j + n_tok, cols]
            cnt = jnp.minimum(w, pos + 1).astype(F32)
            pooled.append(win / cnt - a[:, cols])
        tail = ext_scr[n_tok + 1:n_tok + 16, :]
        hist_out_ref[...] = tail
        ext_scr[1:16, :] = tail

    pool_out = jnp.concatenate(
        [jnp.dot(pooled[gi].astype(BF16), wpool_ref[gi], preferred_element_type=F32)
         for gi in range(len(POOL_WINDOWS))], axis=-1) * pscale_ref[...]

    blk = SUBLANES if sample else CHUNK
    tr = lax.broadcasted_iota(jnp.int32, (CHUNK, CHUNK), 0)
    sc = lax.broadcasted_iota(jnp.int32, (CHUNK, CHUNK), 1)
    mask = (sc <= tr) & ((sc // blk) == (tr // blk))
    wsp = [jnp.where(mask, wsp_ref[hh], jnp.zeros((), BF16)) for hh in range(N_GMLP_HEADS)]
    v_bf = v.astype(BF16)
    mixed_chunks = []
    for ci in range(n_tok // CHUNK):
        rows = slice(ci * CHUNK, (ci + 1) * CHUNK)
        heads = [jnp.dot(wsp[hh], v_bf[rows, hh * GMLP_HEAD:(hh + 1) * GMLP_HEAD],
                         preferred_element_type=F32) for hh in range(N_GMLP_HEADS)]
        mixed_chunks.append(jnp.concatenate(heads, axis=-1) + bsp_ref[...])
    mixed = jnp.concatenate(mixed_chunks, axis=0)
    gmlp_out = u * mixed

    cat = jnp.concatenate([pool_out, gmlp_out], axis=-1)
    h = x + jnp.dot(cat.astype(BF16), wout_ref[...], preferred_element_type=F32)
    hn = _rms(h, gffn_ref[...])

    h_ref[...] = h
    _store_rows(hn_ref, hn, n_tok)
    if sample:
        v_ref[...] = v
    rt = _route(hn, wr_hl_ref, wr_hi_ref, br_ref, cnt_scr, n_tok)
    rt_ref[...] = rt[:, :SUBLANES]
    cnt_ref[...] = cnt_scr[...]


def _full(shape):
    return pl.BlockSpec(shape, lambda *_: (0,) * len(shape))


def _mixer_call(x, hist, cnt0, weights, *, sample, pos0):
    (gmix, win, wpool, pscale, gv, bv, wsp, bsp, wout, gffn, wr_hl, wr_hi, br) = weights
    w_specs = [_full(w.shape) for w in weights] + [_full((1, LANES))]
    if sample:
        n_rows = x.shape[0]
        n_tok = SAMPLE_SEQS * SUBLANES
        n_steps = n_rows // n_tok
        grid = (n_steps,)
        tok_map = lambda i: (i, 0)
        in_specs = [pl.BlockSpec((n_tok, D_MODEL), tok_map),
                    pl.BlockSpec((SAMPLE_SEQS, POOL_HIST, POOL_WIDTH), lambda i: (i, 0, 0))] + w_specs
        hist_shape = jax.ShapeDtypeStruct(hist.shape, F32)
        hist_spec = pl.BlockSpec((SAMPLE_SEQS, POOL_HIST, POOL_WIDTH), lambda i: (i, 0, 0))
        ext = pltpu.VMEM((SAMPLE_SEQS, 16 + SUBLANES, POOL_WIDTH), F32)
        args = (x, hist)
    else:
        b, seq, _ = x.shape
        n_rows = b * seq
        n_tok = MIX_TILE
        n_l = seq // n_tok
        grid = (b, n_l)
        tok_map = lambda bi, li: (bi * n_l + li, 0)
        in_specs = [pl.BlockSpec((None, n_tok, D_MODEL), lambda bi, li: (bi, li, 0))] + w_specs
        hist_shape = jax.ShapeDtypeStruct((b, POOL_HIST, POOL_WIDTH), F32)
        hist_spec = pl.BlockSpec((None, POOL_HIST, POOL_WIDTH), lambda bi, li: (bi, 0, 0))
        ext = pltpu.VMEM((16 + n_tok, POOL_WIDTH), F32)
        args = (x,)
    out_shape = [jax.ShapeDtypeStruct((n_rows, D_MODEL), F32),
                 jax.ShapeDtypeStruct((n_rows * ROW_CHUNKS, LANES), F32),
                 jax.ShapeDtypeStruct((n_rows, SUBLANES), F32),
                 hist_shape,
                 jax.ShapeDtypeStruct((1, LANES), F32)]
    out_specs = [pl.BlockSpec((n_tok, D_MODEL), tok_map),
                 pl.BlockSpec((n_tok * ROW_CHUNKS, LANES), tok_map),
                 pl.BlockSpec((n_tok, SUBLANES), tok_map),
                 hist_spec,
                 _full((1, LANES))]
    if sample:
        out_shape.append(jax.ShapeDtypeStruct((n_rows, GMLP_WIDTH), F32))
        out_specs.append(pl.BlockSpec((n_tok, GMLP_WIDTH), tok_map))
    return pl.pallas_call(
        functools.partial(_mixer_kernel, n_tok=n_tok, sample=sample, pos0=pos0),
        grid=grid, in_specs=in_specs, out_specs=out_specs, out_shape=out_shape,
        scratch_shapes=[ext, pltpu.VMEM((1, LANES), F32)],
        compiler_params=pltpu.CompilerParams(
            dimension_semantics=("arbitrary",) * len(grid), vmem_limit_bytes=VMEM_LIMIT),
        name="mixer_sample" if sample else "mixer_prompt",
    )(*args, *weights, cnt0)


def _dispatch_kernel(pos_ref, hnp_ref, hns_ref, xs_in_ref, xs_ref, sem, *, n_prompt_steps):
    del xs_in_ref
    i = pl.program_id(0)
    n = DISPATCH_CHUNK * ROW_CHUNKS

    def run(src_ref):
        tok0 = i * DISPATCH_CHUNK

        def body(t, carry):
            src = src_ref.at[pl.ds(pl.multiple_of(t * ROW_CHUNKS, ROW_CHUNKS), ROW_CHUNKS)]
            for k in range(2):
                p = pos_ref[2 * (tok0 + t) + k]
                dst = xs_ref.at[pl.ds(pl.multiple_of(p * ROW_CHUNKS, ROW_CHUNKS), ROW_CHUNKS)]
                pltpu.make_async_copy(src, dst, sem).start()
            return carry

        lax.fori_loop(0, DISPATCH_CHUNK, body, 0)
        for _ in range(2):
            pltpu.make_async_copy(src_ref, xs_ref.at[pl.ds(0, n)], sem).wait()

    @pl.when(i < n_prompt_steps)
    def _():
        run(hnp_ref)

    @pl.when(i >= n_prompt_steps)
    def _():
        run(hns_ref)


def _dispatch_call(pos_flat, hn_p, hn_s, xs_init):
    n_p = hn_p.shape[0] // ROW_CHUNKS
    n_s = hn_s.shape[0] // ROW_CHUNKS
    n_prompt_steps = n_p // DISPATCH_CHUNK
    n_steps = n_prompt_steps + n_s // DISPATCH_CHUNK
    blk = DISPATCH_CHUNK * ROW_CHUNKS
    any_spec = pl.BlockSpec(memory_space=pl.ANY)
    return pl.pallas_call(
        functools.partial(_dispatch_kernel, n_prompt_steps=n_prompt_steps),
        grid_spec=pltpu.PrefetchScalarGridSpec(
            num_scalar_prefetch=1, grid=(n_steps,),
            in_specs=[pl.BlockSpec((blk, LANES), lambda i, pos: (jnp.minimum(i, n_prompt_steps - 1), 0)),
                      pl.BlockSpec((blk, LANES), lambda i, pos: (jnp.maximum(i - n_prompt_steps, 0), 0)),
                      any_spec],
            out_specs=any_spec,
            scratch_shapes=[pltpu.SemaphoreType.DMA(())]),
        out_shape=jax.ShapeDtypeStruct(xs_init.shape, F32),
        input_output_aliases={3: 0},
        compiler_params=pltpu.CompilerParams(dimension_semantics=("arbitrary",)),
        name="dispatch",
    )(pos_flat, hn_p, hn_s, xs_init)


def _moe_kernel(te_ref, nu_ref, xs_ref, wg_ref, wu_ref, wd_ref, ys_ref, wg_bf, wu_bf, wd_bf):
    i = pl.program_id(0)
    used = i < nu_ref[0]

    @pl.when(used)
    def _():
        prev = te_ref[jnp.maximum(i - 1, 0)]

        @pl.when((i == 0) | (te_ref[i] != prev))
        def _():
            wg_bf[...] = wg_ref[...].astype(BF16)
            wu_bf[...] = wu_ref[...].astype(BF16)
            wd_bf[...] = wd_ref[...].astype(BF16)

        x = _load_rows(xs_ref, MOE_TILE).astype(BF16)
        g = jnp.dot(x, wg_bf[...], preferred_element_type=F32)
        u = jnp.dot(x, wu_bf[...], preferred_element_type=F32)
        hh = (g * jax.nn.sigmoid(g)) * u
        y = jnp.dot(hh.astype(BF16), wd_bf[...], preferred_element_type=F32)
        _store_rows(ys_ref, y, MOE_TILE)

    @pl.when(jnp.logical_not(used))
    def _():
        ys_ref[...] = jnp.zeros(ys_ref.shape, F32)


def _moe_call(tile_expert, n_used, xs, w_eg, w_eu, w_ed):
    n_tiles = xs.shape[0] // (MOE_TILE * ROW_CHUNKS)
    blk = MOE_TILE * ROW_CHUNKS
    return pl.pallas_call(
        _moe_kernel,
        grid_spec=pltpu.PrefetchScalarGridSpec(
            num_scalar_prefetch=2, grid=(n_tiles,),
            in_specs=[
                pl.BlockSpec((blk, LANES), lambda i, te, nu: (jnp.minimum(i, nu[0] - 1), 0)),
                pl.BlockSpec((None, D_MODEL, D_EXPERT), lambda i, te, nu: (te[i], 0, 0)),
                pl.BlockSpec((None, D_MODEL, D_EXPERT), lambda i, te, nu: (te[i], 0, 0)),
                pl.BlockSpec((None, D_EXPERT, D_MODEL), lambda i, te, nu: (te[i], 0, 0)),
            ],
            out_specs=pl.BlockSpec((blk, LANES), lambda i, te, nu: (i, 0)),
            scratch_shapes=[pltpu.VMEM((D_MODEL, D_EXPERT), BF16),
                            pltpu.VMEM((D_MODEL, D_EXPERT), BF16),
                            pltpu.VMEM((D_EXPERT, D_MODEL), BF16)]),
        out_shape=jax.ShapeDtypeStruct(xs.shape, F32),
        compiler_params=pltpu.CompilerParams(
            dimension_semantics=("arbitrary",), vmem_limit_bytes=VMEM_LIMIT),
        name="experts",
    )(tile_expert, n_used, xs, w_eg, w_eu, w_ed)


def _final_kernel(pos_ref, hp_ref, hs_ref, rt_ref, pp_ref, ps_ref, ys_ref, wple_ref, wgate_ref,
                  gple_ref, gfin_ref, outp_ref, outs_ref, ybuf, sem, *, n_prompt_steps):
    i = pl.program_id(0)
    n = pl.num_programs(0)
    rows = FIN_TILE * ROW_CHUNKS

    def issue(tile, slot):
        def body(t, carry):
            for k in range(2):
                p = pos_ref[2 * (tile * FIN_TILE + t) + k]
                src = ys_ref.at[pl.ds(pl.multiple_of(p * ROW_CHUNKS, ROW_CHUNKS), ROW_CHUNKS)]
                dst = ybuf.at[slot, pl.ds(pl.multiple_of((k * FIN_TILE + t) * ROW_CHUNKS, ROW_CHUNKS),
                                          ROW_CHUNKS)]
                pltpu.make_async_copy(src, dst, sem.at[slot]).start()
            return carry

        lax.fori_loop(0, FIN_TILE, body, 0)

    @pl.when(i == 0)
    def _():
        issue(0, 0)

    @pl.when(i + 1 < n)
    def _():
        issue(i + 1, (i + 1) % 2)

    slot = i % 2
    pltpu.make_async_copy(ys_ref.at[pl.ds(0, 2 * rows)], ybuf.at[slot], sem.at[slot]).wait()

    is_prompt = i < n_prompt_steps
    h = jnp.where(is_prompt, hp_ref[...], hs_ref[...])
    p = jnp.where(is_prompt, pp_ref[...], ps_ref[...])
    rt = rt_ref[...]
    yb = ybuf.at[slot]
    y1 = _load_rows(yb, FIN_TILE, 0)
    y2 = _load_rows(yb, FIN_TILE, FIN_TILE)
    h = h + (rt[:, 4:5] * y1 + rt[:, 5:6] * y2)
    r = _rms(h, gple_ref[...])
    gate = jax.nn.sigmoid(jnp.dot(r.astype(BF16), wgate_ref[...], preferred_element_type=F32))
    h = h + jnp.dot(p.astype(BF16), wple_ref[...], preferred_element_type=F32) * gate
    y = _rms(h, gfin_ref[...])

    @pl.when(is_prompt)
    def _():
        outp_ref[...] = y

    @pl.when(jnp.logical_not(is_prompt))
    def _():
        outs_ref[...] = y


def _final_call(pos_flat, h_p, h_s, rt, p_p, p_s, ys, wple, wgate, gple, gfin):
    n_p, n_s = h_p.shape[0], h_s.shape[0]
    np_steps = n_p // FIN_TILE
    n_steps = np_steps + n_s // FIN_TILE
    pmap = lambda i, pos: (jnp.minimum(i, np_steps - 1), 0)
    smap = lambda i, pos: (jnp.maximum(i - np_steps, 0), 0)
    cmap = lambda i, pos: (0, 0)
    return pl.pallas_call(
        functools.partial(_final_kernel, n_prompt_steps=np_steps),
        grid_spec=pltpu.PrefetchScalarGridSpec(
            num_scalar_prefetch=1, grid=(n_steps,),
            in_specs=[
                pl.BlockSpec((FIN_TILE, D_MODEL), pmap),
                pl.BlockSpec((FIN_TILE, D_MODEL), smap),
                pl.BlockSpec((FIN_TILE, SUBLANES), lambda i, pos: (i, 0)),
                pl.BlockSpec((FIN_TILE, PLE_DIM), pmap),
                pl.BlockSpec((FIN_TILE, PLE_DIM), smap),
                pl.BlockSpec(memory_space=pl.ANY),
                pl.BlockSpec(wple.shape, cmap),
                pl.BlockSpec(wgate.shape, cmap),
                pl.BlockSpec(gple.shape, cmap),
                pl.BlockSpec(gfin.shape, cmap),
            ],
            out_specs=[pl.BlockSpec((FIN_TILE, D_MODEL), pmap),
                       pl.BlockSpec((FIN_TILE, D_MODEL), smap)],
            scratch_shapes=[pltpu.VMEM((2, 2 * FIN_TILE * ROW_CHUNKS, LANES), F32),
                            pltpu.SemaphoreType.DMA((2,))]),
        out_shape=[jax.ShapeDtypeStruct((n_p, D_MODEL), F32),
                   jax.ShapeDtypeStruct((n_s, D_MODEL), F32)],
        compiler_params=pltpu.CompilerParams(
            dimension_semantics=("arbitrary",), vmem_limit_bytes=VMEM_LIMIT),
        name="final",
    )(pos_flat, h_p, h_s, rt, p_p, p_s, ys, wple, wgate, gple, gfin)


def kernel(x_prompt, x_sample, state_pool, p_prompt, p_sample, g_mix, w_in, w_pool, pool_scale,
           g_v, b_v, w_spatial, b_spatial, w_out, g_ffn, w_rg, b_rg, w_re, b_re, w_eg, w_eu, w_ed,
           w_ple, g_ple, w_ple_gate, g_final):
    batch, seq, _ = x_prompt.shape
    dec_batch, dec_seq, _ = x_sample.shape
    assert dec_seq == SUBLANES and g_mix.shape[0] == 1
    n_p = batch * seq
    n_s = dec_batch * dec_seq
    n_tok = n_p + n_s
    past_len = 16384

    row = lambda a: a.reshape(1, -1).astype(F32)
    wr = jnp.zeros((D_MODEL, LANES), F32)
    wr = wr.at[:, :N_EXPERTS].set(w_re[0]).at[:, N_EXPERTS:N_EXPERTS + N_EXPERT_GROUPS].set(w_rg[0])
    wr_hi = wr.astype(BF16)
    wr_lo = (wr - wr_hi.astype(F32)).astype(BF16)
    br = jnp.zeros((1, LANES), F32)
    br = br.at[0, :N_EXPERTS].set(b_re[0]).at[0, N_EXPERTS:N_EXPERTS + N_EXPERT_GROUPS].set(b_rg[0])
    bsp_p = jnp.repeat(jnp.transpose(b_spatial[0]), GMLP_HEAD, axis=1)
    bsp_s = jnp.tile(bsp_p[:dec_seq], (CHUNK // dec_seq, 1))
    wsp_p = w_spatial[0].astype(BF16)
    wsp_s = jnp.tile(w_spatial[0][:, :dec_seq, :dec_seq], (1, CHUNK // dec_seq, CHUNK // dec_seq)).astype(BF16)

    def mixer_weights(wsp, bsp):
        return (row(g_mix[0]), w_in[0].astype(BF16), w_pool[0].astype(BF16), row(pool_scale[0]),
                row(g_v[0]), row(b_v[0]), wsp, bsp, w_out[0].astype(BF16), row(g_ffn[0]),
                jnp.concatenate([wr_hi, wr_lo], axis=1), wr_hi, br)

    cnt0 = jnp.zeros((1, LANES), F32)
    h_p, hn_p, rt_p, hist_p, cnt_p = _mixer_call(
        x_prompt, None, cnt0, mixer_weights(wsp_p, bsp_p), sample=False, pos0=0)
    h_s, hn_s, rt_s, hist_s, cnt_all, v_s = _mixer_call(
        x_sample.reshape(n_s, D_MODEL), state_pool[0], cnt_p, mixer_weights(wsp_s, bsp_s),
        sample=True, pos0=past_len)

    rt = jnp.concatenate([rt_p, rt_s], axis=0)
    counts = cnt_all[0, :N_EXPERTS].astype(jnp.int32)
    padded = ((counts + MOE_TILE - 1) // MOE_TILE) * MOE_TILE
    ends = jnp.cumsum(padded)
    offs = ends - padded
    eid = rt[:, 0:2].astype(jnp.int32)
    rank = rt[:, 2:4].astype(jnp.int32)
    onehot = eid[:, :, None] == jnp.arange(N_EXPERTS, dtype=jnp.int32)[None, None, :]
    pos = jnp.sum(jnp.where(onehot, offs[None, None, :], 0), axis=-1) + rank
    pos_flat = pos.reshape(-1).astype(jnp.int32)
    n_tiles = (2 * n_tok) // MOE_TILE + N_EXPERTS
    n_used = (ends[-1] // MOE_TILE).astype(jnp.int32)
    tile_id = jnp.minimum(jnp.arange(n_tiles, dtype=jnp.int32), n_used - 1)
    tile_expert = jnp.sum((tile_id[:, None] * MOE_TILE) >= ends[None, :], axis=1).astype(jnp.int32)
    tile_expert = jnp.minimum(tile_expert, N_EXPERTS - 1)

    xs_init = jnp.zeros((n_tiles * MOE_TILE * ROW_CHUNKS, LANES), F32)
    xs = _dispatch_call(pos_flat, hn_p, hn_s, xs_init)
    ys = _moe_call(tile_expert, n_used.reshape(1), xs, w_eg[0], w_eu[0], w_ed[0])
    y_p, y_s = _final_call(pos_flat, h_p, h_s, rt, p_prompt[0].reshape(n_p, PLE_DIM),
                           p_sample[0].reshape(n_s, PLE_DIM), ys,
                           w_ple[0].astype(BF16), w_ple_gate[0].astype(BF16), row(g_ple[0]), row(g_final))

    return (y_p.reshape(batch, seq, D_MODEL),
            y_s.reshape(dec_batch, dec_seq, D_MODEL),
            hist_p[None],
            hist_s[None],
            v_s.reshape(1, dec_batch, dec_seq, GMLP_WIDTH))
```

```python
import functools

import jax
import jax.numpy as jnp
from jax import lax
from jax.experimental import pallas as pl
from jax.experimental.pallas import tpu as pltpu

D_MODEL = 1024
POOL_WIDTH = 512
GMLP_WIDTH = 512
IN_WIDTH = POOL_WIDTH + 2 * GMLP_WIDTH
POOL_WINDOWS = (2, 4, 8, 16)
POOL_GROUP = 128
POOL_HIST = 15
CHUNK = 128
N_GMLP_HEADS = 4
GMLP_HEAD = 128
N_EXPERT_GROUPS = 4
EXPERTS_PER_GROUP = 8
N_EXPERTS = 32
D_EXPERT = 512
PLE_DIM = 256
EPS = 1e-6

LANES = 128
SUBLANES = 8
ROW_CHUNKS = D_MODEL // LANES

TOK_TILE = 256
MOE_TILE = 256
RUN_ROWS = 8
LOC_ROWS = 2 * TOK_TILE
LOC_PAD_ROWS = LOC_ROWS + N_EXPERTS * RUN_ROWS
RUN_BLOCK = RUN_ROWS * ROW_CHUNKS
VMEM_LIMIT = 56 * 1024 * 1024

F32 = jnp.float32
BF16 = jnp.bfloat16


def _rms(x, g):
    return x * lax.rsqrt(jnp.mean(x * x, axis=-1, keepdims=True) + EPS) * g


def _gelu_tanh(x):
    return x * (0.5 * (1.0 + jnp.tanh(0.7978845608028654 * (x + 0.044715 * (x * x * x)))))


def _store_rows(ref, val, n_rows):
    for k in range(ROW_CHUNKS):
        ref[pl.ds(k, n_rows, stride=ROW_CHUNKS), :] = val[:, k * LANES:(k + 1) * LANES]


def _load_rows(ref, n_rows):
    return jnp.concatenate(
        [ref[pl.ds(k, n_rows, stride=ROW_CHUNKS), :] for k in range(ROW_CHUNKS)], axis=-1)


def _rows(start, n):
    return pl.ds(pl.multiple_of(start * ROW_CHUNKS, ROW_CHUNKS), n * ROW_CHUNKS)


def _route(hn, wr_hl_ref, wr_hi_ref, br_ref, n_tok):
    hn_hi = hn.astype(BF16)
    hn_lo = (hn - hn_hi.astype(F32)).astype(BF16)
    l1 = jnp.dot(hn_hi, wr_hl_ref[...], preferred_element_type=F32)
    l2 = jnp.dot(hn_lo, wr_hi_ref[...], preferred_element_type=F32)
    logits = l1[:, :LANES] + (l1[:, LANES:] + l2) + br_ref[...]

    lane = lax.broadcasted_iota(jnp.int32, (n_tok, LANES), 1).astype(F32)
    neg = jnp.float32(-jnp.inf)
    big = jnp.float32(1e9)
    gmask = (lane >= N_EXPERTS) & (lane < N_EXPERTS + N_EXPERT_GROUPS)
    gl = jnp.where(gmask, logits, neg)
    gmax = jnp.max(gl, axis=-1, keepdims=True)
    gidx = jnp.min(jnp.where(gl == gmax, lane, big), axis=-1, keepdims=True) - N_EXPERTS
    gden = jnp.sum(jnp.exp(gl - gmax), axis=-1, keepdims=True)
    gprob = 1.0 / gden

    lo = gidx * EXPERTS_PER_GROUP
    emask = (lane >= lo) & (lane < lo + EXPERTS_PER_GROUP)
    el = jnp.where(emask, logits, neg)
    m1 = jnp.max(el, axis=-1, keepdims=True)
    i1 = jnp.min(jnp.where(el == m1, lane, big), axis=-1, keepdims=True)
    el2 = jnp.where(lane == i1, neg, el)
    m2 = jnp.max(el2, axis=-1, keepdims=True)
    i2 = jnp.min(jnp.where(el2 == m2, lane, big), axis=-1, keepdims=True)
    t = jnp.exp(m2 - m1)
    w1 = gprob / (1.0 + t)
    w2 = gprob * t / (1.0 + t)

    sel1 = lane == i1
    sel2 = lane == i2
    oh = jnp.where(sel1 | sel2, 1.0, 0.0).astype(BF16)
    r = lax.broadcasted_iota(jnp.int32, (n_tok, n_tok), 0)
    c = lax.broadcasted_iota(jnp.int32, (n_tok, n_tok), 1)
    tri = jnp.where(c < r, 1.0, 0.0).astype(BF16)
    before = jnp.dot(tri, oh, preferred_element_type=F32)
    er = lax.broadcasted_iota(jnp.int32, (LANES, LANES), 0)
    ec = lax.broadcasted_iota(jnp.int32, (LANES, LANES), 1)
    upper = jnp.where(er < ec, 1.0, 0.0).astype(BF16)
    lstart = jnp.sum(jnp.dot(oh, upper, preferred_element_type=F32), axis=0, keepdims=True)
    counts = jnp.sum(oh.astype(F32), axis=0, keepdims=True)
    lrank1 = jnp.sum(jnp.where(sel1, before, 0.0), axis=-1, keepdims=True)
    lrank2 = jnp.sum(jnp.where(sel2, before, 0.0), axis=-1, keepdims=True)
    lpos1 = jnp.sum(jnp.where(sel1, lstart, 0.0), axis=-1, keepdims=True) + lrank1
    lpos2 = jnp.sum(jnp.where(sel2, lstart, 0.0), axis=-1, keepdims=True) + lrank2

    out = jnp.where(lane == 0, i1, 0.0)
    for k, val in enumerate((i2, lrank1, lrank2, w1, w2, lpos1, lpos2), start=1):
        out = jnp.where(lane == k, val, out)
    return out, counts


def _mixer_kernel(*refs, n_tok, sample, pos0):
    if sample:
        (x_ref, hist_ref, gmix_ref, win_ref, wpool_ref, pscale_ref, gv_ref, bv_ref, wsp_ref,
         bsp_ref, wout_ref, gffn_ref, wr_hl_ref, wr_hi_ref, br_ref, cnt0_ref,
         h_ref, hn_ref, rt_ref, tab_ref, hist_out_ref, cnt_ref, v_ref, ext_scr, cnt_scr) = refs
        first = pl.program_id(0) == 0
    else:
        (x_ref, gmix_ref, win_ref, wpool_ref, pscale_ref, gv_ref, bv_ref, wsp_ref,
         bsp_ref, wout_ref, gffn_ref, wr_hl_ref, wr_hi_ref, br_ref, cnt0_ref,
         h_ref, hn_ref, rt_ref, tab_ref, hist_out_ref, cnt_ref, ext_scr, cnt_scr) = refs
        first = (pl.program_id(0) == 0) & (pl.program_id(1) == 0)

    @pl.when(first)
    def _():
        cnt_scr[...] = cnt0_ref[...]

    x = x_ref[...]
    xn = _rms(x, gmix_ref[...])
    z = jnp.dot(xn.astype(BF16), win_ref[...], preferred_element_type=F32)
    a = z[:, :POOL_WIDTH]
    uv = _gelu_tanh(z[:, POOL_WIDTH:])
    u = uv[:, :GMLP_WIDTH]
    v = uv[:, GMLP_WIDTH:]
    mu = jnp.mean(v, axis=-1, keepdims=True)
    vc = v - mu
    v = vc * lax.rsqrt(jnp.mean(vc * vc, axis=-1, keepdims=True) + EPS) * gv_ref[...] + bv_ref[...]

    pooled = []
    if sample:
        n_seq = n_tok // SUBLANES
        ext_scr[:, 1:1 + POOL_HIST, :] = hist_ref[...]
        ext_scr[:, 1 + POOL_HIST:, :] = a.reshape(n_seq, SUBLANES, POOL_WIDTH)
        hist_out_ref[...] = ext_scr[:, 1 + SUBLANES:, :]
        pos = pos0 + lax.broadcasted_iota(jnp.int32, (n_seq, SUBLANES, 1), 1)
        for gi, w in enumerate(POOL_WINDOWS):
            cols = slice(gi * POOL_GROUP, (gi + 1) * POOL_GROUP)
            win = ext_scr[:, 16:16 + SUBLANES, cols]
            for j in range(1, w):
                win = win + ext_scr[:, 16 - j:16 - j + SUBLANES, cols]
            cnt = jnp.minimum(w, pos + 1).astype(F32)
            pooled.append((win / cnt).reshape(n_tok, POOL_GROUP) - a[:, cols])
    else:
        l = pl.program_id(1)

        @pl.when(l == 0)
        def _():
            ext_scr[0:16, :] = jnp.zeros((16, POOL_WIDTH), F32)

        ext_scr[16:16 + n_tok, :] = a
        pos = pos0 + l * n_tok + lax.broadcasted_iota(jnp.int32, (n_tok, 1), 0)
        for gi, w in enumerate(POOL_WINDOWS):
            cols = slice(gi * POOL_GROUP, (gi + 1) * POOL_GROUP)
            win = a[:, cols]
            for j in range(1, w):
                win = win + ext_scr[16 - j:16 - j + n_tok, cols]
            cnt = jnp.minimum(w, pos + 1).astype(F32)
            pooled.append(win / cnt - a[:, cols])
        tail = ext_scr[n_tok + 1:n_tok + 16, :]
        hist_out_ref[...] = tail
        ext_scr[1:16, :] = tail

    pool_out = jnp.concatenate(
        [jnp.dot(pooled[gi].astype(BF16), wpool_ref[gi], preferred_element_type=F32)
         for gi in range(len(POOL_WINDOWS))], axis=-1) * pscale_ref[...]

    blk = SUBLANES if sample else CHUNK
    tr = lax.broadcasted_iota(jnp.int32, (CHUNK, CHUNK), 0)
    sc = lax.broadcasted_iota(jnp.int32, (CHUNK, CHUNK), 1)
    mask = (sc <= tr) & ((sc // blk) == (tr // blk))
    wsp = [jnp.where(mask, wsp_ref[hh], jnp.zeros((), BF16)) for hh in range(N_GMLP_HEADS)]
    v_bf = v.astype(BF16)
    mixed_chunks = []
    for ci in range(n_tok // CHUNK):
        rows = slice(ci * CHUNK, (ci + 1) * CHUNK)
        heads = [jnp.dot(wsp[hh], v_bf[rows, hh * GMLP_HEAD:(hh + 1) * GMLP_HEAD],
                         preferred_element_type=F32) for hh in range(N_GMLP_HEADS)]
        mixed_chunks.append(jnp.concatenate(heads, axis=-1) + bsp_ref[...])
    mixed = jnp.concatenate(mixed_chunks, axis=0)
    gmlp_out = u * mixed

    cat = jnp.concatenate([pool_out, gmlp_out], axis=-1)
    h = x + jnp.dot(cat.astype(BF16), wout_ref[...], preferred_element_type=F32)
    hn = _rms(h, gffn_ref[...])

    h_ref[...] = h
    hn_ref[...] = hn.astype(BF16)
    if sample:
        v_ref[...] = v
    rt, counts = _route(hn, wr_hl_ref, wr_hi_ref, br_ref, n_tok)
    rt_ref[...] = rt[:, :SUBLANES]
    tab_ref[0:1, :] = cnt_scr[...]
    tab_ref[1:2, :] = counts
    cnt_scr[...] = cnt_scr[...] + counts
    cnt_ref[...] = cnt_scr[...]


def _full(shape):
    return pl.BlockSpec(shape, lambda *_: (0,) * len(shape))


def _mixer_call(x, hist, cnt0, weights, *, sample, pos0):
    w_specs = [_full(w.shape) for w in weights] + [_full((1, LANES))]
    n_tok = TOK_TILE
    if sample:
        n_rows = x.shape[0]
        n_seq = n_tok // SUBLANES
        grid = (n_rows // n_tok,)
        tok_map = lambda i: (i, 0)
        tab_map = lambda i: (i, 0, 0)
        in_specs = [pl.BlockSpec((n_tok, D_MODEL), tok_map),
                    pl.BlockSpec((n_seq, POOL_HIST, POOL_WIDTH), lambda i: (i, 0, 0))] + w_specs
        hist_shape = jax.ShapeDtypeStruct(hist.shape, F32)
        hist_spec = pl.BlockSpec((n_seq, POOL_HIST, POOL_WIDTH), lambda i: (i, 0, 0))
        ext = pltpu.VMEM((n_seq, 16 + SUBLANES, POOL_WIDTH), F32)
        args = (x, hist)
    else:
        b, seq, _ = x.shape
        n_rows = b * seq
        n_l = seq // n_tok
        grid = (b, n_l)
        tok_map = lambda bi, li: (bi * n_l + li, 0)
        tab_map = lambda bi, li: (bi * n_l + li, 0, 0)
        in_specs = [pl.BlockSpec((None, n_tok, D_MODEL), lambda bi, li: (bi, li, 0))] + w_specs
        hist_shape = jax.ShapeDtypeStruct((b, POOL_HIST, POOL_WIDTH), F32)
        hist_spec = pl.BlockSpec((None, POOL_HIST, POOL_WIDTH), lambda bi, li: (bi, 0, 0))
        ext = pltpu.VMEM((16 + n_tok, POOL_WIDTH), F32)
        args = (x,)
    out_shape = [jax.ShapeDtypeStruct((n_rows, D_MODEL), F32),
                 jax.ShapeDtypeStruct((n_rows, D_MODEL), BF16),
                 jax.ShapeDtypeStruct((n_rows, SUBLANES), F32),
                 jax.ShapeDtypeStruct((n_rows // n_tok, 2, LANES), F32),
                 hist_shape,
                 jax.ShapeDtypeStruct((1, LANES), F32)]
    out_specs = [pl.BlockSpec((n_tok, D_MODEL), tok_map),
                 pl.BlockSpec((n_tok, D_MODEL), tok_map),
                 pl.BlockSpec((n_tok, SUBLANES), tok_map),
                 pl.BlockSpec((None, 2, LANES), tab_map),
                 hist_spec,
                 _full((1, LANES))]
    if sample:
        out_shape.append(jax.ShapeDtypeStruct((n_rows, GMLP_WIDTH), F32))
        out_specs.append(pl.BlockSpec((n_tok, GMLP_WIDTH), tok_map))
    return pl.pallas_call(
        functools.partial(_mixer_kernel, n_tok=n_tok, sample=sample, pos0=pos0),
        grid=grid, in_specs=in_specs, out_specs=out_specs, out_shape=out_shape,
        scratch_shapes=[ext, pltpu.VMEM((1, LANES), F32)],
        compiler_params=pltpu.CompilerParams(
            dimension_semantics=("arbitrary",) * len(grid), vmem_limit_bytes=VMEM_LIMIT),
        name="mixer_sample" if sample else "mixer_prompt",
    )(*args, *weights, cnt0)


def _for_each_run_block(tile, n_ref, fn):
    def per_expert(e, carry):
        idx = tile * N_EXPERTS + e
        n_blocks = (n_ref[idx] + RUN_ROWS - 1) // RUN_ROWS

        def per_block(c, carry2):
            fn(idx, c)
            return carry2

        return lax.fori_loop(0, n_blocks, per_block, carry)

    lax.fori_loop(0, N_EXPERTS, per_expert, 0)


def _dispatch_kernel(n_ref, gs_ref, ls_ref, tb_ref, zs_ref, zb_ref, nu_ref,
                     hnp_ref, hns_ref, rt_ref, xs_ref, loc, zbuf, sem, zsem, *, n_prompt_steps):
    i = pl.program_id(0)
    n_steps = pl.num_programs(0)
    slot = i % 2

    def run_copy(buf_slot, idx, c):
        src = loc.at[buf_slot, _rows(ls_ref[idx] + c * RUN_ROWS, RUN_ROWS)]
        dst = xs_ref.at[_rows(gs_ref[idx] + c * RUN_ROWS, RUN_ROWS)]
        return pltpu.make_async_copy(src, dst, sem)

    def drain(n_blocks):
        def body(_, carry):
            run_copy(0, 0, 0).wait()
            return carry
        lax.fori_loop(0, n_blocks, body, 0)

    @pl.when(i == 0)
    def _():
        zbuf[...] = jnp.zeros(zbuf.shape, F32)
        for s in range(2):
            loc[s, pl.ds(LOC_ROWS * ROW_CHUNKS, RUN_BLOCK), :] = jnp.zeros((RUN_BLOCK, LANES), F32)

        def zero_copy(e, c):
            return pltpu.make_async_copy(zbuf.at[pl.ds(0, RUN_BLOCK)],
                                         xs_ref.at[_rows(zs_ref[e] + c * RUN_ROWS, RUN_ROWS)], zsem)

        n_tiles = xs_ref.shape[0] // (MOE_TILE * ROW_CHUNKS)

        def tail_copy(t):
            return pltpu.make_async_copy(zbuf, xs_ref.at[_rows(t * MOE_TILE, MOE_TILE)], zsem)

        def tail_start(t, carry):
            tail_copy(t).start()
            return carry

        def tail_wait(t, carry):
            tail_copy(t).wait()
            return carry

        spill = pltpu.make_async_copy(zbuf.at[pl.ds(0, RUN_BLOCK)],
                                      xs_ref.at[_rows(n_tiles * MOE_TILE, RUN_ROWS)], zsem)

        def per_expert(e, total):
            def per_block(c, carry):
                zero_copy(e, c).start()
                return carry
            lax.fori_loop(0, zb_ref[e], per_block, 0)
            return total + zb_ref[e]

        total = lax.fori_loop(0, N_EXPERTS, per_expert, 0)

        def wait_body(_, carry):
            zero_copy(0, 0).wait()
            return carry
        lax.fori_loop(0, total, wait_body, 0)
        lax.fori_loop(nu_ref[0], n_tiles, tail_start, 0)
        spill.start()
        lax.fori_loop(nu_ref[0], n_tiles, tail_wait, 0)
        spill.wait()

    rt = rt_ref[...]
    lane = lax.broadcasted_iota(jnp.int32, (TOK_TILE, LOC_ROWS), 1).astype(F32)
    perm_t = jnp.where((lane == rt[:, 6:7]) | (lane == rt[:, 7:8]), 1.0, 0.0).astype(BF16)
    hn = jnp.where(i < n_prompt_steps, hnp_ref[...], hns_ref[...])
    grouped = lax.dot_general(perm_t, hn, (((0,), (0,)), ((), ())), preferred_element_type=F32)
    _store_rows(loc.at[slot], grouped, LOC_ROWS)

    @pl.when(i > 0)
    def _():
        drain(tb_ref[i - 1])

    _for_each_run_block(i, n_ref, lambda idx, c: run_copy(slot, idx, c).start())

    @pl.when(i == n_steps - 1)
    def _():
        drain(tb_ref[i])


def _dispatch_call(tables, hn_p, hn_s, rt, n_sorted_rows):
    n_p, n_s = hn_p.shape[0], hn_s.shape[0]
    np_steps = n_p // TOK_TILE
    n_steps = np_steps + n_s // TOK_TILE
    pmap = lambda i, *_: (jnp.minimum(i, np_steps - 1), 0)
    smap = lambda i, *_: (jnp.maximum(i - np_steps, 0), 0)
    out_rows = (n_sorted_rows + RUN_ROWS) * ROW_CHUNKS
    return pl.pallas_call(
        functools.partial(_dispatch_kernel, n_prompt_steps=np_steps),
        grid_spec=pltpu.PrefetchScalarGridSpec(
            num_scalar_prefetch=len(tables), grid=(n_steps,),
            in_specs=[pl.BlockSpec((TOK_TILE, D_MODEL), pmap),
                      pl.BlockSpec((TOK_TILE, D_MODEL), smap),
                      pl.BlockSpec((TOK_TILE, SUBLANES), lambda i, *_: (i, 0))],
            out_specs=pl.BlockSpec(memory_space=pl.ANY),
            scratch_shapes=[pltpu.VMEM((2, (LOC_ROWS + RUN_ROWS) * ROW_CHUNKS, LANES), F32),
                            pltpu.VMEM((MOE_TILE * ROW_CHUNKS, LANES), F32),
                            pltpu.SemaphoreType.DMA(()),
                            pltpu.SemaphoreType.DMA(())]),
        out_shape=jax.ShapeDtypeStruct((out_rows, LANES), F32),
        compiler_params=pltpu.CompilerParams(
            dimension_semantics=("arbitrary",), vmem_limit_bytes=VMEM_LIMIT),
        name="dispatch",
    )(*tables, hn_p, hn_s, rt)


def _moe_kernel(te_ref, nu_ref, xs_ref, wg_ref, wu_ref, wd_ref, ys_ref, wg_bf, wu_bf, wd_bf):
    i = pl.program_id(0)
    used = i < nu_ref[0]

    @pl.when(used)
    def _():
        prev = te_ref[jnp.maximum(i - 1, 0)]

        @pl.when((i == 0) | (te_ref[i] != prev))
        def _():
            wg_bf[...] = wg_ref[...].astype(BF16)
            wu_bf[...] = wu_ref[...].astype(BF16)
            wd_bf[...] = wd_ref[...].astype(BF16)

        x = _load_rows(xs_ref, MOE_TILE).astype(BF16)
        g = jnp.dot(x, wg_bf[...], preferred_element_type=F32)
        u = jnp.dot(x, wu_bf[...], preferred_element_type=F32)
        hh = (g * jax.nn.sigmoid(g)) * u
        y = jnp.dot(hh.astype(BF16), wd_bf[...], preferred_element_type=F32)
        _store_rows(ys_ref, y, MOE_TILE)

    @pl.when(jnp.logical_not(used))
    def _():
        ys_ref[...] = jnp.zeros(ys_ref.shape, F32)


def _moe_call(tile_expert, n_used, xs, n_tiles, w_eg, w_eu, w_ed):
    blk = MOE_TILE * ROW_CHUNKS
    return pl.pallas_call(
        _moe_kernel,
        grid_spec=pltpu.PrefetchScalarGridSpec(
            num_scalar_prefetch=2, grid=(n_tiles,),
            in_specs=[
                pl.BlockSpec((blk, LANES), lambda i, te, nu: (jnp.minimum(i, nu[0] - 1), 0)),
                pl.BlockSpec((None, D_MODEL, D_EXPERT), lambda i, te, nu: (te[i], 0, 0)),
                pl.BlockSpec((None, D_MODEL, D_EXPERT), lambda i, te, nu: (te[i], 0, 0)),
                pl.BlockSpec((None, D_EXPERT, D_MODEL), lambda i, te, nu: (te[i], 0, 0)),
            ],
            out_specs=pl.BlockSpec((blk, LANES), lambda i, te, nu: (i, 0)),
            scratch_shapes=[pltpu.VMEM((D_MODEL, D_EXPERT), BF16),
                            pltpu.VMEM((D_MODEL, D_EXPERT), BF16),
                            pltpu.VMEM((D_EXPERT, D_MODEL), BF16)]),
        out_shape=jax.ShapeDtypeStruct((n_tiles * blk, LANES), F32),
        compiler_params=pltpu.CompilerParams(
            dimension_semantics=("arbitrary",), vmem_limit_bytes=VMEM_LIMIT),
        name="experts",
    )(tile_expert, n_used, xs, w_eg, w_eu, w_ed)


def _final_kernel(n_ref, gs_ref, lsp_ref, tb_ref, hp_ref, hs_ref, rt_ref, lspv_ref, pp_ref, ps_ref,
                  ys_ref, wple_ref, wgate_ref, gple_ref, gfin_ref, outp_ref, outs_ref, ybuf, sem,
                  *, n_prompt_steps):
    i = pl.program_id(0)
    n_steps = pl.num_programs(0)
    slot = i % 2

    def run_copy(buf_slot, idx, c):
        src = ys_ref.at[_rows(gs_ref[idx] + c * RUN_ROWS, RUN_ROWS)]
        dst = ybuf.at[buf_slot, _rows(lsp_ref[idx] + c * RUN_ROWS, RUN_ROWS)]
        return pltpu.make_async_copy(src, dst, sem.at[buf_slot])

    def issue(tile, buf_slot):
        _for_each_run_block(tile, n_ref, lambda idx, c: run_copy(buf_slot, idx, c).start())

    @pl.when(i == 0)
    def _():
        ybuf[...] = jnp.zeros(ybuf.shape, F32)
        issue(0, 0)

    @pl.when(i + 1 < n_steps)
    def _():
        issue(i + 1, 1 - slot)

    def wait_body(_, carry):
        run_copy(slot, 0, 0).wait()
        return carry
    lax.fori_loop(0, tb_ref[i], wait_body, 0)

    is_prompt = i < n_prompt_steps
    h = jnp.where(is_prompt, hp_ref[...], hs_ref[...])
    p = jnp.where(is_prompt, pp_ref[...], ps_ref[...])
    rt = rt_ref[...]
    elane = lax.broadcasted_iota(jnp.int32, (TOK_TILE, LANES), 1).astype(F32)
    lsp = lspv_ref[...]
    lpos1 = jnp.sum(jnp.where(elane == rt[:, 0:1], lsp, 0.0), axis=-1, keepdims=True) + rt[:, 2:3]
    lpos2 = jnp.sum(jnp.where(elane == rt[:, 1:2], lsp, 0.0), axis=-1, keepdims=True) + rt[:, 3:4]
    lane = lax.broadcasted_iota(jnp.int32, (TOK_TILE, LOC_PAD_ROWS), 1).astype(F32)
    yb = _load_rows(ybuf.at[slot], LOC_PAD_ROWS).astype(BF16)
    y1 = jnp.dot(jnp.where(lane == lpos1, 1.0, 0.0).astype(BF16), yb, preferred_element_type=F32)
    y2 = jnp.dot(jnp.where(lane == lpos2, 1.0, 0.0).astype(BF16), yb, preferred_element_type=F32)
    h = h + (rt[:, 4:5] * y1 + rt[:, 5:6] * y2)
    r = _rms(h, gple_ref[...])
    gate = jax.nn.sigmoid(jnp.dot(r.astype(BF16), wgate_ref[...], preferred_element_type=F32))
    h = h + jnp.dot(p.astype(BF16), wple_ref[...], preferred_element_type=F32) * gate
    y = _rms(h, gfin_ref[...])

    @pl.when(is_prompt)
    def _():
        outp_ref[...] = y

    @pl.when(jnp.logical_not(is_prompt))
    def _():
        outs_ref[...] = y


def _final_call(tables, h_p, h_s, rt, lspv, p_p, p_s, ys, wple, wgate, gple, gfin):
    n_p, n_s = h_p.shape[0], h_s.shape[0]
    np_steps = n_p // TOK_TILE
    n_steps = np_steps + n_s // TOK_TILE
    pmap = lambda i, *_: (jnp.minimum(i, np_steps - 1), 0)
    smap = lambda i, *_: (jnp.maximum(i - np_steps, 0), 0)
    cmap = lambda i, *_: (0, 0)
    return pl.pallas_call(
        functools.partial(_final_kernel, n_prompt_steps=np_steps),
        grid_spec=pltpu.PrefetchScalarGridSpec(
            num_scalar_prefetch=len(tables), grid=(n_steps,),
            in_specs=[
                pl.BlockSpec((TOK_TILE, D_MODEL), pmap),
                pl.BlockSpec((TOK_TILE, D_MODEL), smap),
                pl.BlockSpec((TOK_TILE, SUBLANES), lambda i, *_: (i, 0)),
                pl.BlockSpec((None, 1, LANES), lambda i, *_: (i, 0, 0)),
                pl.BlockSpec((TOK_TILE, PLE_DIM), pmap),
                pl.BlockSpec((TOK_TILE, PLE_DIM), smap),
                pl.BlockSpec(memory_space=pl.ANY),
                pl.BlockSpec(wple.shape, cmap),
                pl.BlockSpec(wgate.shape, cmap),
                pl.BlockSpec(gple.shape, cmap),
                pl.BlockSpec(gfin.shape, cmap),
            ],
            out_specs=[pl.BlockSpec((TOK_TILE, D_MODEL), pmap),
                       pl.BlockSpec((TOK_TILE, D_MODEL), smap)],
            scratch_shapes=[pltpu.VMEM((2, LOC_PAD_ROWS * ROW_CHUNKS, LANES), F32),
                            pltpu.SemaphoreType.DMA((2,))]),
        out_shape=[jax.ShapeDtypeStruct((n_p, D_MODEL), F32),
                   jax.ShapeDtypeStruct((n_s, D_MODEL), F32)],
        compiler_params=pltpu.CompilerParams(
            dimension_semantics=("arbitrary",), vmem_limit_bytes=VMEM_LIMIT),
        name="final",
    )(*tables, h_p, h_s, rt, lspv, p_p, p_s, ys, wple, wgate, gple, gfin)


def _ceil_to(x, m):
    return ((x + m - 1) // m) * m


def kernel(x_prompt, x_sample, state_pool, p_prompt, p_sample, g_mix, w_in, w_pool, pool_scale,
           g_v, b_v, w_spatial, b_spatial, w_out, g_ffn, w_rg, b_rg, w_re, b_re, w_eg, w_eu, w_ed,
           w_ple, g_ple, w_ple_gate, g_final):
    batch, seq, _ = x_prompt.shape
    dec_batch, dec_seq, _ = x_sample.shape
    assert dec_seq == SUBLANES and g_mix.shape[0] == 1
    n_p = batch * seq
    n_s = dec_batch * dec_seq
    n_tok = n_p + n_s
    past_len = 16384
    i32 = jnp.int32

    row = lambda a: a.reshape(1, -1).astype(F32)
    wr = jnp.zeros((D_MODEL, LANES), F32)
    wr = wr.at[:, :N_EXPERTS].set(w_re[0]).at[:, N_EXPERTS:N_EXPERTS + N_EXPERT_GROUPS].set(w_rg[0])
    wr_hi = wr.astype(BF16)
    wr_lo = (wr - wr_hi.astype(F32)).astype(BF16)
    br = jnp.zeros((1, LANES), F32)
    br = br.at[0, :N_EXPERTS].set(b_re[0]).at[0, N_EXPERTS:N_EXPERTS + N_EXPERT_GROUPS].set(b_rg[0])
    bsp_p = jnp.repeat(jnp.transpose(b_spatial[0]), GMLP_HEAD, axis=1)
    bsp_s = jnp.tile(bsp_p[:dec_seq], (CHUNK // dec_seq, 1))
    wsp_p = w_spatial[0].astype(BF16)
    wsp_s = jnp.tile(w_spatial[0][:, :dec_seq, :dec_seq], (1, CHUNK // dec_seq, CHUNK // dec_seq)).astype(BF16)

    def mixer_weights(wsp, bsp):
        return (row(g_mix[0]), w_in[0].astype(BF16), w_pool[0].astype(BF16), row(pool_scale[0]),
                row(g_v[0]), row(b_v[0]), wsp, bsp, w_out[0].astype(BF16), row(g_ffn[0]),
                jnp.concatenate([wr_hi, wr_lo], axis=1), wr_hi, br)

    cnt0 = jnp.zeros((1, LANES), F32)
    h_p, hn_p, rt_p, tab_p, hist_p, cnt_p = _mixer_call(
        x_prompt, None, cnt0, mixer_weights(wsp_p, bsp_p), sample=False, pos0=0)
    h_s, hn_s, rt_s, tab_s, hist_s, cnt_all, v_s = _mixer_call(
        x_sample.reshape(n_s, D_MODEL), state_pool[0], cnt_p, mixer_weights(wsp_s, bsp_s),
        sample=True, pos0=past_len)

    rt = jnp.concatenate([rt_p, rt_s], axis=0)
    tab = jnp.concatenate([tab_p, tab_s], axis=0)[:, :, :N_EXPERTS].astype(i32)
    base, n_run = tab[:, 0, :], tab[:, 1, :]
    counts = cnt_all[0, :N_EXPERTS].astype(i32)
    padded = _ceil_to(counts + (RUN_ROWS - 1), MOE_TILE)
    ends = jnp.cumsum(padded)
    offs = ends - padded
    g_start = offs[None, :] + base
    l_start = jnp.cumsum(n_run, axis=1) - n_run
    n_run_pad = _ceil_to(n_run, RUN_ROWS)
    l_start_pad = jnp.cumsum(n_run_pad, axis=1) - n_run_pad
    tile_blocks = jnp.sum(n_run_pad // RUN_ROWS, axis=1)
    z_start = offs + counts
    z_blocks = (padded - counts + RUN_ROWS - 1) // RUN_ROWS
    flat = lambda a: a.reshape(-1).astype(i32)
    lspv = jnp.zeros((tab.shape[0], 1, LANES), F32).at[:, 0, :N_EXPERTS].set(l_start_pad.astype(F32))

    n_tiles = (2 * n_tok + N_EXPERTS * (RUN_ROWS - 1)) // MOE_TILE + N_EXPERTS + 1
    n_used = (ends[-1] // MOE_TILE).astype(i32)
    tile_id = jnp.minimum(jnp.arange(n_tiles, dtype=i32), n_used - 1)
    tile_expert = jnp.sum((tile_id[:, None] * MOE_TILE) >= ends[None, :], axis=1).astype(i32)
    tile_expert = jnp.minimum(tile_expert, N_EXPERTS - 1)

    xs = _dispatch_call((flat(n_run), flat(g_start), flat(l_start), flat(tile_blocks),
                         flat(z_start), flat(z_blocks), n_used.reshape(1)), hn_p, hn_s, rt, n_tiles * MOE_TILE)
    ys = _moe_call(tile_expert, n_used.reshape(1), xs, n_tiles, w_eg[0], w_eu[0], w_ed[0])
    y_p, y_s = _final_call((flat(n_run), flat(g_start), flat(l_start_pad), flat(tile_blocks)),
                           h_p, h_s, rt, lspv, p_prompt[0].reshape(n_p, PLE_DIM),
                           p_sample[0].reshape(n_s, PLE_DIM), ys,
                           w_ple[0].astype(BF16), w_ple_gate[0].astype(BF16), row(g_ple[0]), row(g_final))

    return (y_p.reshape(batch, seq, D_MODEL),
            y_s.reshape(dec_batch, dec_seq, D_MODEL),
            hist_p[None],
            hist_s[None],
            v_s.reshape(1, dec_batch, dec_seq, GMLP_WIDTH))
```

```python
import functools

import jax
import jax.numpy as jnp
from jax import lax
from jax.experimental import pallas as pl
from jax.experimental.pallas import tpu as pltpu

D_MODEL = 1024
POOL_WIDTH = 512
GMLP_WIDTH = 512
IN_WIDTH = POOL_WIDTH + 2 * GMLP_WIDTH
POOL_WINDOWS = (2, 4, 8, 16)
POOL_GROUP = 128
POOL_HIST = 15
CHUNK = 128
N_GMLP_HEADS = 4
GMLP_HEAD = 128
N_EXPERT_GROUPS = 4
EXPERTS_PER_GROUP = 8
N_EXPERTS = 32
D_EXPERT = 512
PLE_DIM = 256
EPS = 1e-6

LANES = 128
SUBLANES = 8
HALF = D_MODEL // 2
ROW_CHUNKS = HALF // LANES

TOK_TILE = 256
MOE_TILE = 256
RUN_ALIGN = 2
COPY_ROWS = (8, 4, 2)
LOC_ROWS = 2 * TOK_TILE + N_EXPERTS * (RUN_ALIGN - 1)
COPY_CAP = (LOC_ROWS // 8 + 1, N_EXPERTS, N_EXPERTS)
VMEM_LIMIT = 56 * 1024 * 1024

F32 = jnp.float32
BF16 = jnp.bfloat16
I32 = jnp.int32
U32 = jnp.uint32


def _rms(x, g):
    return x * lax.rsqrt(jnp.mean(x * x, axis=-1, keepdims=True) + EPS) * g


def _gelu_tanh(x):
    return x * (0.5 * (1.0 + jnp.tanh(0.7978845608028654 * (x + 0.044715 * (x * x * x)))))


def _store_packed(ref, val, n_rows):
    packed = pltpu.pack_elementwise([val[:, :HALF], val[:, HALF:]], packed_dtype=BF16)
    for k in range(ROW_CHUNKS):
        ref[pl.ds(k, n_rows, stride=ROW_CHUNKS), :] = packed[:, k * LANES:(k + 1) * LANES]


def _load_packed(ref, n_rows):
    packed = jnp.concatenate(
        [ref[pl.ds(k, n_rows, stride=ROW_CHUNKS), :] for k in range(ROW_CHUNKS)], axis=-1)
    lo = pltpu.unpack_elementwise(packed, index=0, packed_dtype=BF16, unpacked_dtype=F32)
    hi = pltpu.unpack_elementwise(packed, index=1, packed_dtype=BF16, unpacked_dtype=F32)
    return lo, hi


def _rows(start, n):
    return pl.ds(pl.multiple_of(start * ROW_CHUNKS, SUBLANES), n * ROW_CHUNKS)


def _route(hn, wr_hl_ref, wr_hi_ref, br_ref, n_tok):
    hn_hi = hn.astype(BF16)
    hn_lo = (hn - hn_hi.astype(F32)).astype(BF16)
    l1 = jnp.dot(hn_hi, wr_hl_ref[...], preferred_element_type=F32)
    l2 = jnp.dot(hn_lo, wr_hi_ref[...], preferred_element_type=F32)
    logits = l1[:, :LANES] + (l1[:, LANES:] + l2) + br_ref[...]

    lane = lax.broadcasted_iota(I32, (n_tok, LANES), 1).astype(F32)
    neg = jnp.float32(-jnp.inf)
    big = jnp.float32(1e9)
    gmask = (lane >= N_EXPERTS) & (lane < N_EXPERTS + N_EXPERT_GROUPS)
    gl = jnp.where(gmask, logits, neg)
    gmax = jnp.max(gl, axis=-1, keepdims=True)
    gidx = jnp.min(jnp.where(gl == gmax, lane, big), axis=-1, keepdims=True) - N_EXPERTS
    gden = jnp.sum(jnp.exp(gl - gmax), axis=-1, keepdims=True)
    gprob = 1.0 / gden

    lo = gidx * EXPERTS_PER_GROUP
    emask = (lane >= lo) & (lane < lo + EXPERTS_PER_GROUP)
    el = jnp.where(emask, logits, neg)
    m1 = jnp.max(el, axis=-1, keepdims=True)
    i1 = jnp.min(jnp.where(el == m1, lane, big), axis=-1, keepdims=True)
    el2 = jnp.where(lane == i1, neg, el)
    m2 = jnp.max(el2, axis=-1, keepdims=True)
    i2 = jnp.min(jnp.where(el2 == m2, lane, big), axis=-1, keepdims=True)
    t = jnp.exp(m2 - m1)
    w1 = gprob / (1.0 + t)
    w2 = gprob * t / (1.0 + t)

    sel1 = lane == i1
    sel2 = lane == i2
    oh = jnp.where(sel1 | sel2, 1.0, 0.0).astype(BF16)
    r = lax.broadcasted_iota(I32, (n_tok, n_tok), 0)
    c = lax.broadcasted_iota(I32, (n_tok, n_tok), 1)
    tri = jnp.where(c < r, 1.0, 0.0).astype(BF16)
    before = jnp.dot(tri, oh, preferred_element_type=F32)
    counts = jnp.sum(oh.astype(F32), axis=0, keepdims=True)
    lrank1 = jnp.sum(jnp.where(sel1, before, 0.0), axis=-1, keepdims=True)
    lrank2 = jnp.sum(jnp.where(sel2, before, 0.0), axis=-1, keepdims=True)

    out = jnp.where(lane == 0, i1, 0.0)
    for k, val in enumerate((i2, lrank1, lrank2, w1, w2), start=1):
        out = jnp.where(lane == k, val, out)
    return out, counts


def _mixer_kernel(*refs, n_tok, sample, pos0):
    if sample:
        (x_ref, hist_ref, gmix_ref, win_ref, wpool_ref, pscale_ref, gv_ref, bv_ref, wsp_ref,
         bsp_ref, wout_ref, gffn_ref, wr_hl_ref, wr_hi_ref, br_ref,
         h_ref, hn_ref, rt_ref, cnt_ref, hist_out_ref, v_ref, ext_scr) = refs
    else:
        (x_ref, gmix_ref, win_ref, wpool_ref, pscale_ref, gv_ref, bv_ref, wsp_ref,
         bsp_ref, wout_ref, gffn_ref, wr_hl_ref, wr_hi_ref, br_ref,
         h_ref, hn_ref, rt_ref, cnt_ref, hist_out_ref, ext_scr) = refs

    x = x_ref[...]
    xn = _rms(x, gmix_ref[...])
    z = jnp.dot(xn.astype(BF16), win_ref[...], preferred_element_type=F32)
    a = z[:, :POOL_WIDTH]
    uv = _gelu_tanh(z[:, POOL_WIDTH:])
    u = uv[:, :GMLP_WIDTH]
    v = uv[:, GMLP_WIDTH:]
    mu = jnp.mean(v, axis=-1, keepdims=True)
    vc = v - mu
    v = vc * lax.rsqrt(jnp.mean(vc * vc, axis=-1, keepdims=True) + EPS) * gv_ref[...] + bv_ref[...]

    pooled = []
    if sample:
        n_seq = n_tok // SUBLANES
        ext_scr[:, 1:1 + POOL_HIST, :] = hist_ref[...]
        ext_scr[:, 1 + POOL_HIST:, :] = a.reshape(n_seq, SUBLANES, POOL_WIDTH)
        hist_out_ref[...] = ext_scr[:, 1 + SUBLANES:, :]
        pos = pos0 + lax.broadcasted_iota(I32, (n_seq, SUBLANES, 1), 1)
        for gi, w in enumerate(POOL_WINDOWS):
            cols = slice(gi * POOL_GROUP, (gi + 1) * POOL_GROUP)
            win = ext_scr[:, 16:16 + SUBLANES, cols]
            for j in range(1, w):
                win = win + ext_scr[:, 16 - j:16 - j + SUBLANES, cols]
            cnt = jnp.minimum(w, pos + 1).astype(F32)
            pooled.append((win / cnt).reshape(n_tok, POOL_GROUP) - a[:, cols])
    else:
        l = pl.program_id(1)

        @pl.when(l == 0)
        def _():
            ext_scr[0:16, :] = jnp.zeros((16, POOL_WIDTH), F32)

        ext_scr[16:16 + n_tok, :] = a
        pos = pos0 + l * n_tok + lax.broadcasted_iota(I32, (n_tok, 1), 0)
        for gi, w in enumerate(POOL_WINDOWS):
            cols = slice(gi * POOL_GROUP, (gi + 1) * POOL_GROUP)
            win = a[:, cols]
            for j in range(1, w):
                win = win + ext_scr[16 - j:16 - j + n_tok, cols]
            cnt = jnp.minimum(w, pos + 1).astype(F32)
            pooled.append(win / cnt - a[:, cols])
        tail = ext_scr[n_tok + 1:n_tok + 16, :]
        hist_out_ref[...] = tail
        ext_scr[1:16, :] = tail

    pool_out = jnp.concatenate(
        [jnp.dot(pooled[gi].astype(BF16), wpool_ref[gi], preferred_element_type=F32)
         for gi in range(len(POOL_WINDOWS))], axis=-1) * pscale_ref[...]

    blk = SUBLANES if sample else CHUNK
    tr = lax.broadcasted_iota(I32, (CHUNK, CHUNK), 0)
    sc = lax.broadcasted_iota(I32, (CHUNK, CHUNK), 1)
    mask = (sc <= tr) & ((sc // blk) == (tr // blk))
    wsp = [jnp.where(mask, wsp_ref[hh], jnp.zeros((), BF16)) for hh in range(N_GMLP_HEADS)]
    v_bf = v.astype(BF16)
    mixed_chunks = []
    for ci in range(n_tok // CHUNK):
        rows = slice(ci * CHUNK, (ci + 1) * CHUNK)
        heads = [jnp.dot(wsp[hh], v_bf[rows, hh * GMLP_HEAD:(hh + 1) * GMLP_HEAD],
                         preferred_element_type=F32) for hh in range(N_GMLP_HEADS)]
        mixed_chunks.append(jnp.concatenate(heads, axis=-1) + bsp_ref[...])
    mixed = jnp.concatenate(mixed_chunks, axis=0)
    gmlp_out = u * mixed

    cat = jnp.concatenate([pool_out, gmlp_out], axis=-1)
    h = x + jnp.dot(cat.astype(BF16), wout_ref[...], preferred_element_type=F32)
    hn = _rms(h, gffn_ref[...])

    h_ref[...] = h
    hn_ref[...] = hn.astype(BF16)
    if sample:
        v_ref[...] = v
    rt, counts = _route(hn, wr_hl_ref, wr_hi_ref, br_ref, n_tok)
    rt_ref[...] = rt[:, :SUBLANES]
    cnt_ref[...] = counts


def _full(shape):
    return pl.BlockSpec(shape, lambda *_: (0,) * len(shape))


def _mixer_call(x, hist, weights, *, sample, pos0):
    w_specs = [_full(w.shape) for w in weights]
    n_tok = TOK_TILE
    if sample:
        n_rows = x.shape[0]
        n_seq = n_tok // SUBLANES
        grid = (n_rows // n_tok,)
        tok_map = lambda i: (i, 0)
        tile_map = lambda i: (i, 0, 0)
        in_specs = [pl.BlockSpec((n_tok, D_MODEL), tok_map),
                    pl.BlockSpec((n_seq, POOL_HIST, POOL_WIDTH), lambda i: (i, 0, 0))] + w_specs
        hist_shape = jax.ShapeDtypeStruct(hist.shape, F32)
        hist_spec = pl.BlockSpec((n_seq, POOL_HIST, POOL_WIDTH), lambda i: (i, 0, 0))
        ext = pltpu.VMEM((n_seq, 16 + SUBLANES, POOL_WIDTH), F32)
        args = (x, hist)
    else:
        b, seq, _ = x.shape
        n_rows = b * seq
        n_l = seq // n_tok
        grid = (b, n_l)
        tok_map = lambda bi, li: (bi * n_l + li, 0)
        tile_map = lambda bi, li: (bi * n_l + li, 0, 0)
        in_specs = [pl.BlockSpec((None, n_tok, D_MODEL), lambda bi, li: (bi, li, 0))] + w_specs
        hist_shape = jax.ShapeDtypeStruct((b, POOL_HIST, POOL_WIDTH), F32)
        hist_spec = pl.BlockSpec((None, POOL_HIST, POOL_WIDTH), lambda bi, li: (bi, 0, 0))
        ext = pltpu.VMEM((16 + n_tok, POOL_WIDTH), F32)
        args = (x,)
    out_shape = [jax.ShapeDtypeStruct((n_rows, D_MODEL), F32),
                 jax.ShapeDtypeStruct((n_rows, D_MODEL), BF16),
                 jax.ShapeDtypeStruct((n_rows, SUBLANES), F32),
                 jax.ShapeDtypeStruct((n_rows // n_tok, 1, LANES), F32),
                 hist_shape]
    out_specs = [pl.BlockSpec((n_tok, D_MODEL), tok_map),
                 pl.BlockSpec((n_tok, D_MODEL), tok_map),
                 pl.BlockSpec((n_tok, SUBLANES), tok_map),
                 pl.BlockSpec((None, 1, LANES), tile_map),
                 hist_spec]
    if sample:
        out_shape.append(jax.ShapeDtypeStruct((n_rows, GMLP_WIDTH), F32))
        out_specs.append(pl.BlockSpec((n_tok, GMLP_WIDTH), tok_map))
    return pl.pallas_call(
        functools.partial(_mixer_kernel, n_tok=n_tok, sample=sample, pos0=pos0),
        grid=grid, in_specs=in_specs, out_specs=out_specs, out_shape=out_shape,
        scratch_shapes=[ext],
        compiler_params=pltpu.CompilerParams(
            dimension_semantics=("arbitrary",) * len(grid), vmem_limit_bytes=VMEM_LIMIT),
        name="mixer_sample" if sample else "mixer_prompt",
    )(*args, *weights)


def _for_each_copy(tile, cnt_ref, fn):
    for s, n_rows in enumerate(COPY_ROWS):
        def body(j, carry, s=s, n_rows=n_rows):
            fn(s, n_rows, tile * COPY_CAP[s] + j)
            return carry
        lax.fori_loop(0, cnt_ref[tile * len(COPY_ROWS) + s], body, 0)


def _wait_rows(n_rows_total, make_wait):
    units = n_rows_total // RUN_ALIGN
    bit = 0
    while (RUN_ALIGN << bit) <= LOC_ROWS:
        @pl.when((units >> bit) & 1 == 1)
        def _(bit=bit):
            make_wait(RUN_ALIGN << bit).wait()
        bit += 1


def _local_positions(rt, lstart_row):
    elane = lax.broadcasted_iota(I32, (rt.shape[0], LANES), 1).astype(F32)
    p1 = jnp.sum(jnp.where(elane == rt[:, 0:1], lstart_row, 0.0), axis=-1, keepdims=True) + rt[:, 2:3]
    p2 = jnp.sum(jnp.where(elane == rt[:, 1:2], lstart_row, 0.0), axis=-1, keepdims=True) + rt[:, 3:4]
    return p1, p2


def _dispatch_kernel(l8, g8, l4, g4, l2, g2, cnt_ref, trow_ref, zs_ref, zb_ref, nu_ref,
                     hnp_ref, hns_ref, rtp_ref, rts_ref, lst_ref, xs_ref, loc, zbuf, sem, zsem,
                     *, n_prompt_steps):
    i = pl.program_id(0)
    n_steps = pl.num_programs(0)
    slot = i % 2
    local_tabs, sorted_tabs = (l8, l4, l2), (g8, g4, g2)

    @pl.when(i == 0)
    def _():
        zbuf[...] = jnp.zeros(zbuf.shape, U32)
        n_tiles = xs_ref.shape[0] // (MOE_TILE * ROW_CHUNKS)

        def zero_copy(e, c):
            return pltpu.make_async_copy(zbuf.at[pl.ds(0, 8 * ROW_CHUNKS)],
                                         xs_ref.at[_rows(zs_ref[e] + c * 8, 8)], zsem)

        def tail_copy(t):
            return pltpu.make_async_copy(zbuf, xs_ref.at[_rows(t * MOE_TILE, MOE_TILE)], zsem)

        def per_expert(e, total):
            def per_block(c, carry):
                zero_copy(e, c).start()
                return carry
            lax.fori_loop(0, zb_ref[e], per_block, 0)
            return total + zb_ref[e]

        total = lax.fori_loop(0, N_EXPERTS, per_expert, 0)

        def wait_body(_, carry):
            zero_copy(0, 0).wait()
            return carry
        lax.fori_loop(0, total, wait_body, 0)

        def tail_start(t, carry):
            tail_copy(t).start()
            return carry

        def tail_wait(t, carry):
            tail_copy(t).wait()
            return carry

        spill = pltpu.make_async_copy(zbuf.at[pl.ds(0, 8 * ROW_CHUNKS)],
                                      xs_ref.at[_rows(n_tiles * MOE_TILE, 8)], zsem)
        lax.fori_loop(nu_ref[0], n_tiles, tail_start, 0)
        spill.start()
        lax.fori_loop(nu_ref[0], n_tiles, tail_wait, 0)
        spill.wait()

    is_prompt = i < n_prompt_steps
    rt = jnp.where(is_prompt, rtp_ref[...], rts_ref[...])
    lpos1, lpos2 = _local_positions(rt, lst_ref[...])
    lane = lax.broadcasted_iota(I32, (TOK_TILE, LOC_ROWS), 1).astype(F32)
    perm_t = jnp.where((lane == lpos1) | (lane == lpos2), 1.0, 0.0).astype(BF16)
    hn = jnp.where(is_prompt, hnp_ref[...], hns_ref[...])
    grouped = lax.dot_general(perm_t, hn, (((0,), (0,)), ((), ())), preferred_element_type=F32)
    _store_packed(loc.at[slot], grouped, LOC_ROWS)

    def run_copy(buf_slot, s, n_rows, k):
        src = loc.at[buf_slot, _rows(local_tabs[s][k], n_rows)]
        dst = xs_ref.at[_rows(sorted_tabs[s][k], n_rows)]
        return pltpu.make_async_copy(src, dst, sem.at[buf_slot])

    _for_each_copy(i, cnt_ref, lambda s, n_rows, k: run_copy(slot, s, n_rows, k).start())

    def drain(tile, buf_slot):
        def make_wait(n):
            return pltpu.make_async_copy(loc.at[buf_slot, pl.ds(0, n * ROW_CHUNKS)],
                                         xs_ref.at[pl.ds(0, n * ROW_CHUNKS)], sem.at[buf_slot])
        _wait_rows(trow_ref[tile], make_wait)

    @pl.when(i > 0)
    def _():
        drain(i - 1, 1 - slot)

    @pl.when(i == n_steps - 1)
    def _():
        drain(i, slot)


def _clamped_maps(np_steps):
    pmap = lambda i, *_: (jnp.minimum(i, np_steps - 1), 0)
    smap = lambda i, *_: (jnp.maximum(i - np_steps, 0), 0)
    return pmap, smap


def _dispatch_call(tables, hn_p, hn_s, rt_p, rt_s, lstv, n_sorted_rows):
    n_p, n_s = hn_p.shape[0], hn_s.shape[0]
    np_steps = n_p // TOK_TILE
    n_steps = np_steps + n_s // TOK_TILE
    pmap, smap = _clamped_maps(np_steps)
    out_rows = (n_sorted_rows + 8) * ROW_CHUNKS
    return pl.pallas_call(
        functools.partial(_dispatch_kernel, n_prompt_steps=np_steps),
        grid_spec=pltpu.PrefetchScalarGridSpec(
            num_scalar_prefetch=len(tables), grid=(n_steps,),
            in_specs=[pl.BlockSpec((TOK_TILE, D_MODEL), pmap),
                      pl.BlockSpec((TOK_TILE, D_MODEL), smap),
                      pl.BlockSpec((TOK_TILE, SUBLANES), pmap),
                      pl.BlockSpec((TOK_TILE, SUBLANES), smap),
                      pl.BlockSpec((None, 1, LANES), lambda i, *_: (i, 0, 0))],
            out_specs=pl.BlockSpec(memory_space=pl.ANY),
            scratch_shapes=[pltpu.VMEM((2, LOC_ROWS * ROW_CHUNKS, LANES), U32),
                            pltpu.VMEM((MOE_TILE * ROW_CHUNKS, LANES), U32),
                            pltpu.SemaphoreType.DMA((2,)),
                            pltpu.SemaphoreType.DMA(())]),
        out_shape=jax.ShapeDtypeStruct((out_rows, LANES), U32),
        compiler_params=pltpu.CompilerParams(
            dimension_semantics=("arbitrary",), vmem_limit_bytes=VMEM_LIMIT),
        name="dispatch",
    )(*tables, hn_p, hn_s, rt_p, rt_s, lstv)


def _moe_kernel(te_ref, nu_ref, xs_ref, wg_ref, wu_ref, wd_ref, ys_ref, wg_bf, wu_bf, wd_bf):
    i = pl.program_id(0)
    used = i < nu_ref[0]

    @pl.when(used)
    def _():
        prev = te_ref[jnp.maximum(i - 1, 0)]

        @pl.when((i == 0) | (te_ref[i] != prev))
        def _():
            wg_bf[...] = wg_ref[...].astype(BF16)
            wu_bf[...] = wu_ref[...].astype(BF16)
            wd_bf[...] = wd_ref[...].astype(BF16)

        lo, hi = _load_packed(xs_ref, MOE_TILE)
        x = jnp.concatenate([lo, hi], axis=-1).astype(BF16)
        g = jnp.dot(x, wg_bf[...], preferred_element_type=F32)
        u = jnp.dot(x, wu_bf[...], preferred_element_type=F32)
        hh = (g * jax.nn.sigmoid(g)) * u
        y = jnp.dot(hh.astype(BF16), wd_bf[...], preferred_element_type=F32)
        _store_packed(ys_ref, y, MOE_TILE)

    @pl.when(jnp.logical_not(used))
    def _():
        ij = (lax.broadcasted_iota(I32, (MOE_TILE, D_MODEL), 0)
              + lax.broadcasted_iota(I32, (MOE_TILE, D_MODEL), 1))
        zeros = jnp.where(ij < jnp.minimum(nu_ref[0], 0), 1.0, 0.0)
        _store_packed(ys_ref, zeros, MOE_TILE)


def _moe_call(tile_expert, n_used, xs, n_tiles, w_eg, w_eu, w_ed):
    blk = MOE_TILE * ROW_CHUNKS
    return pl.pallas_call(
        _moe_kernel,
        grid_spec=pltpu.PrefetchScalarGridSpec(
            num_scalar_prefetch=2, grid=(n_tiles,),
            in_specs=[
                pl.BlockSpec((blk, LANES), lambda i, te, nu: (jnp.minimum(i, nu[0] - 1), 0)),
                pl.BlockSpec((None, D_MODEL, D_EXPERT), lambda i, te, nu: (te[i], 0, 0)),
                pl.BlockSpec((None, D_MODEL, D_EXPERT), lambda i, te, nu: (te[i], 0, 0)),
                pl.BlockSpec((None, D_EXPERT, D_MODEL), lambda i, te, nu: (te[i], 0, 0)),
            ],
            out_specs=pl.BlockSpec((blk, LANES), lambda i, te, nu: (i, 0)),
            scratch_shapes=[pltpu.VMEM((D_MODEL, D_EXPERT), BF16),
                            pltpu.VMEM((D_MODEL, D_EXPERT), BF16),
                            pltpu.VMEM((D_EXPERT, D_MODEL), BF16)]),
        out_shape=jax.ShapeDtypeStruct((n_tiles * blk, LANES), U32),
        compiler_params=pltpu.CompilerParams(
            dimension_semantics=("arbitrary",), vmem_limit_bytes=VMEM_LIMIT),
        name="experts",
    )(tile_expert, n_used, xs, w_eg, w_eu, w_ed)


def _final_kernel(l8, g8, l4, g4, l2, g2, cnt_ref, trow_ref,
                  hp_ref, hs_ref, rtp_ref, rts_ref, lst_ref, pp_ref, ps_ref,
                  ys_ref, wple_ref, wgate_ref, gple_ref, gfin_ref, outp_ref, outs_ref, ybuf, sem,
                  *, n_prompt_steps):
    i = pl.program_id(0)
    n_steps = pl.num_programs(0)
    slot = i % 2
    local_tabs, sorted_tabs = (l8, l4, l2), (g8, g4, g2)

    def run_copy(buf_slot, s, n_rows, k):
        src = ys_ref.at[_rows(sorted_tabs[s][k], n_rows)]
        dst = ybuf.at[buf_slot, _rows(local_tabs[s][k], n_rows)]
        return pltpu.make_async_copy(src, dst, sem.at[buf_slot])

    def issue(tile, buf_slot):
        _for_each_copy(tile, cnt_ref, lambda s, n_rows, k: run_copy(buf_slot, s, n_rows, k).start())

    @pl.when(i == 0)
    def _():
        ybuf[...] = jnp.zeros(ybuf.shape, U32)
        issue(0, 0)

    @pl.when(i + 1 < n_steps)
    def _():
        issue(i + 1, 1 - slot)

    def make_wait(n):
        return pltpu.make_async_copy(ys_ref.at[pl.ds(0, n * ROW_CHUNKS)],
                                     ybuf.at[slot, pl.ds(0, n * ROW_CHUNKS)], sem.at[slot])
    _wait_rows(trow_ref[i], make_wait)

    is_prompt = i < n_prompt_steps
    h = jnp.where(is_prompt, hp_ref[...], hs_ref[...])
    p = jnp.where(is_prompt, pp_ref[...], ps_ref[...])
    rt = jnp.where(is_prompt, rtp_ref[...], rts_ref[...])
    lpos1, lpos2 = _local_positions(rt, lst_ref[...])
    lane = lax.broadcasted_iota(I32, (TOK_TILE, LOC_ROWS), 1).astype(F32)
    sel1 = lane == lpos1
    sel2 = lane == lpos2
    wlane = lax.broadcasted_iota(I32, (TOK_TILE, LANES), 1)

    def hi_lo_cols(w):
        w_hi = w.astype(BF16).astype(F32)
        return jnp.where(wlane == 0, w_hi, jnp.where(wlane == 1, w - w_hi, 0.0)).astype(BF16)

    dn = (((0,), (0,)), ((), ()))
    wrow = (lax.dot_general(jnp.where(sel1, 1.0, 0.0).astype(BF16), hi_lo_cols(rt[:, 4:5]), dn,
                            preferred_element_type=F32)
            + lax.dot_general(jnp.where(sel2, 1.0, 0.0).astype(BF16), hi_lo_cols(rt[:, 5:6]), dn,
                              preferred_element_type=F32))
    wrow = wrow[:, 0:1] + wrow[:, 1:2]
    lo, hi = _load_packed(ybuf.at[slot], LOC_ROWS)
    yb = (jnp.concatenate([lo, hi], axis=-1) * wrow).astype(BF16)
    perm = jnp.where(sel1 | sel2, 1.0, 0.0).astype(BF16)
    h = h + jnp.dot(perm, yb, preferred_element_type=F32)
    r = _rms(h, gple_ref[...])
    gate = jax.nn.sigmoid(jnp.dot(r.astype(BF16), wgate_ref[...], preferred_element_type=F32))
    h = h + jnp.dot(p.astype(BF16), wple_ref[...], preferred_element_type=F32) * gate
    y = _rms(h, gfin_ref[...])

    @pl.when(is_prompt)
    def _():
        outp_ref[...] = y

    @pl.when(jnp.logical_not(is_prompt))
    def _():
        outs_ref[...] = y


def _final_call(tables, h_p, h_s, rt_p, rt_s, lstv, p_p, p_s, ys, wple, wgate, gple, gfin):
    n_p, n_s = h_p.shape[0], h_s.shape[0]
    np_steps = n_p // TOK_TILE
    n_steps = np_steps + n_s // TOK_TILE
    pmap, smap = _clamped_maps(np_steps)
    cmap = lambda i, *_: (0, 0)
    return pl.pallas_call(
        functools.partial(_final_kernel, n_prompt_steps=np_steps),
        grid_spec=pltpu.PrefetchScalarGridSpec(
            num_scalar_prefetch=len(tables), grid=(n_steps,),
            in_specs=[
                pl.BlockSpec((TOK_TILE, D_MODEL), pmap),
                pl.BlockSpec((TOK_TILE, D_MODEL), smap),
                pl.BlockSpec((TOK_TILE, SUBLANES), pmap),
                pl.BlockSpec((TOK_TILE, SUBLANES), smap),
                pl.BlockSpec((None, 1, LANES), lambda i, *_: (i, 0, 0)),
                pl.BlockSpec((TOK_TILE, PLE_DIM), pmap),
                pl.BlockSpec((TOK_TILE, PLE_DIM), smap),
                pl.BlockSpec(memory_space=pl.ANY),
                pl.BlockSpec(wple.shape, cmap),
                pl.BlockSpec(wgate.shape, cmap),
                pl.BlockSpec(gple.shape, cmap),
                pl.BlockSpec(gfin.shape, cmap),
            ],
            out_specs=[pl.BlockSpec((TOK_TILE, D_MODEL), pmap),
                       pl.BlockSpec((TOK_TILE, D_MODEL), smap)],
            scratch_shapes=[pltpu.VMEM((2, LOC_ROWS * ROW_CHUNKS, LANES), U32),
                            pltpu.SemaphoreType.DMA((2,))]),
        out_shape=[jax.ShapeDtypeStruct((n_p, D_MODEL), F32),
                   jax.ShapeDtypeStruct((n_s, D_MODEL), F32)],
        compiler_params=pltpu.CompilerParams(
            dimension_semantics=("arbitrary",), vmem_limit_bytes=VMEM_LIMIT),
        name="final",
    )(*tables, h_p, h_s, rt_p, rt_s, lstv, p_p, p_s, ys, wple, wgate, gple, gfin)


def _ceil_to(x, m):
    return ((x + m - 1) // m) * m


def _excl_cumsum(a, axis):
    return jnp.cumsum(a, axis=axis) - a


def _copy_tables(n_copies, first_local, first_sorted, step, cap):
    ends = jnp.cumsum(n_copies, axis=1)
    j = jnp.arange(cap, dtype=I32)[None, :, None]
    e_of = jnp.minimum(jnp.sum(ends[:, None, :] <= j, axis=-1), N_EXPERTS - 1)
    take = lambda a: jnp.take_along_axis(a, e_of, axis=1)
    c = jnp.arange(cap, dtype=I32)[None, :] - (take(ends) - take(n_copies))
    flat = lambda a: a.reshape(-1).astype(I32)
    return flat(take(first_local) + step * c), flat(take(first_sorted) + step * c), ends[:, -1]


def kernel(x_prompt, x_sample, state_pool, p_prompt, p_sample, g_mix, w_in, w_pool, pool_scale,
           g_v, b_v, w_spatial, b_spatial, w_out, g_ffn, w_rg, b_rg, w_re, b_re, w_eg, w_eu, w_ed,
           w_ple, g_ple, w_ple_gate, g_final):
    batch, seq, _ = x_prompt.shape
    dec_batch, dec_seq, _ = x_sample.shape
    assert dec_seq == SUBLANES and g_mix.shape[0] == 1
    n_p = batch * seq
    n_s = dec_batch * dec_seq
    n_tok = n_p + n_s
    past_len = 16384

    row = lambda a: a.reshape(1, -1).astype(F32)
    wr = jnp.concatenate(
        [w_re[0], w_rg[0], jnp.zeros((D_MODEL, LANES - N_EXPERTS - N_EXPERT_GROUPS), F32)], axis=1)
    wr_hi = wr.astype(BF16)
    wr_lo = (wr - wr_hi.astype(F32)).astype(BF16)
    br = jnp.concatenate(
        [b_re[0], b_rg[0], jnp.zeros((LANES - N_EXPERTS - N_EXPERT_GROUPS,), F32)]).reshape(1, LANES)
    bsp_p = jnp.repeat(jnp.transpose(b_spatial[0]), GMLP_HEAD, axis=1)
    bsp_s = jnp.tile(bsp_p[:dec_seq], (CHUNK // dec_seq, 1))
    wsp_p = w_spatial[0].astype(BF16)
    wsp_s = jnp.tile(w_spatial[0][:, :dec_seq, :dec_seq].astype(BF16),
                     (1, CHUNK // dec_seq, CHUNK // dec_seq))

    def mixer_weights(wsp, bsp):
        return (row(g_mix[0]), w_in[0].astype(BF16), w_pool[0].astype(BF16), row(pool_scale[0]),
                row(g_v[0]), row(b_v[0]), wsp, bsp, w_out[0].astype(BF16), row(g_ffn[0]),
                jnp.concatenate([wr_hi, wr_lo], axis=1), wr_hi, br)

    h_p, hn_p, rt_p, cnt_p, hist_p = _mixer_call(
        x_prompt, None, mixer_weights(wsp_p, bsp_p), sample=False, pos0=0)
    h_s, hn_s, rt_s, cnt_s, hist_s, v_s = _mixer_call(
        x_sample.reshape(n_s, D_MODEL), state_pool[0], mixer_weights(wsp_s, bsp_s),
        sample=True, pos0=past_len)

    n_run = jnp.concatenate([cnt_p, cnt_s], axis=0)[:, 0, :N_EXPERTS].astype(I32)
    n_even = _ceil_to(n_run, RUN_ALIGN)
    l_start = _excl_cumsum(n_even, 1)
    tile_rows = jnp.sum(n_even, axis=1)
    base = _excl_cumsum(n_even, 0)
    counts = jnp.sum(n_even, axis=0)
    padded = _ceil_to(counts, MOE_TILE)
    ends = jnp.cumsum(padded)
    offs = ends - padded
    g_start = offs[None, :] + base
    n8 = n_even // 8
    f4 = (n_even // 4) % 2
    f2 = (n_even // 2) % 2
    l8, g8, c8 = _copy_tables(n8, l_start, g_start, 8, COPY_CAP[0])
    l4, g4, c4 = _copy_tables(f4, l_start + 8 * n8, g_start + 8 * n8, 0, COPY_CAP[1])
    l2, g2, c2 = _copy_tables(f2, l_start + 8 * n8 + 4 * f4, g_start + 8 * n8 + 4 * f4, 0, COPY_CAP[2])
    copy_cnt = jnp.stack([c8, c4, c2], axis=1).reshape(-1).astype(I32)
    tile_rows = tile_rows.astype(I32)
    z_start = (offs + counts).astype(I32)
    z_blocks = ((padded - counts + 7) // 8).astype(I32)
    lstv = jnp.zeros((n_run.shape[0], 1, LANES), F32).at[:, 0, :N_EXPERTS].set(l_start.astype(F32))

    max_rows = 2 * n_tok + n_run.shape[0] * N_EXPERTS * (RUN_ALIGN - 1)
    n_tiles = max_rows // MOE_TILE + N_EXPERTS + 1
    n_used = (ends[-1] // MOE_TILE).astype(I32).reshape(1)
    tile_id = jnp.minimum(jnp.arange(n_tiles, dtype=I32), n_used - 1)
    tile_expert = jnp.sum((tile_id[:, None] * MOE_TILE) >= ends[None, :], axis=1).astype(I32)
    tile_expert = jnp.minimum(tile_expert, N_EXPERTS - 1)

    copy_tabs = (l8, g8, l4, g4, l2, g2, copy_cnt, tile_rows)
    xs = _dispatch_call(copy_tabs + (z_start, z_blocks, n_used), hn_p, hn_s, rt_p, rt_s, lstv,
                        n_tiles * MOE_TILE)
    ys = _moe_call(tile_expert, n_used, xs, n_tiles, w_eg[0], w_eu[0], w_ed[0])
    y_p, y_s = _final_call(copy_tabs, h_p, h_s, rt_p, rt_s, lstv,
                           p_prompt[0].reshape(n_p, PLE_DIM), p_sample[0].reshape(n_s, PLE_DIM), ys,
                           w_ple[0].astype(BF16), w_ple_gate[0].astype(BF16), row(g_ple[0]), row(g_final))

    return (y_p.reshape(batch, seq, D_MODEL),
            y_s.reshape(dec_batch, dec_seq, D_MODEL),
            hist_p[None],
            hist_s[None],
            v_s.reshape(1, dec_batch, dec_seq, GMLP_WIDTH))
```

```python
import functools

import jax
import jax.numpy as jnp
from jax import lax
from jax.experimental import pallas as pl
from jax.experimental.pallas import tpu as pltpu

D_MODEL = 1024
POOL_WIDTH = 512
GMLP_WIDTH = 512
IN_WIDTH = POOL_WIDTH + 2 * GMLP_WIDTH
POOL_WINDOWS = (2, 4, 8, 16)
POOL_GROUP = 128
POOL_HIST = 15
CHUNK = 128
N_GMLP_HEADS = 4
GMLP_HEAD = 128
N_EXPERT_GROUPS = 4
EXPERTS_PER_GROUP = 8
N_EXPERTS = 32
D_EXPERT = 512
PLE_DIM = 256
EPS = 1e-6

LANES = 128
SUBLANES = 8
HALF = D_MODEL // 2
ROW_CHUNKS = HALF // LANES

TOK_TILE = 256
MOE_TILE = 256
RUN_ALIGN = 2
COPY_ROWS = (8, 4, 2)
LOC_ROWS = 2 * TOK_TILE + N_EXPERTS * (RUN_ALIGN - 1)
COPY_CAP = (LOC_ROWS // 8 + 1, N_EXPERTS, N_EXPERTS)
VMEM_LIMIT = 56 * 1024 * 1024

F32 = jnp.float32
BF16 = jnp.bfloat16
I32 = jnp.int32
U32 = jnp.uint32


def _rms(x, g):
    return x * lax.rsqrt(jnp.mean(x * x, axis=-1, keepdims=True) + EPS) * g


def _gelu_tanh(x):
    return x * (0.5 * (1.0 + jnp.tanh(0.7978845608028654 * (x + 0.044715 * (x * x * x)))))


def _store_packed(ref, val, n_rows):
    packed = pltpu.pack_elementwise([val[:, :HALF], val[:, HALF:]], packed_dtype=BF16)
    for k in range(ROW_CHUNKS):
        ref[pl.ds(k, n_rows, stride=ROW_CHUNKS), :] = packed[:, k * LANES:(k + 1) * LANES]


def _load_packed(ref, n_rows):
    packed = jnp.concatenate(
        [ref[pl.ds(k, n_rows, stride=ROW_CHUNKS), :] for k in range(ROW_CHUNKS)], axis=-1)
    lo = pltpu.unpack_elementwise(packed, index=0, packed_dtype=BF16, unpacked_dtype=F32)
    hi = pltpu.unpack_elementwise(packed, index=1, packed_dtype=BF16, unpacked_dtype=F32)
    return lo, hi


def _rows(start, n):
    return pl.ds(pl.multiple_of(start * ROW_CHUNKS, SUBLANES), n * ROW_CHUNKS)


def _route(hn, wr_hl_ref, wr_hi_ref, br_ref, n_tok):
    hn_hi = hn.astype(BF16)
    hn_lo = (hn - hn_hi.astype(F32)).astype(BF16)
    l1 = jnp.dot(hn_hi, wr_hl_ref[...], preferred_element_type=F32)
    l2 = jnp.dot(hn_lo, wr_hi_ref[...], preferred_element_type=F32)
    logits = l1[:, :LANES] + (l1[:, LANES:] + l2) + br_ref[...]

    lane = lax.broadcasted_iota(I32, (n_tok, LANES), 1).astype(F32)
    neg = jnp.float32(-jnp.inf)
    big = jnp.float32(1e9)
    gmask = (lane >= N_EXPERTS) & (lane < N_EXPERTS + N_EXPERT_GROUPS)
    gl = jnp.where(gmask, logits, neg)
    gmax = jnp.max(gl, axis=-1, keepdims=True)
    gidx = jnp.min(jnp.where(gl == gmax, lane, big), axis=-1, keepdims=True) - N_EXPERTS
    gden = jnp.sum(jnp.exp(gl - gmax), axis=-1, keepdims=True)
    gprob = 1.0 / gden

    lo = gidx * EXPERTS_PER_GROUP
    emask = (lane >= lo) & (lane < lo + EXPERTS_PER_GROUP)
    el = jnp.where(emask, logits, neg)
    m1 = jnp.max(el, axis=-1, keepdims=True)
    i1 = jnp.min(jnp.where(el == m1, lane, big), axis=-1, keepdims=True)
    el2 = jnp.where(lane == i1, neg, el)
    m2 = jnp.max(el2, axis=-1, keepdims=True)
    i2 = jnp.min(jnp.where(el2 == m2, lane, big), axis=-1, keepdims=True)
    t = jnp.exp(m2 - m1)
    w1 = gprob / (1.0 + t)
    w2 = gprob * t / (1.0 + t)

    sel1 = lane == i1
    sel2 = lane == i2
    oh = jnp.where(sel1 | sel2, 1.0, 0.0).astype(BF16)
    r = lax.broadcasted_iota(I32, (n_tok, n_tok), 0)
    c = lax.broadcasted_iota(I32, (n_tok, n_tok), 1)
    tri = jnp.where(c < r, 1.0, 0.0).astype(BF16)
    before = jnp.dot(tri, oh, preferred_element_type=F32)
    counts = jnp.sum(oh.astype(F32), axis=0, keepdims=True)
    lrank1 = jnp.sum(jnp.where(sel1, before, 0.0), axis=-1, keepdims=True)
    lrank2 = jnp.sum(jnp.where(sel2, before, 0.0), axis=-1, keepdims=True)

    out = jnp.where(lane == 0, i1, 0.0)
    for k, val in enumerate((i2, lrank1, lrank2, w1, w2), start=1):
        out = jnp.where(lane == k, val, out)
    return out, counts


def _mixer_kernel(*refs, n_tok, sample, pos0):
    if sample:
        (x_ref, hist_ref, gmix_ref, win_ref, wpool_ref, pscale_ref, gv_ref, bv_ref, wsp_ref,
         bsp_ref, wout_ref, gffn_ref, wr_hl_ref, wr_hi_ref, br_ref,
         h_ref, hn_ref, rt_ref, cnt_ref, hist_out_ref, v_ref, ext_scr) = refs
    else:
        (x_ref, gmix_ref, win_ref, wpool_ref, pscale_ref, gv_ref, bv_ref, wsp_ref,
         bsp_ref, wout_ref, gffn_ref, wr_hl_ref, wr_hi_ref, br_ref,
         h_ref, hn_ref, rt_ref, cnt_ref, hist_out_ref, ext_scr) = refs

    x = x_ref[...]
    xn = _rms(x, gmix_ref[...])
    z = jnp.dot(xn.astype(BF16), win_ref[...], preferred_element_type=F32)
    a = z[:, :POOL_WIDTH]
    uv = _gelu_tanh(z[:, POOL_WIDTH:])
    u = uv[:, :GMLP_WIDTH]
    v = uv[:, GMLP_WIDTH:]
    mu = jnp.mean(v, axis=-1, keepdims=True)
    vc = v - mu
    v = vc * lax.rsqrt(jnp.mean(vc * vc, axis=-1, keepdims=True) + EPS) * gv_ref[...] + bv_ref[...]

    pooled = []
    if sample:
        n_seq = n_tok // SUBLANES
        ext_scr[:, 1:1 + POOL_HIST, :] = hist_ref[...]
        ext_scr[:, 1 + POOL_HIST:, :] = a.reshape(n_seq, SUBLANES, POOL_WIDTH)
        hist_out_ref[...] = ext_scr[:, 1 + SUBLANES:, :]
        pos = pos0 + lax.broadcasted_iota(I32, (n_seq, SUBLANES, 1), 1)
        for gi, w in enumerate(POOL_WINDOWS):
            cols = slice(gi * POOL_GROUP, (gi + 1) * POOL_GROUP)
            win = ext_scr[:, 16:16 + SUBLANES, cols]
            for j in range(1, w):
                win = win + ext_scr[:, 16 - j:16 - j + SUBLANES, cols]
            cnt = jnp.minimum(w, pos + 1).astype(F32)
            pooled.append((win / cnt).reshape(n_tok, POOL_GROUP) - a[:, cols])
    else:
        l = pl.program_id(1)

        @pl.when(l == 0)
        def _():
            ext_scr[0:16, :] = jnp.zeros((16, POOL_WIDTH), F32)

        ext_scr[16:16 + n_tok, :] = a
        pos = pos0 + l * n_tok + lax.broadcasted_iota(I32, (n_tok, 1), 0)
        for gi, w in enumerate(POOL_WINDOWS):
            cols = slice(gi * POOL_GROUP, (gi + 1) * POOL_GROUP)
            win = a[:, cols]
            for j in range(1, w):
                win = win + ext_scr[16 - j:16 - j + n_tok, cols]
            cnt = jnp.minimum(w, pos + 1).astype(F32)
            pooled.append(win / cnt - a[:, cols])
        tail = ext_scr[n_tok + 1:n_tok + 16, :]
        hist_out_ref[...] = tail
        ext_scr[1:16, :] = tail

    pool_out = jnp.concatenate(
        [jnp.dot(pooled[gi].astype(BF16), wpool_ref[gi], preferred_element_type=F32)
         for gi in range(len(POOL_WINDOWS))], axis=-1) * pscale_ref[...]

    blk = SUBLANES if sample else CHUNK
    tr = lax.broadcasted_iota(I32, (CHUNK, CHUNK), 0)
    sc = lax.broadcasted_iota(I32, (CHUNK, CHUNK), 1)
    mask = (sc <= tr) & ((sc // blk) == (tr // blk))
    wsp = [jnp.where(mask, wsp_ref[hh], jnp.zeros((), BF16)) for hh in range(N_GMLP_HEADS)]
    v_bf = v.astype(BF16)
    mixed_chunks = []
    for ci in range(n_tok // CHUNK):
        rows = slice(ci * CHUNK, (ci + 1) * CHUNK)
        heads = [jnp.dot(wsp[hh], v_bf[rows, hh * GMLP_HEAD:(hh + 1) * GMLP_HEAD],
                         preferred_element_type=F32) for hh in range(N_GMLP_HEADS)]
        mixed_chunks.append(jnp.concatenate(heads, axis=-1) + bsp_ref[...])
    mixed = jnp.concatenate(mixed_chunks, axis=0)
    gmlp_out = u * mixed

    cat = jnp.concatenate([pool_out, gmlp_out], axis=-1)
    h = x + jnp.dot(cat.astype(BF16), wout_ref[...], preferred_element_type=F32)
    hn = _rms(h, gffn_ref[...])

    h_ref[...] = h
    hn_ref[...] = hn.astype(BF16)
    if sample:
        v_ref[...] = v
    rt, counts = _route(hn, wr_hl_ref, wr_hi_ref, br_ref, n_tok)
    rt_ref[...] = rt[:, :SUBLANES]
    cnt_ref[...] = counts


def _full(shape):
    return pl.BlockSpec(shape, lambda *_: (0,) * len(shape))


def _mixer_call(x, hist, weights, *, sample, pos0):
    w_specs = [_full(w.shape) for w in weights]
    n_tok = TOK_TILE
    if sample:
        n_rows = x.shape[0]
        n_seq = n_tok // SUBLANES
        grid = (n_rows // n_tok,)
        tok_map = lambda i: (i, 0)
        tile_map = lambda i: (i, 0, 0)
        in_specs = [pl.BlockSpec((n_tok, D_MODEL), tok_map),
                    pl.BlockSpec((n_seq, POOL_HIST, POOL_WIDTH), lambda i: (i, 0, 0))] + w_specs
        hist_shape = jax.ShapeDtypeStruct(hist.shape, F32)
        hist_spec = pl.BlockSpec((n_seq, POOL_HIST, POOL_WIDTH), lambda i: (i, 0, 0))
        ext = pltpu.VMEM((n_seq, 16 + SUBLANES, POOL_WIDTH), F32)
        args = (x, hist)
    else:
        b, seq, _ = x.shape
        n_rows = b * seq
        n_l = seq // n_tok
        grid = (b, n_l)
        tok_map = lambda bi, li: (bi * n_l + li, 0)
        tile_map = lambda bi, li: (bi * n_l + li, 0, 0)
        in_specs = [pl.BlockSpec((None, n_tok, D_MODEL), lambda bi, li: (bi, li, 0))] + w_specs
        hist_shape = jax.ShapeDtypeStruct((b, POOL_HIST, POOL_WIDTH), F32)
        hist_spec = pl.BlockSpec((None, POOL_HIST, POOL_WIDTH), lambda bi, li: (bi, 0, 0))
        ext = pltpu.VMEM((16 + n_tok, POOL_WIDTH), F32)
        args = (x,)
    out_shape = [jax.ShapeDtypeStruct((n_rows, D_MODEL), F32),
                 jax.ShapeDtypeStruct((n_rows, D_MODEL), BF16),
                 jax.ShapeDtypeStruct((n_rows, SUBLANES), F32),
                 jax.ShapeDtypeStruct((n_rows // n_tok, 1, LANES), F32),
                 hist_shape]
    out_specs = [pl.BlockSpec((n_tok, D_MODEL), tok_map),
                 pl.BlockSpec((n_tok, D_MODEL), tok_map),
                 pl.BlockSpec((n_tok, SUBLANES), tok_map),
                 pl.BlockSpec((None, 1, LANES), tile_map),
                 hist_spec]
    if sample:
        out_shape.append(jax.ShapeDtypeStruct((n_rows, GMLP_WIDTH), F32))
        out_specs.append(pl.BlockSpec((n_tok, GMLP_WIDTH), tok_map))
    return pl.pallas_call(
        functools.partial(_mixer_kernel, n_tok=n_tok, sample=sample, pos0=pos0),
        grid=grid, in_specs=in_specs, out_specs=out_specs, out_shape=out_shape,
        scratch_shapes=[ext],
        compiler_params=pltpu.CompilerParams(
            dimension_semantics=("arbitrary",) * len(grid), vmem_limit_bytes=VMEM_LIMIT),
        name="mixer_sample" if sample else "mixer_prompt",
    )(*args, *weights)


def _for_each_copy(tile, cnt_ref, fn):
    for s, n_rows in enumerate(COPY_ROWS):
        def body(j, carry, s=s, n_rows=n_rows):
            fn(s, n_rows, tile * COPY_CAP[s] + j)
            return carry
        lax.fori_loop(0, cnt_ref[tile * len(COPY_ROWS) + s], body, 0)


def _wait_rows(n_rows_total, make_wait):
    units = n_rows_total // RUN_ALIGN
    bit = 0
    while (RUN_ALIGN << bit) <= LOC_ROWS:
        @pl.when((units >> bit) & 1 == 1)
        def _(bit=bit):
            make_wait(RUN_ALIGN << bit).wait()
        bit += 1


def _local_positions(rt, lstart_row):
    elane = lax.broadcasted_iota(I32, (rt.shape[0], LANES), 1).astype(F32)
    p1 = jnp.sum(jnp.where(elane == rt[:, 0:1], lstart_row, 0.0), axis=-1, keepdims=True) + rt[:, 2:3]
    p2 = jnp.sum(jnp.where(elane == rt[:, 1:2], lstart_row, 0.0), axis=-1, keepdims=True) + rt[:, 3:4]
    return p1, p2


def _dispatch_kernel(l8, g8, l4, g4, l2, g2, cnt_ref, trow_ref, zs_ref, zb_ref, nu_ref,
                     hnp_ref, hns_ref, rtp_ref, rts_ref, lst_ref, xs_ref, loc, zbuf, sem, zsem,
                     *, n_prompt_steps):
    i = pl.program_id(0)
    n_steps = pl.num_programs(0)
    slot = i % 2
    local_tabs, sorted_tabs = (l8, l4, l2), (g8, g4, g2)

    @pl.when(i == 0)
    def _():
        zbuf[...] = jnp.zeros(zbuf.shape, U32)
        n_tiles = xs_ref.shape[0] // (MOE_TILE * ROW_CHUNKS)

        def zero_copy(e, c):
            return pltpu.make_async_copy(zbuf.at[pl.ds(0, 8 * ROW_CHUNKS)],
                                         xs_ref.at[_rows(zs_ref[e] + c * 8, 8)], zsem)

        def tail_copy(t):
            return pltpu.make_async_copy(zbuf, xs_ref.at[_rows(t * MOE_TILE, MOE_TILE)], zsem)

        def per_expert(e, total):
            def per_block(c, carry):
                zero_copy(e, c).start()
                return carry
            lax.fori_loop(0, zb_ref[e], per_block, 0)
            return total + zb_ref[e]

        total = lax.fori_loop(0, N_EXPERTS, per_expert, 0)

        def wait_body(_, carry):
            zero_copy(0, 0).wait()
            return carry
        lax.fori_loop(0, total, wait_body, 0)

        def tail_start(t, carry):
            tail_copy(t).start()
            return carry

        def tail_wait(t, carry):
            tail_copy(t).wait()
            return carry

        spill = pltpu.make_async_copy(zbuf.at[pl.ds(0, 8 * ROW_CHUNKS)],
                                      xs_ref.at[_rows(n_tiles * MOE_TILE, 8)], zsem)
        lax.fori_loop(nu_ref[0], n_tiles, tail_start, 0)
        spill.start()
        lax.fori_loop(nu_ref[0], n_tiles, tail_wait, 0)
        spill.wait()

    is_prompt = i < n_prompt_steps
    rt = jnp.where(is_prompt, rtp_ref[...], rts_ref[...])
    lpos1, lpos2 = _local_positions(rt, lst_ref[...])
    lane = lax.broadcasted_iota(I32, (TOK_TILE, LOC_ROWS), 1).astype(F32)
    perm_t = jnp.where((lane == lpos1) | (lane == lpos2), 1.0, 0.0).astype(BF16)
    hn = jnp.where(is_prompt, hnp_ref[...], hns_ref[...])
    grouped = lax.dot_general(perm_t, hn, (((0,), (0,)), ((), ())), preferred_element_type=F32)
    _store_packed(loc.at[slot], grouped, LOC_ROWS)

    def run_copy(buf_slot, s, n_rows, k):
        src = loc.at[buf_slot, _rows(local_tabs[s][k], n_rows)]
        dst = xs_ref.at[_rows(sorted_tabs[s][k], n_rows)]
        return pltpu.make_async_copy(src, dst, sem.at[buf_slot])

    _for_each_copy(i, cnt_ref, lambda s, n_rows, k: run_copy(slot, s, n_rows, k).start())

    def drain(tile, buf_slot):
        def make_wait(n):
            return pltpu.make_async_copy(loc.at[buf_slot, pl.ds(0, n * ROW_CHUNKS)],
                                         xs_ref.at[pl.ds(0, n * ROW_CHUNKS)], sem.at[buf_slot])
        _wait_rows(trow_ref[tile], make_wait)

    @pl.when(i > 0)
    def _():
        drain(i - 1, 1 - slot)

    @pl.when(i == n_steps - 1)
    def _():
        drain(i, slot)


def _clamped_maps(np_steps):
    pmap = lambda i, *_: (jnp.minimum(i, np_steps - 1), 0)
    smap = lambda i, *_: (jnp.maximum(i - np_steps, 0), 0)
    return pmap, smap


def _dispatch_call(tables, hn_p, hn_s, rt_p, rt_s, lstv, n_sorted_rows):
    n_p, n_s = hn_p.shape[0], hn_s.shape[0]
    np_steps = n_p // TOK_TILE
    n_steps = np_steps + n_s // TOK_TILE
    pmap, smap = _clamped_maps(np_steps)
    out_rows = (n_sorted_rows + 8) * ROW_CHUNKS
    return pl.pallas_call(
        functools.partial(_dispatch_kernel, n_prompt_steps=np_steps),
        grid_spec=pltpu.PrefetchScalarGridSpec(
            num_scalar_prefetch=len(tables), grid=(n_steps,),
            in_specs=[pl.BlockSpec((TOK_TILE, D_MODEL), pmap),
                      pl.BlockSpec((TOK_TILE, D_MODEL), smap),
                      pl.BlockSpec((TOK_TILE, SUBLANES), pmap),
                      pl.BlockSpec((TOK_TILE, SUBLANES), smap),
                      pl.BlockSpec((None, 1, LANES), lambda i, *_: (i, 0, 0))],
            out_specs=pl.BlockSpec(memory_space=pl.ANY),
            scratch_shapes=[pltpu.VMEM((2, LOC_ROWS * ROW_CHUNKS, LANES), U32),
                            pltpu.VMEM((MOE_TILE * ROW_CHUNKS, LANES), U32),
                            pltpu.SemaphoreType.DMA((2,)),
                            pltpu.SemaphoreType.DMA(())]),
        out_shape=jax.ShapeDtypeStruct((out_rows, LANES), U32),
        compiler_params=pltpu.CompilerParams(
            dimension_semantics=("arbitrary",), vmem_limit_bytes=VMEM_LIMIT),
        name="dispatch",
    )(*tables, hn_p, hn_s, rt_p, rt_s, lstv)


def _moe_kernel(te_ref, nu_ref, par_ref, nxt_ref, xs_ref, wg_hbm, wu_hbm, wd_hbm, ys_ref,
                wg_f32, wu_f32, wd_f32, wg_bf, wu_bf, wd_bf, wsem):
    i = pl.program_id(0)
    used = i < nu_ref[0]

    def weight_copies(expert, buf):
        return [pltpu.make_async_copy(hbm.at[expert], vmem.at[buf], wsem.at[buf])
                for hbm, vmem in ((wg_hbm, wg_f32), (wu_hbm, wu_f32), (wd_hbm, wd_f32))]

    @pl.when(used)
    def _():
        prev = te_ref[jnp.maximum(i - 1, 0)]

        @pl.when((i == 0) | (te_ref[i] != prev))
        def _():
            buf = par_ref[i]

            @pl.when(i == 0)
            def _():
                for cp in weight_copies(te_ref[0], 0):
                    cp.start()

            for cp in weight_copies(te_ref[i], buf):
                cp.wait()
            wg_bf[...] = wg_f32[buf].astype(BF16)
            wu_bf[...] = wu_f32[buf].astype(BF16)
            wd_bf[...] = wd_f32[buf].astype(BF16)

            @pl.when(nxt_ref[i] >= 0)
            def _():
                for cp in weight_copies(nxt_ref[i], 1 - buf):
                    cp.start()

        lo, hi = _load_packed(xs_ref, MOE_TILE)
        x = jnp.concatenate([lo, hi], axis=-1).astype(BF16)
        g = jnp.dot(x, wg_bf[...], preferred_element_type=F32)
        u = jnp.dot(x, wu_bf[...], preferred_element_type=F32)
        hh = (g * jax.nn.sigmoid(g)) * u
        y = jnp.dot(hh.astype(BF16), wd_bf[...], preferred_element_type=F32)
        _store_packed(ys_ref, y, MOE_TILE)

    @pl.when(jnp.logical_not(used))
    def _():
        ij = (lax.broadcasted_iota(I32, (MOE_TILE, D_MODEL), 0)
              + lax.broadcasted_iota(I32, (MOE_TILE, D_MODEL), 1))
        zeros = jnp.where(ij < jnp.minimum(nu_ref[0], 0), 1.0, 0.0)
        _store_packed(ys_ref, zeros, MOE_TILE)


def _moe_call(tile_expert, n_used, buf_parity, next_expert, xs, n_tiles, w_eg, w_eu, w_ed):
    blk = MOE_TILE * ROW_CHUNKS
    any_spec = pl.BlockSpec(memory_space=pl.ANY)
    return pl.pallas_call(
        _moe_kernel,
        grid_spec=pltpu.PrefetchScalarGridSpec(
            num_scalar_prefetch=4, grid=(n_tiles,),
            in_specs=[
                pl.BlockSpec((blk, LANES), lambda i, te, nu, *_: (jnp.minimum(i, nu[0] - 1), 0)),
                any_spec, any_spec, any_spec,
            ],
            out_specs=pl.BlockSpec((blk, LANES), lambda i, *_: (i, 0)),
            scratch_shapes=[pltpu.VMEM((2, D_MODEL, D_EXPERT), F32),
                            pltpu.VMEM((2, D_MODEL, D_EXPERT), F32),
                            pltpu.VMEM((2, D_EXPERT, D_MODEL), F32),
                            pltpu.VMEM((D_MODEL, D_EXPERT), BF16),
                            pltpu.VMEM((D_MODEL, D_EXPERT), BF16),
                            pltpu.VMEM((D_EXPERT, D_MODEL), BF16),
                            pltpu.SemaphoreType.DMA((2,))]),
        out_shape=jax.ShapeDtypeStruct((n_tiles * blk, LANES), U32),
        compiler_params=pltpu.CompilerParams(
            dimension_semantics=("arbitrary",), vmem_limit_bytes=VMEM_LIMIT),
        name="experts",
    )(tile_expert, n_used, buf_parity, next_expert, xs, w_eg, w_eu, w_ed)


def _final_kernel(l8, g8, l4, g4, l2, g2, cnt_ref, trow_ref,
                  hp_ref, hs_ref, rtp_ref, rts_ref, lst_ref, pp_ref, ps_ref,
                  ys_ref, wple_ref, wgate_ref, gple_ref, gfin_ref, outp_ref, outs_ref, ybuf, sem,
                  *, n_prompt_steps):
    i = pl.program_id(0)
    n_steps = pl.num_programs(0)
    slot = i % 2
    local_tabs, sorted_tabs = (l8, l4, l2), (g8, g4, g2)

    def run_copy(buf_slot, s, n_rows, k):
        src = ys_ref.at[_rows(sorted_tabs[s][k], n_rows)]
        dst = ybuf.at[buf_slot, _rows(local_tabs[s][k], n_rows)]
        return pltpu.make_async_copy(src, dst, sem.at[buf_slot])

    def issue(tile, buf_slot):
        _for_each_copy(tile, cnt_ref, lambda s, n_rows, k: run_copy(buf_slot, s, n_rows, k).start())

    @pl.when(i == 0)
    def _():
        ybuf[...] = jnp.zeros(ybuf.shape, U32)
        issue(0, 0)

    @pl.when(i + 1 < n_steps)
    def _():
        issue(i + 1, 1 - slot)

    def make_wait(n):
        return pltpu.make_async_copy(ys_ref.at[pl.ds(0, n * ROW_CHUNKS)],
                                     ybuf.at[slot, pl.ds(0, n * ROW_CHUNKS)], sem.at[slot])
    _wait_rows(trow_ref[i], make_wait)

    is_prompt = i < n_prompt_steps
    h = jnp.where(is_prompt, hp_ref[...], hs_ref[...])
    p = jnp.where(is_prompt, pp_ref[...], ps_ref[...])
    rt = jnp.where(is_prompt, rtp_ref[...], rts_ref[...])
    lpos1, lpos2 = _local_positions(rt, lst_ref[...])
    lane = lax.broadcasted_iota(I32, (TOK_TILE, LOC_ROWS), 1).astype(F32)
    sel1 = lane == lpos1
    sel2 = lane == lpos2
    wlane = lax.broadcasted_iota(I32, (TOK_TILE, LANES), 1)

    def hi_lo_cols(w):
        w_hi = w.astype(BF16).astype(F32)
        return jnp.where(wlane == 0, w_hi, jnp.where(wlane == 1, w - w_hi, 0.0)).astype(BF16)

    dn = (((0,), (0,)), ((), ()))
    wrow = (lax.dot_general(jnp.where(sel1, 1.0, 0.0).astype(BF16), hi_lo_cols(rt[:, 4:5]), dn,
                            preferred_element_type=F32)
            + lax.dot_general(jnp.where(sel2, 1.0, 0.0).astype(BF16), hi_lo_cols(rt[:, 5:6]), dn,
                              preferred_element_type=F32))
    wrow = wrow[:, 0:1] + wrow[:, 1:2]
    lo, hi = _load_packed(ybuf.at[slot], LOC_ROWS)
    yb = (jnp.concatenate([lo, hi], axis=-1) * wrow).astype(BF16)
    perm = jnp.where(sel1 | sel2, 1.0, 0.0).astype(BF16)
    h = h + jnp.dot(perm, yb, preferred_element_type=F32)
    r = _rms(h, gple_ref[...])
    gate = jax.nn.sigmoid(jnp.dot(r.astype(BF16), wgate_ref[...], preferred_element_type=F32))
    h = h + jnp.dot(p.astype(BF16), wple_ref[...], preferred_element_type=F32) * gate
    y = _rms(h, gfin_ref[...])

    @pl.when(is_prompt)
    def _():
        outp_ref[...] = y

    @pl.when(jnp.logical_not(is_prompt))
    def _():
        outs_ref[...] = y


def _final_call(tables, h_p, h_s, rt_p, rt_s, lstv, p_p, p_s, ys, wple, wgate, gple, gfin):
    n_p, n_s = h_p.shape[0], h_s.shape[0]
    np_steps = n_p // TOK_TILE
    n_steps = np_steps + n_s // TOK_TILE
    pmap, smap = _clamped_maps(np_steps)
    cmap = lambda i, *_: (0, 0)
    return pl.pallas_call(
        functools.partial(_final_kernel, n_prompt_steps=np_steps),
        grid_spec=pltpu.PrefetchScalarGridSpec(
            num_scalar_prefetch=len(tables), grid=(n_steps,),
            in_specs=[
                pl.BlockSpec((TOK_TILE, D_MODEL), pmap),
                pl.BlockSpec((TOK_TILE, D_MODEL), smap),
                pl.BlockSpec((TOK_TILE, SUBLANES), pmap),
                pl.BlockSpec((TOK_TILE, SUBLANES), smap),
                pl.BlockSpec((None, 1, LANES), lambda i, *_: (i, 0, 0)),
                pl.BlockSpec((TOK_TILE, PLE_DIM), pmap),
                pl.BlockSpec((TOK_TILE, PLE_DIM), smap),
                pl.BlockSpec(memory_space=pl.ANY),
                pl.BlockSpec(wple.shape, cmap),
                pl.BlockSpec(wgate.shape, cmap),
                pl.BlockSpec(gple.shape, cmap),
                pl.BlockSpec(gfin.shape, cmap),
            ],
            out_specs=[pl.BlockSpec((TOK_TILE, D_MODEL), pmap),
                       pl.BlockSpec((TOK_TILE, D_MODEL), smap)],
            scratch_shapes=[pltpu.VMEM((2, LOC_ROWS * ROW_CHUNKS, LANES), U32),
                            pltpu.SemaphoreType.DMA((2,))]),
        out_shape=[jax.ShapeDtypeStruct((n_p, D_MODEL), F32),
                   jax.ShapeDtypeStruct((n_s, D_MODEL), F32)],
        compiler_params=pltpu.CompilerParams(
            dimension_semantics=("arbitrary",), vmem_limit_bytes=VMEM_LIMIT),
        name="final",
    )(*tables, h_p, h_s, rt_p, rt_s, lstv, p_p, p_s, ys, wple, wgate, gple, gfin)


def _ceil_to(x, m):
    return ((x + m - 1) // m) * m


def _excl_cumsum(a, axis):
    return jnp.cumsum(a, axis=axis) - a


def _copy_tables(n_copies, first_local, first_sorted, step, cap):
    ends = jnp.cumsum(n_copies, axis=1)
    j = jnp.arange(cap, dtype=I32)[None, :, None]
    e_of = jnp.minimum(jnp.sum(ends[:, None, :] <= j, axis=-1), N_EXPERTS - 1)
    pick = e_of[:, :, None] == jnp.arange(N_EXPERTS, dtype=I32)[None, None, :]
    take = lambda a: jnp.sum(jnp.where(pick, a[:, None, :], 0), axis=-1)
    c = jnp.arange(cap, dtype=I32)[None, :] - (take(ends) - take(n_copies))
    flat = lambda a: a.reshape(-1).astype(I32)
    return flat(take(first_local) + step * c), flat(take(first_sorted) + step * c), ends[:, -1]


def kernel(x_prompt, x_sample, state_pool, p_prompt, p_sample, g_mix, w_in, w_pool, pool_scale,
           g_v, b_v, w_spatial, b_spatial, w_out, g_ffn, w_rg, b_rg, w_re, b_re, w_eg, w_eu, w_ed,
           w_ple, g_ple, w_ple_gate, g_final):
    batch, seq, _ = x_prompt.shape
    dec_batch, dec_seq, _ = x_sample.shape
    assert dec_seq == SUBLANES and g_mix.shape[0] == 1
    n_p = batch * seq
    n_s = dec_batch * dec_seq
    n_tok = n_p + n_s
    past_len = 16384

    row = lambda a: a.reshape(1, -1).astype(F32)
    wr = jnp.concatenate(
        [w_re[0], w_rg[0], jnp.zeros((D_MODEL, LANES - N_EXPERTS - N_EXPERT_GROUPS), F32)], axis=1)
    wr_hi = wr.astype(BF16)
    wr_lo = (wr - wr_hi.astype(F32)).astype(BF16)
    br = jnp.concatenate(
        [b_re[0], b_rg[0], jnp.zeros((LANES - N_EXPERTS - N_EXPERT_GROUPS,), F32)]).reshape(1, LANES)
    bsp_p = jnp.repeat(jnp.transpose(b_spatial[0]), GMLP_HEAD, axis=1)
    bsp_s = jnp.tile(bsp_p[:dec_seq], (CHUNK // dec_seq, 1))
    wsp_p = w_spatial[0].astype(BF16)
    wsp_s = jnp.tile(w_spatial[0][:, :dec_seq, :dec_seq].astype(BF16),
                     (1, CHUNK // dec_seq, CHUNK // dec_seq))

    def mixer_weights(wsp, bsp):
        return (row(g_mix[0]), w_in[0].astype(BF16), w_pool[0].astype(BF16), row(pool_scale[0]),
                row(g_v[0]), row(b_v[0]), wsp, bsp, w_out[0].astype(BF16), row(g_ffn[0]),
                jnp.concatenate([wr_hi, wr_lo], axis=1), wr_hi, br)

    h_p, hn_p, rt_p, cnt_p, hist_p = _mixer_call(
        x_prompt, None, mixer_weights(wsp_p, bsp_p), sample=False, pos0=0)
    h_s, hn_s, rt_s, cnt_s, hist_s, v_s = _mixer_call(
        x_sample.reshape(n_s, D_MODEL), state_pool[0], mixer_weights(wsp_s, bsp_s),
        sample=True, pos0=past_len)

    n_run = jnp.concatenate([cnt_p, cnt_s], axis=0)[:, 0, :N_EXPERTS].astype(I32)
    n_even = _ceil_to(n_run, RUN_ALIGN)
    l_start = _excl_cumsum(n_even, 1)
    tile_rows = jnp.sum(n_even, axis=1)
    base = _excl_cumsum(n_even, 0)
    counts = jnp.sum(n_even, axis=0)
    padded = _ceil_to(counts, MOE_TILE)
    ends = jnp.cumsum(padded)
    offs = ends - padded
    g_start = offs[None, :] + base
    n8 = n_even // 8
    f4 = (n_even // 4) % 2
    f2 = (n_even // 2) % 2
    l8, g8, c8 = _copy_tables(n8, l_start, g_start, 8, COPY_CAP[0])
    l4, g4, c4 = _copy_tables(f4, l_start + 8 * n8, g_start + 8 * n8, 0, COPY_CAP[1])
    l2, g2, c2 = _copy_tables(f2, l_start + 8 * n8 + 4 * f4, g_start + 8 * n8 + 4 * f4, 0, COPY_CAP[2])
    copy_cnt = jnp.stack([c8, c4, c2], axis=1).reshape(-1).astype(I32)
    tile_rows = tile_rows.astype(I32)
    z_start = (offs + counts).astype(I32)
    z_blocks = ((padded - counts + 7) // 8).astype(I32)
    lstv = jnp.zeros((n_run.shape[0], 1, LANES), F32).at[:, 0, :N_EXPERTS].set(l_start.astype(F32))

    max_rows = 2 * n_tok + n_run.shape[0] * N_EXPERTS * (RUN_ALIGN - 1)
    n_tiles = max_rows // MOE_TILE + N_EXPERTS + 1
    n_used = (ends[-1] // MOE_TILE).astype(I32).reshape(1)
    tile_id = jnp.minimum(jnp.arange(n_tiles, dtype=I32), n_used - 1)
    tile_expert = jnp.sum((tile_id[:, None] * MOE_TILE) >= ends[None, :], axis=1).astype(I32)
    tile_expert = jnp.minimum(tile_expert, N_EXPERTS - 1)
    e_ids = jnp.arange(N_EXPERTS, dtype=I32)
    nonempty = padded > 0
    seg_index = jnp.cumsum(nonempty.astype(I32)) - 1
    later = nonempty[None, :] & (e_ids[None, :] > e_ids[:, None])
    next_of = jnp.min(jnp.where(later, e_ids[None, :], N_EXPERTS), axis=1)
    next_of = jnp.where(next_of == N_EXPERTS, -1, next_of)
    pick_e = tile_expert[:, None] == e_ids[None, :]
    buf_parity = (jnp.sum(jnp.where(pick_e, seg_index[None, :], 0), axis=1) % 2).astype(I32)
    next_expert = jnp.sum(jnp.where(pick_e, next_of[None, :], 0), axis=1).astype(I32)

    copy_tabs = (l8, g8, l4, g4, l2, g2, copy_cnt, tile_rows)
    xs = _dispatch_call(copy_tabs + (z_start, z_blocks, n_used), hn_p, hn_s, rt_p, rt_s, lstv,
                        n_tiles * MOE_TILE)
    ys = _moe_call(tile_expert, n_used, buf_parity, next_expert, xs, n_tiles, w_eg[0], w_eu[0], w_ed[0])
    y_p, y_s = _final_call(copy_tabs, h_p, h_s, rt_p, rt_s, lstv,
                           p_prompt[0].reshape(n_p, PLE_DIM), p_sample[0].reshape(n_s, PLE_DIM), ys,
                           w_ple[0].astype(BF16), w_ple_gate[0].astype(BF16), row(g_ple[0]), row(g_final))

    return (y_p.reshape(batch, seq, D_MODEL),
            y_s.reshape(dec_batch, dec_seq, D_MODEL),
            hist_p[None],
            hist_s[None],
            v_s.reshape(1, dec_batch, dec_seq, GMLP_WIDTH))
```

```python
import functools

import jax
import jax.numpy as jnp
from jax import lax
from jax.experimental import pallas as pl
from jax.experimental.pallas import tpu as pltpu

D_MODEL = 1024
POOL_WIDTH = 512
GMLP_WIDTH = 512
IN_WIDTH = POOL_WIDTH + 2 * GMLP_WIDTH
POOL_WINDOWS = (2, 4, 8, 16)
POOL_GROUP = 128
POOL_HIST = 15
CHUNK = 128
N_GMLP_HEADS = 4
GMLP_HEAD = 128
N_EXPERT_GROUPS = 4
EXPERTS_PER_GROUP = 8
N_EXPERTS = 32
D_EXPERT = 512
PLE_DIM = 256
EPS = 1e-6

LANES = 128
SUBLANES = 8
HALF = D_MODEL // 2
ROW_CHUNKS = HALF // LANES

TOK_TILE = 256
MIX_TILE = 512
MOE_TILE = 512
RUN_ALIGN = 2
COPY_ROWS = (8, 4, 2)
LOC_ROWS = 2 * TOK_TILE + N_EXPERTS * (RUN_ALIGN - 1)
COPY_CAP = (LOC_ROWS // 8 + 1, N_EXPERTS, N_EXPERTS)
VMEM_LIMIT = 56 * 1024 * 1024

F32 = jnp.float32
BF16 = jnp.bfloat16
I32 = jnp.int32
U32 = jnp.uint32


def _rms(x, g):
    return x * lax.rsqrt(jnp.mean(x * x, axis=-1, keepdims=True) + EPS) * g


def _gelu_tanh(x):
    return x * (0.5 * (1.0 + jnp.tanh(0.7978845608028654 * (x + 0.044715 * (x * x * x)))))


def _store_packed(ref, val, n_rows):
    packed = pltpu.pack_elementwise([val[:, :HALF], val[:, HALF:]], packed_dtype=BF16)
    for k in range(ROW_CHUNKS):
        ref[pl.ds(k, n_rows, stride=ROW_CHUNKS), :] = packed[:, k * LANES:(k + 1) * LANES]


def _load_packed(ref, n_rows):
    packed = jnp.concatenate(
        [ref[pl.ds(k, n_rows, stride=ROW_CHUNKS), :] for k in range(ROW_CHUNKS)], axis=-1)
    lo = pltpu.unpack_elementwise(packed, index=0, packed_dtype=BF16, unpacked_dtype=F32)
    hi = pltpu.unpack_elementwise(packed, index=1, packed_dtype=BF16, unpacked_dtype=F32)
    return lo, hi


def _rows(start, n):
    return pl.ds(pl.multiple_of(start * ROW_CHUNKS, SUBLANES), n * ROW_CHUNKS)


def _route(hn, wr_hl_ref, wr_hi_ref, br_ref, n_tok):
    hn_hi = hn.astype(BF16)
    hn_lo = (hn - hn_hi.astype(F32)).astype(BF16)
    l1 = jnp.dot(hn_hi, wr_hl_ref[...], preferred_element_type=F32)
    l2 = jnp.dot(hn_lo, wr_hi_ref[...], preferred_element_type=F32)
    logits = l1[:, :LANES] + (l1[:, LANES:] + l2) + br_ref[...]

    lane = lax.broadcasted_iota(I32, (n_tok, LANES), 1).astype(F32)
    neg = jnp.float32(-jnp.inf)
    big = jnp.float32(1e9)
    gmask = (lane >= N_EXPERTS) & (lane < N_EXPERTS + N_EXPERT_GROUPS)
    gl = jnp.where(gmask, logits, neg)
    gmax = jnp.max(gl, axis=-1, keepdims=True)
    gidx = jnp.min(jnp.where(gl == gmax, lane, big), axis=-1, keepdims=True) - N_EXPERTS
    gden = jnp.sum(jnp.exp(gl - gmax), axis=-1, keepdims=True)
    gprob = 1.0 / gden

    lo = gidx * EXPERTS_PER_GROUP
    emask = (lane >= lo) & (lane < lo + EXPERTS_PER_GROUP)
    el = jnp.where(emask, logits, neg)
    m1 = jnp.max(el, axis=-1, keepdims=True)
    i1 = jnp.min(jnp.where(el == m1, lane, big), axis=-1, keepdims=True)
    el2 = jnp.where(lane == i1, neg, el)
    m2 = jnp.max(el2, axis=-1, keepdims=True)
    i2 = jnp.min(jnp.where(el2 == m2, lane, big), axis=-1, keepdims=True)
    t = jnp.exp(m2 - m1)
    w1 = gprob / (1.0 + t)
    w2 = gprob * t / (1.0 + t)

    sel1 = lane == i1
    sel2 = lane == i2
    oh = jnp.where(sel1 | sel2, 1.0, 0.0).astype(BF16)
    r = lax.broadcasted_iota(I32, (n_tok, n_tok), 0)
    c = lax.broadcasted_iota(I32, (n_tok, n_tok), 1)
    tri = jnp.where(c < r, 1.0, 0.0).astype(BF16)
    before = jnp.dot(tri, oh, preferred_element_type=F32)
    counts = jnp.sum(oh.astype(F32), axis=0, keepdims=True)
    lrank1 = jnp.sum(jnp.where(sel1, before, 0.0), axis=-1, keepdims=True)
    lrank2 = jnp.sum(jnp.where(sel2, before, 0.0), axis=-1, keepdims=True)

    out = jnp.where(lane == 0, i1, 0.0)
    for k, val in enumerate((i2, lrank1, lrank2, w1, w2), start=1):
        out = jnp.where(lane == k, val, out)
    return out, counts


def _mixer_kernel(*refs, n_tok, sample, pos0):
    if sample:
        (x_ref, hist_ref, gmix_ref, win_ref, wpool_ref, pscale_ref, gv_ref, bv_ref, wsp_ref,
         bsp_ref, wout_ref, gffn_ref, wr_hl_ref, wr_hi_ref, br_ref,
         h_ref, hn_ref, rt_ref, cnt_ref, hist_out_ref, v_ref, ext_scr) = refs
    else:
        (x_ref, gmix_ref, win_ref, wpool_ref, pscale_ref, gv_ref, bv_ref, wsp_ref,
         bsp_ref, wout_ref, gffn_ref, wr_hl_ref, wr_hi_ref, br_ref,
         h_ref, hn_ref, rt_ref, cnt_ref, hist_out_ref, ext_scr) = refs

    x = x_ref[...]
    xn = _rms(x, gmix_ref[...])
    z = jnp.dot(xn.astype(BF16), win_ref[...], preferred_element_type=F32)
    a = z[:, :POOL_WIDTH]
    uv = _gelu_tanh(z[:, POOL_WIDTH:])
    u = uv[:, :GMLP_WIDTH]
    v = uv[:, GMLP_WIDTH:]
    mu = jnp.mean(v, axis=-1, keepdims=True)
    vc = v - mu
    v = vc * lax.rsqrt(jnp.mean(vc * vc, axis=-1, keepdims=True) + EPS) * gv_ref[...] + bv_ref[...]

    pooled = []
    if sample:
        n_seq = n_tok // SUBLANES
        ext_scr[:, 1:1 + POOL_HIST, :] = hist_ref[...]
        ext_scr[:, 1 + POOL_HIST:, :] = a.reshape(n_seq, SUBLANES, POOL_WIDTH)
        hist_out_ref[...] = ext_scr[:, 1 + SUBLANES:, :]
        pos = pos0 + lax.broadcasted_iota(I32, (n_seq, SUBLANES, 1), 1)
        for gi, w in enumerate(POOL_WINDOWS):
            cols = slice(gi * POOL_GROUP, (gi + 1) * POOL_GROUP)
            win = ext_scr[:, 16:16 + SUBLANES, cols]
            for j in range(1, w):
                win = win + ext_scr[:, 16 - j:16 - j + SUBLANES, cols]
            cnt = jnp.minimum(w, pos + 1).astype(F32)
            pooled.append((win / cnt).reshape(n_tok, POOL_GROUP) - a[:, cols])
    else:
        l = pl.program_id(1)

        @pl.when(l == 0)
        def _():
            ext_scr[0:16, :] = jnp.zeros((16, POOL_WIDTH), F32)

        ext_scr[16:16 + n_tok, :] = a
        pos = pos0 + l * n_tok + lax.broadcasted_iota(I32, (n_tok, 1), 0)
        for gi, w in enumerate(POOL_WINDOWS):
            cols = slice(gi * POOL_GROUP, (gi + 1) * POOL_GROUP)
            win = a[:, cols]
            for j in range(1, w):
                win = win + ext_scr[16 - j:16 - j + n_tok, cols]
            cnt = jnp.minimum(w, pos + 1).astype(F32)
            pooled.append(win / cnt - a[:, cols])
        tail = ext_scr[n_tok + 1:n_tok + 16, :]
        hist_out_ref[...] = tail
        ext_scr[1:16, :] = tail

    pool_out = jnp.concatenate(
        [jnp.dot(pooled[gi].astype(BF16), wpool_ref[gi], preferred_element_type=F32)
         for gi in range(len(POOL_WINDOWS))], axis=-1) * pscale_ref[...]

    blk = SUBLANES if sample else CHUNK
    tr = lax.broadcasted_iota(I32, (CHUNK, CHUNK), 0)
    sc = lax.broadcasted_iota(I32, (CHUNK, CHUNK), 1)
    mask = (sc <= tr) & ((sc // blk) == (tr // blk))
    wsp = [jnp.where(mask, wsp_ref[hh], jnp.zeros((), BF16)) for hh in range(N_GMLP_HEADS)]
    v_bf = v.astype(BF16)
    mixed_chunks = []
    for ci in range(n_tok // CHUNK):
        rows = slice(ci * CHUNK, (ci + 1) * CHUNK)
        heads = [jnp.dot(wsp[hh], v_bf[rows, hh * GMLP_HEAD:(hh + 1) * GMLP_HEAD],
                         preferred_element_type=F32) for hh in range(N_GMLP_HEADS)]
        mixed_chunks.append(jnp.concatenate(heads, axis=-1) + bsp_ref[...])
    mixed = jnp.concatenate(mixed_chunks, axis=0)
    gmlp_out = u * mixed

    cat = jnp.concatenate([pool_out, gmlp_out], axis=-1)
    h = x + jnp.dot(cat.astype(BF16), wout_ref[...], preferred_element_type=F32)
    hn = _rms(h, gffn_ref[...])

    h_ref[...] = h
    hn_ref[...] = hn.astype(BF16)
    if sample:
        v_ref[...] = v
    for s in range(n_tok // TOK_TILE):
        rows = slice(s * TOK_TILE, (s + 1) * TOK_TILE)
        rt, counts = _route(hn[rows], wr_hl_ref, wr_hi_ref, br_ref, TOK_TILE)
        rt_ref[rows, :] = rt[:, :SUBLANES]
        cnt_ref[s] = counts


def _full(shape):
    return pl.BlockSpec(shape, lambda *_: (0,) * len(shape))


def _mixer_call(x, hist, weights, *, sample, pos0):
    w_specs = [_full(w.shape) for w in weights]
    n_tok = MIX_TILE
    if sample:
        n_rows = x.shape[0]
        n_seq = n_tok // SUBLANES
        grid = (n_rows // n_tok,)
        tok_map = lambda i: (i, 0)
        tile_map = lambda i: (i, 0, 0)
        in_specs = [pl.BlockSpec((n_tok, D_MODEL), tok_map),
                    pl.BlockSpec((n_seq, POOL_HIST, POOL_WIDTH), lambda i: (i, 0, 0))] + w_specs
        hist_shape = jax.ShapeDtypeStruct(hist.shape, F32)
        hist_spec = pl.BlockSpec((n_seq, POOL_HIST, POOL_WIDTH), lambda i: (i, 0, 0))
        ext = pltpu.VMEM((n_seq, 16 + SUBLANES, POOL_WIDTH), F32)
        args = (x, hist)
    else:
        b, seq, _ = x.shape
        n_rows = b * seq
        n_l = seq // n_tok
        grid = (b, n_l)
        tok_map = lambda bi, li: (bi * n_l + li, 0)
        tile_map = lambda bi, li: (bi * n_l + li, 0, 0)
        in_specs = [pl.BlockSpec((None, n_tok, D_MODEL), lambda bi, li: (bi, li, 0))] + w_specs
        hist_shape = jax.ShapeDtypeStruct((b, POOL_HIST, POOL_WIDTH), F32)
        hist_spec = pl.BlockSpec((None, POOL_HIST, POOL_WIDTH), lambda bi, li: (bi, 0, 0))
        ext = pltpu.VMEM((16 + n_tok, POOL_WIDTH), F32)
        args = (x,)
    out_shape = [jax.ShapeDtypeStruct((n_rows, D_MODEL), F32),
                 jax.ShapeDtypeStruct((n_rows, D_MODEL), BF16),
                 jax.ShapeDtypeStruct((n_rows, SUBLANES), F32),
                 jax.ShapeDtypeStruct((n_rows // TOK_TILE, 1, LANES), F32),
                 hist_shape]
    out_specs = [pl.BlockSpec((n_tok, D_MODEL), tok_map),
                 pl.BlockSpec((n_tok, D_MODEL), tok_map),
                 pl.BlockSpec((n_tok, SUBLANES), tok_map),
                 pl.BlockSpec((n_tok // TOK_TILE, 1, LANES), tile_map),
                 hist_spec]
    if sample:
        out_shape.append(jax.ShapeDtypeStruct((n_rows, GMLP_WIDTH), F32))
        out_specs.append(pl.BlockSpec((n_tok, GMLP_WIDTH), tok_map))
    return pl.pallas_call(
        functools.partial(_mixer_kernel, n_tok=n_tok, sample=sample, pos0=pos0),
        grid=grid, in_specs=in_specs, out_specs=out_specs, out_shape=out_shape,
        scratch_shapes=[ext],
        compiler_params=pltpu.CompilerParams(
            dimension_semantics=("arbitrary",) * len(grid), vmem_limit_bytes=VMEM_LIMIT),
        name="mixer_sample" if sample else "mixer_prompt",
    )(*args, *weights)


def _for_each_copy(tile, cnt_ref, fn):
    for s, n_rows in enumerate(COPY_ROWS):
        def body(j, carry, s=s, n_rows=n_rows):
            fn(s, n_rows, tile * COPY_CAP[s] + j)
            return carry
        lax.fori_loop(0, cnt_ref[tile * len(COPY_ROWS) + s], body, 0)


def _wait_rows(n_rows_total, make_wait):
    units = n_rows_total // RUN_ALIGN
    bit = 0
    while (RUN_ALIGN << bit) <= LOC_ROWS:
        @pl.when((units >> bit) & 1 == 1)
        def _(bit=bit):
            make_wait(RUN_ALIGN << bit).wait()
        bit += 1


def _local_positions(rt, lstart_row):
    elane = lax.broadcasted_iota(I32, (rt.shape[0], LANES), 1).astype(F32)
    p1 = jnp.sum(jnp.where(elane == rt[:, 0:1], lstart_row, 0.0), axis=-1, keepdims=True) + rt[:, 2:3]
    p2 = jnp.sum(jnp.where(elane == rt[:, 1:2], lstart_row, 0.0), axis=-1, keepdims=True) + rt[:, 3:4]
    return p1, p2


def _dispatch_kernel(l8, g8, l4, g4, l2, g2, cnt_ref, trow_ref, zs_ref, zb_ref, nu_ref,
                     hnp_ref, hns_ref, rtp_ref, rts_ref, lst_ref, xs_ref, loc, zbuf, sem, zsem,
                     *, n_prompt_steps):
    i = pl.program_id(0)
    n_steps = pl.num_programs(0)
    slot = i % 2
    local_tabs, sorted_tabs = (l8, l4, l2), (g8, g4, g2)

    @pl.when(i == 0)
    def _():
        zbuf[...] = jnp.zeros(zbuf.shape, U32)
        n_tiles = xs_ref.shape[0] // (MOE_TILE * ROW_CHUNKS)

        def zero_copy(e, c):
            return pltpu.make_async_copy(zbuf.at[pl.ds(0, 8 * ROW_CHUNKS)],
                                         xs_ref.at[_rows(zs_ref[e] + c * 8, 8)], zsem)

        def tail_copy(t):
            return pltpu.make_async_copy(zbuf, xs_ref.at[_rows(t * MOE_TILE, MOE_TILE)], zsem)

        def per_expert(e, total):
            def per_block(c, carry):
                zero_copy(e, c).start()
                return carry
            lax.fori_loop(0, zb_ref[e], per_block, 0)
            return total + zb_ref[e]

        total = lax.fori_loop(0, N_EXPERTS, per_expert, 0)

        def wait_body(_, carry):
            zero_copy(0, 0).wait()
            return carry
        lax.fori_loop(0, total, wait_body, 0)

        def tail_start(t, carry):
            tail_copy(t).start()
            return carry

        def tail_wait(t, carry):
            tail_copy(t).wait()
            return carry

        spill = pltpu.make_async_copy(zbuf.at[pl.ds(0, 8 * ROW_CHUNKS)],
                                      xs_ref.at[_rows(n_tiles * MOE_TILE, 8)], zsem)
        lax.fori_loop(nu_ref[0], n_tiles, tail_start, 0)
        spill.start()
        lax.fori_loop(nu_ref[0], n_tiles, tail_wait, 0)
        spill.wait()

    is_prompt = i < n_prompt_steps
    rt = jnp.where(is_prompt, rtp_ref[...], rts_ref[...])
    lpos1, lpos2 = _local_positions(rt, lst_ref[...])
    lane = lax.broadcasted_iota(I32, (TOK_TILE, LOC_ROWS), 1).astype(F32)
    perm_t = jnp.where((lane == lpos1) | (lane == lpos2), 1.0, 0.0).astype(BF16)
    hn = jnp.where(is_prompt, hnp_ref[...], hns_ref[...])
    grouped = lax.dot_general(perm_t, hn, (((0,), (0,)), ((), ())), preferred_element_type=F32)
    _store_packed(loc.at[slot], grouped, LOC_ROWS)

    def run_copy(buf_slot, s, n_rows, k):
        src = loc.at[buf_slot, _rows(local_tabs[s][k], n_rows)]
        dst = xs_ref.at[_rows(sorted_tabs[s][k], n_rows)]
        return pltpu.make_async_copy(src, dst, sem.at[buf_slot])

    _for_each_copy(i, cnt_ref, lambda s, n_rows, k: run_copy(slot, s, n_rows, k).start())

    def drain(tile, buf_slot):
        def make_wait(n):
            return pltpu.make_async_copy(loc.at[buf_slot, pl.ds(0, n * ROW_CHUNKS)],
                                         xs_ref.at[pl.ds(0, n * ROW_CHUNKS)], sem.at[buf_slot])
        _wait_rows(trow_ref[tile], make_wait)

    @pl.when(i > 0)
    def _():
        drain(i - 1, 1 - slot)

    @pl.when(i == n_steps - 1)
    def _():
        drain(i, slot)


def _clamped_maps(np_steps):
    pmap = lambda i, *_: (jnp.minimum(i, np_steps - 1), 0)
    smap = lambda i, *_: (jnp.maximum(i - np_steps, 0), 0)
    return pmap, smap


def _dispatch_call(tables, hn_p, hn_s, rt_p, rt_s, lstv, n_sorted_rows):
    n_p, n_s = hn_p.shape[0], hn_s.shape[0]
    np_steps = n_p // TOK_TILE
    n_steps = np_steps + n_s // TOK_TILE
    pmap, smap = _clamped_maps(np_steps)
    out_rows = (n_sorted_rows + 8) * ROW_CHUNKS
    return pl.pallas_call(
        functools.partial(_dispatch_kernel, n_prompt_steps=np_steps),
        grid_spec=pltpu.PrefetchScalarGridSpec(
            num_scalar_prefetch=len(tables), grid=(n_steps,),
            in_specs=[pl.BlockSpec((TOK_TILE, D_MODEL), pmap),
                      pl.BlockSpec((TOK_TILE, D_MODEL), smap),
                      pl.BlockSpec((TOK_TILE, SUBLANES), pmap),
                      pl.BlockSpec((TOK_TILE, SUBLANES), smap),
                      pl.BlockSpec((None, 1, LANES), lambda i, *_: (i, 0, 0))],
            out_specs=pl.BlockSpec(memory_space=pl.ANY),
            scratch_shapes=[pltpu.VMEM((2, LOC_ROWS * ROW_CHUNKS, LANES), U32),
                            pltpu.VMEM((MOE_TILE * ROW_CHUNKS, LANES), U32),
                            pltpu.SemaphoreType.DMA((2,)),
                            pltpu.SemaphoreType.DMA(())]),
        out_shape=jax.ShapeDtypeStruct((out_rows, LANES), U32),
        compiler_params=pltpu.CompilerParams(
            dimension_semantics=("arbitrary",), vmem_limit_bytes=VMEM_LIMIT),
        name="dispatch",
    )(*tables, hn_p, hn_s, rt_p, rt_s, lstv)


def _moe_kernel(te_ref, nu_ref, par_ref, nxt_ref, half_ref, xs_ref, wg_hbm, wu_hbm, wd_hbm, ys_ref,
                wg_f32, wu_f32, wd_f32, wg_bf, wu_bf, wd_bf, wsem):
    i = pl.program_id(0)
    used = i < nu_ref[0]

    def store_zeros(ref, n_rows):
        ij = (lax.broadcasted_iota(I32, (n_rows, D_MODEL), 0)
              + lax.broadcasted_iota(I32, (n_rows, D_MODEL), 1))
        _store_packed(ref, jnp.where(ij < jnp.minimum(nu_ref[0], 0), 1.0, 0.0), n_rows)

    def weight_copies(expert, buf):
        return [pltpu.make_async_copy(hbm.at[expert], vmem.at[buf], wsem.at[buf])
                for hbm, vmem in ((wg_hbm, wg_f32), (wu_hbm, wu_f32), (wd_hbm, wd_f32))]

    @pl.when(used)
    def _():
        prev = te_ref[jnp.maximum(i - 1, 0)]

        @pl.when((i == 0) | (te_ref[i] != prev))
        def _():
            buf = par_ref[i]

            @pl.when(i == 0)
            def _():
                for cp in weight_copies(te_ref[0], 0):
                    cp.start()

            for cp in weight_copies(te_ref[i], buf):
                cp.wait()
            wg_bf[...] = wg_f32[buf].astype(BF16)
            wu_bf[...] = wu_f32[buf].astype(BF16)
            wd_bf[...] = wd_f32[buf].astype(BF16)

            @pl.when(nxt_ref[i] >= 0)
            def _():
                for cp in weight_copies(nxt_ref[i], 1 - buf):
                    cp.start()

        def expert_rows(n_rows):
            lo, hi = _load_packed(xs_ref, n_rows)
            x = jnp.concatenate([lo, hi], axis=-1).astype(BF16)
            g = jnp.dot(x, wg_bf[...], preferred_element_type=F32)
            u = jnp.dot(x, wu_bf[...], preferred_element_type=F32)
            hh = (g * jax.nn.sigmoid(g)) * u
            y = jnp.dot(hh.astype(BF16), wd_bf[...], preferred_element_type=F32)
            _store_packed(ys_ref, y, n_rows)

        @pl.when(half_ref[i] == 0)
        def _():
            expert_rows(MOE_TILE)

        @pl.when(half_ref[i] != 0)
        def _():
            expert_rows(MOE_TILE // 2)
            store_zeros(ys_ref.at[pl.ds(MOE_TILE // 2 * ROW_CHUNKS, MOE_TILE // 2 * ROW_CHUNKS)],
                        MOE_TILE // 2)

    @pl.when(jnp.logical_not(used))
    def _():
        store_zeros(ys_ref, MOE_TILE)


def _moe_call(tile_expert, n_used, buf_parity, next_expert, half_only, xs, n_tiles, w_eg, w_eu, w_ed):
    blk = MOE_TILE * ROW_CHUNKS
    any_spec = pl.BlockSpec(memory_space=pl.ANY)
    return pl.pallas_call(
        _moe_kernel,
        grid_spec=pltpu.PrefetchScalarGridSpec(
            num_scalar_prefetch=5, grid=(n_tiles,),
            in_specs=[
                pl.BlockSpec((blk, LANES), lambda i, te, nu, *_: (jnp.minimum(i, nu[0] - 1), 0)),
                any_spec, any_spec, any_spec,
            ],
            out_specs=pl.BlockSpec((blk, LANES), lambda i, *_: (i, 0)),
            scratch_shapes=[pltpu.VMEM((2, D_MODEL, D_EXPERT), F32),
                            pltpu.VMEM((2, D_MODEL, D_EXPERT), F32),
                            pltpu.VMEM((2, D_EXPERT, D_MODEL), F32),
                            pltpu.VMEM((D_MODEL, D_EXPERT), BF16),
                            pltpu.VMEM((D_MODEL, D_EXPERT), BF16),
                            pltpu.VMEM((D_EXPERT, D_MODEL), BF16),
                            pltpu.SemaphoreType.DMA((2,))]),
        out_shape=jax.ShapeDtypeStruct((n_tiles * blk, LANES), U32),
        compiler_params=pltpu.CompilerParams(
            dimension_semantics=("arbitrary",), vmem_limit_bytes=VMEM_LIMIT),
        name="experts",
    )(tile_expert, n_used, buf_parity, next_expert, half_only, xs, w_eg, w_eu, w_ed)


def _final_kernel(l8, g8, l4, g4, l2, g2, cnt_ref, trow_ref,
                  hp_ref, hs_ref, rtp_ref, rts_ref, lst_ref, pp_ref, ps_ref,
                  ys_ref, wple_ref, wgate_ref, gple_ref, gfin_ref, outp_ref, outs_ref, ybuf, sem,
                  *, n_prompt_steps):
    i = pl.program_id(0)
    n_steps = pl.num_programs(0)
    slot = i % 2
    local_tabs, sorted_tabs = (l8, l4, l2), (g8, g4, g2)

    def run_copy(buf_slot, s, n_rows, k):
        src = ys_ref.at[_rows(sorted_tabs[s][k], n_rows)]
        dst = ybuf.at[buf_slot, _rows(local_tabs[s][k], n_rows)]
        return pltpu.make_async_copy(src, dst, sem.at[buf_slot])

    def issue(tile, buf_slot):
        _for_each_copy(tile, cnt_ref, lambda s, n_rows, k: run_copy(buf_slot, s, n_rows, k).start())

    @pl.when(i == 0)
    def _():
        ybuf[...] = jnp.zeros(ybuf.shape, U32)
        issue(0, 0)

    @pl.when(i + 1 < n_steps)
    def _():
        issue(i + 1, 1 - slot)

    def make_wait(n):
        return pltpu.make_async_copy(ys_ref.at[pl.ds(0, n * ROW_CHUNKS)],
                                     ybuf.at[slot, pl.ds(0, n * ROW_CHUNKS)], sem.at[slot])
    _wait_rows(trow_ref[i], make_wait)

    is_prompt = i < n_prompt_steps
    h = jnp.where(is_prompt, hp_ref[...], hs_ref[...])
    p = jnp.where(is_prompt, pp_ref[...], ps_ref[...])
    rt = jnp.where(is_prompt, rtp_ref[...], rts_ref[...])
    lpos1, lpos2 = _local_positions(rt, lst_ref[...])
    lane = lax.broadcasted_iota(I32, (TOK_TILE, LOC_ROWS), 1).astype(F32)
    sel1 = lane == lpos1
    sel2 = lane == lpos2
    wlane = lax.broadcasted_iota(I32, (TOK_TILE, LANES), 1)

    def hi_lo_cols(w):
        w_hi = w.astype(BF16).astype(F32)
        return jnp.where(wlane == 0, w_hi, jnp.where(wlane == 1, w - w_hi, 0.0)).astype(BF16)

    dn = (((0,), (0,)), ((), ()))
    wrow = (lax.dot_general(jnp.where(sel1, 1.0, 0.0).astype(BF16), hi_lo_cols(rt[:, 4:5]), dn,
                            preferred_element_type=F32)
            + lax.dot_general(jnp.where(sel2, 1.0, 0.0).astype(BF16), hi_lo_cols(rt[:, 5:6]), dn,
                              preferred_element_type=F32))
    wrow = wrow[:, 0:1] + wrow[:, 1:2]
    lo, hi = _load_packed(ybuf.at[slot], LOC_ROWS)
    yb = (jnp.concatenate([lo, hi], axis=-1) * wrow).astype(BF16)
    perm = jnp.where(sel1 | sel2, 1.0, 0.0).astype(BF16)
    h = h + jnp.dot(perm, yb, preferred_element_type=F32)
    r = _rms(h, gple_ref[...])
    gate = jax.nn.sigmoid(jnp.dot(r.astype(BF16), wgate_ref[...], preferred_element_type=F32))
    h = h + jnp.dot(p.astype(BF16), wple_ref[...], preferred_element_type=F32) * gate
    y = _rms(h, gfin_ref[...])

    @pl.when(is_prompt)
    def _():
        outp_ref[...] = y

    @pl.when(jnp.logical_not(is_prompt))
    def _():
        outs_ref[...] = y


def _final_call(tables, h_p, h_s, rt_p, rt_s, lstv, p_p, p_s, ys, wple, wgate, gple, gfin):
    n_p, n_s = h_p.shape[0], h_s.shape[0]
    np_steps = n_p // TOK_TILE
    n_steps = np_steps + n_s // TOK_TILE
    pmap, smap = _clamped_maps(np_steps)
    cmap = lambda i, *_: (0, 0)
    return pl.pallas_call(
        functools.partial(_final_kernel, n_prompt_steps=np_steps),
        grid_spec=pltpu.PrefetchScalarGridSpec(
            num_scalar_prefetch=len(tables), grid=(n_steps,),
            in_specs=[
                pl.BlockSpec((TOK_TILE, D_MODEL), pmap),
                pl.BlockSpec((TOK_TILE, D_MODEL), smap),
                pl.BlockSpec((TOK_TILE, SUBLANES), pmap),
                pl.BlockSpec((TOK_TILE, SUBLANES), smap),
                pl.BlockSpec((None, 1, LANES), lambda i, *_: (i, 0, 0)),
                pl.BlockSpec((TOK_TILE, PLE_DIM), pmap),
                pl.BlockSpec((TOK_TILE, PLE_DIM), smap),
                pl.BlockSpec(memory_space=pl.ANY),
                pl.BlockSpec(wple.shape, cmap),
                pl.BlockSpec(wgate.shape, cmap),
                pl.BlockSpec(gple.shape, cmap),
                pl.BlockSpec(gfin.shape, cmap),
            ],
            out_specs=[pl.BlockSpec((TOK_TILE, D_MODEL), pmap),
                       pl.BlockSpec((TOK_TILE, D_MODEL), smap)],
            scratch_shapes=[pltpu.VMEM((2, LOC_ROWS * ROW_CHUNKS, LANES), U32),
                            pltpu.SemaphoreType.DMA((2,))]),
        out_shape=[jax.ShapeDtypeStruct((n_p, D_MODEL), F32),
                   jax.ShapeDtypeStruct((n_s, D_MODEL), F32)],
        compiler_params=pltpu.CompilerParams(
            dimension_semantics=("arbitrary",), vmem_limit_bytes=VMEM_LIMIT),
        name="final",
    )(*tables, h_p, h_s, rt_p, rt_s, lstv, p_p, p_s, ys, wple, wgate, gple, gfin)


def _ceil_to(x, m):
    return ((x + m - 1) // m) * m


def _excl_cumsum(a, axis):
    return jnp.cumsum(a, axis=axis) - a


def _copy_tables(n_copies, first_local, first_sorted, step, cap):
    ends = jnp.cumsum(n_copies, axis=1)
    j = jnp.arange(cap, dtype=I32)[None, :, None]
    e_of = jnp.minimum(jnp.sum(ends[:, None, :] <= j, axis=-1), N_EXPERTS - 1)
    pick = e_of[:, :, None] == jnp.arange(N_EXPERTS, dtype=I32)[None, None, :]
    take = lambda a: jnp.sum(jnp.where(pick, a[:, None, :], 0), axis=-1)
    c = jnp.arange(cap, dtype=I32)[None, :] - (take(ends) - take(n_copies))
    flat = lambda a: a.reshape(-1).astype(I32)
    return flat(take(first_local) + step * c), flat(take(first_sorted) + step * c), ends[:, -1]


def kernel(x_prompt, x_sample, state_pool, p_prompt, p_sample, g_mix, w_in, w_pool, pool_scale,
           g_v, b_v, w_spatial, b_spatial, w_out, g_ffn, w_rg, b_rg, w_re, b_re, w_eg, w_eu, w_ed,
           w_ple, g_ple, w_ple_gate, g_final):
    batch, seq, _ = x_prompt.shape
    dec_batch, dec_seq, _ = x_sample.shape
    assert dec_seq == SUBLANES and g_mix.shape[0] == 1
    n_p = batch * seq
    n_s = dec_batch * dec_seq
    n_tok = n_p + n_s
    past_len = 16384

    row = lambda a: a.reshape(1, -1).astype(F32)
    wr = jnp.concatenate(
        [w_re[0], w_rg[0], jnp.zeros((D_MODEL, LANES - N_EXPERTS - N_EXPERT_GROUPS), F32)], axis=1)
    wr_hi = wr.astype(BF16)
    wr_lo = (wr - wr_hi.astype(F32)).astype(BF16)
    br = jnp.concatenate(
        [b_re[0], b_rg[0], jnp.zeros((LANES - N_EXPERTS - N_EXPERT_GROUPS,), F32)]).reshape(1, LANES)
    bsp_p = jnp.repeat(jnp.transpose(b_spatial[0]), GMLP_HEAD, axis=1)
    bsp_s = jnp.tile(bsp_p[:dec_seq], (CHUNK // dec_seq, 1))
    wsp_p = w_spatial[0].astype(BF16)
    wsp_s = jnp.tile(w_spatial[0][:, :dec_seq, :dec_seq].astype(BF16),
                     (1, CHUNK // dec_seq, CHUNK // dec_seq))

    def mixer_weights(wsp, bsp):
        return (row(g_mix[0]), w_in[0].astype(BF16), w_pool[0].astype(BF16), row(pool_scale[0]),
                row(g_v[0]), row(b_v[0]), wsp, bsp, w_out[0].astype(BF16), row(g_ffn[0]),
                jnp.concatenate([wr_hi, wr_lo], axis=1), wr_hi, br)

    h_p, hn_p, rt_p, cnt_p, hist_p = _mixer_call(
        x_prompt, None, mixer_weights(wsp_p, bsp_p), sample=False, pos0=0)
    h_s, hn_s, rt_s, cnt_s, hist_s, v_s = _mixer_call(
        x_sample.reshape(n_s, D_MODEL), state_pool[0], mixer_weights(wsp_s, bsp_s),
        sample=True, pos0=past_len)

    n_run = jnp.concatenate([cnt_p, cnt_s], axis=0)[:, 0, :N_EXPERTS].astype(I32)
    n_even = _ceil_to(n_run, RUN_ALIGN)
    l_start = _excl_cumsum(n_even, 1)
    tile_rows = jnp.sum(n_even, axis=1)
    base = _excl_cumsum(n_even, 0)
    counts = jnp.sum(n_even, axis=0)
    padded = _ceil_to(counts, MOE_TILE)
    ends = jnp.cumsum(padded)
    offs = ends - padded
    g_start = offs[None, :] + base
    n8 = n_even // 8
    f4 = (n_even // 4) % 2
    f2 = (n_even // 2) % 2
    l8, g8, c8 = _copy_tables(n8, l_start, g_start, 8, COPY_CAP[0])
    l4, g4, c4 = _copy_tables(f4, l_start + 8 * n8, g_start + 8 * n8, 0, COPY_CAP[1])
    l2, g2, c2 = _copy_tables(f2, l_start + 8 * n8 + 4 * f4, g_start + 8 * n8 + 4 * f4, 0, COPY_CAP[2])
    copy_cnt = jnp.stack([c8, c4, c2], axis=1).reshape(-1).astype(I32)
    tile_rows = tile_rows.astype(I32)
    z_start = (offs + counts).astype(I32)
    z_blocks = ((padded - counts + 7) // 8).astype(I32)
    lstv = jnp.zeros((n_run.shape[0], 1, LANES), F32).at[:, 0, :N_EXPERTS].set(l_start.astype(F32))

    max_rows = 2 * n_tok + n_run.shape[0] * N_EXPERTS * (RUN_ALIGN - 1)
    n_tiles = max_rows // MOE_TILE + N_EXPERTS + 1
    n_used = (ends[-1] // MOE_TILE).astype(I32).reshape(1)
    tile_id = jnp.minimum(jnp.arange(n_tiles, dtype=I32), n_used - 1)
    tile_expert = jnp.sum((tile_id[:, None] * MOE_TILE) >= ends[None, :], axis=1).astype(I32)
    tile_expert = jnp.minimum(tile_expert, N_EXPERTS - 1)
    pick_e = tile_expert[:, None] == jnp.arange(N_EXPERTS, dtype=I32)[None, :]
    data_end = jnp.sum(jnp.where(pick_e, (offs + counts)[None, :], 0), axis=1)
    half_only = (data_end - tile_id * MOE_TILE <= MOE_TILE // 2).astype(I32)
    e_ids = jnp.arange(N_EXPERTS, dtype=I32)
    nonempty = padded > 0
    seg_index = jnp.cumsum(nonempty.astype(I32)) - 1
    later = nonempty[None, :] & (e_ids[None, :] > e_ids[:, None])
    next_of = jnp.min(jnp.where(later, e_ids[None, :], N_EXPERTS), axis=1)
    next_of = jnp.where(next_of == N_EXPERTS, -1, next_of)
    buf_parity = (jnp.sum(jnp.where(pick_e, seg_index[None, :], 0), axis=1) % 2).astype(I32)
    next_expert = jnp.sum(jnp.where(pick_e, next_of[None, :], 0), axis=1).astype(I32)

    copy_tabs = (l8, g8, l4, g4, l2, g2, copy_cnt, tile_rows)
    xs = _dispatch_call(copy_tabs + (z_start, z_blocks, n_used), hn_p, hn_s, rt_p, rt_s, lstv,
                        n_tiles * MOE_TILE)
    ys = _moe_call(tile_expert, n_used, buf_parity, next_expert, half_only, xs, n_tiles,
                   w_eg[0], w_eu[0], w_ed[0])
    y_p, y_s = _final_call(copy_tabs, h_p, h_s, rt_p, rt_s, lstv,
                           p_prompt[0].reshape(n_p, PLE_DIM), p_sample[0].reshape(n_s, PLE_DIM), ys,
                           w_ple[0].astype(BF16), w_ple_gate[0].astype(BF16), row(g_ple[0]), row(g_final))

    return (y_p.reshape(batch, seq, D_MODEL),
            y_s.reshape(dec_batch, dec_seq, D_MODEL),
            hist_p[None],
            hist_s[None],
            v_s.reshape(1, dec_batch, dec_seq, GMLP_WIDTH))
```

```python
import functools

import jax
import jax.numpy as jnp
from jax import lax
from jax.experimental import pallas as pl
from jax.experimental.pallas import tpu as pltpu

D_MODEL = 1024
POOL_WIDTH = 512
GMLP_WIDTH = 512
IN_WIDTH = POOL_WIDTH + 2 * GMLP_WIDTH
POOL_WINDOWS = (2, 4, 8, 16)
POOL_GROUP = 128
POOL_HIST = 15
CHUNK = 128
POOL_SPAN = CHUNK + 16
N_GMLP_HEADS = 4
GMLP_HEAD = 128
N_EXPERT_GROUPS = 4
EXPERTS_PER_GROUP = 8
N_EXPERTS = 32
D_EXPERT = 512
PLE_DIM = 256
EPS = 1e-6

LANES = 128
SUBLANES = 8
HALF = D_MODEL // 2
ROW_CHUNKS = HALF // LANES

TOK_TILE = 256
MIX_TILE = 512
MOE_TILE = 512
RUN_ALIGN = 2
COPY_ROWS = (8, 4, 2)
LOC_ROWS = 2 * TOK_TILE + N_EXPERTS * (RUN_ALIGN - 1)
COPY_CAP = (LOC_ROWS // 8 + 1, N_EXPERTS, N_EXPERTS)
VMEM_LIMIT = 56 * 1024 * 1024

F32 = jnp.float32
BF16 = jnp.bfloat16
I32 = jnp.int32
U32 = jnp.uint32


def _rms(x, g):
    return x * lax.rsqrt(jnp.mean(x * x, axis=-1, keepdims=True) + EPS) * g


def _gelu_tanh(x):
    inner = x * (0.7978845608028654 + (0.7978845608028654 * 0.044715) * (x * x))
    half = 0.5 * x
    return half + half * jnp.tanh(inner)


def _store_packed(ref, val, n_rows):
    packed = pltpu.pack_elementwise([val[:, :HALF], val[:, HALF:]], packed_dtype=BF16)
    for k in range(ROW_CHUNKS):
        ref[pl.ds(k, n_rows, stride=ROW_CHUNKS), :] = packed[:, k * LANES:(k + 1) * LANES]


def _load_packed(ref, n_rows):
    packed = jnp.concatenate(
        [ref[pl.ds(k, n_rows, stride=ROW_CHUNKS), :] for k in range(ROW_CHUNKS)], axis=-1)
    lo = pltpu.unpack_elementwise(packed, index=0, packed_dtype=BF16, unpacked_dtype=F32)
    hi = pltpu.unpack_elementwise(packed, index=1, packed_dtype=BF16, unpacked_dtype=F32)
    return lo, hi


def _rows(start, n):
    return pl.ds(pl.multiple_of(start * ROW_CHUNKS, SUBLANES), n * ROW_CHUNKS)


def _route(hn, wr_ref, br_ref, n_tok):
    logits = jnp.dot(hn, wr_ref[...], preferred_element_type=F32) + br_ref[...]

    lane = lax.broadcasted_iota(I32, (n_tok, LANES), 1).astype(F32)
    neg = jnp.float32(-jnp.inf)
    big = jnp.float32(1e9)
    gmask = (lane >= N_EXPERTS) & (lane < N_EXPERTS + N_EXPERT_GROUPS)
    gl = jnp.where(gmask, logits, neg)
    gmax = jnp.max(gl, axis=-1, keepdims=True)
    gidx = jnp.min(jnp.where(gl == gmax, lane, big), axis=-1, keepdims=True) - N_EXPERTS
    gden = jnp.sum(jnp.exp(gl - gmax), axis=-1, keepdims=True)
    gprob = 1.0 / gden

    lo = gidx * EXPERTS_PER_GROUP
    emask = (lane >= lo) & (lane < lo + EXPERTS_PER_GROUP)
    el = jnp.where(emask, logits, neg)
    m1 = jnp.max(el, axis=-1, keepdims=True)
    i1 = jnp.min(jnp.where(el == m1, lane, big), axis=-1, keepdims=True)
    el2 = jnp.where(lane == i1, neg, el)
    m2 = jnp.max(el2, axis=-1, keepdims=True)
    i2 = jnp.min(jnp.where(el2 == m2, lane, big), axis=-1, keepdims=True)
    t = jnp.exp(m2 - m1)
    w1 = gprob / (1.0 + t)
    w2 = gprob * t / (1.0 + t)

    sel1 = lane == i1
    sel2 = lane == i2
    oh = jnp.where(sel1 | sel2, 1.0, 0.0).astype(BF16)
    r = lax.broadcasted_iota(I32, (n_tok, n_tok), 0)
    c = lax.broadcasted_iota(I32, (n_tok, n_tok), 1)
    tri = jnp.where(c < r, 1.0, 0.0).astype(BF16)
    before = jnp.dot(tri, oh, preferred_element_type=F32)
    counts = jnp.sum(oh.astype(F32), axis=0, keepdims=True)
    lrank1 = jnp.sum(jnp.where(sel1, before, 0.0), axis=-1, keepdims=True)
    lrank2 = jnp.sum(jnp.where(sel2, before, 0.0), axis=-1, keepdims=True)

    out = jnp.where(lane == 0, i1, 0.0)
    for k, val in enumerate((i2, lrank1, lrank2, w1, w2), start=1):
        out = jnp.where(lane == k, val, out)
    return out, counts


def _mixer_kernel(*refs, n_tok, sample, pos0):
    if sample:
        (x_ref, hist_ref, gmix_ref, win_ref, wpool_ref, pscale_ref, gv_ref, bv_ref, wsp_ref,
         bsp_ref, wout_ref, gffn_ref, wr_ref, br_ref, band_ref, band0_ref,
         h_ref, hn_ref, rt_ref, cnt_ref, hist_out_ref, v_ref, ext_scr) = refs
    else:
        (x_ref, gmix_ref, win_ref, wpool_ref, pscale_ref, gv_ref, bv_ref, wsp_ref,
         bsp_ref, wout_ref, gffn_ref, wr_ref, br_ref, band_ref, band0_ref,
         h_ref, hn_ref, rt_ref, cnt_ref, hist_out_ref, ext_scr) = refs

    x = x_ref[...]
    xn = _rms(x, gmix_ref[...])
    z = jnp.dot(xn.astype(BF16), win_ref[...], preferred_element_type=F32)
    a = z[:, :POOL_WIDTH]
    uv = _gelu_tanh(z[:, POOL_WIDTH:])
    u = uv[:, :GMLP_WIDTH]
    v = uv[:, GMLP_WIDTH:]
    mu = jnp.mean(v, axis=-1, keepdims=True)
    vc = v - mu
    v = vc * lax.rsqrt(jnp.mean(vc * vc, axis=-1, keepdims=True) + EPS) * gv_ref[...] + bv_ref[...]

    pooled = []
    if sample:
        n_seq = n_tok // SUBLANES
        ext_scr[:, 1:1 + POOL_HIST, :] = hist_ref[...]
        ext_scr[:, 1 + POOL_HIST:, :] = a.reshape(n_seq, SUBLANES, POOL_WIDTH)
        hist_out_ref[...] = ext_scr[:, 1 + SUBLANES:, :]
        pos = pos0 + lax.broadcasted_iota(I32, (n_seq, SUBLANES, 1), 1)
        for gi, w in enumerate(POOL_WINDOWS):
            cols = slice(gi * POOL_GROUP, (gi + 1) * POOL_GROUP)
            win = ext_scr[:, 16:16 + SUBLANES, cols]
            for j in range(1, w):
                win = win + ext_scr[:, 16 - j:16 - j + SUBLANES, cols]
            cnt = jnp.minimum(w, pos + 1).astype(F32)
            pooled.append((win / cnt).reshape(n_tok, POOL_GROUP) - a[:, cols])
    else:
        l = pl.program_id(1)

        @pl.when(l == 0)
        def _():
            ext_scr[0:16, :] = jnp.zeros((16, POOL_WIDTH), F32)

        ext_scr[16:16 + n_tok, :] = a
        chunks = []
        for ci in range(n_tok // CHUNK):
            ext_c = ext_scr[ci * CHUNK:ci * CHUNK + POOL_SPAN, :].astype(BF16)
            groups = []
            for gi in range(len(POOL_WINDOWS)):
                band = band_ref[gi]
                if ci == 0:
                    band = jnp.where(l == 0, band0_ref[gi], band)
                groups.append(jnp.dot(band, ext_c[:, gi * POOL_GROUP:(gi + 1) * POOL_GROUP],
                                      preferred_element_type=F32))
            chunks.append(groups)
        pooled = [jnp.concatenate([chunks[ci][gi] for ci in range(n_tok // CHUNK)], axis=0)
                  for gi in range(len(POOL_WINDOWS))]
        tail = ext_scr[n_tok + 1:n_tok + 16, :]
        hist_out_ref[...] = tail
        ext_scr[1:16, :] = tail

    pool_out = jnp.concatenate(
        [jnp.dot(pooled[gi].astype(BF16), wpool_ref[gi], preferred_element_type=F32)
         for gi in range(len(POOL_WINDOWS))], axis=-1) * pscale_ref[...]

    blk = SUBLANES if sample else CHUNK
    tr = lax.broadcasted_iota(I32, (CHUNK, CHUNK), 0)
    sc = lax.broadcasted_iota(I32, (CHUNK, CHUNK), 1)
    mask = (sc <= tr) & ((sc // blk) == (tr // blk))
    wsp = [jnp.where(mask, wsp_ref[hh], jnp.zeros((), BF16)) for hh in range(N_GMLP_HEADS)]
    v_bf = v.astype(BF16)
    mixed_chunks = []
    for ci in range(n_tok // CHUNK):
        rows = slice(ci * CHUNK, (ci + 1) * CHUNK)
        heads = [jnp.dot(wsp[hh], v_bf[rows, hh * GMLP_HEAD:(hh + 1) * GMLP_HEAD],
                         preferred_element_type=F32) for hh in range(N_GMLP_HEADS)]
        mixed_chunks.append(jnp.concatenate(heads, axis=-1) + bsp_ref[...])
    mixed = jnp.concatenate(mixed_chunks, axis=0)
    gmlp_out = u * mixed

    cat = jnp.concatenate([pool_out, gmlp_out], axis=-1)
    h = x + jnp.dot(cat.astype(BF16), wout_ref[...], preferred_element_type=F32)
    hn = _rms(h, gffn_ref[...]).astype(BF16)

    h_ref[...] = h
    hn_ref[...] = hn
    if sample:
        v_ref[...] = v
    for s in range(n_tok // TOK_TILE):
        rows = slice(s * TOK_TILE, (s + 1) * TOK_TILE)
        rt, counts = _route(hn[rows], wr_ref, br_ref, TOK_TILE)
        rt_ref[rows, :] = rt[:, :SUBLANES]
        cnt_ref[s] = counts


def _full(shape):
    return pl.BlockSpec(shape, lambda *_: (0,) * len(shape))


def _mixer_call(x, hist, weights, *, sample, pos0):
    w_specs = [_full(w.shape) for w in weights]
    n_tok = MIX_TILE
    if sample:
        n_rows = x.shape[0]
        n_seq = n_tok // SUBLANES
        grid = (n_rows // n_tok,)
        tok_map = lambda i: (i, 0)
        tile_map = lambda i: (i, 0, 0)
        in_specs = [pl.BlockSpec((n_tok, D_MODEL), tok_map),
                    pl.BlockSpec((n_seq, POOL_HIST, POOL_WIDTH), lambda i: (i, 0, 0))] + w_specs
        hist_shape = jax.ShapeDtypeStruct(hist.shape, F32)
        hist_spec = pl.BlockSpec((n_seq, POOL_HIST, POOL_WIDTH), lambda i: (i, 0, 0))
        ext = pltpu.VMEM((n_seq, 16 + SUBLANES, POOL_WIDTH), F32)
        args = (x, hist)
    else:
        b, seq, _ = x.shape
        n_rows = b * seq
        n_l = seq // n_tok
        grid = (b, n_l)
        tok_map = lambda bi, li: (bi * n_l + li, 0)
        tile_map = lambda bi, li: (bi * n_l + li, 0, 0)
        in_specs = [pl.BlockSpec((None, n_tok, D_MODEL), lambda bi, li: (bi, li, 0))] + w_specs
        hist_shape = jax.ShapeDtypeStruct((b, POOL_HIST, POOL_WIDTH), F32)
        hist_spec = pl.BlockSpec((None, POOL_HIST, POOL_WIDTH), lambda bi, li: (bi, 0, 0))
        ext = pltpu.VMEM((16 + n_tok, POOL_WIDTH), F32)
        args = (x,)
    out_shape = [jax.ShapeDtypeStruct((n_rows, D_MODEL), F32),
                 jax.ShapeDtypeStruct((n_rows, D_MODEL), BF16),
                 jax.ShapeDtypeStruct((n_rows, SUBLANES), F32),
                 jax.ShapeDtypeStruct((n_rows // TOK_TILE, 1, LANES), F32),
                 hist_shape]
    out_specs = [pl.BlockSpec((n_tok, D_MODEL), tok_map),
                 pl.BlockSpec((n_tok, D_MODEL), tok_map),
                 pl.BlockSpec((n_tok, SUBLANES), tok_map),
                 pl.BlockSpec((n_tok // TOK_TILE, 1, LANES), tile_map),
                 hist_spec]
    if sample:
        out_shape.append(jax.ShapeDtypeStruct((n_rows, GMLP_WIDTH), F32))
        out_specs.append(pl.BlockSpec((n_tok, GMLP_WIDTH), tok_map))
    return pl.pallas_call(
        functools.partial(_mixer_kernel, n_tok=n_tok, sample=sample, pos0=pos0),
        grid=grid, in_specs=in_specs, out_specs=out_specs, out_shape=out_shape,
        scratch_shapes=[ext],
        compiler_params=pltpu.CompilerParams(
            dimension_semantics=("arbitrary",) * len(grid), vmem_limit_bytes=VMEM_LIMIT),
        name="mixer_sample" if sample else "mixer_prompt",
    )(*args, *weights)


def _for_each_copy(tile, cnt_ref, fn):
    for s, n_rows in enumerate(COPY_ROWS):
        def body(j, carry, s=s, n_rows=n_rows):
            fn(s, n_rows, tile * COPY_CAP[s] + j)
            return carry
        lax.fori_loop(0, cnt_ref[tile * len(COPY_ROWS) + s], body, 0)


def _wait_rows(n_rows_total, make_wait):
    units = n_rows_total // RUN_ALIGN
    bit = 0
    while (RUN_ALIGN << bit) <= LOC_ROWS:
        @pl.when((units >> bit) & 1 == 1)
        def _(bit=bit):
            make_wait(RUN_ALIGN << bit).wait()
        bit += 1


def _as_rows(*cols):
    n = cols[0].shape[0]
    lane = lax.broadcasted_iota(I32, (n, LANES), 1)
    packed = jnp.zeros((n, LANES), F32)
    for k, col in enumerate(cols):
        packed = jnp.where(lane == k, col, packed)
    rows = packed.T
    return [rows[k:k + 1, :] for k in range(len(cols))]


def _local_positions(rt, lstart_row):
    elane = lax.broadcasted_iota(I32, (rt.shape[0], LANES), 1).astype(F32)
    p1 = jnp.sum(jnp.where(elane == rt[:, 0:1], lstart_row, 0.0), axis=-1, keepdims=True) + rt[:, 2:3]
    p2 = jnp.sum(jnp.where(elane == rt[:, 1:2], lstart_row, 0.0), axis=-1, keepdims=True) + rt[:, 3:4]
    return p1, p2


def _dispatch_kernel(l8, g8, l4, g4, l2, g2, cnt_ref, trow_ref, zs_ref, zb_ref, nu_ref,
                     hnp_ref, hns_ref, rtp_ref, rts_ref, lst_ref, xs_ref, loc, zbuf, sem, zsem,
                     *, n_prompt_steps):
    i = pl.program_id(0)
    n_steps = pl.num_programs(0)
    slot = i % 2
    local_tabs, sorted_tabs = (l8, l4, l2), (g8, g4, g2)

    @pl.when(i == 0)
    def _():
        zbuf[...] = jnp.zeros(zbuf.shape, U32)
        n_tiles = xs_ref.shape[0] // (MOE_TILE * ROW_CHUNKS)

        def zero_copy(e, c):
            return pltpu.make_async_copy(zbuf.at[pl.ds(0, 8 * ROW_CHUNKS)],
                                         xs_ref.at[_rows(zs_ref[e] + c * 8, 8)], zsem)

        def tail_copy(t):
            return pltpu.make_async_copy(zbuf, xs_ref.at[_rows(t * MOE_TILE, MOE_TILE)], zsem)

        def per_expert(e, total):
            def per_block(c, carry):
                zero_copy(e, c).start()
                return carry
            lax.fori_loop(0, zb_ref[e], per_block, 0)
            return total + zb_ref[e]

        total = lax.fori_loop(0, N_EXPERTS, per_expert, 0)

        def wait_body(_, carry):
            zero_copy(0, 0).wait()
            return carry
        lax.fori_loop(0, total, wait_body, 0)

        def tail_start(t, carry):
            tail_copy(t).start()
            return carry

        def tail_wait(t, carry):
            tail_copy(t).wait()
            return carry

        spill = pltpu.make_async_copy(zbuf.at[pl.ds(0, 8 * ROW_CHUNKS)],
                                      xs_ref.at[_rows(n_tiles * MOE_TILE, 8)], zsem)
        lax.fori_loop(nu_ref[0], n_tiles, tail_start, 0)
        spill.start()
        lax.fori_loop(nu_ref[0], n_tiles, tail_wait, 0)
        spill.wait()

    is_prompt = i < n_prompt_steps
    rt = jnp.where(is_prompt, rtp_ref[...], rts_ref[...])
    lpos1, lpos2 = _local_positions(rt, lst_ref[...])
    row1, row2 = _as_rows(lpos1, lpos2)
    grow = lax.broadcasted_iota(I32, (LOC_ROWS, TOK_TILE), 0).astype(F32)
    perm = jnp.where((grow == row1) | (grow == row2), 1.0, 0.0).astype(BF16)
    hn = jnp.where(is_prompt, hnp_ref[...], hns_ref[...])
    grouped = jnp.dot(perm, hn, preferred_element_type=F32)
    _store_packed(loc.at[slot], grouped, LOC_ROWS)

    def run_copy(buf_slot, s, n_rows, k):
        src = loc.at[buf_slot, _rows(local_tabs[s][k], n_rows)]
        dst = xs_ref.at[_rows(sorted_tabs[s][k], n_rows)]
        return pltpu.make_async_copy(src, dst, sem.at[buf_slot])

    _for_each_copy(i, cnt_ref, lambda s, n_rows, k: run_copy(slot, s, n_rows, k).start())

    def drain(tile, buf_slot):
        def make_wait(n):
            return pltpu.make_async_copy(loc.at[buf_slot, pl.ds(0, n * ROW_CHUNKS)],
                                         xs_ref.at[pl.ds(0, n * ROW_CHUNKS)], sem.at[buf_slot])
        _wait_rows(trow_ref[tile], make_wait)

    @pl.when(i > 0)
    def _():
        drain(i - 1, 1 - slot)

    @pl.when(i == n_steps - 1)
    def _():
        drain(i, slot)


def _clamped_maps(np_steps):
    pmap = lambda i, *_: (jnp.minimum(i, np_steps - 1), 0)
    smap = lambda i, *_: (jnp.maximum(i - np_steps, 0), 0)
    return pmap, smap


def _dispatch_call(tables, hn_p, hn_s, rt_p, rt_s, lstv, n_sorted_rows):
    n_p, n_s = hn_p.shape[0], hn_s.shape[0]
    np_steps = n_p // TOK_TILE
    n_steps = np_steps + n_s // TOK_TILE
    pmap, smap = _clamped_maps(np_steps)
    out_rows = (n_sorted_rows + 8) * ROW_CHUNKS
    return pl.pallas_call(
        functools.partial(_dispatch_kernel, n_prompt_steps=np_steps),
        grid_spec=pltpu.PrefetchScalarGridSpec(
            num_scalar_prefetch=len(tables), grid=(n_steps,),
            in_specs=[pl.BlockSpec((TOK_TILE, D_MODEL), pmap),
                      pl.BlockSpec((TOK_TILE, D_MODEL), smap),
                      pl.BlockSpec((TOK_TILE, SUBLANES), pmap),
                      pl.BlockSpec((TOK_TILE, SUBLANES), smap),
                      pl.BlockSpec((None, 1, LANES), lambda i, *_: (i, 0, 0))],
            out_specs=pl.BlockSpec(memory_space=pl.ANY),
            scratch_shapes=[pltpu.VMEM((2, LOC_ROWS * ROW_CHUNKS, LANES), U32),
                            pltpu.VMEM((MOE_TILE * ROW_CHUNKS, LANES), U32),
                            pltpu.SemaphoreType.DMA((2,)),
                            pltpu.SemaphoreType.DMA(())]),
        out_shape=jax.ShapeDtypeStruct((out_rows, LANES), U32),
        compiler_params=pltpu.CompilerParams(
            dimension_semantics=("arbitrary",), vmem_limit_bytes=VMEM_LIMIT),
        name="dispatch",
    )(*tables, hn_p, hn_s, rt_p, rt_s, lstv)


def _moe_kernel(te_ref, nu_ref, par_ref, nxt_ref, half_ref, xs_ref, wg_hbm, wu_hbm, wd_hbm, ys_ref,
                wg_f32, wu_f32, wd_f32, wg_bf, wu_bf, wd_bf, wsem):
    i = pl.program_id(0)
    used = i < nu_ref[0]

    def store_zeros(ref, n_rows):
        ij = (lax.broadcasted_iota(I32, (n_rows, D_MODEL), 0)
              + lax.broadcasted_iota(I32, (n_rows, D_MODEL), 1))
        _store_packed(ref, jnp.where(ij < jnp.minimum(nu_ref[0], 0), 1.0, 0.0), n_rows)

    def weight_copies(expert, buf):
        return [pltpu.make_async_copy(hbm.at[expert], vmem.at[buf], wsem.at[buf])
                for hbm, vmem in ((wg_hbm, wg_f32), (wu_hbm, wu_f32), (wd_hbm, wd_f32))]

    @pl.when(used)
    def _():
        prev = te_ref[jnp.maximum(i - 1, 0)]

        @pl.when((i == 0) | (te_ref[i] != prev))
        def _():
            buf = par_ref[i]

            @pl.when(i == 0)
            def _():
                for cp in weight_copies(te_ref[0], 0):
                    cp.start()

            for cp in weight_copies(te_ref[i], buf):
                cp.wait()
            wg_bf[...] = wg_f32[buf].astype(BF16)
            wu_bf[...] = wu_f32[buf].astype(BF16)
            wd_bf[...] = wd_f32[buf].astype(BF16)

            @pl.when(nxt_ref[i] >= 0)
            def _():
                for cp in weight_copies(nxt_ref[i], 1 - buf):
                    cp.start()

        def expert_rows(n_rows):
            lo, hi = _load_packed(xs_ref, n_rows)
            x = jnp.concatenate([lo, hi], axis=-1).astype(BF16)
            g = jnp.dot(x, wg_bf[...], preferred_element_type=F32)
            u = jnp.dot(x, wu_bf[...], preferred_element_type=F32)
            hh = (g * jax.nn.sigmoid(g)) * u
            y = jnp.dot(hh.astype(BF16), wd_bf[...], preferred_element_type=F32)
            _store_packed(ys_ref, y, n_rows)

        @pl.when(half_ref[i] == 0)
        def _():
            expert_rows(MOE_TILE)

        @pl.when(half_ref[i] != 0)
        def _():
            expert_rows(MOE_TILE // 2)
            store_zeros(ys_ref.at[pl.ds(MOE_TILE // 2 * ROW_CHUNKS, MOE_TILE // 2 * ROW_CHUNKS)],
                        MOE_TILE // 2)

    @pl.when(jnp.logical_not(used))
    def _():
        store_zeros(ys_ref, MOE_TILE)


def _moe_call(tile_expert, n_used, buf_parity, next_expert, half_only, xs, n_tiles, w_eg, w_eu, w_ed):
    blk = MOE_TILE * ROW_CHUNKS
    any_spec = pl.BlockSpec(memory_space=pl.ANY)
    return pl.pallas_call(
        _moe_kernel,
        grid_spec=pltpu.PrefetchScalarGridSpec(
            num_scalar_prefetch=5, grid=(n_tiles,),
            in_specs=[
                pl.BlockSpec((blk, LANES), lambda i, te, nu, *_: (jnp.minimum(i, nu[0] - 1), 0)),
                any_spec, any_spec, any_spec,
            ],
            out_specs=pl.BlockSpec((blk, LANES), lambda i, *_: (i, 0)),
            scratch_shapes=[pltpu.VMEM((2, D_MODEL, D_EXPERT), F32),
                            pltpu.VMEM((2, D_MODEL, D_EXPERT), F32),
                            pltpu.VMEM((2, D_EXPERT, D_MODEL), F32),
                            pltpu.VMEM((D_MODEL, D_EXPERT), BF16),
                            pltpu.VMEM((D_MODEL, D_EXPERT), BF16),
                            pltpu.VMEM((D_EXPERT, D_MODEL), BF16),
                            pltpu.SemaphoreType.DMA((2,))]),
        out_shape=jax.ShapeDtypeStruct((n_tiles * blk, LANES), U32),
        compiler_params=pltpu.CompilerParams(
            dimension_semantics=("arbitrary",), vmem_limit_bytes=VMEM_LIMIT),
        name="experts",
    )(tile_expert, n_used, buf_parity, next_expert, half_only, xs, w_eg, w_eu, w_ed)


def _final_kernel(l8, g8, l4, g4, l2, g2, cnt_ref, trow_ref,
                  hp_ref, hs_ref, rtp_ref, rts_ref, lst_ref, pp_ref, ps_ref,
                  ys_ref, wple_ref, wgate_ref, gple_ref, gfin_ref, outp_ref, outs_ref, ybuf, sem,
                  *, n_prompt_steps):
    i = pl.program_id(0)
    n_steps = pl.num_programs(0)
    slot = i % 2
    local_tabs, sorted_tabs = (l8, l4, l2), (g8, g4, g2)

    def run_copy(buf_slot, s, n_rows, k):
        src = ys_ref.at[_rows(sorted_tabs[s][k], n_rows)]
        dst = ybuf.at[buf_slot, _rows(local_tabs[s][k], n_rows)]
        return pltpu.make_async_copy(src, dst, sem.at[buf_slot])

    def issue(tile, buf_slot):
        _for_each_copy(tile, cnt_ref, lambda s, n_rows, k: run_copy(buf_slot, s, n_rows, k).start())

    @pl.when(i == 0)
    def _():
        ybuf[...] = jnp.zeros(ybuf.shape, U32)
        issue(0, 0)

    @pl.when(i + 1 < n_steps)
    def _():
        issue(i + 1, 1 - slot)

    def make_wait(n):
        return pltpu.make_async_copy(ys_ref.at[pl.ds(0, n * ROW_CHUNKS)],
                                     ybuf.at[slot, pl.ds(0, n * ROW_CHUNKS)], sem.at[slot])
    _wait_rows(trow_ref[i], make_wait)

    is_prompt = i < n_prompt_steps
    h = jnp.where(is_prompt, hp_ref[...], hs_ref[...])
    p = jnp.where(is_prompt, pp_ref[...], ps_ref[...])
    rt = jnp.where(is_prompt, rtp_ref[...], rts_ref[...])
    lpos1, lpos2 = _local_positions(rt, lst_ref[...])
    lane = lax.broadcasted_iota(I32, (TOK_TILE, LOC_ROWS), 1).astype(F32)
    sel1 = lane == lpos1
    sel2 = lane == lpos2
    row1, row2, w1_row, w2_row = _as_rows(lpos1, lpos2, rt[:, 4:5], rt[:, 5:6])
    grow = lax.broadcasted_iota(I32, (LOC_ROWS, TOK_TILE), 0).astype(F32)
    wrow = jnp.sum(jnp.where(grow == row1, w1_row, 0.0) + jnp.where(grow == row2, w2_row, 0.0),
                   axis=-1, keepdims=True)
    lo, hi = _load_packed(ybuf.at[slot], LOC_ROWS)
    yb = (jnp.concatenate([lo, hi], axis=-1) * wrow).astype(BF16)
    perm = jnp.where(sel1 | sel2, 1.0, 0.0).astype(BF16)
    h = h + jnp.dot(perm, yb, preferred_element_type=F32)
    r = _rms(h, gple_ref[...])
    gate = jax.nn.sigmoid(jnp.dot(r.astype(BF16), wgate_ref[...], preferred_element_type=F32))
    h = h + jnp.dot(p.astype(BF16), wple_ref[...], preferred_element_type=F32) * gate
    y = _rms(h, gfin_ref[...])

    @pl.when(is_prompt)
    def _():
        outp_ref[...] = y

    @pl.when(jnp.logical_not(is_prompt))
    def _():
        outs_ref[...] = y


def _final_call(tables, h_p, h_s, rt_p, rt_s, lstv, p_p, p_s, ys, wple, wgate, gple, gfin):
    n_p, n_s = h_p.shape[0], h_s.shape[0]
    np_steps = n_p // TOK_TILE
    n_steps = np_steps + n_s // TOK_TILE
    pmap, smap = _clamped_maps(np_steps)
    cmap = lambda i, *_: (0, 0)
    return pl.pallas_call(
        functools.partial(_final_kernel, n_prompt_steps=np_steps),
        grid_spec=pltpu.PrefetchScalarGridSpec(
            num_scalar_prefetch=len(tables), grid=(n_steps,),
            in_specs=[
                pl.BlockSpec((TOK_TILE, D_MODEL), pmap),
                pl.BlockSpec((TOK_TILE, D_MODEL), smap),
                pl.BlockSpec((TOK_TILE, SUBLANES), pmap),
                pl.BlockSpec((TOK_TILE, SUBLANES), smap),
                pl.BlockSpec((None, 1, LANES), lambda i, *_: (i, 0, 0)),
                pl.BlockSpec((TOK_TILE, PLE_DIM), pmap),
                pl.BlockSpec((TOK_TILE, PLE_DIM), smap),
                pl.BlockSpec(memory_space=pl.ANY),
                pl.BlockSpec(wple.shape, cmap),
                pl.BlockSpec(wgate.shape, cmap),
                pl.BlockSpec(gple.shape, cmap),
                pl.BlockSpec(gfin.shape, cmap),
            ],
            out_specs=[pl.BlockSpec((TOK_TILE, D_MODEL), pmap),
                       pl.BlockSpec((TOK_TILE, D_MODEL), smap)],
            scratch_shapes=[pltpu.VMEM((2, LOC_ROWS * ROW_CHUNKS, LANES), U32),
                            pltpu.SemaphoreType.DMA((2,))]),
        out_shape=[jax.ShapeDtypeStruct((n_p, D_MODEL), F32),
                   jax.ShapeDtypeStruct((n_s, D_MODEL), F32)],
        compiler_params=pltpu.CompilerParams(
            dimension_semantics=("arbitrary",), vmem_limit_bytes=VMEM_LIMIT),
        name="final",
    )(*tables, h_p, h_s, rt_p, rt_s, lstv, p_p, p_s, ys, wple, wgate, gple, gfin)


def _ceil_to(x, m):
    return ((x + m - 1) // m) * m


def _excl_cumsum(a, axis):
    return jnp.cumsum(a, axis=axis) - a


def _copy_tables(n_copies, first_local, first_sorted, step, cap):
    ends = jnp.cumsum(n_copies, axis=1)
    j = jnp.arange(cap, dtype=I32)[None, :, None]
    e_of = jnp.minimum(jnp.sum(ends[:, None, :] <= j, axis=-1), N_EXPERTS - 1)
    pick = e_of[:, :, None] == jnp.arange(N_EXPERTS, dtype=I32)[None, None, :]
    take = lambda a: jnp.sum(jnp.where(pick, a[:, None, :], 0), axis=-1)
    c = jnp.arange(cap, dtype=I32)[None, :] - (take(ends) - take(n_copies))
    flat = lambda a: a.reshape(-1).astype(I32)
    return flat(take(first_local) + step * c), flat(take(first_sorted) + step * c), ends[:, -1]


def kernel(x_prompt, x_sample, state_pool, p_prompt, p_sample, g_mix, w_in, w_pool, pool_scale,
           g_v, b_v, w_spatial, b_spatial, w_out, g_ffn, w_rg, b_rg, w_re, b_re, w_eg, w_eu, w_ed,
           w_ple, g_ple, w_ple_gate, g_final):
    batch, seq, _ = x_prompt.shape
    dec_batch, dec_seq, _ = x_sample.shape
    assert dec_seq == SUBLANES and g_mix.shape[0] == 1
    n_p = batch * seq
    n_s = dec_batch * dec_seq
    n_tok = n_p + n_s
    past_len = 16384

    row = lambda a: a.reshape(1, -1).astype(F32)
    wr = jnp.concatenate(
        [w_re[0], w_rg[0], jnp.zeros((D_MODEL, LANES - N_EXPERTS - N_EXPERT_GROUPS), F32)], axis=1)
    br = jnp.concatenate(
        [b_re[0], b_rg[0], jnp.zeros((LANES - N_EXPERTS - N_EXPERT_GROUPS,), F32)]).reshape(1, LANES)
    bsp_p = jnp.repeat(jnp.transpose(b_spatial[0]), GMLP_HEAD, axis=1)
    bsp_s = jnp.tile(bsp_p[:dec_seq], (CHUNK // dec_seq, 1))
    wsp_p = w_spatial[0].astype(BF16)
    wsp_s = jnp.tile(w_spatial[0][:, :dec_seq, :dec_seq].astype(BF16),
                     (1, CHUNK // dec_seq, CHUNK // dec_seq))

    def pool_band(first_chunk):
        t = jnp.arange(CHUNK, dtype=I32)[:, None]
        k = jnp.arange(POOL_SPAN, dtype=I32)[None, :]
        mats = []
        for w in POOL_WINDOWS:
            cnt = jnp.minimum(w, t + 1) if first_chunk else jnp.full_like(t, w)
            in_window = (k >= t + 17 - w) & (k <= t + 16)
            mats.append(jnp.where(in_window, 1.0 / cnt.astype(F32), 0.0) - (k == t + 16).astype(F32))
        return jnp.stack(mats).astype(BF16)

    def mixer_weights(wsp, bsp):
        return (row(g_mix[0]), w_in[0].astype(BF16), w_pool[0].astype(BF16), row(pool_scale[0]),
                row(g_v[0]), row(b_v[0]), wsp, bsp, w_out[0].astype(BF16), row(g_ffn[0]),
                wr.astype(BF16), br, pool_band(False), pool_band(True))

    h_p, hn_p, rt_p, cnt_p, hist_p = _mixer_call(
        x_prompt, None, mixer_weights(wsp_p, bsp_p), sample=False, pos0=0)
    h_s, hn_s, rt_s, cnt_s, hist_s, v_s = _mixer_call(
        x_sample.reshape(n_s, D_MODEL), state_pool[0], mixer_weights(wsp_s, bsp_s),
        sample=True, pos0=past_len)

    n_run = jnp.concatenate([cnt_p, cnt_s], axis=0)[:, 0, :N_EXPERTS].astype(I32)
    n_even = _ceil_to(n_run, RUN_ALIGN)
    l_start = _excl_cumsum(n_even, 1)
    tile_rows = jnp.sum(n_even, axis=1)
    base = _excl_cumsum(n_even, 0)
    counts = jnp.sum(n_even, axis=0)
    padded = _ceil_to(counts, MOE_TILE)
    ends = jnp.cumsum(padded)
    offs = ends - padded
    g_start = offs[None, :] + base
    n8 = n_even // 8
    f4 = (n_even // 4) % 2
    f2 = (n_even // 2) % 2
    l8, g8, c8 = _copy_tables(n8, l_start, g_start, 8, COPY_CAP[0])
    l4, g4, c4 = _copy_tables(f4, l_start + 8 * n8, g_start + 8 * n8, 0, COPY_CAP[1])
    l2, g2, c2 = _copy_tables(f2, l_start + 8 * n8 + 4 * f4, g_start + 8 * n8 + 4 * f4, 0, COPY_CAP[2])
    copy_cnt = jnp.stack([c8, c4, c2], axis=1).reshape(-1).astype(I32)
    tile_rows = tile_rows.astype(I32)
    z_start = (offs + counts).astype(I32)
    z_blocks = ((padded - counts + 7) // 8).astype(I32)
    lstv = jnp.zeros((n_run.shape[0], 1, LANES), F32).at[:, 0, :N_EXPERTS].set(l_start.astype(F32))

    max_rows = 2 * n_tok + n_run.shape[0] * N_EXPERTS * (RUN_ALIGN - 1)
    n_tiles = max_rows // MOE_TILE + N_EXPERTS + 1
    n_used = (ends[-1] // MOE_TILE).astype(I32).reshape(1)
    tile_id = jnp.minimum(jnp.arange(n_tiles, dtype=I32), n_used - 1)
    tile_expert = jnp.sum((tile_id[:, None] * MOE_TILE) >= ends[None, :], axis=1).astype(I32)
    tile_expert = jnp.minimum(tile_expert, N_EXPERTS - 1)
    pick_e = tile_expert[:, None] == jnp.arange(N_EXPERTS, dtype=I32)[None, :]
    data_end = jnp.sum(jnp.where(pick_e, (offs + counts)[None, :], 0), axis=1)
    half_only = (data_end - tile_id * MOE_TILE <= MOE_TILE // 2).astype(I32)
    e_ids = jnp.arange(N_EXPERTS, dtype=I32)
    nonempty = padded > 0
    seg_index = jnp.cumsum(nonempty.astype(I32)) - 1
    later = nonempty[None, :] & (e_ids[None, :] > e_ids[:, None])
    next_of = jnp.min(jnp.where(later, e_ids[None, :], N_EXPERTS), axis=1)
    next_of = jnp.where(next_of == N_EXPERTS, -1, next_of)
    buf_parity = (jnp.sum(jnp.where(pick_e, seg_index[None, :], 0), axis=1) % 2).astype(I32)
    next_expert = jnp.sum(jnp.where(pick_e, next_of[None, :], 0), axis=1).astype(I32)

    copy_tabs = (l8, g8, l4, g4, l2, g2, copy_cnt, tile_rows)
    xs = _dispatch_call(copy_tabs + (z_start, z_blocks, n_used), hn_p, hn_s, rt_p, rt_s, lstv,
                        n_tiles * MOE_TILE)
    ys = _moe_call(tile_expert, n_used, buf_parity, next_expert, half_only, xs, n_tiles,
                   w_eg[0], w_eu[0], w_ed[0])
    y_p, y_s = _final_call(copy_tabs, h_p, h_s, rt_p, rt_s, lstv,
                           p_prompt[0].reshape(n_p, PLE_DIM), p_sample[0].reshape(n_s, PLE_DIM), ys,
                           w_ple[0].astype(BF16), w_ple_gate[0].astype(BF16), row(g_ple[0]), row(g_final))

    return (y_p.reshape(batch, seq, D_MODEL),
            y_s.reshape(dec_batch, dec_seq, D_MODEL),
            hist_p[None],
            hist_s[None],
            v_s.reshape(1, dec_batch, dec_seq, GMLP_WIDTH))
```

```python
import functools

import jax
import jax.numpy as jnp
from jax import lax
from jax.experimental import pallas as pl
from jax.experimental.pallas import tpu as pltpu

D_MODEL = 1024
POOL_WIDTH = 512
GMLP_WIDTH = 512
IN_WIDTH = POOL_WIDTH + 2 * GMLP_WIDTH
POOL_WINDOWS = (2, 4, 8, 16)
POOL_GROUP = 128
POOL_HIST = 15
CHUNK = 128
POOL_SPAN = CHUNK + 16
N_GMLP_HEADS = 4
GMLP_HEAD = 128
N_EXPERT_GROUPS = 4
EXPERTS_PER_GROUP = 8
N_EXPERTS = 32
D_EXPERT = 512
PLE_DIM = 256
EPS = 1e-6

LANES = 128
SUBLANES = 8
HALF = D_MODEL // 2
ROW_CHUNKS = HALF // LANES

TOK_TILE = 256
MIX_TILE = 512
MOE_TILE = 512
RUN_ALIGN = 2
COPY_ROWS = (8, 4, 2)
LOC_ROWS = 2 * TOK_TILE + N_EXPERTS * (RUN_ALIGN - 1)
COPY_CAP = (LOC_ROWS // 8 + 1, N_EXPERTS, N_EXPERTS)
VMEM_LIMIT = 56 * 1024 * 1024

F32 = jnp.float32
BF16 = jnp.bfloat16
I32 = jnp.int32
U32 = jnp.uint32


def _rms(x, g):
    return x * lax.rsqrt(jnp.mean(x * x, axis=-1, keepdims=True) + EPS) * g


def _gelu_tanh(x):
    inner = x * (0.7978845608028654 + (0.7978845608028654 * 0.044715) * (x * x))
    half = 0.5 * x
    return half + half * jnp.tanh(inner)


def _store_packed(ref, val, n_rows):
    packed = pltpu.pack_elementwise([val[:, :HALF], val[:, HALF:]], packed_dtype=BF16)
    for k in range(ROW_CHUNKS):
        ref[pl.ds(k, n_rows, stride=ROW_CHUNKS), :] = packed[:, k * LANES:(k + 1) * LANES]


def _load_packed(ref, n_rows):
    packed = jnp.concatenate(
        [ref[pl.ds(k, n_rows, stride=ROW_CHUNKS), :] for k in range(ROW_CHUNKS)], axis=-1)
    lo = pltpu.unpack_elementwise(packed, index=0, packed_dtype=BF16, unpacked_dtype=F32)
    hi = pltpu.unpack_elementwise(packed, index=1, packed_dtype=BF16, unpacked_dtype=F32)
    return lo, hi


def _rows(start, n):
    return pl.ds(pl.multiple_of(start * ROW_CHUNKS, SUBLANES), n * ROW_CHUNKS)


def _route(hn, wr_ref, br_ref, n_tok):
    logits = jnp.dot(hn, wr_ref[...], preferred_element_type=F32) + br_ref[...]

    lane = lax.broadcasted_iota(I32, (n_tok, LANES), 1).astype(F32)
    neg = jnp.float32(-jnp.inf)
    big = jnp.float32(1e9)
    gmask = (lane >= N_EXPERTS) & (lane < N_EXPERTS + N_EXPERT_GROUPS)
    gl = jnp.where(gmask, logits, neg)
    gmax = jnp.max(gl, axis=-1, keepdims=True)
    gidx = jnp.min(jnp.where(gl == gmax, lane, big), axis=-1, keepdims=True) - N_EXPERTS
    gden = jnp.sum(jnp.exp(gl - gmax), axis=-1, keepdims=True)
    gprob = 1.0 / gden

    lo = gidx * EXPERTS_PER_GROUP
    emask = (lane >= lo) & (lane < lo + EXPERTS_PER_GROUP)
    el = jnp.where(emask, logits, neg)
    m1 = jnp.max(el, axis=-1, keepdims=True)
    i1 = jnp.min(jnp.where(el == m1, lane, big), axis=-1, keepdims=True)
    el2 = jnp.where(lane == i1, neg, el)
    m2 = jnp.max(el2, axis=-1, keepdims=True)
    i2 = jnp.min(jnp.where(el2 == m2, lane, big), axis=-1, keepdims=True)
    t = jnp.exp(m2 - m1)
    w1 = gprob / (1.0 + t)
    w2 = gprob * t / (1.0 + t)

    sel1 = lane == i1
    sel2 = lane == i2
    oh = jnp.where(sel1 | sel2, 1.0, 0.0).astype(BF16)
    r = lax.broadcasted_iota(I32, (n_tok, n_tok), 0)
    c = lax.broadcasted_iota(I32, (n_tok, n_tok), 1)
    tri = jnp.where(c < r, 1.0, 0.0).astype(BF16)
    before = jnp.dot(tri, oh, preferred_element_type=F32)
    counts = jnp.sum(oh.astype(F32), axis=0, keepdims=True)
    assert RUN_ALIGN == 2
    c_even = counts + (counts - 2.0 * jnp.floor(counts * 0.5))
    c_hi = jnp.floor(c_even * (1.0 / 256.0))
    piece = lax.broadcasted_iota(I32, (SUBLANES, LANES), 0)
    pieces = jnp.where(piece == 0, c_hi, jnp.where(piece == 1, c_even - 256.0 * c_hi, 0.0))
    er = lax.broadcasted_iota(I32, (LANES, LANES), 0)
    ec = lax.broadcasted_iota(I32, (LANES, LANES), 1)
    upper = jnp.where(er < ec, 1.0, 0.0).astype(BF16)
    prefix = jnp.dot(pieces.astype(BF16), upper, preferred_element_type=F32)
    lstart = 256.0 * prefix[0:1, :] + prefix[1:2, :]
    lpos1 = jnp.sum(jnp.where(sel1, before + lstart, 0.0), axis=-1, keepdims=True)
    lpos2 = jnp.sum(jnp.where(sel2, before + lstart, 0.0), axis=-1, keepdims=True)

    out = jnp.where(lane == 0, i1, 0.0)
    for k, val in enumerate((i2, lpos1, lpos2, w1, w2), start=1):
        out = jnp.where(lane == k, val, out)
    return out, counts


def _mixer_kernel(*refs, n_tok, sample, pos0):
    if sample:
        (x_ref, hist_ref, gmix_ref, win_ref, wpool_ref, pscale_ref, gv_ref, bv_ref, wsp_ref,
         bsp_ref, wout_ref, gffn_ref, wr_ref, br_ref, band_ref, band0_ref,
         h_ref, hn_ref, rt_ref, cnt_ref, hist_out_ref, v_ref, ext_scr) = refs
    else:
        (x_ref, gmix_ref, win_ref, wpool_ref, pscale_ref, gv_ref, bv_ref, wsp_ref,
         bsp_ref, wout_ref, gffn_ref, wr_ref, br_ref, band_ref, band0_ref,
         h_ref, hn_ref, rt_ref, cnt_ref, hist_out_ref, ext_scr) = refs

    x = x_ref[...]
    xn = _rms(x, gmix_ref[...])
    z = jnp.dot(xn.astype(BF16), win_ref[...], preferred_element_type=F32)
    a = z[:, :POOL_WIDTH]
    uv = _gelu_tanh(z[:, POOL_WIDTH:])
    u = uv[:, :GMLP_WIDTH]
    v = uv[:, GMLP_WIDTH:]
    mu = jnp.mean(v, axis=-1, keepdims=True)
    vc = v - mu
    v = vc * lax.rsqrt(jnp.mean(vc * vc, axis=-1, keepdims=True) + EPS) * gv_ref[...] + bv_ref[...]

    pooled = []
    if sample:
        n_seq = n_tok // SUBLANES
        ext_scr[:, 1:1 + POOL_HIST, :] = hist_ref[...]
        ext_scr[:, 1 + POOL_HIST:, :] = a.reshape(n_seq, SUBLANES, POOL_WIDTH)
        hist_out_ref[...] = ext_scr[:, 1 + SUBLANES:, :]
        pos = pos0 + lax.broadcasted_iota(I32, (n_seq, SUBLANES, 1), 1)
        for gi, w in enumerate(POOL_WINDOWS):
            cols = slice(gi * POOL_GROUP, (gi + 1) * POOL_GROUP)
            win = ext_scr[:, 16:16 + SUBLANES, cols]
            for j in range(1, w):
                win = win + ext_scr[:, 16 - j:16 - j + SUBLANES, cols]
            cnt = jnp.minimum(w, pos + 1).astype(F32)
            pooled.append((win / cnt).reshape(n_tok, POOL_GROUP) - a[:, cols])
    else:
        l = pl.program_id(1)

        @pl.when(l == 0)
        def _():
            ext_scr[0:16, :] = jnp.zeros((16, POOL_WIDTH), F32)

        ext_scr[16:16 + n_tok, :] = a
        chunks = []
        for ci in range(n_tok // CHUNK):
            ext_c = ext_scr[ci * CHUNK:ci * CHUNK + POOL_SPAN, :].astype(BF16)
            groups = []
            for gi in range(len(POOL_WINDOWS)):
                band = band_ref[gi]
                if ci == 0:
                    band = jnp.where(l == 0, band0_ref[gi], band)
                groups.append(jnp.dot(band, ext_c[:, gi * POOL_GROUP:(gi + 1) * POOL_GROUP],
                                      preferred_element_type=F32))
            chunks.append(groups)
        pooled = [jnp.concatenate([chunks[ci][gi] for ci in range(n_tok // CHUNK)], axis=0)
                  for gi in range(len(POOL_WINDOWS))]
        tail = ext_scr[n_tok + 1:n_tok + 16, :]
        hist_out_ref[...] = tail
        ext_scr[1:16, :] = tail

    pool_out = jnp.concatenate(
        [jnp.dot(pooled[gi].astype(BF16), wpool_ref[gi], preferred_element_type=F32)
         for gi in range(len(POOL_WINDOWS))], axis=-1) * pscale_ref[...]

    blk = SUBLANES if sample else CHUNK
    tr = lax.broadcasted_iota(I32, (CHUNK, CHUNK), 0)
    sc = lax.broadcasted_iota(I32, (CHUNK, CHUNK), 1)
    mask = (sc <= tr) & ((sc // blk) == (tr // blk))
    wsp = [jnp.where(mask, wsp_ref[hh], jnp.zeros((), BF16)) for hh in range(N_GMLP_HEADS)]
    v_bf = v.astype(BF16)
    mixed_chunks = []
    for ci in range(n_tok // CHUNK):
        rows = slice(ci * CHUNK, (ci + 1) * CHUNK)
        heads = [jnp.dot(wsp[hh], v_bf[rows, hh * GMLP_HEAD:(hh + 1) * GMLP_HEAD],
                         preferred_element_type=F32) for hh in range(N_GMLP_HEADS)]
        mixed_chunks.append(jnp.concatenate(heads, axis=-1) + bsp_ref[...])
    mixed = jnp.concatenate(mixed_chunks, axis=0)
    gmlp_out = u * mixed

    cat = jnp.concatenate([pool_out, gmlp_out], axis=-1)
    h = x + jnp.dot(cat.astype(BF16), wout_ref[...], preferred_element_type=F32)
    hn = _rms(h, gffn_ref[...]).astype(BF16)

    h_ref[...] = h
    hn_ref[...] = hn
    if sample:
        v_ref[...] = v
    for s in range(n_tok // TOK_TILE):
        rows = slice(s * TOK_TILE, (s + 1) * TOK_TILE)
        rt, counts = _route(hn[rows], wr_ref, br_ref, TOK_TILE)
        rt_ref[rows, :] = rt[:, :SUBLANES]
        cnt_ref[s] = counts


def _full(shape):
    return pl.BlockSpec(shape, lambda *_: (0,) * len(shape))


def _mixer_call(x, hist, weights, *, sample, pos0):
    w_specs = [_full(w.shape) for w in weights]
    n_tok = MIX_TILE
    if sample:
        n_rows = x.shape[0]
        n_seq = n_tok // SUBLANES
        grid = (n_rows // n_tok,)
        tok_map = lambda i: (i, 0)
        tile_map = lambda i: (i, 0, 0)
        in_specs = [pl.BlockSpec((n_tok, D_MODEL), tok_map),
                    pl.BlockSpec((n_seq, POOL_HIST, POOL_WIDTH), lambda i: (i, 0, 0))] + w_specs
        hist_shape = jax.ShapeDtypeStruct(hist.shape, F32)
        hist_spec = pl.BlockSpec((n_seq, POOL_HIST, POOL_WIDTH), lambda i: (i, 0, 0))
        ext = pltpu.VMEM((n_seq, 16 + SUBLANES, POOL_WIDTH), F32)
        args = (x, hist)
    else:
        b, seq, _ = x.shape
        n_rows = b * seq
        n_l = seq // n_tok
        grid = (b, n_l)
        tok_map = lambda bi, li: (bi * n_l + li, 0)
        tile_map = lambda bi, li: (bi * n_l + li, 0, 0)
        in_specs = [pl.BlockSpec((None, n_tok, D_MODEL), lambda bi, li: (bi, li, 0))] + w_specs
        hist_shape = jax.ShapeDtypeStruct((b, POOL_HIST, POOL_WIDTH), F32)
        hist_spec = pl.BlockSpec((None, POOL_HIST, POOL_WIDTH), lambda bi, li: (bi, 0, 0))
        ext = pltpu.VMEM((16 + n_tok, POOL_WIDTH), F32)
        args = (x,)
    out_shape = [jax.ShapeDtypeStruct((n_rows, D_MODEL), F32),
                 jax.ShapeDtypeStruct((n_rows, D_MODEL), BF16),
                 jax.ShapeDtypeStruct((n_rows, SUBLANES), F32),
                 jax.ShapeDtypeStruct((n_rows // TOK_TILE, 1, LANES), F32),
                 hist_shape]
    out_specs = [pl.BlockSpec((n_tok, D_MODEL), tok_map),
                 pl.BlockSpec((n_tok, D_MODEL), tok_map),
                 pl.BlockSpec((n_tok, SUBLANES), tok_map),
                 pl.BlockSpec((n_tok // TOK_TILE, 1, LANES), tile_map),
                 hist_spec]
    if sample:
        out_shape.append(jax.ShapeDtypeStruct((n_rows, GMLP_WIDTH), F32))
        out_specs.append(pl.BlockSpec((n_tok, GMLP_WIDTH), tok_map))
    return pl.pallas_call(
        functools.partial(_mixer_kernel, n_tok=n_tok, sample=sample, pos0=pos0),
        grid=grid, in_specs=in_specs, out_specs=out_specs, out_shape=out_shape,
        scratch_shapes=[ext],
        compiler_params=pltpu.CompilerParams(
            dimension_semantics=("arbitrary",) * len(grid), vmem_limit_bytes=VMEM_LIMIT),
        name="mixer_sample" if sample else "mixer_prompt",
    )(*args, *weights)


def _for_each_copy(tile, cnt_ref, fn):
    for s, n_rows in enumerate(COPY_ROWS):
        def body(j, carry, s=s, n_rows=n_rows):
            fn(s, n_rows, tile * COPY_CAP[s] + j)
            return carry
        lax.fori_loop(0, cnt_ref[tile * len(COPY_ROWS) + s], body, 0)


def _wait_rows(n_rows_total, make_wait):
    units = n_rows_total // RUN_ALIGN
    bit = 0
    while (RUN_ALIGN << bit) <= LOC_ROWS:
        @pl.when((units >> bit) & 1 == 1)
        def _(bit=bit):
            make_wait(RUN_ALIGN << bit).wait()
        bit += 1


def _as_rows(*cols):
    n = cols[0].shape[0]
    lane = lax.broadcasted_iota(I32, (n, LANES), 1)
    packed = jnp.zeros((n, LANES), F32)
    for k, col in enumerate(cols):
        packed = jnp.where(lane == k, col, packed)
    rows = packed.T
    return [rows[k:k + 1, :] for k in range(len(cols))]


def _dispatch_kernel(l8, g8, l4, g4, l2, g2, cnt_ref, trow_ref, zs_ref, zb_ref, nu_ref,
                     hnp_ref, hns_ref, rtp_ref, rts_ref, xs_ref, loc, zbuf, sem, zsem,
                     *, n_prompt_steps):
    i = pl.program_id(0)
    n_steps = pl.num_programs(0)
    slot = i % 2
    local_tabs, sorted_tabs = (l8, l4, l2), (g8, g4, g2)

    @pl.when(i == 0)
    def _():
        zbuf[...] = jnp.zeros(zbuf.shape, U32)
        n_tiles = xs_ref.shape[0] // (MOE_TILE * ROW_CHUNKS)

        def zero_copy(e, c):
            return pltpu.make_async_copy(zbuf.at[pl.ds(0, 8 * ROW_CHUNKS)],
                                         xs_ref.at[_rows(zs_ref[e] + c * 8, 8)], zsem)

        def tail_copy(t):
            return pltpu.make_async_copy(zbuf, xs_ref.at[_rows(t * MOE_TILE, MOE_TILE)], zsem)

        def per_expert(e, total):
            def per_block(c, carry):
                zero_copy(e, c).start()
                return carry
            lax.fori_loop(0, zb_ref[e], per_block, 0)
            return total + zb_ref[e]

        total = lax.fori_loop(0, N_EXPERTS, per_expert, 0)

        def wait_body(_, carry):
            zero_copy(0, 0).wait()
            return carry
        lax.fori_loop(0, total, wait_body, 0)

        def tail_start(t, carry):
            tail_copy(t).start()
            return carry

        def tail_wait(t, carry):
            tail_copy(t).wait()
            return carry

        spill = pltpu.make_async_copy(zbuf.at[pl.ds(0, 8 * ROW_CHUNKS)],
                                      xs_ref.at[_rows(n_tiles * MOE_TILE, 8)], zsem)
        lax.fori_loop(nu_ref[0], n_tiles, tail_start, 0)
        spill.start()
        lax.fori_loop(nu_ref[0], n_tiles, tail_wait, 0)
        spill.wait()

    is_prompt = i < n_prompt_steps
    rt = jnp.where(is_prompt, rtp_ref[...], rts_ref[...])
    row1, row2 = _as_rows(rt[:, 2:3], rt[:, 3:4])
    grow = lax.broadcasted_iota(I32, (LOC_ROWS, TOK_TILE), 0).astype(F32)
    perm = jnp.where((grow == row1) | (grow == row2), 1.0, 0.0).astype(BF16)
    hn = jnp.where(is_prompt, hnp_ref[...], hns_ref[...])
    grouped = jnp.dot(perm, hn, preferred_element_type=F32)
    _store_packed(loc.at[slot], grouped, LOC_ROWS)

    def run_copy(buf_slot, s, n_rows, k):
        src = loc.at[buf_slot, _rows(local_tabs[s][k], n_rows)]
        dst = xs_ref.at[_rows(sorted_tabs[s][k], n_rows)]
        return pltpu.make_async_copy(src, dst, sem.at[buf_slot])

    _for_each_copy(i, cnt_ref, lambda s, n_rows, k: run_copy(slot, s, n_rows, k).start())

    def drain(tile, buf_slot):
        def make_wait(n):
            return pltpu.make_async_copy(loc.at[buf_slot, pl.ds(0, n * ROW_CHUNKS)],
                                         xs_ref.at[pl.ds(0, n * ROW_CHUNKS)], sem.at[buf_slot])
        _wait_rows(trow_ref[tile], make_wait)

    @pl.when(i > 0)
    def _():
        drain(i - 1, 1 - slot)

    @pl.when(i == n_steps - 1)
    def _():
        drain(i, slot)


def _clamped_maps(np_steps):
    pmap = lambda i, *_: (jnp.minimum(i, np_steps - 1), 0)
    smap = lambda i, *_: (jnp.maximum(i - np_steps, 0), 0)
    return pmap, smap


def _dispatch_call(tables, hn_p, hn_s, rt_p, rt_s, n_sorted_rows):
    n_p, n_s = hn_p.shape[0], hn_s.shape[0]
    np_steps = n_p // TOK_TILE
    n_steps = np_steps + n_s // TOK_TILE
    pmap, smap = _clamped_maps(np_steps)
    out_rows = (n_sorted_rows + 8) * ROW_CHUNKS
    return pl.pallas_call(
        functools.partial(_dispatch_kernel, n_prompt_steps=np_steps),
        grid_spec=pltpu.PrefetchScalarGridSpec(
            num_scalar_prefetch=len(tables), grid=(n_steps,),
            in_specs=[pl.BlockSpec((TOK_TILE, D_MODEL), pmap),
                      pl.BlockSpec((TOK_TILE, D_MODEL), smap),
                      pl.BlockSpec((TOK_TILE, SUBLANES), pmap),
                      pl.BlockSpec((TOK_TILE, SUBLANES), smap)],
            out_specs=pl.BlockSpec(memory_space=pl.ANY),
            scratch_shapes=[pltpu.VMEM((2, LOC_ROWS * ROW_CHUNKS, LANES), U32),
                            pltpu.VMEM((MOE_TILE * ROW_CHUNKS, LANES), U32),
                            pltpu.SemaphoreType.DMA((2,)),
                            pltpu.SemaphoreType.DMA(())]),
        out_shape=jax.ShapeDtypeStruct((out_rows, LANES), U32),
        compiler_params=pltpu.CompilerParams(
            dimension_semantics=("arbitrary",), vmem_limit_bytes=VMEM_LIMIT),
        name="dispatch",
    )(*tables, hn_p, hn_s, rt_p, rt_s)


def _moe_kernel(t0_ref, nt_ref, half_ref, nu_ref, xs_hbm, wg_ref, wu_ref, wd_ref, ys_hbm,
                xbuf, ybuf, wg_bf, wu_bf, wd_bf, xsem, ysem):
    e = pl.program_id(0)
    n_used = nu_ref[0]
    n_tiles = ys_hbm.shape[0] // (MOE_TILE * ROW_CHUNKS)
    half_rows = MOE_TILE // 2

    def x_copy(g, slot):
        return pltpu.make_async_copy(xs_hbm.at[_rows(g * MOE_TILE, MOE_TILE)], xbuf.at[slot],
                                     xsem.at[slot])

    def y_copy(g, slot):
        return pltpu.make_async_copy(ybuf.at[slot], ys_hbm.at[_rows(g * MOE_TILE, MOE_TILE)],
                                     ysem.at[slot])

    def store_zeros(ref, n_rows):
        ij = (lax.broadcasted_iota(I32, (n_rows, D_MODEL), 0)
              + lax.broadcasted_iota(I32, (n_rows, D_MODEL), 1))
        _store_packed(ref, jnp.where(ij < jnp.minimum(nu_ref[0], 0), 1.0, 0.0), n_rows)

    @pl.when(e == 0)
    def _():
        x_copy(0, 0).start()

    @pl.when(nt_ref[e] > 0)
    def _():
        wg_bf[...] = wg_ref[...].astype(BF16)
        wu_bf[...] = wu_ref[...].astype(BF16)
        wd_bf[...] = wd_ref[...].astype(BF16)

        def tile_body(t, carry):
            g = t0_ref[e] + t
            slot = g % 2

            @pl.when(g + 1 < n_used)
            def _():
                x_copy(g + 1, 1 - slot).start()

            x_copy(g, slot).wait()

            @pl.when(g >= 2)
            def _():
                y_copy(g - 2, slot).wait()

            def expert_rows(n_rows):
                lo, hi = _load_packed(xbuf.at[slot], n_rows)
                x = jnp.concatenate([lo, hi], axis=-1).astype(BF16)
                gt = jnp.dot(x, wg_bf[...], preferred_element_type=F32)
                up = jnp.dot(x, wu_bf[...], preferred_element_type=F32)
                hh = (gt * jax.nn.sigmoid(gt)) * up
                y = jnp.dot(hh.astype(BF16), wd_bf[...], preferred_element_type=F32)
                _store_packed(ybuf.at[slot], y, n_rows)

            @pl.when(half_ref[g] == 0)
            def _():
                expert_rows(MOE_TILE)

            @pl.when(half_ref[g] != 0)
            def _():
                expert_rows(half_rows)
                store_zeros(ybuf.at[slot, pl.ds(half_rows * ROW_CHUNKS, half_rows * ROW_CHUNKS)],
                            half_rows)

            y_copy(g, slot).start()
            return carry

        lax.fori_loop(0, nt_ref[e], tile_body, 0)

    @pl.when(e == pl.num_programs(0) - 1)
    def _():
        last = n_used - 1
        y_copy(last, last % 2).wait()

        @pl.when(n_used >= 2)
        def _():
            y_copy(last - 1, (last - 1) % 2).wait()

        store_zeros(ybuf.at[0], MOE_TILE)

        def zero_start(g, carry):
            y_copy(g, 0).start()
            return carry

        def zero_wait(g, carry):
            y_copy(g, 0).wait()
            return carry

        lax.fori_loop(n_used, n_tiles, zero_start, 0)
        lax.fori_loop(n_used, n_tiles, zero_wait, 0)


def _moe_call(first_tile, n_expert_tiles, half_only, n_used, xs, n_tiles, w_eg, w_eu, w_ed):
    blk = MOE_TILE * ROW_CHUNKS
    any_spec = pl.BlockSpec(memory_space=pl.ANY)
    return pl.pallas_call(
        _moe_kernel,
        grid_spec=pltpu.PrefetchScalarGridSpec(
            num_scalar_prefetch=4, grid=(N_EXPERTS,),
            in_specs=[
                any_spec,
                pl.BlockSpec((None, D_MODEL, D_EXPERT), lambda e, *_: (e, 0, 0)),
                pl.BlockSpec((None, D_MODEL, D_EXPERT), lambda e, *_: (e, 0, 0)),
                pl.BlockSpec((None, D_EXPERT, D_MODEL), lambda e, *_: (e, 0, 0)),
            ],
            out_specs=any_spec,
            scratch_shapes=[pltpu.VMEM((2, blk, LANES), U32),
                            pltpu.VMEM((2, blk, LANES), U32),
                            pltpu.VMEM((D_MODEL, D_EXPERT), BF16),
                            pltpu.VMEM((D_MODEL, D_EXPERT), BF16),
                            pltpu.VMEM((D_EXPERT, D_MODEL), BF16),
                            pltpu.SemaphoreType.DMA((2,)),
                            pltpu.SemaphoreType.DMA((2,))]),
        out_shape=jax.ShapeDtypeStruct((n_tiles * blk, LANES), U32),
        compiler_params=pltpu.CompilerParams(
            dimension_semantics=("arbitrary",), vmem_limit_bytes=VMEM_LIMIT),
        name="experts",
    )(first_tile, n_expert_tiles, half_only, n_used, xs, w_eg, w_eu, w_ed)


def _final_kernel(l8, g8, l4, g4, l2, g2, cnt_ref, trow_ref,
                  hp_ref, hs_ref, rtp_ref, rts_ref, pp_ref, ps_ref,
                  ys_ref, wple_ref, wgate_ref, gple_ref, gfin_ref, outp_ref, outs_ref, ybuf, sem,
                  *, n_prompt_steps):
    i = pl.program_id(0)
    n_steps = pl.num_programs(0)
    slot = i % 2
    local_tabs, sorted_tabs = (l8, l4, l2), (g8, g4, g2)

    def run_copy(buf_slot, s, n_rows, k):
        src = ys_ref.at[_rows(sorted_tabs[s][k], n_rows)]
        dst = ybuf.at[buf_slot, _rows(local_tabs[s][k], n_rows)]
        return pltpu.make_async_copy(src, dst, sem.at[buf_slot])

    def issue(tile, buf_slot):
        _for_each_copy(tile, cnt_ref, lambda s, n_rows, k: run_copy(buf_slot, s, n_rows, k).start())

    @pl.when(i == 0)
    def _():
        ybuf[...] = jnp.zeros(ybuf.shape, U32)
        issue(0, 0)

    @pl.when(i + 1 < n_steps)
    def _():
        issue(i + 1, 1 - slot)

    def make_wait(n):
        return pltpu.make_async_copy(ys_ref.at[pl.ds(0, n * ROW_CHUNKS)],
                                     ybuf.at[slot, pl.ds(0, n * ROW_CHUNKS)], sem.at[slot])
    _wait_rows(trow_ref[i], make_wait)

    is_prompt = i < n_prompt_steps
    h = jnp.where(is_prompt, hp_ref[...], hs_ref[...])
    p = jnp.where(is_prompt, pp_ref[...], ps_ref[...])
    rt = jnp.where(is_prompt, rtp_ref[...], rts_ref[...])
    lpos1, lpos2 = rt[:, 2:3], rt[:, 3:4]
    lane = lax.broadcasted_iota(I32, (TOK_TILE, LOC_ROWS), 1).astype(F32)
    sel1 = lane == lpos1
    sel2 = lane == lpos2
    row1, row2, w1_row, w2_row = _as_rows(lpos1, lpos2, rt[:, 4:5], rt[:, 5:6])
    grow = lax.broadcasted_iota(I32, (LOC_ROWS, TOK_TILE), 0).astype(F32)
    wrow = jnp.sum(jnp.where(grow == row1, w1_row, 0.0) + jnp.where(grow == row2, w2_row, 0.0),
                   axis=-1, keepdims=True)
    lo, hi = _load_packed(ybuf.at[slot], LOC_ROWS)
    yb = (jnp.concatenate([lo, hi], axis=-1) * wrow).astype(BF16)
    perm = jnp.where(sel1 | sel2, 1.0, 0.0).astype(BF16)
    h = h + jnp.dot(perm, yb, preferred_element_type=F32)
    r = _rms(h, gple_ref[...])
    gate = jax.nn.sigmoid(jnp.dot(r.astype(BF16), wgate_ref[...], preferred_element_type=F32))
    h = h + jnp.dot(p.astype(BF16), wple_ref[...], preferred_element_type=F32) * gate
    y = _rms(h, gfin_ref[...])

    @pl.when(is_prompt)
    def _():
        outp_ref[...] = y

    @pl.when(jnp.logical_not(is_prompt))
    def _():
        outs_ref[...] = y


def _final_call(tables, h_p, h_s, rt_p, rt_s, p_p, p_s, ys, wple, wgate, gple, gfin):
    n_p, n_s = h_p.shape[0], h_s.shape[0]
    np_steps = n_p // TOK_TILE
    n_steps = np_steps + n_s // TOK_TILE
    pmap, smap = _clamped_maps(np_steps)
    cmap = lambda i, *_: (0, 0)
    return pl.pallas_call(
        functools.partial(_final_kernel, n_prompt_steps=np_steps),
        grid_spec=pltpu.PrefetchScalarGridSpec(
            num_scalar_prefetch=len(tables), grid=(n_steps,),
            in_specs=[
                pl.BlockSpec((TOK_TILE, D_MODEL), pmap),
                pl.BlockSpec((TOK_TILE, D_MODEL), smap),
                pl.BlockSpec((TOK_TILE, SUBLANES), pmap),
                pl.BlockSpec((TOK_TILE, SUBLANES), smap),
                pl.BlockSpec((TOK_TILE, PLE_DIM), pmap),
                pl.BlockSpec((TOK_TILE, PLE_DIM), smap),
                pl.BlockSpec(memory_space=pl.ANY),
                pl.BlockSpec(wple.shape, cmap),
                pl.BlockSpec(wgate.shape, cmap),
                pl.BlockSpec(gple.shape, cmap),
                pl.BlockSpec(gfin.shape, cmap),
            ],
            out_specs=[pl.BlockSpec((TOK_TILE, D_MODEL), pmap),
                       pl.BlockSpec((TOK_TILE, D_MODEL), smap)],
            scratch_shapes=[pltpu.VMEM((2, LOC_ROWS * ROW_CHUNKS, LANES), U32),
                            pltpu.SemaphoreType.DMA((2,))]),
        out_shape=[jax.ShapeDtypeStruct((n_p, D_MODEL), F32),
                   jax.ShapeDtypeStruct((n_s, D_MODEL), F32)],
        compiler_params=pltpu.CompilerParams(
            dimension_semantics=("arbitrary",), vmem_limit_bytes=VMEM_LIMIT),
        name="final",
    )(*tables, h_p, h_s, rt_p, rt_s, p_p, p_s, ys, wple, wgate, gple, gfin)


def _ceil_to(x, m):
    return ((x + m - 1) // m) * m


def _excl_cumsum(a, axis):
    return jnp.cumsum(a, axis=axis) - a


def _copy_tables(n_copies, first_local, first_sorted, step, cap):
    ends = jnp.cumsum(n_copies, axis=1)
    j = jnp.arange(cap, dtype=I32)[None, :, None]
    e_of = jnp.minimum(jnp.sum(ends[:, None, :] <= j, axis=-1), N_EXPERTS - 1)
    pick = e_of[:, :, None] == jnp.arange(N_EXPERTS, dtype=I32)[None, None, :]
    take = lambda a: jnp.sum(jnp.where(pick, a[:, None, :], 0), axis=-1)
    c = jnp.arange(cap, dtype=I32)[None, :] - (take(ends) - take(n_copies))
    flat = lambda a: a.reshape(-1).astype(I32)
    return flat(take(first_local) + step * c), flat(take(first_sorted) + step * c), ends[:, -1]


def kernel(x_prompt, x_sample, state_pool, p_prompt, p_sample, g_mix, w_in, w_pool, pool_scale,
           g_v, b_v, w_spatial, b_spatial, w_out, g_ffn, w_rg, b_rg, w_re, b_re, w_eg, w_eu, w_ed,
           w_ple, g_ple, w_ple_gate, g_final):
    batch, seq, _ = x_prompt.shape
    dec_batch, dec_seq, _ = x_sample.shape
    assert dec_seq == SUBLANES and g_mix.shape[0] == 1
    n_p = batch * seq
    n_s = dec_batch * dec_seq
    n_tok = n_p + n_s
    past_len = 16384

    row = lambda a: a.reshape(1, -1).astype(F32)
    wr = jnp.concatenate(
        [w_re[0], w_rg[0], jnp.zeros((D_MODEL, LANES - N_EXPERTS - N_EXPERT_GROUPS), F32)], axis=1)
    br = jnp.concatenate(
        [b_re[0], b_rg[0], jnp.zeros((LANES - N_EXPERTS - N_EXPERT_GROUPS,), F32)]).reshape(1, LANES)
    bsp_p = jnp.repeat(jnp.transpose(b_spatial[0]), GMLP_HEAD, axis=1)
    bsp_s = jnp.tile(bsp_p[:dec_seq], (CHUNK // dec_seq, 1))
    wsp_p = w_spatial[0].astype(BF16)
    wsp_s = jnp.tile(w_spatial[0][:, :dec_seq, :dec_seq].astype(BF16),
                     (1, CHUNK // dec_seq, CHUNK // dec_seq))

    def pool_band(first_chunk):
        t = jnp.arange(CHUNK, dtype=I32)[:, None]
        k = jnp.arange(POOL_SPAN, dtype=I32)[None, :]
        mats = []
        for w in POOL_WINDOWS:
            cnt = jnp.minimum(w, t + 1) if first_chunk else jnp.full_like(t, w)
            in_window = (k >= t + 17 - w) & (k <= t + 16)
            mats.append(jnp.where(in_window, 1.0 / cnt.astype(F32), 0.0) - (k == t + 16).astype(F32))
        return jnp.stack(mats).astype(BF16)

    def mixer_weights(wsp, bsp):
        return (row(g_mix[0]), w_in[0].astype(BF16), w_pool[0].astype(BF16), row(pool_scale[0]),
                row(g_v[0]), row(b_v[0]), wsp, bsp, w_out[0].astype(BF16), row(g_ffn[0]),
                wr.astype(BF16), br, pool_band(False), pool_band(True))

    h_p, hn_p, rt_p, cnt_p, hist_p = _mixer_call(
        x_prompt, None, mixer_weights(wsp_p, bsp_p), sample=False, pos0=0)
    h_s, hn_s, rt_s, cnt_s, hist_s, v_s = _mixer_call(
        x_sample.reshape(n_s, D_MODEL), state_pool[0], mixer_weights(wsp_s, bsp_s),
        sample=True, pos0=past_len)

    n_run = jnp.concatenate([cnt_p, cnt_s], axis=0)[:, 0, :N_EXPERTS].astype(I32)
    n_even = _ceil_to(n_run, RUN_ALIGN)
    l_start = _excl_cumsum(n_even, 1)
    tile_rows = jnp.sum(n_even, axis=1)
    base = _excl_cumsum(n_even, 0)
    counts = jnp.sum(n_even, axis=0)
    padded = _ceil_to(counts, MOE_TILE)
    ends = jnp.cumsum(padded)
    offs = ends - padded
    g_start = offs[None, :] + base
    n8 = n_even // 8
    f4 = (n_even // 4) % 2
    f2 = (n_even // 2) % 2
    l8, g8, c8 = _copy_tables(n8, l_start, g_start, 8, COPY_CAP[0])
    l4, g4, c4 = _copy_tables(f4, l_start + 8 * n8, g_start + 8 * n8, 0, COPY_CAP[1])
    l2, g2, c2 = _copy_tables(f2, l_start + 8 * n8 + 4 * f4, g_start + 8 * n8 + 4 * f4, 0, COPY_CAP[2])
    copy_cnt = jnp.stack([c8, c4, c2], axis=1).reshape(-1).astype(I32)
    tile_rows = tile_rows.astype(I32)
    z_start = (offs + counts).astype(I32)
    z_blocks = ((padded - counts + 7) // 8).astype(I32)

    max_rows = 2 * n_tok + n_run.shape[0] * N_EXPERTS * (RUN_ALIGN - 1)
    n_tiles = max_rows // MOE_TILE + N_EXPERTS + 1
    n_used = (ends[-1] // MOE_TILE).astype(I32).reshape(1)
    tile_id = jnp.minimum(jnp.arange(n_tiles, dtype=I32), n_used - 1)
    tile_expert = jnp.sum((tile_id[:, None] * MOE_TILE) >= ends[None, :], axis=1).astype(I32)
    tile_expert = jnp.minimum(tile_expert, N_EXPERTS - 1)
    pick_e = tile_expert[:, None] == jnp.arange(N_EXPERTS, dtype=I32)[None, :]
    data_end = jnp.sum(jnp.where(pick_e, (offs + counts)[None, :], 0), axis=1)
    half_only = (data_end - tile_id * MOE_TILE <= MOE_TILE // 2).astype(I32)
    copy_tabs = (l8, g8, l4, g4, l2, g2, copy_cnt, tile_rows)
    xs = _dispatch_call(copy_tabs + (z_start, z_blocks, n_used), hn_p, hn_s, rt_p, rt_s,
                        n_tiles * MOE_TILE)
    ys = _moe_call((offs // MOE_TILE).astype(I32), (padded // MOE_TILE).astype(I32), half_only, n_used,
                   xs, n_tiles, w_eg[0], w_eu[0], w_ed[0])
    y_p, y_s = _final_call(copy_tabs, h_p, h_s, rt_p, rt_s,
                           p_prompt[0].reshape(n_p, PLE_DIM), p_sample[0].reshape(n_s, PLE_DIM), ys,
                           w_ple[0].astype(BF16), w_ple_gate[0].astype(BF16), row(g_ple[0]), row(g_final))

    return (y_p.reshape(batch, seq, D_MODEL),
            y_s.reshape(dec_batch, dec_seq, D_MODEL),
            hist_p[None],
            hist_s[None],
            v_s.reshape(1, dec_batch, dec_seq, GMLP_WIDTH))
```

```python
import functools

import jax
import jax.numpy as jnp
from jax import lax
from jax.experimental import pallas as pl
from jax.experimental.pallas import tpu as pltpu

D_MODEL = 1024
POOL_WIDTH = 512
GMLP_WIDTH = 512
IN_WIDTH = POOL_WIDTH + 2 * GMLP_WIDTH
POOL_WINDOWS = (2, 4, 8, 16)
POOL_GROUP = 128
POOL_HIST = 15
CHUNK = 128
POOL_SPAN = CHUNK + 16
N_GMLP_HEADS = 4
GMLP_HEAD = 128
N_EXPERT_GROUPS = 4
EXPERTS_PER_GROUP = 8
N_EXPERTS = 32
D_EXPERT = 512
PLE_DIM = 256
EPS = 1e-6

LANES = 128
SUBLANES = 8
HALF = D_MODEL // 2
ROW_CHUNKS = HALF // LANES

TOK_TILE = 256
MIX_TILE = 512
MOE_TILE = 512
RUN_ALIGN = 2
COPY_ROWS = (8, 4, 2)
LOC_ROWS = 2 * TOK_TILE + N_EXPERTS * (RUN_ALIGN - 1)
COPY_CAP = (LOC_ROWS // 8 + 1, N_EXPERTS, N_EXPERTS)
VMEM_LIMIT = 56 * 1024 * 1024

F32 = jnp.float32
BF16 = jnp.bfloat16
I32 = jnp.int32
U32 = jnp.uint32


def _rms(x, g):
    return x * lax.rsqrt(jnp.mean(x * x, axis=-1, keepdims=True) + EPS) * g


def _gelu_tanh(x):
    inner = x * (0.7978845608028654 + (0.7978845608028654 * 0.044715) * (x * x))
    half = 0.5 * x
    return half + half * jnp.tanh(inner)


def _store_packed(ref, val, n_rows):
    packed = pltpu.pack_elementwise([val[:, :HALF], val[:, HALF:]], packed_dtype=BF16)
    for k in range(ROW_CHUNKS):
        ref[pl.ds(k, n_rows, stride=ROW_CHUNKS), :] = packed[:, k * LANES:(k + 1) * LANES]


def _load_packed(ref, n_rows):
    packed = jnp.concatenate(
        [ref[pl.ds(k, n_rows, stride=ROW_CHUNKS), :] for k in range(ROW_CHUNKS)], axis=-1)
    lo = pltpu.unpack_elementwise(packed, index=0, packed_dtype=BF16, unpacked_dtype=F32)
    hi = pltpu.unpack_elementwise(packed, index=1, packed_dtype=BF16, unpacked_dtype=F32)
    return lo, hi


def _rows(start, n):
    return pl.ds(pl.multiple_of(start * ROW_CHUNKS, SUBLANES), n * ROW_CHUNKS)


def _route(hn, wr_ref, br_ref, n_tok):
    logits = jnp.dot(hn, wr_ref[...], preferred_element_type=F32) + br_ref[...]

    lane = lax.broadcasted_iota(I32, (n_tok, LANES), 1).astype(F32)
    neg = jnp.float32(-jnp.inf)
    big = jnp.float32(1e9)
    gmask = (lane >= N_EXPERTS) & (lane < N_EXPERTS + N_EXPERT_GROUPS)
    gl = jnp.where(gmask, logits, neg)
    gmax = jnp.max(gl, axis=-1, keepdims=True)
    gidx = jnp.min(jnp.where(gl == gmax, lane, big), axis=-1, keepdims=True) - N_EXPERTS
    gden = jnp.sum(jnp.exp(gl - gmax), axis=-1, keepdims=True)
    gprob = 1.0 / gden

    lo = gidx * EXPERTS_PER_GROUP
    emask = (lane >= lo) & (lane < lo + EXPERTS_PER_GROUP)
    el = jnp.where(emask, logits, neg)
    m1 = jnp.max(el, axis=-1, keepdims=True)
    i1 = jnp.min(jnp.where(el == m1, lane, big), axis=-1, keepdims=True)
    el2 = jnp.where(lane == i1, neg, el)
    m2 = jnp.max(el2, axis=-1, keepdims=True)
    i2 = jnp.min(jnp.where(el2 == m2, lane, big), axis=-1, keepdims=True)
    t = jnp.exp(m2 - m1)
    w1 = gprob / (1.0 + t)
    w2 = gprob * t / (1.0 + t)

    sel1 = lane == i1
    sel2 = lane == i2
    oh = jnp.where(sel1 | sel2, 1.0, 0.0).astype(BF16)
    r = lax.broadcasted_iota(I32, (n_tok, n_tok), 0)
    c = lax.broadcasted_iota(I32, (n_tok, n_tok), 1)
    tri = jnp.where(c < r, 1.0, 0.0).astype(BF16)
    before = jnp.dot(tri, oh, preferred_element_type=F32)
    counts = jnp.sum(oh.astype(F32), axis=0, keepdims=True)
    assert RUN_ALIGN == 2
    c_even = counts + (counts - 2.0 * jnp.floor(counts * 0.5))
    c_hi = jnp.floor(c_even * (1.0 / 256.0))
    piece = lax.broadcasted_iota(I32, (SUBLANES, LANES), 0)
    pieces = jnp.where(piece == 0, c_hi, jnp.where(piece == 1, c_even - 256.0 * c_hi, 0.0))
    er = lax.broadcasted_iota(I32, (LANES, LANES), 0)
    ec = lax.broadcasted_iota(I32, (LANES, LANES), 1)
    upper = jnp.where(er < ec, 1.0, 0.0).astype(BF16)
    prefix = jnp.dot(pieces.astype(BF16), upper, preferred_element_type=F32)
    lstart = 256.0 * prefix[0:1, :] + prefix[1:2, :]
    lpos1 = jnp.sum(jnp.where(sel1, before + lstart, 0.0), axis=-1, keepdims=True)
    lpos2 = jnp.sum(jnp.where(sel2, before + lstart, 0.0), axis=-1, keepdims=True)

    out = jnp.where(lane == 0, i1, 0.0)
    for k, val in enumerate((i2, lpos1, lpos2, w1, w2), start=1):
        out = jnp.where(lane == k, val, out)
    return out, counts


def _mixer_kernel(*refs, n_tok, sample, pos0):
    if sample:
        (x_ref, hist_ref, gmix_ref, win_ref, wpool_ref, pscale_ref, gv_ref, bv_ref, wsp_ref,
         bsp_ref, wout_ref, gffn_ref, wr_ref, br_ref, band_ref, band0_ref,
         h_ref, hn_ref, rt_ref, cnt_ref, hist_out_ref, v_ref, ext_scr) = refs
    else:
        (x_ref, gmix_ref, win_ref, wpool_ref, pscale_ref, gv_ref, bv_ref, wsp_ref,
         bsp_ref, wout_ref, gffn_ref, wr_ref, br_ref, band_ref, band0_ref,
         h_ref, hn_ref, rt_ref, cnt_ref, hist_out_ref, ext_scr) = refs

    x = x_ref[...]
    xn = _rms(x, gmix_ref[...])
    z = jnp.dot(xn.astype(BF16), win_ref[...], preferred_element_type=F32)
    a = z[:, :POOL_WIDTH]
    uv = _gelu_tanh(z[:, POOL_WIDTH:])
    u = uv[:, :GMLP_WIDTH]
    v = uv[:, GMLP_WIDTH:]
    mu = jnp.mean(v, axis=-1, keepdims=True)
    vc = v - mu
    v = vc * lax.rsqrt(jnp.mean(vc * vc, axis=-1, keepdims=True) + EPS) * gv_ref[...] + bv_ref[...]

    pooled = []
    if sample:
        n_seq = n_tok // SUBLANES
        ext_scr[:, 1:1 + POOL_HIST, :] = hist_ref[...]
        ext_scr[:, 1 + POOL_HIST:, :] = a.reshape(n_seq, SUBLANES, POOL_WIDTH)
        hist_out_ref[...] = ext_scr[:, 1 + SUBLANES:, :]
        pos = pos0 + lax.broadcasted_iota(I32, (n_seq, SUBLANES, 1), 1)
        for gi, w in enumerate(POOL_WINDOWS):
            cols = slice(gi * POOL_GROUP, (gi + 1) * POOL_GROUP)
            win = ext_scr[:, 16:16 + SUBLANES, cols]
            for j in range(1, w):
                win = win + ext_scr[:, 16 - j:16 - j + SUBLANES, cols]
            cnt = jnp.minimum(w, pos + 1).astype(F32)
            pooled.append((win / cnt).reshape(n_tok, POOL_GROUP) - a[:, cols])
    else:
        l = pl.program_id(1)

        @pl.when(l == 0)
        def _():
            ext_scr[0:16, :] = jnp.zeros((16, POOL_WIDTH), F32)

        ext_scr[16:16 + n_tok, :] = a
        chunks = []
        for ci in range(n_tok // CHUNK):
            ext_c = ext_scr[ci * CHUNK:ci * CHUNK + POOL_SPAN, :].astype(BF16)
            groups = []
            for gi in range(len(POOL_WINDOWS)):
                band = band_ref[gi]
                if ci == 0:
                    band = jnp.where(l == 0, band0_ref[gi], band)
                groups.append(jnp.dot(band, ext_c[:, gi * POOL_GROUP:(gi + 1) * POOL_GROUP],
                                      preferred_element_type=F32))
            chunks.append(groups)
        pooled = [jnp.concatenate([chunks[ci][gi] for ci in range(n_tok // CHUNK)], axis=0)
                  for gi in range(len(POOL_WINDOWS))]
        tail = ext_scr[n_tok + 1:n_tok + 16, :]
        hist_out_ref[...] = tail
        ext_scr[1:16, :] = tail

    pool_out = jnp.concatenate(
        [jnp.dot(pooled[gi].astype(BF16), wpool_ref[gi], preferred_element_type=F32)
         for gi in range(len(POOL_WINDOWS))], axis=-1) * pscale_ref[...]

    blk = SUBLANES if sample else CHUNK
    tr = lax.broadcasted_iota(I32, (CHUNK, CHUNK), 0)
    sc = lax.broadcasted_iota(I32, (CHUNK, CHUNK), 1)
    mask = (sc <= tr) & ((sc // blk) == (tr // blk))
    wsp = [jnp.where(mask, wsp_ref[hh], jnp.zeros((), BF16)) for hh in range(N_GMLP_HEADS)]
    v_bf = v.astype(BF16)
    mixed_chunks = []
    for ci in range(n_tok // CHUNK):
        rows = slice(ci * CHUNK, (ci + 1) * CHUNK)
        heads = [jnp.dot(wsp[hh], v_bf[rows, hh * GMLP_HEAD:(hh + 1) * GMLP_HEAD],
                         preferred_element_type=F32) for hh in range(N_GMLP_HEADS)]
        mixed_chunks.append(jnp.concatenate(heads, axis=-1) + bsp_ref[...])
    mixed = jnp.concatenate(mixed_chunks, axis=0)
    gmlp_out = u * mixed

    cat = jnp.concatenate([pool_out, gmlp_out], axis=-1)
    h = x + jnp.dot(cat.astype(BF16), wout_ref[...], preferred_element_type=F32)
    hn = _rms(h, gffn_ref[...]).astype(BF16)

    h_ref[...] = h
    hn_ref[...] = hn
    if sample:
        v_ref[...] = v
    for s in range(n_tok // TOK_TILE):
        rows = slice(s * TOK_TILE, (s + 1) * TOK_TILE)
        rt, counts = _route(hn[rows], wr_ref, br_ref, TOK_TILE)
        rt_ref[rows, :] = rt[:, :SUBLANES]
        cnt_ref[s] = counts


def _full(shape):
    return pl.BlockSpec(shape, lambda *_: (0,) * len(shape))


def _mixer_call(x, hist, weights, *, sample, pos0):
    w_specs = [_full(w.shape) for w in weights]
    n_tok = MIX_TILE
    if sample:
        n_rows = x.shape[0]
        n_seq = n_tok // SUBLANES
        grid = (n_rows // n_tok,)
        tok_map = lambda i: (i, 0)
        tile_map = lambda i: (i, 0, 0)
        in_specs = [pl.BlockSpec((n_tok, D_MODEL), tok_map),
                    pl.BlockSpec((n_seq, POOL_HIST, POOL_WIDTH), lambda i: (i, 0, 0))] + w_specs
        hist_shape = jax.ShapeDtypeStruct(hist.shape, F32)
        hist_spec = pl.BlockSpec((n_seq, POOL_HIST, POOL_WIDTH), lambda i: (i, 0, 0))
        ext = pltpu.VMEM((n_seq, 16 + SUBLANES, POOL_WIDTH), F32)
        args = (x, hist)
    else:
        b, seq, _ = x.shape
        n_rows = b * seq
        n_l = seq // n_tok
        grid = (b, n_l)
        tok_map = lambda bi, li: (bi * n_l + li, 0)
        tile_map = lambda bi, li: (bi * n_l + li, 0, 0)
        in_specs = [pl.BlockSpec((None, n_tok, D_MODEL), lambda bi, li: (bi, li, 0))] + w_specs
        hist_shape = jax.ShapeDtypeStruct((b, POOL_HIST, POOL_WIDTH), F32)
        hist_spec = pl.BlockSpec((None, POOL_HIST, POOL_WIDTH), lambda bi, li: (bi, 0, 0))
        ext = pltpu.VMEM((16 + n_tok, POOL_WIDTH), F32)
        args = (x,)
    out_shape = [jax.ShapeDtypeStruct((n_rows, D_MODEL), F32),
                 jax.ShapeDtypeStruct((n_rows, D_MODEL), BF16),
                 jax.ShapeDtypeStruct((n_rows, SUBLANES), F32),
                 jax.ShapeDtypeStruct((n_rows // TOK_TILE, 1, LANES), F32),
                 hist_shape]
    out_specs = [pl.BlockSpec((n_tok, D_MODEL), tok_map),
                 pl.BlockSpec((n_tok, D_MODEL), tok_map),
                 pl.BlockSpec((n_tok, SUBLANES), tok_map),
                 pl.BlockSpec((n_tok // TOK_TILE, 1, LANES), tile_map),
                 hist_spec]
    if sample:
        out_shape.append(jax.ShapeDtypeStruct((n_rows, GMLP_WIDTH), F32))
        out_specs.append(pl.BlockSpec((n_tok, GMLP_WIDTH), tok_map))
    return pl.pallas_call(
        functools.partial(_mixer_kernel, n_tok=n_tok, sample=sample, pos0=pos0),
        grid=grid, in_specs=in_specs, out_specs=out_specs, out_shape=out_shape,
        scratch_shapes=[ext],
        compiler_params=pltpu.CompilerParams(
            dimension_semantics=("arbitrary",) * len(grid), vmem_limit_bytes=VMEM_LIMIT),
        name="mixer_sample" if sample else "mixer_prompt",
    )(*args, *weights)


def _for_each_copy(tile, cnt_ref, fn):
    for s, n_rows in enumerate(COPY_ROWS):
        def body(j, carry, s=s, n_rows=n_rows):
            fn(s, n_rows, tile * COPY_CAP[s] + j)
            return carry
        lax.fori_loop(0, cnt_ref[tile * len(COPY_ROWS) + s], body, 0)


def _wait_rows(n_rows_total, make_wait):
    units = n_rows_total // RUN_ALIGN
    bit = 0
    while (RUN_ALIGN << bit) <= LOC_ROWS:
        @pl.when((units >> bit) & 1 == 1)
        def _(bit=bit):
            make_wait(RUN_ALIGN << bit).wait()
        bit += 1


def _as_rows(*cols):
    n = cols[0].shape[0]
    lane = lax.broadcasted_iota(I32, (n, LANES), 1)
    packed = jnp.zeros((n, LANES), F32)
    for k, col in enumerate(cols):
        packed = jnp.where(lane == k, col, packed)
    rows = packed.T
    return [rows[k:k + 1, :] for k in range(len(cols))]


def _dispatch_kernel(l8, g8, l4, g4, l2, g2, cnt_ref, trow_ref, zs_ref, zb_ref, nu_ref,
                     hnp_ref, hns_ref, rtp_ref, rts_ref, xs_ref, loc, zbuf, sem, zsem,
                     *, n_prompt_steps):
    i = pl.program_id(0)
    n_steps = pl.num_programs(0)
    slot = i % 2
    local_tabs, sorted_tabs = (l8, l4, l2), (g8, g4, g2)

    @pl.when(i == 0)
    def _():
        zbuf[...] = jnp.zeros(zbuf.shape, U32)
        n_tiles = xs_ref.shape[0] // (MOE_TILE * ROW_CHUNKS)

        def zero_copy(e, c):
            return pltpu.make_async_copy(zbuf.at[pl.ds(0, 8 * ROW_CHUNKS)],
                                         xs_ref.at[_rows(zs_ref[e] + c * 8, 8)], zsem)

        def tail_copy(t):
            return pltpu.make_async_copy(zbuf, xs_ref.at[_rows(t * MOE_TILE, MOE_TILE)], zsem)

        def per_expert(e, total):
            def per_block(c, carry):
                zero_copy(e, c).start()
                return carry
            lax.fori_loop(0, zb_ref[e], per_block, 0)
            return total + zb_ref[e]

        total = lax.fori_loop(0, N_EXPERTS, per_expert, 0)

        def wait_body(_, carry):
            zero_copy(0, 0).wait()
            return carry
        lax.fori_loop(0, total, wait_body, 0)

        def tail_start(t, carry):
            tail_copy(t).start()
            return carry

        def tail_wait(t, carry):
            tail_copy(t).wait()
            return carry

        spill = pltpu.make_async_copy(zbuf.at[pl.ds(0, 8 * ROW_CHUNKS)],
                                      xs_ref.at[_rows(n_tiles * MOE_TILE, 8)], zsem)
        lax.fori_loop(nu_ref[0], n_tiles, tail_start, 0)
        spill.start()
        lax.fori_loop(nu_ref[0], n_tiles, tail_wait, 0)
        spill.wait()

    is_prompt = i < n_prompt_steps
    rt = jnp.where(is_prompt, rtp_ref[...], rts_ref[...])
    row1, row2 = _as_rows(rt[:, 2:3], rt[:, 3:4])
    grow = lax.broadcasted_iota(I32, (LOC_ROWS, TOK_TILE), 0).astype(F32)
    perm = jnp.where((grow == row1) | (grow == row2), 1.0, 0.0).astype(BF16)
    hn = jnp.where(is_prompt, hnp_ref[...], hns_ref[...])
    grouped = jnp.dot(perm, hn, preferred_element_type=F32)
    _store_packed(loc.at[slot], grouped, LOC_ROWS)

    def run_copy(buf_slot, s, n_rows, k):
        src = loc.at[buf_slot, _rows(local_tabs[s][k], n_rows)]
        dst = xs_ref.at[_rows(sorted_tabs[s][k], n_rows)]
        return pltpu.make_async_copy(src, dst, sem.at[buf_slot])

    _for_each_copy(i, cnt_ref, lambda s, n_rows, k: run_copy(slot, s, n_rows, k).start())

    def drain(tile, buf_slot):
        def make_wait(n):
            return pltpu.make_async_copy(loc.at[buf_slot, pl.ds(0, n * ROW_CHUNKS)],
                                         xs_ref.at[pl.ds(0, n * ROW_CHUNKS)], sem.at[buf_slot])
        _wait_rows(trow_ref[tile], make_wait)

    @pl.when(i > 0)
    def _():
        drain(i - 1, 1 - slot)

    @pl.when(i == n_steps - 1)
    def _():
        drain(i, slot)


def _clamped_maps(np_steps):
    pmap = lambda i, *_: (jnp.minimum(i, np_steps - 1), 0)
    smap = lambda i, *_: (jnp.maximum(i - np_steps, 0), 0)
    return pmap, smap


def _dispatch_call(tables, hn_p, hn_s, rt_p, rt_s, n_sorted_rows):
    n_p, n_s = hn_p.shape[0], hn_s.shape[0]
    np_steps = n_p // TOK_TILE
    n_steps = np_steps + n_s // TOK_TILE
    pmap, smap = _clamped_maps(np_steps)
    out_rows = (n_sorted_rows + 8) * ROW_CHUNKS
    return pl.pallas_call(
        functools.partial(_dispatch_kernel, n_prompt_steps=np_steps),
        grid_spec=pltpu.PrefetchScalarGridSpec(
            num_scalar_prefetch=len(tables), grid=(n_steps,),
            in_specs=[pl.BlockSpec((TOK_TILE, D_MODEL), pmap),
                      pl.BlockSpec((TOK_TILE, D_MODEL), smap),
                      pl.BlockSpec((TOK_TILE, SUBLANES), pmap),
                      pl.BlockSpec((TOK_TILE, SUBLANES), smap)],
            out_specs=pl.BlockSpec(memory_space=pl.ANY),
            scratch_shapes=[pltpu.VMEM((2, LOC_ROWS * ROW_CHUNKS, LANES), U32),
                            pltpu.VMEM((MOE_TILE * ROW_CHUNKS, LANES), U32),
                            pltpu.SemaphoreType.DMA((2,)),
                            pltpu.SemaphoreType.DMA(())]),
        out_shape=jax.ShapeDtypeStruct((out_rows, LANES), U32),
        compiler_params=pltpu.CompilerParams(
            dimension_semantics=("arbitrary",), vmem_limit_bytes=VMEM_LIMIT),
        name="dispatch",
    )(*tables, hn_p, hn_s, rt_p, rt_s)


def _moe_kernel(te_ref, nu_ref, par_ref, nxt_ref, half_ref, xs_ref, wg_hbm, wu_hbm, wd_hbm, ys_ref,
                wg_f32, wu_f32, wd_f32, wg_bf, wu_bf, wd_bf, wsem):
    i = pl.program_id(0)
    used = i < nu_ref[0]

    def store_zeros(ref, n_rows):
        ij = (lax.broadcasted_iota(I32, (n_rows, D_MODEL), 0)
              + lax.broadcasted_iota(I32, (n_rows, D_MODEL), 1))
        _store_packed(ref, jnp.where(ij < jnp.minimum(nu_ref[0], 0), 1.0, 0.0), n_rows)

    def weight_copies(expert, buf):
        return [pltpu.make_async_copy(hbm.at[expert], vmem.at[buf], wsem.at[buf])
                for hbm, vmem in ((wg_hbm, wg_f32), (wu_hbm, wu_f32), (wd_hbm, wd_f32))]

    @pl.when(used)
    def _():
        prev = te_ref[jnp.maximum(i - 1, 0)]

        @pl.when((i == 0) | (te_ref[i] != prev))
        def _():
            buf = par_ref[i]

            @pl.when(i == 0)
            def _():
                for cp in weight_copies(te_ref[0], 0):
                    cp.start()

            for cp in weight_copies(te_ref[i], buf):
                cp.wait()
            wg_bf[...] = wg_f32[buf].astype(BF16)
            wu_bf[...] = wu_f32[buf].astype(BF16)
            wd_bf[...] = wd_f32[buf].astype(BF16)

            @pl.when(nxt_ref[i] >= 0)
            def _():
                for cp in weight_copies(nxt_ref[i], 1 - buf):
                    cp.start()

        def expert_rows(n_rows):
            lo, hi = _load_packed(xs_ref, n_rows)
            x = jnp.concatenate([lo, hi], axis=-1).astype(BF16)
            g = jnp.dot(x, wg_bf[...], preferred_element_type=F32)
            u = jnp.dot(x, wu_bf[...], preferred_element_type=F32)
            hh = (g * jax.nn.sigmoid(g)) * u
            y = jnp.dot(hh.astype(BF16), wd_bf[...], preferred_element_type=F32)
            _store_packed(ys_ref, y, n_rows)

        @pl.when(half_ref[i] == 0)
        def _():
            expert_rows(MOE_TILE)

        @pl.when(half_ref[i] != 0)
        def _():
            expert_rows(MOE_TILE // 2)
            store_zeros(ys_ref.at[pl.ds(MOE_TILE // 2 * ROW_CHUNKS, MOE_TILE // 2 * ROW_CHUNKS)],
                        MOE_TILE // 2)

    @pl.when(jnp.logical_not(used))
    def _():
        store_zeros(ys_ref, MOE_TILE)


def _moe_call(tile_expert, n_used, buf_parity, next_expert, half_only, xs, n_tiles, w_eg, w_eu, w_ed):
    blk = MOE_TILE * ROW_CHUNKS
    any_spec = pl.BlockSpec(memory_space=pl.ANY)
    return pl.pallas_call(
        _moe_kernel,
        grid_spec=pltpu.PrefetchScalarGridSpec(
            num_scalar_prefetch=5, grid=(n_tiles,),
            in_specs=[
                pl.BlockSpec((blk, LANES), lambda i, te, nu, *_: (jnp.minimum(i, nu[0] - 1), 0)),
                any_spec, any_spec, any_spec,
            ],
            out_specs=pl.BlockSpec((blk, LANES), lambda i, *_: (i, 0)),
            scratch_shapes=[pltpu.VMEM((2, D_MODEL, D_EXPERT), F32),
                            pltpu.VMEM((2, D_MODEL, D_EXPERT), F32),
                            pltpu.VMEM((2, D_EXPERT, D_MODEL), F32),
                            pltpu.VMEM((D_MODEL, D_EXPERT), BF16),
                            pltpu.VMEM((D_MODEL, D_EXPERT), BF16),
                            pltpu.VMEM((D_EXPERT, D_MODEL), BF16),
                            pltpu.SemaphoreType.DMA((2,))]),
        out_shape=jax.ShapeDtypeStruct((n_tiles * blk, LANES), U32),
        compiler_params=pltpu.CompilerParams(
            dimension_semantics=("arbitrary",), vmem_limit_bytes=VMEM_LIMIT),
        name="experts",
    )(tile_expert, n_used, buf_parity, next_expert, half_only, xs, w_eg, w_eu, w_ed)


def _final_kernel(l8, g8, l4, g4, l2, g2, cnt_ref, trow_ref,
                  hp_ref, hs_ref, rtp_ref, rts_ref, pp_ref, ps_ref,
                  ys_ref, wple_ref, wgate_ref, gple_ref, gfin_ref, outp_ref, outs_ref, ybuf, sem,
                  *, n_prompt_steps):
    i = pl.program_id(0)
    n_steps = pl.num_programs(0)
    slot = i % 2
    local_tabs, sorted_tabs = (l8, l4, l2), (g8, g4, g2)

    def run_copy(buf_slot, s, n_rows, k):
        src = ys_ref.at[_rows(sorted_tabs[s][k], n_rows)]
        dst = ybuf.at[buf_slot, _rows(local_tabs[s][k], n_rows)]
        return pltpu.make_async_copy(src, dst, sem.at[buf_slot])

    def issue(tile, buf_slot):
        _for_each_copy(tile, cnt_ref, lambda s, n_rows, k: run_copy(buf_slot, s, n_rows, k).start())

    @pl.when(i == 0)
    def _():
        ybuf[...] = jnp.zeros(ybuf.shape, U32)
        issue(0, 0)

    @pl.when(i + 1 < n_steps)
    def _():
        issue(i + 1, 1 - slot)

    def make_wait(n):
        return pltpu.make_async_copy(ys_ref.at[pl.ds(0, n * ROW_CHUNKS)],
                                     ybuf.at[slot, pl.ds(0, n * ROW_CHUNKS)], sem.at[slot])
    _wait_rows(trow_ref[i], make_wait)

    is_prompt = i < n_prompt_steps
    h = jnp.where(is_prompt, hp_ref[...], hs_ref[...])
    p = jnp.where(is_prompt, pp_ref[...], ps_ref[...])
    rt = jnp.where(is_prompt, rtp_ref[...], rts_ref[...])
    lpos1, lpos2 = rt[:, 2:3], rt[:, 3:4]
    lane = lax.broadcasted_iota(I32, (TOK_TILE, LOC_ROWS), 1).astype(F32)
    sel1 = lane == lpos1
    sel2 = lane == lpos2
    row1, row2, w1_row, w2_row = _as_rows(lpos1, lpos2, rt[:, 4:5], rt[:, 5:6])
    grow = lax.broadcasted_iota(I32, (LOC_ROWS, TOK_TILE), 0).astype(F32)
    wrow = jnp.sum(jnp.where(grow == row1, w1_row, 0.0) + jnp.where(grow == row2, w2_row, 0.0),
                   axis=-1, keepdims=True)
    lo, hi = _load_packed(ybuf.at[slot], LOC_ROWS)
    yb = (jnp.concatenate([lo, hi], axis=-1) * wrow).astype(BF16)
    perm = jnp.where(sel1 | sel2, 1.0, 0.0).astype(BF16)
    h = h + jnp.dot(perm, yb, preferred_element_type=F32)
    r = _rms(h, gple_ref[...])
    gate = jax.nn.sigmoid(jnp.dot(r.astype(BF16), wgate_ref[...], preferred_element_type=F32))
    h = h + jnp.dot(p.astype(BF16), wple_ref[...], preferred_element_type=F32) * gate
    y = _rms(h, gfin_ref[...])

    @pl.when(is_prompt)
    def _():
        outp_ref[...] = y

    @pl.when(jnp.logical_not(is_prompt))
    def _():
        outs_ref[...] = y


def _final_call(tables, h_p, h_s, rt_p, rt_s, p_p, p_s, ys, wple, wgate, gple, gfin):
    n_p, n_s = h_p.shape[0], h_s.shape[0]
    np_steps = n_p // TOK_TILE
    n_steps = np_steps + n_s // TOK_TILE
    pmap, smap = _clamped_maps(np_steps)
    cmap = lambda i, *_: (0, 0)
    return pl.pallas_call(
        functools.partial(_final_kernel, n_prompt_steps=np_steps),
        grid_spec=pltpu.PrefetchScalarGridSpec(
            num_scalar_prefetch=len(tables), grid=(n_steps,),
            in_specs=[
                pl.BlockSpec((TOK_TILE, D_MODEL), pmap),
                pl.BlockSpec((TOK_TILE, D_MODEL), smap),
                pl.BlockSpec((TOK_TILE, SUBLANES), pmap),
                pl.BlockSpec((TOK_TILE, SUBLANES), smap),
                pl.BlockSpec((TOK_TILE, PLE_DIM), pmap),
                pl.BlockSpec((TOK_TILE, PLE_DIM), smap),
                pl.BlockSpec(memory_space=pl.ANY),
                pl.BlockSpec(wple.shape, cmap),
                pl.BlockSpec(wgate.shape, cmap),
                pl.BlockSpec(gple.shape, cmap),
                pl.BlockSpec(gfin.shape, cmap),
            ],
            out_specs=[pl.BlockSpec((TOK_TILE, D_MODEL), pmap),
                       pl.BlockSpec((TOK_TILE, D_MODEL), smap)],
            scratch_shapes=[pltpu.VMEM((2, LOC_ROWS * ROW_CHUNKS, LANES), U32),
                            pltpu.SemaphoreType.DMA((2,))]),
        out_shape=[jax.ShapeDtypeStruct((n_p, D_MODEL), F32),
                   jax.ShapeDtypeStruct((n_s, D_MODEL), F32)],
        compiler_params=pltpu.CompilerParams(
            dimension_semantics=("arbitrary",), vmem_limit_bytes=VMEM_LIMIT),
        name="final",
    )(*tables, h_p, h_s, rt_p, rt_s, p_p, p_s, ys, wple, wgate, gple, gfin)


def _ceil_to(x, m):
    return ((x + m - 1) // m) * m


def _excl_cumsum(a, axis):
    return jnp.cumsum(a, axis=axis) - a


def _copy_tables(n_copies, first_local, first_sorted, step, cap):
    ends = jnp.cumsum(n_copies, axis=1)
    j = jnp.arange(cap, dtype=I32)[None, :, None]
    e_of = jnp.minimum(jnp.sum(ends[:, None, :] <= j, axis=-1), N_EXPERTS - 1)
    pick = e_of[:, :, None] == jnp.arange(N_EXPERTS, dtype=I32)[None, None, :]
    take = lambda a: jnp.sum(jnp.where(pick, a[:, None, :], 0), axis=-1)
    c = jnp.arange(cap, dtype=I32)[None, :] - (take(ends) - take(n_copies))
    flat = lambda a: a.reshape(-1).astype(I32)
    return flat(take(first_local) + step * c), flat(take(first_sorted) + step * c), ends[:, -1]


def kernel(x_prompt, x_sample, state_pool, p_prompt, p_sample, g_mix, w_in, w_pool, pool_scale,
           g_v, b_v, w_spatial, b_spatial, w_out, g_ffn, w_rg, b_rg, w_re, b_re, w_eg, w_eu, w_ed,
           w_ple, g_ple, w_ple_gate, g_final):
    batch, seq, _ = x_prompt.shape
    dec_batch, dec_seq, _ = x_sample.shape
    assert dec_seq == SUBLANES and g_mix.shape[0] == 1
    n_p = batch * seq
    n_s = dec_batch * dec_seq
    n_tok = n_p + n_s
    past_len = 16384

    row = lambda a: a.reshape(1, -1).astype(F32)
    wr = jnp.concatenate(
        [w_re[0], w_rg[0], jnp.zeros((D_MODEL, LANES - N_EXPERTS - N_EXPERT_GROUPS), F32)], axis=1)
    br = jnp.concatenate(
        [b_re[0], b_rg[0], jnp.zeros((LANES - N_EXPERTS - N_EXPERT_GROUPS,), F32)]).reshape(1, LANES)
    bsp_p = jnp.repeat(jnp.transpose(b_spatial[0]), GMLP_HEAD, axis=1)
    bsp_s = jnp.tile(bsp_p[:dec_seq], (CHUNK // dec_seq, 1))
    wsp_p = w_spatial[0].astype(BF16)
    wsp_s = jnp.tile(w_spatial[0][:, :dec_seq, :dec_seq].astype(BF16),
                     (1, CHUNK // dec_seq, CHUNK // dec_seq))

    def pool_band(first_chunk):
        t = jnp.arange(CHUNK, dtype=I32)[:, None]
        k = jnp.arange(POOL_SPAN, dtype=I32)[None, :]
        mats = []
        for w in POOL_WINDOWS:
            cnt = jnp.minimum(w, t + 1) if first_chunk else jnp.full_like(t, w)
            in_window = (k >= t + 17 - w) & (k <= t + 16)
            mats.append(jnp.where(in_window, 1.0 / cnt.astype(F32), 0.0) - (k == t + 16).astype(F32))
        return jnp.stack(mats).astype(BF16)

    def mixer_weights(wsp, bsp):
        return (row(g_mix[0]), w_in[0].astype(BF16), w_pool[0].astype(BF16), row(pool_scale[0]),
                row(g_v[0]), row(b_v[0]), wsp, bsp, w_out[0].astype(BF16), row(g_ffn[0]),
                wr.astype(BF16), br, pool_band(False), pool_band(True))

    h_p, hn_p, rt_p, cnt_p, hist_p = _mixer_call(
        x_prompt, None, mixer_weights(wsp_p, bsp_p), sample=False, pos0=0)
    h_s, hn_s, rt_s, cnt_s, hist_s, v_s = _mixer_call(
        x_sample.reshape(n_s, D_MODEL), state_pool[0], mixer_weights(wsp_s, bsp_s),
        sample=True, pos0=past_len)

    n_run = jnp.concatenate([cnt_p, cnt_s], axis=0)[:, 0, :N_EXPERTS].astype(I32)
    n_even = _ceil_to(n_run, RUN_ALIGN)
    l_start = _excl_cumsum(n_even, 1)
    tile_rows = jnp.sum(n_even, axis=1)
    base = _excl_cumsum(n_even, 0)
    counts = jnp.sum(n_even, axis=0)
    padded = _ceil_to(counts, MOE_TILE)
    ends = jnp.cumsum(padded)
    offs = ends - padded
    g_start = offs[None, :] + base
    n8 = n_even // 8
    f4 = (n_even // 4) % 2
    f2 = (n_even // 2) % 2
    l8, g8, c8 = _copy_tables(n8, l_start, g_start, 8, COPY_CAP[0])
    l4, g4, c4 = _copy_tables(f4, l_start + 8 * n8, g_start + 8 * n8, 0, COPY_CAP[1])
    l2, g2, c2 = _copy_tables(f2, l_start + 8 * n8 + 4 * f4, g_start + 8 * n8 + 4 * f4, 0, COPY_CAP[2])
    copy_cnt = jnp.stack([c8, c4, c2], axis=1).reshape(-1).astype(I32)
    tile_rows = tile_rows.astype(I32)
    z_start = (offs + counts).astype(I32)
    z_blocks = ((padded - counts + 7) // 8).astype(I32)

    max_rows = 2 * n_tok + n_run.shape[0] * N_EXPERTS * (RUN_ALIGN - 1)
    n_tiles = max_rows // MOE_TILE + N_EXPERTS + 1
    n_used = (ends[-1] // MOE_TILE).astype(I32).reshape(1)
    tile_id = jnp.minimum(jnp.arange(n_tiles, dtype=I32), n_used - 1)
    tile_expert = jnp.sum((tile_id[:, None] * MOE_TILE) >= ends[None, :], axis=1).astype(I32)
    tile_expert = jnp.minimum(tile_expert, N_EXPERTS - 1)
    pick_e = tile_expert[:, None] == jnp.arange(N_EXPERTS, dtype=I32)[None, :]
    data_end = jnp.sum(jnp.where(pick_e, (offs + counts)[None, :], 0), axis=1)
    half_only = (data_end - tile_id * MOE_TILE <= MOE_TILE // 2).astype(I32)
    e_ids = jnp.arange(N_EXPERTS, dtype=I32)
    nonempty = padded > 0
    seg_index = jnp.cumsum(nonempty.astype(I32)) - 1
    later = nonempty[None, :] & (e_ids[None, :] > e_ids[:, None])
    next_of = jnp.min(jnp.where(later, e_ids[None, :], N_EXPERTS), axis=1)
    next_of = jnp.where(next_of == N_EXPERTS, -1, next_of)
    buf_parity = (jnp.sum(jnp.where(pick_e, seg_index[None, :], 0), axis=1) % 2).astype(I32)
    next_expert = jnp.sum(jnp.where(pick_e, next_of[None, :], 0), axis=1).astype(I32)

    copy_tabs = (l8, g8, l4, g4, l2, g2, copy_cnt, tile_rows)
    xs = _dispatch_call(copy_tabs + (z_start, z_blocks, n_used), hn_p, hn_s, rt_p, rt_s,
                        n_tiles * MOE_TILE)
    ys = _moe_call(tile_expert, n_used, buf_parity, next_expert, half_only, xs, n_tiles,
                   w_eg[0], w_eu[0], w_ed[0])
    y_p, y_s = _final_call(copy_tabs, h_p, h_s, rt_p, rt_s,
                           p_prompt[0].reshape(n_p, PLE_DIM), p_sample[0].reshape(n_s, PLE_DIM), ys,
                           w_ple[0].astype(BF16), w_ple_gate[0].astype(BF16), row(g_ple[0]), row(g_final))

    return (y_p.reshape(batch, seq, D_MODEL),
            y_s.reshape(dec_batch, dec_seq, D_MODEL),
            hist_p[None],
            hist_s[None],
            v_s.reshape(1, dec_batch, dec_seq, GMLP_WIDTH))
```

```python
import functools

import jax
import jax.numpy as jnp
from jax import lax
from jax.experimental import pallas as pl
from jax.experimental.pallas import tpu as pltpu

D_MODEL = 1024
POOL_WIDTH = 512
GMLP_WIDTH = 512
IN_WIDTH = POOL_WIDTH + 2 * GMLP_WIDTH
POOL_WINDOWS = (2, 4, 8, 16)
POOL_GROUP = 128
POOL_HIST = 15
CHUNK = 128
POOL_SPAN = CHUNK + 16
N_GMLP_HEADS = 4
GMLP_HEAD = 128
N_EXPERT_GROUPS = 4
EXPERTS_PER_GROUP = 8
N_EXPERTS = 32
D_EXPERT = 512
PLE_DIM = 256
EPS = 1e-6

LANES = 128
SUBLANES = 8
HALF = D_MODEL // 2
ROW_CHUNKS = HALF // LANES

TOK_TILE = 256
MIX_TILE = 512
MOE_TILE = 512
RUN_ALIGN = 2
COPY_ROWS = (8, 4, 2)
LOC_ROWS = 2 * TOK_TILE + N_EXPERTS * (RUN_ALIGN - 1)
COPY_CAP = (LOC_ROWS // 8 + 1, N_EXPERTS, N_EXPERTS)
VMEM_LIMIT = 56 * 1024 * 1024

F32 = jnp.float32
BF16 = jnp.bfloat16
I32 = jnp.int32
U32 = jnp.uint32


def _rms(x, g):
    return x * lax.rsqrt(jnp.mean(x * x, axis=-1, keepdims=True) + EPS) * g


def _gelu_tanh(x):
    inner = x * (0.7978845608028654 + (0.7978845608028654 * 0.044715) * (x * x))
    half = 0.5 * x
    return half + half * jnp.tanh(inner)


def _store_packed(ref, val, n_rows):
    packed = pltpu.pack_elementwise([val[:, :HALF], val[:, HALF:]], packed_dtype=BF16)
    for k in range(ROW_CHUNKS):
        ref[pl.ds(k, n_rows, stride=ROW_CHUNKS), :] = packed[:, k * LANES:(k + 1) * LANES]


def _load_packed(ref, n_rows):
    packed = jnp.concatenate(
        [ref[pl.ds(k, n_rows, stride=ROW_CHUNKS), :] for k in range(ROW_CHUNKS)], axis=-1)
    lo = pltpu.unpack_elementwise(packed, index=0, packed_dtype=BF16, unpacked_dtype=F32)
    hi = pltpu.unpack_elementwise(packed, index=1, packed_dtype=BF16, unpacked_dtype=F32)
    return lo, hi


def _rows(start, n):
    return pl.ds(pl.multiple_of(start * ROW_CHUNKS, SUBLANES), n * ROW_CHUNKS)


def _route(hn, wr_ref, br_ref, n_tok):
    logits = jnp.dot(hn, wr_ref[...], preferred_element_type=F32) + br_ref[...]

    lane = lax.broadcasted_iota(I32, (n_tok, LANES), 1).astype(F32)
    neg = jnp.float32(-jnp.inf)
    big = jnp.float32(1e9)
    gmask = (lane >= N_EXPERTS) & (lane < N_EXPERTS + N_EXPERT_GROUPS)
    gl = jnp.where(gmask, logits, neg)
    gmax = jnp.max(gl, axis=-1, keepdims=True)
    gidx = jnp.min(jnp.where(gl == gmax, lane, big), axis=-1, keepdims=True) - N_EXPERTS
    gden = jnp.sum(jnp.exp(gl - gmax), axis=-1, keepdims=True)
    gprob = 1.0 / gden

    lo = gidx * EXPERTS_PER_GROUP
    emask = (lane >= lo) & (lane < lo + EXPERTS_PER_GROUP)
    el = jnp.where(emask, logits, neg)
    m1 = jnp.max(el, axis=-1, keepdims=True)
    i1 = jnp.min(jnp.where(el == m1, lane, big), axis=-1, keepdims=True)
    el2 = jnp.where(lane == i1, neg, el)
    m2 = jnp.max(el2, axis=-1, keepdims=True)
    i2 = jnp.min(jnp.where(el2 == m2, lane, big), axis=-1, keepdims=True)
    t = jnp.exp(m2 - m1)
    w1 = gprob / (1.0 + t)
    w2 = gprob * t / (1.0 + t)

    sel1 = lane == i1
    sel2 = lane == i2
    oh = jnp.where(sel1 | sel2, 1.0, 0.0).astype(BF16)
    r = lax.broadcasted_iota(I32, (n_tok, n_tok), 0)
    c = lax.broadcasted_iota(I32, (n_tok, n_tok), 1)
    tri = jnp.where(c < r, 1.0, 0.0).astype(BF16)
    before = jnp.dot(tri, oh, preferred_element_type=F32)
    counts = jnp.sum(oh.astype(F32), axis=0, keepdims=True)
    assert RUN_ALIGN == 2
    c_even = counts + (counts - 2.0 * jnp.floor(counts * 0.5))
    c_hi = jnp.floor(c_even * (1.0 / 256.0))
    piece = lax.broadcasted_iota(I32, (SUBLANES, LANES), 0)
    pieces = jnp.where(piece == 0, c_hi, jnp.where(piece == 1, c_even - 256.0 * c_hi, 0.0))
    er = lax.broadcasted_iota(I32, (LANES, LANES), 0)
    ec = lax.broadcasted_iota(I32, (LANES, LANES), 1)
    upper = jnp.where(er < ec, 1.0, 0.0).astype(BF16)
    prefix = jnp.dot(pieces.astype(BF16), upper, preferred_element_type=F32)
    lstart = 256.0 * prefix[0:1, :] + prefix[1:2, :]
    lpos1 = jnp.sum(jnp.where(sel1, before + lstart, 0.0), axis=-1, keepdims=True)
    lpos2 = jnp.sum(jnp.where(sel2, before + lstart, 0.0), axis=-1, keepdims=True)

    out = jnp.where(lane == 0, i1, 0.0)
    for k, val in enumerate((i2, lpos1, lpos2, w1, w2), start=1):
        out = jnp.where(lane == k, val, out)
    return out, counts


def _mixer_kernel(*refs, n_tok, sample, pos0):
    if sample:
        (x_ref, hist_ref, gmix_ref, win_ref, wpool_ref, pscale_ref, gv_ref, bv_ref, wsp_ref,
         bsp_ref, wout_ref, gffn_ref, wr_ref, br_ref, band_ref, band0_ref,
         h_ref, hn_ref, rt_ref, cnt_ref, hist_out_ref, v_ref, ext_scr) = refs
    else:
        (x_ref, gmix_ref, win_ref, wpool_ref, pscale_ref, gv_ref, bv_ref, wsp_ref,
         bsp_ref, wout_ref, gffn_ref, wr_ref, br_ref, band_ref, band0_ref,
         h_ref, hn_ref, rt_ref, cnt_ref, hist_out_ref, ext_scr) = refs

    x = x_ref[...]
    xn = _rms(x, gmix_ref[...])
    z = jnp.dot(xn.astype(BF16), win_ref[...], preferred_element_type=F32)
    a = z[:, :POOL_WIDTH]
    uv = _gelu_tanh(z[:, POOL_WIDTH:])
    u = uv[:, :GMLP_WIDTH]
    v = uv[:, GMLP_WIDTH:]
    mu = jnp.mean(v, axis=-1, keepdims=True)
    vc = v - mu
    v = vc * lax.rsqrt(jnp.mean(vc * vc, axis=-1, keepdims=True) + EPS) * gv_ref[...] + bv_ref[...]

    pooled = []
    if sample:
        n_seq = n_tok // SUBLANES
        ext_scr[:, 1:1 + POOL_HIST, :] = hist_ref[...]
        ext_scr[:, 1 + POOL_HIST:, :] = a.reshape(n_seq, SUBLANES, POOL_WIDTH)
        hist_out_ref[...] = ext_scr[:, 1 + SUBLANES:, :]
        pos = pos0 + lax.broadcasted_iota(I32, (n_seq, SUBLANES, 1), 1)
        for gi, w in enumerate(POOL_WINDOWS):
            cols = slice(gi * POOL_GROUP, (gi + 1) * POOL_GROUP)
            win = ext_scr[:, 16:16 + SUBLANES, cols]
            for j in range(1, w):
                win = win + ext_scr[:, 16 - j:16 - j + SUBLANES, cols]
            cnt = jnp.minimum(w, pos + 1).astype(F32)
            pooled.append((win / cnt).reshape(n_tok, POOL_GROUP) - a[:, cols])
    else:
        l = pl.program_id(1)

        @pl.when(l == 0)
        def _():
            ext_scr[0:16, :] = jnp.zeros((16, POOL_WIDTH), F32)

        ext_scr[16:16 + n_tok, :] = a
        chunks = []
        for ci in range(n_tok // CHUNK):
            ext_c = ext_scr[ci * CHUNK:ci * CHUNK + POOL_SPAN, :].astype(BF16)
            groups = []
            for gi in range(len(POOL_WINDOWS)):
                band = band_ref[gi]
                if ci == 0:
                    band = jnp.where(l == 0, band0_ref[gi], band)
                groups.append(jnp.dot(band, ext_c[:, gi * POOL_GROUP:(gi + 1) * POOL_GROUP],
                                      preferred_element_type=F32))
            chunks.append(groups)
        pooled = [jnp.concatenate([chunks[ci][gi] for ci in range(n_tok // CHUNK)], axis=0)
                  for gi in range(len(POOL_WINDOWS))]
        tail = ext_scr[n_tok + 1:n_tok + 16, :]
        hist_out_ref[...] = tail
        ext_scr[1:16, :] = tail

    pool_out = jnp.concatenate(
        [jnp.dot(pooled[gi].astype(BF16), wpool_ref[gi], preferred_element_type=F32)
         for gi in range(len(POOL_WINDOWS))], axis=-1) * pscale_ref[...]

    blk = SUBLANES if sample else CHUNK
    tr = lax.broadcasted_iota(I32, (CHUNK, CHUNK), 0)
    sc = lax.broadcasted_iota(I32, (CHUNK, CHUNK), 1)
    mask = (sc <= tr) & ((sc // blk) == (tr // blk))
    wsp = [jnp.where(mask, wsp_ref[hh], jnp.zeros((), BF16)) for hh in range(N_GMLP_HEADS)]
    v_bf = v.astype(BF16)
    mixed_chunks = []
    for ci in range(n_tok // CHUNK):
        rows = slice(ci * CHUNK, (ci + 1) * CHUNK)
        heads = [jnp.dot(wsp[hh], v_bf[rows, hh * GMLP_HEAD:(hh + 1) * GMLP_HEAD],
                         preferred_element_type=F32) for hh in range(N_GMLP_HEADS)]
        mixed_chunks.append(jnp.concatenate(heads, axis=-1) + bsp_ref[...])
    mixed = jnp.concatenate(mixed_chunks, axis=0)
    gmlp_out = u * mixed

    cat = jnp.concatenate([pool_out, gmlp_out], axis=-1)
    h = x + jnp.dot(cat.astype(BF16), wout_ref[...], preferred_element_type=F32)
    hn = _rms(h, gffn_ref[...]).astype(BF16)

    h_ref[...] = h
    hn_ref[...] = hn
    if sample:
        v_ref[...] = v
    for s in range(n_tok // TOK_TILE):
        rows = slice(s * TOK_TILE, (s + 1) * TOK_TILE)
        rt, counts = _route(hn[rows], wr_ref, br_ref, TOK_TILE)
        rt_ref[rows, :] = rt[:, :SUBLANES]
        cnt_ref[s] = counts


def _full(shape):
    return pl.BlockSpec(shape, lambda *_: (0,) * len(shape))


def _mixer_call(x, hist, weights, *, sample, pos0):
    w_specs = [_full(w.shape) for w in weights]
    n_tok = MIX_TILE
    if sample:
        n_rows = x.shape[0]
        n_seq = n_tok // SUBLANES
        grid = (n_rows // n_tok,)
        tok_map = lambda i: (i, 0)
        tile_map = lambda i: (i, 0, 0)
        in_specs = [pl.BlockSpec((n_tok, D_MODEL), tok_map),
                    pl.BlockSpec((n_seq, POOL_HIST, POOL_WIDTH), lambda i: (i, 0, 0))] + w_specs
        hist_shape = jax.ShapeDtypeStruct(hist.shape, F32)
        hist_spec = pl.BlockSpec((n_seq, POOL_HIST, POOL_WIDTH), lambda i: (i, 0, 0))
        ext = pltpu.VMEM((n_seq, 16 + SUBLANES, POOL_WIDTH), F32)
        args = (x, hist)
    else:
        b, seq, _ = x.shape
        n_rows = b * seq
        n_l = seq // n_tok
        grid = (b, n_l)
        tok_map = lambda bi, li: (bi * n_l + li, 0)
        tile_map = lambda bi, li: (bi * n_l + li, 0, 0)
        in_specs = [pl.BlockSpec((None, n_tok, D_MODEL), lambda bi, li: (bi, li, 0))] + w_specs
        hist_shape = jax.ShapeDtypeStruct((b, POOL_HIST, POOL_WIDTH), F32)
        hist_spec = pl.BlockSpec((None, POOL_HIST, POOL_WIDTH), lambda bi, li: (bi, 0, 0))
        ext = pltpu.VMEM((16 + n_tok, POOL_WIDTH), F32)
        args = (x,)
    out_shape = [jax.ShapeDtypeStruct((n_rows, D_MODEL), F32),
                 jax.ShapeDtypeStruct((n_rows, D_MODEL), BF16),
                 jax.ShapeDtypeStruct((n_rows, SUBLANES), F32),
                 jax.ShapeDtypeStruct((n_rows // TOK_TILE, 1, LANES), F32),
                 hist_shape]
    out_specs = [pl.BlockSpec((n_tok, D_MODEL), tok_map),
                 pl.BlockSpec((n_tok, D_MODEL), tok_map),
                 pl.BlockSpec((n_tok, SUBLANES), tok_map),
                 pl.BlockSpec((n_tok // TOK_TILE, 1, LANES), tile_map),
                 hist_spec]
    if sample:
        out_shape.append(jax.ShapeDtypeStruct((n_rows, GMLP_WIDTH), F32))
        out_specs.append(pl.BlockSpec((n_tok, GMLP_WIDTH), tok_map))
    return pl.pallas_call(
        functools.partial(_mixer_kernel, n_tok=n_tok, sample=sample, pos0=pos0),
        grid=grid, in_specs=in_specs, out_specs=out_specs, out_shape=out_shape,
        scratch_shapes=[ext],
        compiler_params=pltpu.CompilerParams(
            dimension_semantics=("arbitrary",) * len(grid), vmem_limit_bytes=VMEM_LIMIT),
        name="mixer_sample" if sample else "mixer_prompt",
    )(*args, *weights)


def _for_each_copy(tile, cnt_ref, fn, active=True):
    for s, n_rows in enumerate(COPY_ROWS):
        def body(j, carry, s=s, n_rows=n_rows):
            fn(s, n_rows, tile * COPY_CAP[s] + j)
            return carry
        lax.fori_loop(0, jnp.where(active, cnt_ref[tile * len(COPY_ROWS) + s], 0), body, 0)


def _wait_rows(n_rows_total, make_wait):
    units = n_rows_total // RUN_ALIGN
    bit = 0
    while (RUN_ALIGN << bit) <= LOC_ROWS:
        @pl.when((units >> bit) & 1 == 1)
        def _(bit=bit):
            make_wait(RUN_ALIGN << bit).wait()
        bit += 1


def _as_rows(*cols):
    n = cols[0].shape[0]
    lane = lax.broadcasted_iota(I32, (n, LANES), 1)
    packed = jnp.zeros((n, LANES), F32)
    for k, col in enumerate(cols):
        packed = jnp.where(lane == k, col, packed)
    rows = packed.T
    return [rows[k:k + 1, :] for k in range(len(cols))]


def _dispatch_kernel(l8, g8, l4, g4, l2, g2, cnt_ref, trow_ref, zs_ref, zb_ref, nu_ref,
                     hnp_ref, hns_ref, rtp_ref, rts_ref, xs_ref, loc, zbuf, sem, zsem,
                     *, n_prompt_steps):
    i = pl.program_id(0)
    n_steps = pl.num_programs(0)
    slot = i % 2
    local_tabs, sorted_tabs = (l8, l4, l2), (g8, g4, g2)

    @pl.when(i == 0)
    def _():
        zbuf[...] = jnp.zeros(zbuf.shape, U32)
        n_tiles = xs_ref.shape[0] // (MOE_TILE * ROW_CHUNKS)

        def zero_copy(e, c):
            return pltpu.make_async_copy(zbuf.at[pl.ds(0, 8 * ROW_CHUNKS)],
                                         xs_ref.at[_rows(zs_ref[e] + c * 8, 8)], zsem)

        def tail_copy(t):
            return pltpu.make_async_copy(zbuf, xs_ref.at[_rows(t * MOE_TILE, MOE_TILE)], zsem)

        def per_expert(e, total):
            def per_block(c, carry):
                zero_copy(e, c).start()
                return carry
            lax.fori_loop(0, zb_ref[e], per_block, 0)
            return total + zb_ref[e]

        total = lax.fori_loop(0, N_EXPERTS, per_expert, 0)

        def wait_body(_, carry):
            zero_copy(0, 0).wait()
            return carry
        lax.fori_loop(0, total, wait_body, 0)

        def tail_start(t, carry):
            tail_copy(t).start()
            return carry

        def tail_wait(t, carry):
            tail_copy(t).wait()
            return carry

        spill = pltpu.make_async_copy(zbuf.at[pl.ds(0, 8 * ROW_CHUNKS)],
                                      xs_ref.at[_rows(n_tiles * MOE_TILE, 8)], zsem)
        lax.fori_loop(nu_ref[0], n_tiles, tail_start, 0)
        spill.start()
        lax.fori_loop(nu_ref[0], n_tiles, tail_wait, 0)
        spill.wait()

    is_prompt = i < n_prompt_steps
    rt = jnp.where(is_prompt, rtp_ref[...], rts_ref[...])
    row1, row2 = _as_rows(rt[:, 2:3], rt[:, 3:4])
    grow = lax.broadcasted_iota(I32, (LOC_ROWS, TOK_TILE), 0).astype(F32)
    perm = jnp.where((grow == row1) | (grow == row2), 1.0, 0.0).astype(BF16)
    hn = jnp.where(is_prompt, hnp_ref[...], hns_ref[...])
    grouped = jnp.dot(perm, hn, preferred_element_type=F32)
    _store_packed(loc.at[slot], grouped, LOC_ROWS)

    def run_copy(buf_slot, s, n_rows, k):
        src = loc.at[buf_slot, _rows(local_tabs[s][k], n_rows)]
        dst = xs_ref.at[_rows(sorted_tabs[s][k], n_rows)]
        return pltpu.make_async_copy(src, dst, sem.at[buf_slot])

    _for_each_copy(i, cnt_ref, lambda s, n_rows, k: run_copy(slot, s, n_rows, k).start())

    def drain(tile, buf_slot):
        def make_wait(n):
            return pltpu.make_async_copy(loc.at[buf_slot, pl.ds(0, n * ROW_CHUNKS)],
                                         xs_ref.at[pl.ds(0, n * ROW_CHUNKS)], sem.at[buf_slot])
        _wait_rows(trow_ref[tile], make_wait)

    @pl.when(i > 0)
    def _():
        drain(i - 1, 1 - slot)

    @pl.when(i == n_steps - 1)
    def _():
        drain(i, slot)


def _clamped_maps(np_steps):
    pmap = lambda i, *_: (jnp.minimum(i, np_steps - 1), 0)
    smap = lambda i, *_: (jnp.maximum(i - np_steps, 0), 0)
    return pmap, smap


def _dispatch_call(tables, hn_p, hn_s, rt_p, rt_s, n_sorted_rows):
    n_p, n_s = hn_p.shape[0], hn_s.shape[0]
    np_steps = n_p // TOK_TILE
    n_steps = np_steps + n_s // TOK_TILE
    pmap, smap = _clamped_maps(np_steps)
    out_rows = (n_sorted_rows + 8) * ROW_CHUNKS
    return pl.pallas_call(
        functools.partial(_dispatch_kernel, n_prompt_steps=np_steps),
        grid_spec=pltpu.PrefetchScalarGridSpec(
            num_scalar_prefetch=len(tables), grid=(n_steps,),
            in_specs=[pl.BlockSpec((TOK_TILE, D_MODEL), pmap),
                      pl.BlockSpec((TOK_TILE, D_MODEL), smap),
                      pl.BlockSpec((TOK_TILE, SUBLANES), pmap),
                      pl.BlockSpec((TOK_TILE, SUBLANES), smap)],
            out_specs=pl.BlockSpec(memory_space=pl.ANY),
            scratch_shapes=[pltpu.VMEM((2, LOC_ROWS * ROW_CHUNKS, LANES), U32),
                            pltpu.VMEM((MOE_TILE * ROW_CHUNKS, LANES), U32),
                            pltpu.SemaphoreType.DMA((2,)),
                            pltpu.SemaphoreType.DMA(())]),
        out_shape=jax.ShapeDtypeStruct((out_rows, LANES), U32),
        compiler_params=pltpu.CompilerParams(
            dimension_semantics=("arbitrary",), vmem_limit_bytes=VMEM_LIMIT),
        name="dispatch",
    )(*tables, hn_p, hn_s, rt_p, rt_s)


def _moe_kernel(te_ref, nu_ref, par_ref, nxt_ref, half_ref, xs_ref, wg_hbm, wu_hbm, wd_hbm, ys_ref,
                wg_f32, wu_f32, wd_f32, wg_bf, wu_bf, wd_bf, wsem):
    i = pl.program_id(0)
    used = i < nu_ref[0]

    def store_zeros(ref, n_rows):
        ij = (lax.broadcasted_iota(I32, (n_rows, D_MODEL), 0)
              + lax.broadcasted_iota(I32, (n_rows, D_MODEL), 1))
        _store_packed(ref, jnp.where(ij < jnp.minimum(nu_ref[0], 0), 1.0, 0.0), n_rows)

    def weight_copies(expert, buf):
        return [pltpu.make_async_copy(hbm.at[expert], vmem.at[buf], wsem.at[buf])
                for hbm, vmem in ((wg_hbm, wg_f32), (wu_hbm, wu_f32), (wd_hbm, wd_f32))]

    @pl.when(used)
    def _():
        prev = te_ref[jnp.maximum(i - 1, 0)]

        @pl.when((i == 0) | (te_ref[i] != prev))
        def _():
            buf = par_ref[i]

            @pl.when(i == 0)
            def _():
                for cp in weight_copies(te_ref[0], 0):
                    cp.start()

            for cp in weight_copies(te_ref[i], buf):
                cp.wait()
            wg_bf[...] = wg_f32[buf].astype(BF16)
            wu_bf[...] = wu_f32[buf].astype(BF16)
            wd_bf[...] = wd_f32[buf].astype(BF16)

            @pl.when(nxt_ref[i] >= 0)
            def _():
                for cp in weight_copies(nxt_ref[i], 1 - buf):
                    cp.start()

        def expert_rows(n_rows):
            lo, hi = _load_packed(xs_ref, n_rows)
            x = jnp.concatenate([lo, hi], axis=-1).astype(BF16)
            g = jnp.dot(x, wg_bf[...], preferred_element_type=F32)
            u = jnp.dot(x, wu_bf[...], preferred_element_type=F32)
            hh = (g * jax.nn.sigmoid(g)) * u
            y = jnp.dot(hh.astype(BF16), wd_bf[...], preferred_element_type=F32)
            _store_packed(ys_ref, y, n_rows)

        @pl.when(half_ref[i] == 0)
        def _():
            expert_rows(MOE_TILE)

        @pl.when(half_ref[i] != 0)
        def _():
            expert_rows(MOE_TILE // 2)
            store_zeros(ys_ref.at[pl.ds(MOE_TILE // 2 * ROW_CHUNKS, MOE_TILE // 2 * ROW_CHUNKS)],
                        MOE_TILE // 2)

    @pl.when(jnp.logical_not(used))
    def _():
        store_zeros(ys_ref, MOE_TILE)


def _moe_call(tile_expert, n_used, buf_parity, next_expert, half_only, xs, n_tiles, w_eg, w_eu, w_ed):
    blk = MOE_TILE * ROW_CHUNKS
    any_spec = pl.BlockSpec(memory_space=pl.ANY)
    return pl.pallas_call(
        _moe_kernel,
        grid_spec=pltpu.PrefetchScalarGridSpec(
            num_scalar_prefetch=5, grid=(n_tiles,),
            in_specs=[
                pl.BlockSpec((blk, LANES), lambda i, te, nu, *_: (jnp.minimum(i, nu[0] - 1), 0)),
                any_spec, any_spec, any_spec,
            ],
            out_specs=pl.BlockSpec((blk, LANES), lambda i, *_: (i, 0)),
            scratch_shapes=[pltpu.VMEM((2, D_MODEL, D_EXPERT), F32),
                            pltpu.VMEM((2, D_MODEL, D_EXPERT), F32),
                            pltpu.VMEM((2, D_EXPERT, D_MODEL), F32),
                            pltpu.VMEM((D_MODEL, D_EXPERT), BF16),
                            pltpu.VMEM((D_MODEL, D_EXPERT), BF16),
                            pltpu.VMEM((D_EXPERT, D_MODEL), BF16),
                            pltpu.SemaphoreType.DMA((2,))]),
        out_shape=jax.ShapeDtypeStruct((n_tiles * blk, LANES), U32),
        compiler_params=pltpu.CompilerParams(
            dimension_semantics=("arbitrary",), vmem_limit_bytes=VMEM_LIMIT),
        name="experts",
    )(tile_expert, n_used, buf_parity, next_expert, half_only, xs, w_eg, w_eu, w_ed)


def _final_kernel(l8, g8, l4, g4, l2, g2, cnt_ref, trow_ref,
                  h_ref, rt_ref, p_ref, ys_ref, wple_ref, wgate_ref, gple_ref, gfin_ref, out_ref,
                  ybuf, yb_s, perm_s, sem, *, tile_base, n_tiles):
    s = pl.program_id(0)
    a_slot = s % 2
    b_slot = 1 - a_slot
    local_tabs, sorted_tabs = (l8, l4, l2), (g8, g4, g2)

    def run_copy(buf_slot, sz, n_rows, k):
        src = ys_ref.at[_rows(sorted_tabs[sz][k], n_rows)]
        dst = ybuf.at[buf_slot, _rows(local_tabs[sz][k], n_rows)]
        return pltpu.make_async_copy(src, dst, sem.at[buf_slot])

    def issue(tile, buf_slot, active):
        _for_each_copy(tile_base + jnp.minimum(tile, n_tiles - 1), cnt_ref,
                       lambda sz, n_rows, k: run_copy(buf_slot, sz, n_rows, k).start(), active)

    @pl.when(s == 0)
    def _():
        ybuf[...] = jnp.zeros(ybuf.shape, U32)
        yb_s[...] = jnp.zeros(yb_s.shape, BF16)
        perm_s[...] = jnp.zeros(perm_s.shape, BF16)
        issue(0, 0, True)

    issue(s + 1, b_slot, s + 1 < n_tiles)

    def make_wait(n):
        return pltpu.make_async_copy(ys_ref.at[pl.ds(0, n * ROW_CHUNKS)],
                                     ybuf.at[a_slot, pl.ds(0, n * ROW_CHUNKS)], sem.at[a_slot])
    a_rows = jnp.where(s < n_tiles, trow_ref[tile_base + jnp.minimum(s, n_tiles - 1)], 0)
    _wait_rows(a_rows, make_wait)

    h = h_ref[...]
    h = h + jnp.dot(perm_s[b_slot], yb_s[b_slot], preferred_element_type=F32)
    r = _rms(h, gple_ref[...])
    gate = jax.nn.sigmoid(jnp.dot(r.astype(BF16), wgate_ref[...], preferred_element_type=F32))
    h = h + jnp.dot(p_ref[...].astype(BF16), wple_ref[...], preferred_element_type=F32) * gate
    out_ref[...] = _rms(h, gfin_ref[...])

    rt = rt_ref[...]
    lpos1, lpos2 = rt[:, 2:3], rt[:, 3:4]
    lane = lax.broadcasted_iota(I32, (TOK_TILE, LOC_ROWS), 1).astype(F32)
    perm_s[a_slot] = jnp.where((lane == lpos1) | (lane == lpos2), 1.0, 0.0).astype(BF16)
    row1, row2, w1_row, w2_row = _as_rows(lpos1, lpos2, rt[:, 4:5], rt[:, 5:6])
    grow = lax.broadcasted_iota(I32, (LOC_ROWS, TOK_TILE), 0).astype(F32)
    wrow = jnp.sum(jnp.where(grow == row1, w1_row, 0.0) + jnp.where(grow == row2, w2_row, 0.0),
                   axis=-1, keepdims=True)
    lo, hi = _load_packed(ybuf.at[a_slot], LOC_ROWS)
    yb_s[a_slot] = (jnp.concatenate([lo, hi], axis=-1) * wrow).astype(BF16)


def _final_call(tables, h, rt, p, ys, wple, wgate, gple, gfin, *, tile_base):
    n_rows = h.shape[0]
    n_tiles = n_rows // TOK_TILE
    b_map = lambda s, *_: (jnp.maximum(s - 1, 0), 0)
    a_map = lambda s, *_: (jnp.minimum(s, n_tiles - 1), 0)
    cmap = lambda s, *_: (0, 0)
    return pl.pallas_call(
        functools.partial(_final_kernel, tile_base=tile_base, n_tiles=n_tiles),
        grid_spec=pltpu.PrefetchScalarGridSpec(
            num_scalar_prefetch=len(tables), grid=(n_tiles + 1,),
            in_specs=[
                pl.BlockSpec((TOK_TILE, D_MODEL), b_map),
                pl.BlockSpec((TOK_TILE, SUBLANES), a_map),
                pl.BlockSpec((TOK_TILE, PLE_DIM), b_map),
                pl.BlockSpec(memory_space=pl.ANY),
                pl.BlockSpec(wple.shape, cmap),
                pl.BlockSpec(wgate.shape, cmap),
                pl.BlockSpec(gple.shape, cmap),
                pl.BlockSpec(gfin.shape, cmap),
            ],
            out_specs=pl.BlockSpec((TOK_TILE, D_MODEL), b_map),
            scratch_shapes=[pltpu.VMEM((2, LOC_ROWS * ROW_CHUNKS, LANES), U32),
                            pltpu.VMEM((2, LOC_ROWS, D_MODEL), BF16),
                            pltpu.VMEM((2, TOK_TILE, LOC_ROWS), BF16),
                            pltpu.SemaphoreType.DMA((2,))]),
        out_shape=jax.ShapeDtypeStruct((n_rows, D_MODEL), F32),
        compiler_params=pltpu.CompilerParams(
            dimension_semantics=("arbitrary",), vmem_limit_bytes=VMEM_LIMIT),
        name="final",
    )(*tables, h, rt, p, ys, wple, wgate, gple, gfin)


def _ceil_to(x, m):
    return ((x + m - 1) // m) * m


def _excl_cumsum(a, axis):
    return jnp.cumsum(a, axis=axis) - a


def _copy_tables(n_copies, first_local, first_sorted, step, cap):
    ends = jnp.cumsum(n_copies, axis=1)
    j = jnp.arange(cap, dtype=I32)[None, :, None]
    e_of = jnp.minimum(jnp.sum(ends[:, None, :] <= j, axis=-1), N_EXPERTS - 1)
    pick = e_of[:, :, None] == jnp.arange(N_EXPERTS, dtype=I32)[None, None, :]
    take = lambda a: jnp.sum(jnp.where(pick, a[:, None, :], 0), axis=-1)
    c = jnp.arange(cap, dtype=I32)[None, :] - (take(ends) - take(n_copies))
    flat = lambda a: a.reshape(-1).astype(I32)
    return flat(take(first_local) + step * c), flat(take(first_sorted) + step * c), ends[:, -1]


def kernel(x_prompt, x_sample, state_pool, p_prompt, p_sample, g_mix, w_in, w_pool, pool_scale,
           g_v, b_v, w_spatial, b_spatial, w_out, g_ffn, w_rg, b_rg, w_re, b_re, w_eg, w_eu, w_ed,
           w_ple, g_ple, w_ple_gate, g_final):
    batch, seq, _ = x_prompt.shape
    dec_batch, dec_seq, _ = x_sample.shape
    assert dec_seq == SUBLANES and g_mix.shape[0] == 1
    n_p = batch * seq
    n_s = dec_batch * dec_seq
    n_tok = n_p + n_s
    past_len = 16384

    row = lambda a: a.reshape(1, -1).astype(F32)
    wr = jnp.concatenate(
        [w_re[0], w_rg[0], jnp.zeros((D_MODEL, LANES - N_EXPERTS - N_EXPERT_GROUPS), F32)], axis=1)
    br = jnp.concatenate(
        [b_re[0], b_rg[0], jnp.zeros((LANES - N_EXPERTS - N_EXPERT_GROUPS,), F32)]).reshape(1, LANES)
    bsp_p = jnp.repeat(jnp.transpose(b_spatial[0]), GMLP_HEAD, axis=1)
    bsp_s = jnp.tile(bsp_p[:dec_seq], (CHUNK // dec_seq, 1))
    wsp_p = w_spatial[0].astype(BF16)
    wsp_s = jnp.tile(w_spatial[0][:, :dec_seq, :dec_seq].astype(BF16),
                     (1, CHUNK // dec_seq, CHUNK // dec_seq))

    def pool_band(first_chunk):
        t = jnp.arange(CHUNK, dtype=I32)[:, None]
        k = jnp.arange(POOL_SPAN, dtype=I32)[None, :]
        mats = []
        for w in POOL_WINDOWS:
            cnt = jnp.minimum(w, t + 1) if first_chunk else jnp.full_like(t, w)
            in_window = (k >= t + 17 - w) & (k <= t + 16)
            mats.append(jnp.where(in_window, 1.0 / cnt.astype(F32), 0.0) - (k == t + 16).astype(F32))
        return jnp.stack(mats).astype(BF16)

    def mixer_weights(wsp, bsp):
        return (row(g_mix[0]), w_in[0].astype(BF16), w_pool[0].astype(BF16), row(pool_scale[0]),
                row(g_v[0]), row(b_v[0]), wsp, bsp, w_out[0].astype(BF16), row(g_ffn[0]),
                wr.astype(BF16), br, pool_band(False), pool_band(True))

    h_p, hn_p, rt_p, cnt_p, hist_p = _mixer_call(
        x_prompt, None, mixer_weights(wsp_p, bsp_p), sample=False, pos0=0)
    h_s, hn_s, rt_s, cnt_s, hist_s, v_s = _mixer_call(
        x_sample.reshape(n_s, D_MODEL), state_pool[0], mixer_weights(wsp_s, bsp_s),
        sample=True, pos0=past_len)

    n_run = jnp.concatenate([cnt_p, cnt_s], axis=0)[:, 0, :N_EXPERTS].astype(I32)
    n_even = _ceil_to(n_run, RUN_ALIGN)
    l_start = _excl_cumsum(n_even, 1)
    tile_rows = jnp.sum(n_even, axis=1)
    base = _excl_cumsum(n_even, 0)
    counts = jnp.sum(n_even, axis=0)
    padded = _ceil_to(counts, MOE_TILE)
    ends = jnp.cumsum(padded)
    offs = ends - padded
    g_start = offs[None, :] + base
    n8 = n_even // 8
    f4 = (n_even // 4) % 2
    f2 = (n_even // 2) % 2
    l8, g8, c8 = _copy_tables(n8, l_start, g_start, 8, COPY_CAP[0])
    l4, g4, c4 = _copy_tables(f4, l_start + 8 * n8, g_start + 8 * n8, 0, COPY_CAP[1])
    l2, g2, c2 = _copy_tables(f2, l_start + 8 * n8 + 4 * f4, g_start + 8 * n8 + 4 * f4, 0, COPY_CAP[2])
    copy_cnt = jnp.stack([c8, c4, c2], axis=1).reshape(-1).astype(I32)
    tile_rows = tile_rows.astype(I32)
    z_start = (offs + counts).astype(I32)
    z_blocks = ((padded - counts + 7) // 8).astype(I32)

    max_rows = 2 * n_tok + n_run.shape[0] * N_EXPERTS * (RUN_ALIGN - 1)
    n_tiles = max_rows // MOE_TILE + N_EXPERTS + 1
    n_used = (ends[-1] // MOE_TILE).astype(I32).reshape(1)
    tile_id = jnp.minimum(jnp.arange(n_tiles, dtype=I32), n_used - 1)
    tile_expert = jnp.sum((tile_id[:, None] * MOE_TILE) >= ends[None, :], axis=1).astype(I32)
    tile_expert = jnp.minimum(tile_expert, N_EXPERTS - 1)
    pick_e = tile_expert[:, None] == jnp.arange(N_EXPERTS, dtype=I32)[None, :]
    data_end = jnp.sum(jnp.where(pick_e, (offs + counts)[None, :], 0), axis=1)
    half_only = (data_end - tile_id * MOE_TILE <= MOE_TILE // 2).astype(I32)
    e_ids = jnp.arange(N_EXPERTS, dtype=I32)
    nonempty = padded > 0
    seg_index = jnp.cumsum(nonempty.astype(I32)) - 1
    later = nonempty[None, :] & (e_ids[None, :] > e_ids[:, None])
    next_of = jnp.min(jnp.where(later, e_ids[None, :], N_EXPERTS), axis=1)
    next_of = jnp.where(next_of == N_EXPERTS, -1, next_of)
    buf_parity = (jnp.sum(jnp.where(pick_e, seg_index[None, :], 0), axis=1) % 2).astype(I32)
    next_expert = jnp.sum(jnp.where(pick_e, next_of[None, :], 0), axis=1).astype(I32)

    copy_tabs = (l8, g8, l4, g4, l2, g2, copy_cnt, tile_rows)
    xs = _dispatch_call(copy_tabs + (z_start, z_blocks, n_used), hn_p, hn_s, rt_p, rt_s,
                        n_tiles * MOE_TILE)
    ys = _moe_call(tile_expert, n_used, buf_parity, next_expert, half_only, xs, n_tiles,
                   w_eg[0], w_eu[0], w_ed[0])
    fin_w = (w_ple[0].astype(BF16), w_ple_gate[0].astype(BF16), row(g_ple[0]), row(g_final))
    y_p = _final_call(copy_tabs, h_p, rt_p, p_prompt[0].reshape(n_p, PLE_DIM), ys, *fin_w, tile_base=0)
    y_s = _final_call(copy_tabs, h_s, rt_s, p_sample[0].reshape(n_s, PLE_DIM), ys, *fin_w,
                      tile_base=n_p // TOK_TILE)

    return (y_p.reshape(batch, seq, D_MODEL),
            y_s.reshape(dec_batch, dec_seq, D_MODEL),
            hist_p[None],
            hist_s[None],
            v_s.reshape(1, dec_batch, dec_seq, GMLP_WIDTH))
```

```python
import functools

import jax
import jax.numpy as jnp
from jax import lax
from jax.experimental import pallas as pl
from jax.experimental.pallas import tpu as pltpu

D_MODEL = 1024
POOL_WIDTH = 512
GMLP_WIDTH = 512
IN_WIDTH = POOL_WIDTH + 2 * GMLP_WIDTH
POOL_WINDOWS = (2, 4, 8, 16)
POOL_GROUP = 128
POOL_HIST = 15
CHUNK = 128
POOL_SPAN = CHUNK + 16
N_GMLP_HEADS = 4
GMLP_HEAD = 128
N_EXPERT_GROUPS = 4
EXPERTS_PER_GROUP = 8
N_EXPERTS = 32
D_EXPERT = 512
PLE_DIM = 256
EPS = 1e-6

LANES = 128
SUBLANES = 8
HALF = D_MODEL // 2
ROW_CHUNKS = HALF // LANES

TOK_TILE = 256
MIX_TILE = 512
MOE_TILE = 512
ROW_GROUP = 128
RUN_ALIGN = 2
COPY_ROWS = (8, 4, 2)
ZERO_ROWS = 64
LOC_ROWS = 2 * TOK_TILE + N_EXPERTS * (RUN_ALIGN - 1)
COPY_CAP = (LOC_ROWS // 8 + 1, N_EXPERTS, N_EXPERTS)
VMEM_LIMIT = 56 * 1024 * 1024

F32 = jnp.float32
BF16 = jnp.bfloat16
I32 = jnp.int32
U32 = jnp.uint32


def _rms(x, g):
    return x * lax.rsqrt(jnp.mean(x * x, axis=-1, keepdims=True) + EPS) * g


def _gelu_tanh(x):
    inner = x * (0.7978845608028654 + (0.7978845608028654 * 0.044715) * (x * x))
    half = 0.5 * x
    return half + half * jnp.tanh(inner)


def _store_packed(ref, val, n_rows):
    packed = pltpu.pack_elementwise([val[:, :HALF], val[:, HALF:]], packed_dtype=BF16)
    for k in range(ROW_CHUNKS):
        ref[pl.ds(k, n_rows, stride=ROW_CHUNKS), :] = packed[:, k * LANES:(k + 1) * LANES]


def _load_packed(ref, n_rows):
    packed = jnp.concatenate(
        [ref[pl.ds(k, n_rows, stride=ROW_CHUNKS), :] for k in range(ROW_CHUNKS)], axis=-1)
    lo = pltpu.unpack_elementwise(packed, index=0, packed_dtype=BF16, unpacked_dtype=F32)
    hi = pltpu.unpack_elementwise(packed, index=1, packed_dtype=BF16, unpacked_dtype=F32)
    return lo, hi


def _rows(start, n):
    return pl.ds(pl.multiple_of(start * ROW_CHUNKS, SUBLANES), n * ROW_CHUNKS)


def _route(hn, wr_ref, br_ref, n_tok):
    logits = jnp.dot(hn, wr_ref[...], preferred_element_type=F32) + br_ref[...]

    lane = lax.broadcasted_iota(I32, (n_tok, LANES), 1).astype(F32)
    neg = jnp.float32(-jnp.inf)
    big = jnp.float32(1e9)
    gmask = (lane >= N_EXPERTS) & (lane < N_EXPERTS + N_EXPERT_GROUPS)
    gl = jnp.where(gmask, logits, neg)
    gmax = jnp.max(gl, axis=-1, keepdims=True)
    gidx = jnp.min(jnp.where(gl == gmax, lane, big), axis=-1, keepdims=True) - N_EXPERTS
    gden = jnp.sum(jnp.exp(gl - gmax), axis=-1, keepdims=True)
    gprob = 1.0 / gden

    lo = gidx * EXPERTS_PER_GROUP
    emask = (lane >= lo) & (lane < lo + EXPERTS_PER_GROUP)
    el = jnp.where(emask, logits, neg)
    m1 = jnp.max(el, axis=-1, keepdims=True)
    i1 = jnp.min(jnp.where(el == m1, lane, big), axis=-1, keepdims=True)
    el2 = jnp.where(lane == i1, neg, el)
    m2 = jnp.max(el2, axis=-1, keepdims=True)
    i2 = jnp.min(jnp.where(el2 == m2, lane, big), axis=-1, keepdims=True)
    t = jnp.exp(m2 - m1)
    w1 = gprob / (1.0 + t)
    w2 = gprob * t / (1.0 + t)

    sel1 = lane == i1
    sel2 = lane == i2
    oh = jnp.where(sel1 | sel2, 1.0, 0.0).astype(BF16)
    r = lax.broadcasted_iota(I32, (n_tok, n_tok), 0)
    c = lax.broadcasted_iota(I32, (n_tok, n_tok), 1)
    tri = jnp.where(c < r, 1.0, 0.0).astype(BF16)
    before = jnp.dot(tri, oh, preferred_element_type=F32)
    counts = jnp.sum(oh.astype(F32), axis=0, keepdims=True)
    assert RUN_ALIGN == 2
    c_even = counts + (counts - 2.0 * jnp.floor(counts * 0.5))
    c_hi = jnp.floor(c_even * (1.0 / 256.0))
    piece = lax.broadcasted_iota(I32, (SUBLANES, LANES), 0)
    pieces = jnp.where(piece == 0, c_hi, jnp.where(piece == 1, c_even - 256.0 * c_hi, 0.0))
    er = lax.broadcasted_iota(I32, (LANES, LANES), 0)
    ec = lax.broadcasted_iota(I32, (LANES, LANES), 1)
    upper = jnp.where(er < ec, 1.0, 0.0).astype(BF16)
    prefix = jnp.dot(pieces.astype(BF16), upper, preferred_element_type=F32)
    lstart = 256.0 * prefix[0:1, :] + prefix[1:2, :]
    lpos1 = jnp.sum(jnp.where(sel1, before + lstart, 0.0), axis=-1, keepdims=True)
    lpos2 = jnp.sum(jnp.where(sel2, before + lstart, 0.0), axis=-1, keepdims=True)

    out = jnp.where(lane == 0, i1, 0.0)
    for k, val in enumerate((i2, lpos1, lpos2, w1, w2), start=1):
        out = jnp.where(lane == k, val, out)
    return out, counts


def _mixer_kernel(*refs, n_tok, sample, pos0):
    if sample:
        (x_ref, hist_ref, gmix_ref, win_ref, wpool_ref, pscale_ref, gv_ref, bv_ref, wsp_ref,
         bsp_ref, wout_ref, gffn_ref, wr_ref, br_ref, band_ref, band0_ref,
         h_ref, hn_ref, rt_ref, cnt_ref, hist_out_ref, v_ref, ext_scr) = refs
    else:
        (x_ref, gmix_ref, win_ref, wpool_ref, pscale_ref, gv_ref, bv_ref, wsp_ref,
         bsp_ref, wout_ref, gffn_ref, wr_ref, br_ref, band_ref, band0_ref,
         h_ref, hn_ref, rt_ref, cnt_ref, hist_out_ref, ext_scr) = refs

    x = x_ref[...]
    xn = _rms(x, gmix_ref[...])
    z = jnp.dot(xn.astype(BF16), win_ref[...], preferred_element_type=F32)
    a = z[:, :POOL_WIDTH]
    uv = _gelu_tanh(z[:, POOL_WIDTH:])
    u = uv[:, :GMLP_WIDTH]
    v = uv[:, GMLP_WIDTH:]
    mu = jnp.mean(v, axis=-1, keepdims=True)
    vc = v - mu
    v = vc * lax.rsqrt(jnp.mean(vc * vc, axis=-1, keepdims=True) + EPS) * gv_ref[...] + bv_ref[...]

    pooled = []
    if sample:
        n_seq = n_tok // SUBLANES
        ext_scr[:, 1:1 + POOL_HIST, :] = hist_ref[...]
        ext_scr[:, 1 + POOL_HIST:, :] = a.reshape(n_seq, SUBLANES, POOL_WIDTH)
        hist_out_ref[...] = ext_scr[:, 1 + SUBLANES:, :]
        pos = pos0 + lax.broadcasted_iota(I32, (n_seq, SUBLANES, 1), 1)
        for gi, w in enumerate(POOL_WINDOWS):
            cols = slice(gi * POOL_GROUP, (gi + 1) * POOL_GROUP)
            win = ext_scr[:, 16:16 + SUBLANES, cols]
            for j in range(1, w):
                win = win + ext_scr[:, 16 - j:16 - j + SUBLANES, cols]
            cnt = jnp.minimum(w, pos + 1).astype(F32)
            pooled.append((win / cnt).reshape(n_tok, POOL_GROUP) - a[:, cols])
    else:
        l = pl.program_id(1)

        @pl.when(l == 0)
        def _():
            ext_scr[0:16, :] = jnp.zeros((16, POOL_WIDTH), F32)

        ext_scr[16:16 + n_tok, :] = a
        chunks = []
        for ci in range(n_tok // CHUNK):
            ext_c = ext_scr[ci * CHUNK:ci * CHUNK + POOL_SPAN, :].astype(BF16)
            groups = []
            for gi in range(len(POOL_WINDOWS)):
                band = band_ref[gi]
                if ci == 0:
                    band = jnp.where(l == 0, band0_ref[gi], band)
                groups.append(jnp.dot(band, ext_c[:, gi * POOL_GROUP:(gi + 1) * POOL_GROUP],
                                      preferred_element_type=F32))
            chunks.append(groups)
        pooled = [jnp.concatenate([chunks[ci][gi] for ci in range(n_tok // CHUNK)], axis=0)
                  for gi in range(len(POOL_WINDOWS))]
        tail = ext_scr[n_tok + 1:n_tok + 16, :]
        hist_out_ref[...] = tail
        ext_scr[1:16, :] = tail

    pool_out = jnp.concatenate(
        [jnp.dot(pooled[gi].astype(BF16), wpool_ref[gi], preferred_element_type=F32)
         for gi in range(len(POOL_WINDOWS))], axis=-1) * pscale_ref[...]

    blk = SUBLANES if sample else CHUNK
    tr = lax.broadcasted_iota(I32, (CHUNK, CHUNK), 0)
    sc = lax.broadcasted_iota(I32, (CHUNK, CHUNK), 1)
    mask = (sc <= tr) & ((sc // blk) == (tr // blk))
    wsp = [jnp.where(mask, wsp_ref[hh], jnp.zeros((), BF16)) for hh in range(N_GMLP_HEADS)]
    v_bf = v.astype(BF16)
    mixed_chunks = []
    for ci in range(n_tok // CHUNK):
        rows = slice(ci * CHUNK, (ci + 1) * CHUNK)
        heads = [jnp.dot(wsp[hh], v_bf[rows, hh * GMLP_HEAD:(hh + 1) * GMLP_HEAD],
                         preferred_element_type=F32) for hh in range(N_GMLP_HEADS)]
        mixed_chunks.append(jnp.concatenate(heads, axis=-1) + bsp_ref[...])
    mixed = jnp.concatenate(mixed_chunks, axis=0)
    gmlp_out = u * mixed

    cat = jnp.concatenate([pool_out, gmlp_out], axis=-1)
    h = x + jnp.dot(cat.astype(BF16), wout_ref[...], preferred_element_type=F32)
    hn = _rms(h, gffn_ref[...]).astype(BF16)

    h_ref[...] = h
    hn_ref[...] = hn
    if sample:
        v_ref[...] = v
    for s in range(n_tok // TOK_TILE):
        rows = slice(s * TOK_TILE, (s + 1) * TOK_TILE)
        rt, counts = _route(hn[rows], wr_ref, br_ref, TOK_TILE)
        rt_ref[rows, :] = rt[:, :SUBLANES]
        cnt_ref[s] = counts


def _full(shape):
    return pl.BlockSpec(shape, lambda *_: (0,) * len(shape))


def _mixer_call(x, hist, weights, *, sample, pos0):
    w_specs = [_full(w.shape) for w in weights]
    n_tok = MIX_TILE
    if sample:
        n_rows = x.shape[0]
        n_seq = n_tok // SUBLANES
        grid = (n_rows // n_tok,)
        tok_map = lambda i: (i, 0)
        tile_map = lambda i: (i, 0, 0)
        in_specs = [pl.BlockSpec((n_tok, D_MODEL), tok_map),
                    pl.BlockSpec((n_seq, POOL_HIST, POOL_WIDTH), lambda i: (i, 0, 0))] + w_specs
        hist_shape = jax.ShapeDtypeStruct(hist.shape, F32)
        hist_spec = pl.BlockSpec((n_seq, POOL_HIST, POOL_WIDTH), lambda i: (i, 0, 0))
        ext = pltpu.VMEM((n_seq, 16 + SUBLANES, POOL_WIDTH), F32)
        args = (x, hist)
    else:
        b, seq, _ = x.shape
        n_rows = b * seq
        n_l = seq // n_tok
        grid = (b, n_l)
        tok_map = lambda bi, li: (bi * n_l + li, 0)
        tile_map = lambda bi, li: (bi * n_l + li, 0, 0)
        in_specs = [pl.BlockSpec((None, n_tok, D_MODEL), lambda bi, li: (bi, li, 0))] + w_specs
        hist_shape = jax.ShapeDtypeStruct((b, POOL_HIST, POOL_WIDTH), F32)
        hist_spec = pl.BlockSpec((None, POOL_HIST, POOL_WIDTH), lambda bi, li: (bi, 0, 0))
        ext = pltpu.VMEM((16 + n_tok, POOL_WIDTH), F32)
        args = (x,)
    out_shape = [jax.ShapeDtypeStruct((n_rows, D_MODEL), F32),
                 jax.ShapeDtypeStruct((n_rows, D_MODEL), BF16),
                 jax.ShapeDtypeStruct((n_rows, SUBLANES), F32),
                 jax.ShapeDtypeStruct((n_rows // TOK_TILE, 1, LANES), F32),
                 hist_shape]
    out_specs = [pl.BlockSpec((n_tok, D_MODEL), tok_map),
                 pl.BlockSpec((n_tok, D_MODEL), tok_map),
                 pl.BlockSpec((n_tok, SUBLANES), tok_map),
                 pl.BlockSpec((n_tok // TOK_TILE, 1, LANES), tile_map),
                 hist_spec]
    if sample:
        out_shape.append(jax.ShapeDtypeStruct((n_rows, GMLP_WIDTH), F32))
        out_specs.append(pl.BlockSpec((n_tok, GMLP_WIDTH), tok_map))
    return pl.pallas_call(
        functools.partial(_mixer_kernel, n_tok=n_tok, sample=sample, pos0=pos0),
        grid=grid, in_specs=in_specs, out_specs=out_specs, out_shape=out_shape,
        scratch_shapes=[ext],
        compiler_params=pltpu.CompilerParams(
            dimension_semantics=("arbitrary",) * len(grid), vmem_limit_bytes=VMEM_LIMIT),
        name="mixer_sample" if sample else "mixer_prompt",
    )(*args, *weights)


def _for_each_copy(tile, cnt_ref, fn, active=True):
    for s, n_rows in enumerate(COPY_ROWS):
        def body(j, carry, s=s, n_rows=n_rows):
            fn(s, n_rows, tile * COPY_CAP[s] + j)
            return carry
        lax.fori_loop(0, jnp.where(active, cnt_ref[tile * len(COPY_ROWS) + s], 0), body, 0)


def _wait_rows(n_rows_total, make_wait):
    units = n_rows_total // RUN_ALIGN
    bit = 0
    while (RUN_ALIGN << bit) <= LOC_ROWS:
        @pl.when((units >> bit) & 1 == 1)
        def _(bit=bit):
            make_wait(RUN_ALIGN << bit).wait()
        bit += 1


def _as_rows(*cols):
    n = cols[0].shape[0]
    lane = lax.broadcasted_iota(I32, (n, LANES), 1)
    packed = jnp.zeros((n, LANES), F32)
    for k, col in enumerate(cols):
        packed = jnp.where(lane == k, col, packed)
    rows = packed.T
    return [rows[k:k + 1, :] for k in range(len(cols))]


def _dispatch_kernel(l8, g8, l4, g4, l2, g2, cnt_ref, trow_ref, zs_ref, zbig_ref, zb_ref, nu_ref,
                     hnp_ref, hns_ref, rtp_ref, rts_ref, xs_ref, loc, zbuf, sem, zsem,
                     *, n_prompt_steps):
    i = pl.program_id(0)
    n_steps = pl.num_programs(0)
    slot = i % 2
    local_tabs, sorted_tabs = (l8, l4, l2), (g8, g4, g2)

    @pl.when(i == 0)
    def _():
        zbuf[...] = jnp.zeros(zbuf.shape, U32)
        n_tiles = xs_ref.shape[0] // (MOE_TILE * ROW_CHUNKS)

        def zero_copy(e, c, n_rows, first):
            return pltpu.make_async_copy(zbuf.at[pl.ds(0, n_rows * ROW_CHUNKS)],
                                         xs_ref.at[_rows(first + c * n_rows, n_rows)], zsem)

        def tail_copy(t):
            return pltpu.make_async_copy(zbuf, xs_ref.at[_rows(t * MOE_TILE, MOE_TILE)], zsem)

        def per_expert(e, totals):
            def big(c, carry):
                zero_copy(e, c, ZERO_ROWS, zs_ref[e]).start()
                return carry

            def small(c, carry):
                zero_copy(e, c, 8, zs_ref[e] + zbig_ref[e] * ZERO_ROWS).start()
                return carry

            lax.fori_loop(0, zbig_ref[e], big, 0)
            lax.fori_loop(0, zb_ref[e], small, 0)
            return totals[0] + zbig_ref[e], totals[1] + zb_ref[e]

        n_big, n_small = lax.fori_loop(0, N_EXPERTS, per_expert, (0, 0))

        def wait_big(_, carry):
            zero_copy(0, 0, ZERO_ROWS, 0).wait()
            return carry

        def wait_small(_, carry):
            zero_copy(0, 0, 8, 0).wait()
            return carry

        lax.fori_loop(0, n_big, wait_big, 0)
        lax.fori_loop(0, n_small, wait_small, 0)

        def tail_start(t, carry):
            tail_copy(t).start()
            return carry

        def tail_wait(t, carry):
            tail_copy(t).wait()
            return carry

        spill = pltpu.make_async_copy(zbuf.at[pl.ds(0, 8 * ROW_CHUNKS)],
                                      xs_ref.at[_rows(n_tiles * MOE_TILE, 8)], zsem)
        lax.fori_loop(nu_ref[0], n_tiles, tail_start, 0)
        spill.start()
        lax.fori_loop(nu_ref[0], n_tiles, tail_wait, 0)
        spill.wait()

    is_prompt = i < n_prompt_steps
    rt = jnp.where(is_prompt, rtp_ref[...], rts_ref[...])
    row1, row2 = _as_rows(rt[:, 2:3], rt[:, 3:4])
    grow = lax.broadcasted_iota(I32, (LOC_ROWS, TOK_TILE), 0).astype(F32)
    perm = jnp.where((grow == row1) | (grow == row2), 1.0, 0.0).astype(BF16)
    hn = jnp.where(is_prompt, hnp_ref[...], hns_ref[...])
    grouped = jnp.dot(perm, hn, preferred_element_type=F32)
    _store_packed(loc.at[slot], grouped, LOC_ROWS)

    def run_copy(buf_slot, s, n_rows, k):
        src = loc.at[buf_slot, _rows(local_tabs[s][k], n_rows)]
        dst = xs_ref.at[_rows(sorted_tabs[s][k], n_rows)]
        return pltpu.make_async_copy(src, dst, sem.at[buf_slot])

    _for_each_copy(i, cnt_ref, lambda s, n_rows, k: run_copy(slot, s, n_rows, k).start())

    def drain(tile, buf_slot):
        def make_wait(n):
            return pltpu.make_async_copy(loc.at[buf_slot, pl.ds(0, n * ROW_CHUNKS)],
                                         xs_ref.at[pl.ds(0, n * ROW_CHUNKS)], sem.at[buf_slot])
        _wait_rows(trow_ref[tile], make_wait)

    @pl.when(i > 0)
    def _():
        drain(i - 1, 1 - slot)

    @pl.when(i == n_steps - 1)
    def _():
        drain(i, slot)


def _clamped_maps(np_steps):
    pmap = lambda i, *_: (jnp.minimum(i, np_steps - 1), 0)
    smap = lambda i, *_: (jnp.maximum(i - np_steps, 0), 0)
    return pmap, smap


def _dispatch_call(tables, hn_p, hn_s, rt_p, rt_s, n_sorted_rows):
    n_p, n_s = hn_p.shape[0], hn_s.shape[0]
    np_steps = n_p // TOK_TILE
    n_steps = np_steps + n_s // TOK_TILE
    pmap, smap = _clamped_maps(np_steps)
    out_rows = (n_sorted_rows + 8) * ROW_CHUNKS
    return pl.pallas_call(
        functools.partial(_dispatch_kernel, n_prompt_steps=np_steps),
        grid_spec=pltpu.PrefetchScalarGridSpec(
            num_scalar_prefetch=len(tables), grid=(n_steps,),
            in_specs=[pl.BlockSpec((TOK_TILE, D_MODEL), pmap),
                      pl.BlockSpec((TOK_TILE, D_MODEL), smap),
                      pl.BlockSpec((TOK_TILE, SUBLANES), pmap),
                      pl.BlockSpec((TOK_TILE, SUBLANES), smap)],
            out_specs=pl.BlockSpec(memory_space=pl.ANY),
            scratch_shapes=[pltpu.VMEM((2, LOC_ROWS * ROW_CHUNKS, LANES), U32),
                            pltpu.VMEM((MOE_TILE * ROW_CHUNKS, LANES), U32),
                            pltpu.SemaphoreType.DMA((2,)),
                            pltpu.SemaphoreType.DMA(())]),
        out_shape=jax.ShapeDtypeStruct((out_rows, LANES), U32),
        compiler_params=pltpu.CompilerParams(
            dimension_semantics=("arbitrary",), vmem_limit_bytes=VMEM_LIMIT),
        name="dispatch",
    )(*tables, hn_p, hn_s, rt_p, rt_s)


def _moe_kernel(te_ref, nu_ref, par_ref, nxt_ref, ngrp_ref, xs_ref, wg_hbm, wu_hbm, wd_hbm, ys_ref,
                wg_f32, wu_f32, wd_f32, wg_bf, wu_bf, wd_bf, wsem):
    i = pl.program_id(0)
    used = i < nu_ref[0]

    def store_zeros(ref, n_rows):
        ij = (lax.broadcasted_iota(I32, (n_rows, D_MODEL), 0)
              + lax.broadcasted_iota(I32, (n_rows, D_MODEL), 1))
        _store_packed(ref, jnp.where(ij < jnp.minimum(nu_ref[0], 0), 1.0, 0.0), n_rows)

    def weight_copies(expert, buf):
        return [pltpu.make_async_copy(hbm.at[expert], vmem.at[buf], wsem.at[buf])
                for hbm, vmem in ((wg_hbm, wg_f32), (wu_hbm, wu_f32), (wd_hbm, wd_f32))]

    @pl.when(used)
    def _():
        prev = te_ref[jnp.maximum(i - 1, 0)]

        @pl.when((i == 0) | (te_ref[i] != prev))
        def _():
            buf = par_ref[i]

            @pl.when(i == 0)
            def _():
                for cp in weight_copies(te_ref[0], 0):
                    cp.start()

            for cp in weight_copies(te_ref[i], buf):
                cp.wait()
            wg_bf[...] = wg_f32[buf].astype(BF16)
            wu_bf[...] = wu_f32[buf].astype(BF16)
            wd_bf[...] = wd_f32[buf].astype(BF16)

            @pl.when(nxt_ref[i] >= 0)
            def _():
                for cp in weight_copies(nxt_ref[i], 1 - buf):
                    cp.start()

        def expert_rows(n_rows):
            lo, hi = _load_packed(xs_ref, n_rows)
            x = jnp.concatenate([lo, hi], axis=-1).astype(BF16)
            g = jnp.dot(x, wg_bf[...], preferred_element_type=F32)
            u = jnp.dot(x, wu_bf[...], preferred_element_type=F32)
            hh = (g * jax.nn.sigmoid(g)) * u
            y = jnp.dot(hh.astype(BF16), wd_bf[...], preferred_element_type=F32)
            _store_packed(ys_ref, y, n_rows)

        for n_groups in range(1, MOE_TILE // ROW_GROUP + 1):
            @pl.when(ngrp_ref[i] == n_groups)
            def _(n_rows=n_groups * ROW_GROUP):
                expert_rows(n_rows)
                if n_rows < MOE_TILE:
                    rest = MOE_TILE - n_rows
                    store_zeros(ys_ref.at[pl.ds(n_rows * ROW_CHUNKS, rest * ROW_CHUNKS)], rest)

    @pl.when(jnp.logical_not(used))
    def _():
        store_zeros(ys_ref, MOE_TILE)


def _moe_call(tile_expert, n_used, buf_parity, next_expert, n_groups_used, xs, n_tiles, w_eg, w_eu, w_ed):
    blk = MOE_TILE * ROW_CHUNKS
    any_spec = pl.BlockSpec(memory_space=pl.ANY)
    return pl.pallas_call(
        _moe_kernel,
        grid_spec=pltpu.PrefetchScalarGridSpec(
            num_scalar_prefetch=5, grid=(n_tiles,),
            in_specs=[
                pl.BlockSpec((blk, LANES), lambda i, te, nu, *_: (jnp.minimum(i, nu[0] - 1), 0)),
                any_spec, any_spec, any_spec,
            ],
            out_specs=pl.BlockSpec((blk, LANES), lambda i, *_: (i, 0)),
            scratch_shapes=[pltpu.VMEM((2, D_MODEL, D_EXPERT), F32),
                            pltpu.VMEM((2, D_MODEL, D_EXPERT), F32),
                            pltpu.VMEM((2, D_EXPERT, D_MODEL), F32),
                            pltpu.VMEM((D_MODEL, D_EXPERT), BF16),
                            pltpu.VMEM((D_MODEL, D_EXPERT), BF16),
                            pltpu.VMEM((D_EXPERT, D_MODEL), BF16),
                            pltpu.SemaphoreType.DMA((2,))]),
        out_shape=jax.ShapeDtypeStruct((n_tiles * blk, LANES), U32),
        compiler_params=pltpu.CompilerParams(
            dimension_semantics=("arbitrary",), vmem_limit_bytes=VMEM_LIMIT),
        name="experts",
    )(tile_expert, n_used, buf_parity, next_expert, n_groups_used, xs, w_eg, w_eu, w_ed)


def _final_kernel(l8, g8, l4, g4, l2, g2, cnt_ref, trow_ref,
                  h_ref, rt_ref, p_ref, ys_ref, wple_ref, wgate_ref, gple_ref, gfin_ref, out_ref,
                  ybuf, yb_s, perm_s, sem, *, tile_base, n_tiles):
    s = pl.program_id(0)
    a_slot = s % 2
    b_slot = 1 - a_slot
    local_tabs, sorted_tabs = (l8, l4, l2), (g8, g4, g2)

    def run_copy(buf_slot, sz, n_rows, k):
        src = ys_ref.at[_rows(sorted_tabs[sz][k], n_rows)]
        dst = ybuf.at[buf_slot, _rows(local_tabs[sz][k], n_rows)]
        return pltpu.make_async_copy(src, dst, sem.at[buf_slot])

    def issue(tile, buf_slot, active):
        _for_each_copy(tile_base + jnp.minimum(tile, n_tiles - 1), cnt_ref,
                       lambda sz, n_rows, k: run_copy(buf_slot, sz, n_rows, k).start(), active)

    @pl.when(s == 0)
    def _():
        ybuf[...] = jnp.zeros(ybuf.shape, U32)
        yb_s[...] = jnp.zeros(yb_s.shape, BF16)
        perm_s[...] = jnp.zeros(perm_s.shape, BF16)
        issue(0, 0, True)

    issue(s + 1, b_slot, s + 1 < n_tiles)

    def make_wait(n):
        return pltpu.make_async_copy(ys_ref.at[pl.ds(0, n * ROW_CHUNKS)],
                                     ybuf.at[a_slot, pl.ds(0, n * ROW_CHUNKS)], sem.at[a_slot])
    a_rows = jnp.where(s < n_tiles, trow_ref[tile_base + jnp.minimum(s, n_tiles - 1)], 0)
    _wait_rows(a_rows, make_wait)

    h = h_ref[...]
    h = h + jnp.dot(perm_s[b_slot], yb_s[b_slot], preferred_element_type=F32)
    r = _rms(h, gple_ref[...])
    gate = jax.nn.sigmoid(jnp.dot(r.astype(BF16), wgate_ref[...], preferred_element_type=F32))
    h = h + jnp.dot(p_ref[...].astype(BF16), wple_ref[...], preferred_element_type=F32) * gate
    out_ref[...] = _rms(h, gfin_ref[...])

    rt = rt_ref[...]
    lpos1, lpos2 = rt[:, 2:3], rt[:, 3:4]
    lane = lax.broadcasted_iota(I32, (TOK_TILE, LOC_ROWS), 1).astype(F32)
    perm_s[a_slot] = jnp.where((lane == lpos1) | (lane == lpos2), 1.0, 0.0).astype(BF16)
    row1, row2, w1_row, w2_row = _as_rows(lpos1, lpos2, rt[:, 4:5], rt[:, 5:6])
    grow = lax.broadcasted_iota(I32, (LOC_ROWS, TOK_TILE), 0).astype(F32)
    wrow = jnp.sum(jnp.where(grow == row1, w1_row, 0.0) + jnp.where(grow == row2, w2_row, 0.0),
                   axis=-1, keepdims=True)
    lo, hi = _load_packed(ybuf.at[a_slot], LOC_ROWS)
    yb_s[a_slot] = (jnp.concatenate([lo, hi], axis=-1) * wrow).astype(BF16)


def _final_call(tables, h, rt, p, ys, wple, wgate, gple, gfin, *, tile_base):
    n_rows = h.shape[0]
    n_tiles = n_rows // TOK_TILE
    b_map = lambda s, *_: (jnp.maximum(s - 1, 0), 0)
    a_map = lambda s, *_: (jnp.minimum(s, n_tiles - 1), 0)
    cmap = lambda s, *_: (0, 0)
    return pl.pallas_call(
        functools.partial(_final_kernel, tile_base=tile_base, n_tiles=n_tiles),
        grid_spec=pltpu.PrefetchScalarGridSpec(
            num_scalar_prefetch=len(tables), grid=(n_tiles + 1,),
            in_specs=[
                pl.BlockSpec((TOK_TILE, D_MODEL), b_map),
                pl.BlockSpec((TOK_TILE, SUBLANES), a_map),
                pl.BlockSpec((TOK_TILE, PLE_DIM), b_map),
                pl.BlockSpec(memory_space=pl.ANY),
                pl.BlockSpec(wple.shape, cmap),
                pl.BlockSpec(wgate.shape, cmap),
                pl.BlockSpec(gple.shape, cmap),
                pl.BlockSpec(gfin.shape, cmap),
            ],
            out_specs=pl.BlockSpec((TOK_TILE, D_MODEL), b_map),
            scratch_shapes=[pltpu.VMEM((2, LOC_ROWS * ROW_CHUNKS, LANES), U32),
                            pltpu.VMEM((2, LOC_ROWS, D_MODEL), BF16),
                            pltpu.VMEM((2, TOK_TILE, LOC_ROWS), BF16),
                            pltpu.SemaphoreType.DMA((2,))]),
        out_shape=jax.ShapeDtypeStruct((n_rows, D_MODEL), F32),
        compiler_params=pltpu.CompilerParams(
            dimension_semantics=("arbitrary",), vmem_limit_bytes=VMEM_LIMIT),
        name="final",
    )(*tables, h, rt, p, ys, wple, wgate, gple, gfin)


def _ceil_to(x, m):
    return ((x + m - 1) // m) * m


def _excl_cumsum(a, axis):
    return jnp.cumsum(a, axis=axis) - a


def _copy_tables(n_copies, first_local, first_sorted, step, cap):
    ends = jnp.cumsum(n_copies, axis=1)
    j = jnp.arange(cap, dtype=I32)[None, :, None]
    e_of = jnp.minimum(jnp.sum(ends[:, None, :] <= j, axis=-1), N_EXPERTS - 1)
    pick = e_of[:, :, None] == jnp.arange(N_EXPERTS, dtype=I32)[None, None, :]
    take = lambda a: jnp.sum(jnp.where(pick, a[:, None, :], 0), axis=-1)
    c = jnp.arange(cap, dtype=I32)[None, :] - (take(ends) - take(n_copies))
    flat = lambda a: a.reshape(-1).astype(I32)
    return flat(take(first_local) + step * c), flat(take(first_sorted) + step * c), ends[:, -1]


def kernel(x_prompt, x_sample, state_pool, p_prompt, p_sample, g_mix, w_in, w_pool, pool_scale,
           g_v, b_v, w_spatial, b_spatial, w_out, g_ffn, w_rg, b_rg, w_re, b_re, w_eg, w_eu, w_ed,
           w_ple, g_ple, w_ple_gate, g_final):
    batch, seq, _ = x_prompt.shape
    dec_batch, dec_seq, _ = x_sample.shape
    assert dec_seq == SUBLANES and g_mix.shape[0] == 1
    n_p = batch * seq
    n_s = dec_batch * dec_seq
    n_tok = n_p + n_s
    past_len = 16384

    row = lambda a: a.reshape(1, -1).astype(F32)
    wr = jnp.concatenate(
        [w_re[0], w_rg[0], jnp.zeros((D_MODEL, LANES - N_EXPERTS - N_EXPERT_GROUPS), F32)], axis=1)
    br = jnp.concatenate(
        [b_re[0], b_rg[0], jnp.zeros((LANES - N_EXPERTS - N_EXPERT_GROUPS,), F32)]).reshape(1, LANES)
    bsp_p = jnp.repeat(jnp.transpose(b_spatial[0]), GMLP_HEAD, axis=1)
    bsp_s = jnp.tile(bsp_p[:dec_seq], (CHUNK // dec_seq, 1))
    wsp_p = w_spatial[0].astype(BF16)
    wsp_s = jnp.tile(w_spatial[0][:, :dec_seq, :dec_seq].astype(BF16),
                     (1, CHUNK // dec_seq, CHUNK // dec_seq))

    def pool_band(first_chunk):
        t = jnp.arange(CHUNK, dtype=I32)[:, None]
        k = jnp.arange(POOL_SPAN, dtype=I32)[None, :]
        mats = []
        for w in POOL_WINDOWS:
            cnt = jnp.minimum(w, t + 1) if first_chunk else jnp.full_like(t, w)
            in_window = (k >= t + 17 - w) & (k <= t + 16)
            mats.append(jnp.where(in_window, 1.0 / cnt.astype(F32), 0.0) - (k == t + 16).astype(F32))
        return jnp.stack(mats).astype(BF16)

    def mixer_weights(wsp, bsp):
        return (row(g_mix[0]), w_in[0].astype(BF16), w_pool[0].astype(BF16), row(pool_scale[0]),
                row(g_v[0]), row(b_v[0]), wsp, bsp, w_out[0].astype(BF16), row(g_ffn[0]),
                wr.astype(BF16), br, pool_band(False), pool_band(True))

    h_p, hn_p, rt_p, cnt_p, hist_p = _mixer_call(
        x_prompt, None, mixer_weights(wsp_p, bsp_p), sample=False, pos0=0)
    h_s, hn_s, rt_s, cnt_s, hist_s, v_s = _mixer_call(
        x_sample.reshape(n_s, D_MODEL), state_pool[0], mixer_weights(wsp_s, bsp_s),
        sample=True, pos0=past_len)

    n_run = jnp.concatenate([cnt_p, cnt_s], axis=0)[:, 0, :N_EXPERTS].astype(I32)
    n_even = _ceil_to(n_run, RUN_ALIGN)
    l_start = _excl_cumsum(n_even, 1)
    tile_rows = jnp.sum(n_even, axis=1)
    base = _excl_cumsum(n_even, 0)
    counts = jnp.sum(n_even, axis=0)
    padded = _ceil_to(counts, MOE_TILE)
    ends = jnp.cumsum(padded)
    offs = ends - padded
    g_start = offs[None, :] + base
    n8 = n_even // 8
    f4 = (n_even // 4) % 2
    f2 = (n_even // 2) % 2
    l8, g8, c8 = _copy_tables(n8, l_start, g_start, 8, COPY_CAP[0])
    l4, g4, c4 = _copy_tables(f4, l_start + 8 * n8, g_start + 8 * n8, 0, COPY_CAP[1])
    l2, g2, c2 = _copy_tables(f2, l_start + 8 * n8 + 4 * f4, g_start + 8 * n8 + 4 * f4, 0, COPY_CAP[2])
    copy_cnt = jnp.stack([c8, c4, c2], axis=1).reshape(-1).astype(I32)
    tile_rows = tile_rows.astype(I32)
    z_start = (offs + counts).astype(I32)
    z_big = ((padded - counts) // ZERO_ROWS).astype(I32)
    z_blocks = (((padded - counts) % ZERO_ROWS + 7) // 8).astype(I32)

    max_rows = 2 * n_tok + n_run.shape[0] * N_EXPERTS * (RUN_ALIGN - 1)
    n_tiles = max_rows // MOE_TILE + N_EXPERTS + 1
    n_used = (ends[-1] // MOE_TILE).astype(I32).reshape(1)
    tile_id = jnp.minimum(jnp.arange(n_tiles, dtype=I32), n_used - 1)
    tile_expert = jnp.sum((tile_id[:, None] * MOE_TILE) >= ends[None, :], axis=1).astype(I32)
    tile_expert = jnp.minimum(tile_expert, N_EXPERTS - 1)
    pick_e = tile_expert[:, None] == jnp.arange(N_EXPERTS, dtype=I32)[None, :]
    data_end = jnp.sum(jnp.where(pick_e, (offs + counts)[None, :], 0), axis=1)
    n_groups_used = jnp.clip((data_end - tile_id * MOE_TILE + ROW_GROUP - 1) // ROW_GROUP,
                             1, MOE_TILE // ROW_GROUP).astype(I32)
    e_ids = jnp.arange(N_EXPERTS, dtype=I32)
    nonempty = padded > 0
    seg_index = jnp.cumsum(nonempty.astype(I32)) - 1
    later = nonempty[None, :] & (e_ids[None, :] > e_ids[:, None])
    next_of = jnp.min(jnp.where(later, e_ids[None, :], N_EXPERTS), axis=1)
    next_of = jnp.where(next_of == N_EXPERTS, -1, next_of)
    buf_parity = (jnp.sum(jnp.where(pick_e, seg_index[None, :], 0), axis=1) % 2).astype(I32)
    next_expert = jnp.sum(jnp.where(pick_e, next_of[None, :], 0), axis=1).astype(I32)

    copy_tabs = (l8, g8, l4, g4, l2, g2, copy_cnt, tile_rows)
    xs = _dispatch_call(copy_tabs + (z_start, z_big, z_blocks, n_used), hn_p, hn_s, rt_p, rt_s,
                        n_tiles * MOE_TILE)
    ys = _moe_call(tile_expert, n_used, buf_parity, next_expert, n_groups_used, xs, n_tiles,
                   w_eg[0], w_eu[0], w_ed[0])
    fin_w = (w_ple[0].astype(BF16), w_ple_gate[0].astype(BF16), row(g_ple[0]), row(g_final))
    y_p = _final_call(copy_tabs, h_p, rt_p, p_prompt[0].reshape(n_p, PLE_DIM), ys, *fin_w, tile_base=0)
    y_s = _final_call(copy_tabs, h_s, rt_s, p_sample[0].reshape(n_s, PLE_DIM), ys, *fin_w,
                      tile_base=n_p // TOK_TILE)

    return (y_p.reshape(batch, seq, D_MODEL),
            y_s.reshape(dec_batch, dec_seq, D_MODEL),
            hist_p[None],
            hist_s[None],
            v_s.reshape(1, dec_batch, dec_seq, GMLP_WIDTH))
```

```python
import functools

import jax
import jax.numpy as jnp
from jax import lax
from jax.experimental import pallas as pl
from jax.experimental.pallas import tpu as pltpu

D_MODEL = 1024
POOL_WIDTH = 512
GMLP_WIDTH = 512
IN_WIDTH = POOL_WIDTH + 2 * GMLP_WIDTH
POOL_WINDOWS = (2, 4, 8, 16)
POOL_GROUP = 128
POOL_HIST = 15
CHUNK = 128
POOL_SPAN = CHUNK + 16
N_GMLP_HEADS = 4
GMLP_HEAD = 128
N_EXPERT_GROUPS = 4
EXPERTS_PER_GROUP = 8
N_EXPERTS = 32
D_EXPERT = 512
PLE_DIM = 256
EPS = 1e-6

LANES = 128
SUBLANES = 8
HALF = D_MODEL // 2
ROW_CHUNKS = HALF // LANES

TOK_TILE = 256
MIX_TILE = 512
MOE_TILE = 640
ROW_GROUP = 128
RUN_ALIGN = 2
COPY_ROWS = (8, 4, 2)
ZERO_ROWS = 64
LOC_ROWS = 2 * TOK_TILE + N_EXPERTS * (RUN_ALIGN - 1)
COPY_CAP = (LOC_ROWS // 8 + 1, N_EXPERTS, N_EXPERTS)
VMEM_LIMIT = 56 * 1024 * 1024

F32 = jnp.float32
BF16 = jnp.bfloat16
I32 = jnp.int32
U32 = jnp.uint32


def _rms(x, g):
    return x * lax.rsqrt(jnp.mean(x * x, axis=-1, keepdims=True) + EPS) * g


def _gelu_tanh(x):
    inner = x * (0.7978845608028654 + (0.7978845608028654 * 0.044715) * (x * x))
    half = 0.5 * x
    return half + half * jnp.tanh(inner)


def _store_packed(ref, val, n_rows):
    packed = pltpu.pack_elementwise([val[:, :HALF], val[:, HALF:]], packed_dtype=BF16)
    for k in range(ROW_CHUNKS):
        ref[pl.ds(k, n_rows, stride=ROW_CHUNKS), :] = packed[:, k * LANES:(k + 1) * LANES]


def _load_packed(ref, n_rows):
    packed = jnp.concatenate(
        [ref[pl.ds(k, n_rows, stride=ROW_CHUNKS), :] for k in range(ROW_CHUNKS)], axis=-1)
    lo = pltpu.unpack_elementwise(packed, index=0, packed_dtype=BF16, unpacked_dtype=F32)
    hi = pltpu.unpack_elementwise(packed, index=1, packed_dtype=BF16, unpacked_dtype=F32)
    return lo, hi


def _rows(start, n):
    return pl.ds(pl.multiple_of(start * ROW_CHUNKS, SUBLANES), n * ROW_CHUNKS)


def _route(hn, wr_ref, br_ref, n_tok):
    logits = jnp.dot(hn, wr_ref[...], preferred_element_type=F32) + br_ref[...]

    lane = lax.broadcasted_iota(I32, (n_tok, LANES), 1).astype(F32)
    neg = jnp.float32(-jnp.inf)
    big = jnp.float32(1e9)
    gmask = (lane >= N_EXPERTS) & (lane < N_EXPERTS + N_EXPERT_GROUPS)
    gl = jnp.where(gmask, logits, neg)
    gmax = jnp.max(gl, axis=-1, keepdims=True)
    gidx = jnp.min(jnp.where(gl == gmax, lane, big), axis=-1, keepdims=True) - N_EXPERTS
    gden = jnp.sum(jnp.exp(gl - gmax), axis=-1, keepdims=True)
    gprob = 1.0 / gden

    lo = gidx * EXPERTS_PER_GROUP
    emask = (lane >= lo) & (lane < lo + EXPERTS_PER_GROUP)
    el = jnp.where(emask, logits, neg)
    m1 = jnp.max(el, axis=-1, keepdims=True)
    i1 = jnp.min(jnp.where(el == m1, lane, big), axis=-1, keepdims=True)
    el2 = jnp.where(lane == i1, neg, el)
    m2 = jnp.max(el2, axis=-1, keepdims=True)
    i2 = jnp.min(jnp.where(el2 == m2, lane, big), axis=-1, keepdims=True)
    t = jnp.exp(m2 - m1)
    w1 = gprob / (1.0 + t)
    w2 = gprob * t / (1.0 + t)

    sel1 = lane == i1
    sel2 = lane == i2
    oh = jnp.where(sel1 | sel2, 1.0, 0.0).astype(BF16)
    r = lax.broadcasted_iota(I32, (n_tok, n_tok), 0)
    c = lax.broadcasted_iota(I32, (n_tok, n_tok), 1)
    tri = jnp.where(c < r, 1.0, 0.0).astype(BF16)
    before = jnp.dot(tri, oh, preferred_element_type=F32)
    counts = jnp.sum(oh.astype(F32), axis=0, keepdims=True)
    assert RUN_ALIGN == 2
    c_even = counts + (counts - 2.0 * jnp.floor(counts * 0.5))
    c_hi = jnp.floor(c_even * (1.0 / 256.0))
    piece = lax.broadcasted_iota(I32, (SUBLANES, LANES), 0)
    pieces = jnp.where(piece == 0, c_hi, jnp.where(piece == 1, c_even - 256.0 * c_hi, 0.0))
    er = lax.broadcasted_iota(I32, (LANES, LANES), 0)
    ec = lax.broadcasted_iota(I32, (LANES, LANES), 1)
    upper = jnp.where(er < ec, 1.0, 0.0).astype(BF16)
    prefix = jnp.dot(pieces.astype(BF16), upper, preferred_element_type=F32)
    lstart = 256.0 * prefix[0:1, :] + prefix[1:2, :]
    lpos1 = jnp.sum(jnp.where(sel1, before + lstart, 0.0), axis=-1, keepdims=True)
    lpos2 = jnp.sum(jnp.where(sel2, before + lstart, 0.0), axis=-1, keepdims=True)

    out = jnp.where(lane == 0, i1, 0.0)
    for k, val in enumerate((i2, lpos1, lpos2, w1, w2), start=1):
        out = jnp.where(lane == k, val, out)
    return out, counts


def _mixer_kernel(*refs, n_tok, sample, pos0):
    if sample:
        (x_ref, hist_ref, gmix_ref, win_ref, wpool_ref, pscale_ref, gv_ref, bv_ref, wsp_ref,
         bsp_ref, wout_ref, gffn_ref, wr_ref, br_ref, band_ref, band0_ref,
         h_ref, hn_ref, rt_ref, cnt_ref, hist_out_ref, v_ref, ext_scr) = refs
    else:
        (x_ref, gmix_ref, win_ref, wpool_ref, pscale_ref, gv_ref, bv_ref, wsp_ref,
         bsp_ref, wout_ref, gffn_ref, wr_ref, br_ref, band_ref, band0_ref,
         h_ref, hn_ref, rt_ref, cnt_ref, hist_out_ref, ext_scr) = refs

    x = x_ref[...]
    xn = _rms(x, gmix_ref[...])
    z = jnp.dot(xn.astype(BF16), win_ref[...], preferred_element_type=F32)
    a = z[:, :POOL_WIDTH]
    uv = _gelu_tanh(z[:, POOL_WIDTH:])
    u = uv[:, :GMLP_WIDTH]
    v = uv[:, GMLP_WIDTH:]
    mu = jnp.mean(v, axis=-1, keepdims=True)
    vc = v - mu
    v = vc * lax.rsqrt(jnp.mean(vc * vc, axis=-1, keepdims=True) + EPS) * gv_ref[...] + bv_ref[...]

    pooled = []
    if sample:
        n_seq = n_tok // SUBLANES
        ext_scr[:, 1:1 + POOL_HIST, :] = hist_ref[...]
        ext_scr[:, 1 + POOL_HIST:, :] = a.reshape(n_seq, SUBLANES, POOL_WIDTH)
        hist_out_ref[...] = ext_scr[:, 1 + SUBLANES:, :]
        pos = pos0 + lax.broadcasted_iota(I32, (n_seq, SUBLANES, 1), 1)
        for gi, w in enumerate(POOL_WINDOWS):
            cols = slice(gi * POOL_GROUP, (gi + 1) * POOL_GROUP)
            win = ext_scr[:, 16:16 + SUBLANES, cols]
            for j in range(1, w):
                win = win + ext_scr[:, 16 - j:16 - j + SUBLANES, cols]
            cnt = jnp.minimum(w, pos + 1).astype(F32)
            pooled.append((win / cnt).reshape(n_tok, POOL_GROUP) - a[:, cols])
    else:
        l = pl.program_id(1)

        @pl.when(l == 0)
        def _():
            ext_scr[0:16, :] = jnp.zeros((16, POOL_WIDTH), F32)

        ext_scr[16:16 + n_tok, :] = a
        chunks = []
        for ci in range(n_tok // CHUNK):
            ext_c = ext_scr[ci * CHUNK:ci * CHUNK + POOL_SPAN, :].astype(BF16)
            groups = []
            for gi in range(len(POOL_WINDOWS)):
                band = band_ref[gi]
                if ci == 0:
                    band = jnp.where(l == 0, band0_ref[gi], band)
                groups.append(jnp.dot(band, ext_c[:, gi * POOL_GROUP:(gi + 1) * POOL_GROUP],
                                      preferred_element_type=F32))
            chunks.append(groups)
        pooled = [jnp.concatenate([chunks[ci][gi] for ci in range(n_tok // CHUNK)], axis=0)
                  for gi in range(len(POOL_WINDOWS))]
        tail = ext_scr[n_tok + 1:n_tok + 16, :]
        hist_out_ref[...] = tail
        ext_scr[1:16, :] = tail

    pool_out = jnp.concatenate(
        [jnp.dot(pooled[gi].astype(BF16), wpool_ref[gi], preferred_element_type=F32)
         for gi in range(len(POOL_WINDOWS))], axis=-1) * pscale_ref[...]

    blk = SUBLANES if sample else CHUNK
    tr = lax.broadcasted_iota(I32, (CHUNK, CHUNK), 0)
    sc = lax.broadcasted_iota(I32, (CHUNK, CHUNK), 1)
    mask = (sc <= tr) & ((sc // blk) == (tr // blk))
    wsp = [jnp.where(mask, wsp_ref[hh], jnp.zeros((), BF16)) for hh in range(N_GMLP_HEADS)]
    v_bf = v.astype(BF16)
    mixed_chunks = []
    for ci in range(n_tok // CHUNK):
        rows = slice(ci * CHUNK, (ci + 1) * CHUNK)
        heads = [jnp.dot(wsp[hh], v_bf[rows, hh * GMLP_HEAD:(hh + 1) * GMLP_HEAD],
                         preferred_element_type=F32) for hh in range(N_GMLP_HEADS)]
        mixed_chunks.append(jnp.concatenate(heads, axis=-1) + bsp_ref[...])
    mixed = jnp.concatenate(mixed_chunks, axis=0)
    gmlp_out = u * mixed

    cat = jnp.concatenate([pool_out, gmlp_out], axis=-1)
    h = x + jnp.dot(cat.astype(BF16), wout_ref[...], preferred_element_type=F32)
    hn = _rms(h, gffn_ref[...]).astype(BF16)

    h_ref[...] = h
    hn_ref[...] = hn
    if sample:
        v_ref[...] = v
    for s in range(n_tok // TOK_TILE):
        rows = slice(s * TOK_TILE, (s + 1) * TOK_TILE)
        rt, counts = _route(hn[rows], wr_ref, br_ref, TOK_TILE)
        rt_ref[rows, :] = rt[:, :SUBLANES]
        cnt_ref[s] = counts


def _full(shape):
    return pl.BlockSpec(shape, lambda *_: (0,) * len(shape))


def _mixer_call(x, hist, weights, *, sample, pos0):
    w_specs = [_full(w.shape) for w in weights]
    n_tok = MIX_TILE
    if sample:
        n_rows = x.shape[0]
        n_seq = n_tok // SUBLANES
        grid = (n_rows // n_tok,)
        tok_map = lambda i: (i, 0)
        tile_map = lambda i: (i, 0, 0)
        in_specs = [pl.BlockSpec((n_tok, D_MODEL), tok_map),
                    pl.BlockSpec((n_seq, POOL_HIST, POOL_WIDTH), lambda i: (i, 0, 0))] + w_specs
        hist_shape = jax.ShapeDtypeStruct(hist.shape, F32)
        hist_spec = pl.BlockSpec((n_seq, POOL_HIST, POOL_WIDTH), lambda i: (i, 0, 0))
        ext = pltpu.VMEM((n_seq, 16 + SUBLANES, POOL_WIDTH), F32)
        args = (x, hist)
    else:
        b, seq, _ = x.shape
        n_rows = b * seq
        n_l = seq // n_tok
        grid = (b, n_l)
        tok_map = lambda bi, li: (bi * n_l + li, 0)
        tile_map = lambda bi, li: (bi * n_l + li, 0, 0)
        in_specs = [pl.BlockSpec((None, n_tok, D_MODEL), lambda bi, li: (bi, li, 0))] + w_specs
        hist_shape = jax.ShapeDtypeStruct((b, POOL_HIST, POOL_WIDTH), F32)
        hist_spec = pl.BlockSpec((None, POOL_HIST, POOL_WIDTH), lambda bi, li: (bi, 0, 0))
        ext = pltpu.VMEM((16 + n_tok, POOL_WIDTH), F32)
        args = (x,)
    out_shape = [jax.ShapeDtypeStruct((n_rows, D_MODEL), F32),
                 jax.ShapeDtypeStruct((n_rows, D_MODEL), BF16),
                 jax.ShapeDtypeStruct((n_rows, SUBLANES), F32),
                 jax.ShapeDtypeStruct((n_rows // TOK_TILE, 1, LANES), F32),
                 hist_shape]
    out_specs = [pl.BlockSpec((n_tok, D_MODEL), tok_map),
                 pl.BlockSpec((n_tok, D_MODEL), tok_map),
                 pl.BlockSpec((n_tok, SUBLANES), tok_map),
                 pl.BlockSpec((n_tok // TOK_TILE, 1, LANES), tile_map),
                 hist_spec]
    if sample:
        out_shape.append(jax.ShapeDtypeStruct((n_rows, GMLP_WIDTH), F32))
        out_specs.append(pl.BlockSpec((n_tok, GMLP_WIDTH), tok_map))
    return pl.pallas_call(
        functools.partial(_mixer_kernel, n_tok=n_tok, sample=sample, pos0=pos0),
        grid=grid, in_specs=in_specs, out_specs=out_specs, out_shape=out_shape,
        scratch_shapes=[ext],
        compiler_params=pltpu.CompilerParams(
            dimension_semantics=("arbitrary",) * len(grid), vmem_limit_bytes=VMEM_LIMIT),
        name="mixer_sample" if sample else "mixer_prompt",
    )(*args, *weights)


def _for_each_copy(tile, cnt_ref, fn, active=True):
    for s, n_rows in enumerate(COPY_ROWS):
        def body(j, carry, s=s, n_rows=n_rows):
            fn(s, n_rows, tile * COPY_CAP[s] + j)
            return carry
        lax.fori_loop(0, jnp.where(active, cnt_ref[tile * len(COPY_ROWS) + s], 0), body, 0)


def _wait_rows(n_rows_total, make_wait):
    units = n_rows_total // RUN_ALIGN
    bit = 0
    while (RUN_ALIGN << bit) <= LOC_ROWS:
        @pl.when((units >> bit) & 1 == 1)
        def _(bit=bit):
            make_wait(RUN_ALIGN << bit).wait()
        bit += 1


def _as_rows(*cols):
    n = cols[0].shape[0]
    lane = lax.broadcasted_iota(I32, (n, LANES), 1)
    packed = jnp.zeros((n, LANES), F32)
    for k, col in enumerate(cols):
        packed = jnp.where(lane == k, col, packed)
    rows = packed.T
    return [rows[k:k + 1, :] for k in range(len(cols))]


def _dispatch_kernel(l8, g8, l4, g4, l2, g2, cnt_ref, trow_ref, zs_ref, zbig_ref, zb_ref, nu_ref,
                     hnp_ref, hns_ref, rtp_ref, rts_ref, xs_ref, loc, zbuf, sem, zsem,
                     *, n_prompt_steps):
    i = pl.program_id(0)
    n_steps = pl.num_programs(0)
    slot = i % 2
    local_tabs, sorted_tabs = (l8, l4, l2), (g8, g4, g2)

    @pl.when(i == 0)
    def _():
        zbuf[...] = jnp.zeros(zbuf.shape, U32)
        n_tiles = xs_ref.shape[0] // (MOE_TILE * ROW_CHUNKS)

        def zero_copy(e, c, n_rows, first):
            return pltpu.make_async_copy(zbuf.at[pl.ds(0, n_rows * ROW_CHUNKS)],
                                         xs_ref.at[_rows(first + c * n_rows, n_rows)], zsem)

        def tail_copy(t):
            return pltpu.make_async_copy(zbuf, xs_ref.at[_rows(t * MOE_TILE, MOE_TILE)], zsem)

        def per_expert(e, totals):
            def big(c, carry):
                zero_copy(e, c, ZERO_ROWS, zs_ref[e]).start()
                return carry

            def small(c, carry):
                zero_copy(e, c, 8, zs_ref[e] + zbig_ref[e] * ZERO_ROWS).start()
                return carry

            lax.fori_loop(0, zbig_ref[e], big, 0)
            lax.fori_loop(0, zb_ref[e], small, 0)
            return totals[0] + zbig_ref[e], totals[1] + zb_ref[e]

        n_big, n_small = lax.fori_loop(0, N_EXPERTS, per_expert, (0, 0))

        def wait_big(_, carry):
            zero_copy(0, 0, ZERO_ROWS, 0).wait()
            return carry

        def wait_small(_, carry):
            zero_copy(0, 0, 8, 0).wait()
            return carry

        lax.fori_loop(0, n_big, wait_big, 0)
        lax.fori_loop(0, n_small, wait_small, 0)

        def tail_start(t, carry):
            tail_copy(t).start()
            return carry

        def tail_wait(t, carry):
            tail_copy(t).wait()
            return carry

        spill = pltpu.make_async_copy(zbuf.at[pl.ds(0, 8 * ROW_CHUNKS)],
                                      xs_ref.at[_rows(n_tiles * MOE_TILE, 8)], zsem)
        lax.fori_loop(nu_ref[0], n_tiles, tail_start, 0)
        spill.start()
        lax.fori_loop(nu_ref[0], n_tiles, tail_wait, 0)
        spill.wait()

    is_prompt = i < n_prompt_steps
    rt = jnp.where(is_prompt, rtp_ref[...], rts_ref[...])
    row1, row2 = _as_rows(rt[:, 2:3], rt[:, 3:4])
    grow = lax.broadcasted_iota(I32, (LOC_ROWS, TOK_TILE), 0).astype(F32)
    perm = jnp.where((grow == row1) | (grow == row2), 1.0, 0.0).astype(BF16)
    hn = jnp.where(is_prompt, hnp_ref[...], hns_ref[...])
    grouped = jnp.dot(perm, hn, preferred_element_type=F32)
    _store_packed(loc.at[slot], grouped, LOC_ROWS)

    def run_copy(buf_slot, s, n_rows, k):
        src = loc.at[buf_slot, _rows(local_tabs[s][k], n_rows)]
        dst = xs_ref.at[_rows(sorted_tabs[s][k], n_rows)]
        return pltpu.make_async_copy(src, dst, sem.at[buf_slot])

    _for_each_copy(i, cnt_ref, lambda s, n_rows, k: run_copy(slot, s, n_rows, k).start())

    def drain(tile, buf_slot):
        def make_wait(n):
            return pltpu.make_async_copy(loc.at[buf_slot, pl.ds(0, n * ROW_CHUNKS)],
                                         xs_ref.at[pl.ds(0, n * ROW_CHUNKS)], sem.at[buf_slot])
        _wait_rows(trow_ref[tile], make_wait)

    @pl.when(i > 0)
    def _():
        drain(i - 1, 1 - slot)

    @pl.when(i == n_steps - 1)
    def _():
        drain(i, slot)


def _clamped_maps(np_steps):
    pmap = lambda i, *_: (jnp.minimum(i, np_steps - 1), 0)
    smap = lambda i, *_: (jnp.maximum(i - np_steps, 0), 0)
    return pmap, smap


def _dispatch_call(tables, hn_p, hn_s, rt_p, rt_s, n_sorted_rows):
    n_p, n_s = hn_p.shape[0], hn_s.shape[0]
    np_steps = n_p // TOK_TILE
    n_steps = np_steps + n_s // TOK_TILE
    pmap, smap = _clamped_maps(np_steps)
    out_rows = (n_sorted_rows + 8) * ROW_CHUNKS
    return pl.pallas_call(
        functools.partial(_dispatch_kernel, n_prompt_steps=np_steps),
        grid_spec=pltpu.PrefetchScalarGridSpec(
            num_scalar_prefetch=len(tables), grid=(n_steps,),
            in_specs=[pl.BlockSpec((TOK_TILE, D_MODEL), pmap),
                      pl.BlockSpec((TOK_TILE, D_MODEL), smap),
                      pl.BlockSpec((TOK_TILE, SUBLANES), pmap),
                      pl.BlockSpec((TOK_TILE, SUBLANES), smap)],
            out_specs=pl.BlockSpec(memory_space=pl.ANY),
            scratch_shapes=[pltpu.VMEM((2, LOC_ROWS * ROW_CHUNKS, LANES), U32),
                            pltpu.VMEM((MOE_TILE * ROW_CHUNKS, LANES), U32),
                            pltpu.SemaphoreType.DMA((2,)),
                            pltpu.SemaphoreType.DMA(())]),
        out_shape=jax.ShapeDtypeStruct((out_rows, LANES), U32),
        compiler_params=pltpu.CompilerParams(
            dimension_semantics=("arbitrary",), vmem_limit_bytes=VMEM_LIMIT),
        name="dispatch",
    )(*tables, hn_p, hn_s, rt_p, rt_s)


def _moe_kernel(te_ref, nu_ref, par_ref, nxt_ref, ngrp_ref, xs_ref, wg_hbm, wu_hbm, wd_hbm, ys_ref,
                wg_f32, wu_f32, wd_f32, wg_bf, wu_bf, wd_bf, wsem):
    i = pl.program_id(0)
    used = i < nu_ref[0]

    def store_zeros(ref, n_rows):
        ij = (lax.broadcasted_iota(I32, (n_rows, D_MODEL), 0)
              + lax.broadcasted_iota(I32, (n_rows, D_MODEL), 1))
        _store_packed(ref, jnp.where(ij < jnp.minimum(nu_ref[0], 0), 1.0, 0.0), n_rows)

    def weight_copies(expert, buf):
        return [pltpu.make_async_copy(hbm.at[expert], vmem.at[buf], wsem.at[buf])
                for hbm, vmem in ((wg_hbm, wg_f32), (wu_hbm, wu_f32), (wd_hbm, wd_f32))]

    @pl.when(used)
    def _():
        prev = te_ref[jnp.maximum(i - 1, 0)]

        @pl.when((i == 0) | (te_ref[i] != prev))
        def _():
            buf = par_ref[i]

            @pl.when(i == 0)
            def _():
                for cp in weight_copies(te_ref[0], 0):
                    cp.start()

            for cp in weight_copies(te_ref[i], buf):
                cp.wait()
            wg_bf[...] = wg_f32[buf].astype(BF16)
            wu_bf[...] = wu_f32[buf].astype(BF16)
            wd_bf[...] = wd_f32[buf].astype(BF16)

            @pl.when(nxt_ref[i] >= 0)
            def _():
                for cp in weight_copies(nxt_ref[i], 1 - buf):
                    cp.start()

        def expert_rows(n_rows):
            lo, hi = _load_packed(xs_ref, n_rows)
            x = jnp.concatenate([lo, hi], axis=-1).astype(BF16)
            g = jnp.dot(x, wg_bf[...], preferred_element_type=F32)
            u = jnp.dot(x, wu_bf[...], preferred_element_type=F32)
            hh = (g * jax.nn.sigmoid(g)) * u
            y = jnp.dot(hh.astype(BF16), wd_bf[...], preferred_element_type=F32)
            _store_packed(ys_ref, y, n_rows)

        for n_groups in range(1, MOE_TILE // ROW_GROUP + 1):
            @pl.when(ngrp_ref[i] == n_groups)
            def _(n_rows=n_groups * ROW_GROUP):
                expert_rows(n_rows)
                if n_rows < MOE_TILE:
                    rest = MOE_TILE - n_rows
                    store_zeros(ys_ref.at[pl.ds(n_rows * ROW_CHUNKS, rest * ROW_CHUNKS)], rest)

    @pl.when(jnp.logical_not(used))
    def _():
        store_zeros(ys_ref, MOE_TILE)


def _moe_call(tile_expert, n_used, buf_parity, next_expert, n_groups_used, xs, n_tiles, w_eg, w_eu, w_ed):
    blk = MOE_TILE * ROW_CHUNKS
    any_spec = pl.BlockSpec(memory_space=pl.ANY)
    return pl.pallas_call(
        _moe_kernel,
        grid_spec=pltpu.PrefetchScalarGridSpec(
            num_scalar_prefetch=5, grid=(n_tiles,),
            in_specs=[
                pl.BlockSpec((blk, LANES), lambda i, te, nu, *_: (jnp.minimum(i, nu[0] - 1), 0)),
                any_spec, any_spec, any_spec,
            ],
            out_specs=pl.BlockSpec((blk, LANES), lambda i, *_: (i, 0)),
            scratch_shapes=[pltpu.VMEM((2, D_MODEL, D_EXPERT), F32),
                            pltpu.VMEM((2, D_MODEL, D_EXPERT), F32),
                            pltpu.VMEM((2, D_EXPERT, D_MODEL), F32),
                            pltpu.VMEM((D_MODEL, D_EXPERT), BF16),
                            pltpu.VMEM((D_MODEL, D_EXPERT), BF16),
                            pltpu.VMEM((D_EXPERT, D_MODEL), BF16),
                            pltpu.SemaphoreType.DMA((2,))]),
        out_shape=jax.ShapeDtypeStruct((n_tiles * blk, LANES), U32),
        compiler_params=pltpu.CompilerParams(
            dimension_semantics=("arbitrary",), vmem_limit_bytes=VMEM_LIMIT),
        name="experts",
    )(tile_expert, n_used, buf_parity, next_expert, n_groups_used, xs, w_eg, w_eu, w_ed)


def _final_kernel(l8, g8, l4, g4, l2, g2, cnt_ref, trow_ref,
                  h_ref, rt_ref, p_ref, ys_ref, wple_ref, wgate_ref, gple_ref, gfin_ref, out_ref,
                  ybuf, yb_s, perm_s, sem, *, tile_base, n_tiles):
    s = pl.program_id(0)
    a_slot = s % 2
    b_slot = 1 - a_slot
    local_tabs, sorted_tabs = (l8, l4, l2), (g8, g4, g2)

    def run_copy(buf_slot, sz, n_rows, k):
        src = ys_ref.at[_rows(sorted_tabs[sz][k], n_rows)]
        dst = ybuf.at[buf_slot, _rows(local_tabs[sz][k], n_rows)]
        return pltpu.make_async_copy(src, dst, sem.at[buf_slot])

    def issue(tile, buf_slot, active):
        _for_each_copy(tile_base + jnp.minimum(tile, n_tiles - 1), cnt_ref,
                       lambda sz, n_rows, k: run_copy(buf_slot, sz, n_rows, k).start(), active)

    @pl.when(s == 0)
    def _():
        ybuf[...] = jnp.zeros(ybuf.shape, U32)
        yb_s[...] = jnp.zeros(yb_s.shape, BF16)
        perm_s[...] = jnp.zeros(perm_s.shape, BF16)
        issue(0, 0, True)

    issue(s + 1, b_slot, s + 1 < n_tiles)

    def make_wait(n):
        return pltpu.make_async_copy(ys_ref.at[pl.ds(0, n * ROW_CHUNKS)],
                                     ybuf.at[a_slot, pl.ds(0, n * ROW_CHUNKS)], sem.at[a_slot])
    a_rows = jnp.where(s < n_tiles, trow_ref[tile_base + jnp.minimum(s, n_tiles - 1)], 0)
    _wait_rows(a_rows, make_wait)

    h = h_ref[...]
    h = h + jnp.dot(perm_s[b_slot], yb_s[b_slot], preferred_element_type=F32)
    r = _rms(h, gple_ref[...])
    gate = jax.nn.sigmoid(jnp.dot(r.astype(BF16), wgate_ref[...], preferred_element_type=F32))
    h = h + jnp.dot(p_ref[...].astype(BF16), wple_ref[...], preferred_element_type=F32) * gate
    out_ref[...] = _rms(h, gfin_ref[...])

    rt = rt_ref[...]
    lpos1, lpos2 = rt[:, 2:3], rt[:, 3:4]
    lane = lax.broadcasted_iota(I32, (TOK_TILE, LOC_ROWS), 1).astype(F32)
    perm_s[a_slot] = jnp.where((lane == lpos1) | (lane == lpos2), 1.0, 0.0).astype(BF16)
    row1, row2, w1_row, w2_row = _as_rows(lpos1, lpos2, rt[:, 4:5], rt[:, 5:6])
    grow = lax.broadcasted_iota(I32, (LOC_ROWS, TOK_TILE), 0).astype(F32)
    wrow = jnp.sum(jnp.where(grow == row1, w1_row, 0.0) + jnp.where(grow == row2, w2_row, 0.0),
                   axis=-1, keepdims=True)
    lo, hi = _load_packed(ybuf.at[a_slot], LOC_ROWS)
    yb_s[a_slot] = (jnp.concatenate([lo, hi], axis=-1) * wrow).astype(BF16)


def _final_call(tables, h, rt, p, ys, wple, wgate, gple, gfin, *, tile_base):
    n_rows = h.shape[0]
    n_tiles = n_rows // TOK_TILE
    b_map = lambda s, *_: (jnp.maximum(s - 1, 0), 0)
    a_map = lambda s, *_: (jnp.minimum(s, n_tiles - 1), 0)
    cmap = lambda s, *_: (0, 0)
    return pl.pallas_call(
        functools.partial(_final_kernel, tile_base=tile_base, n_tiles=n_tiles),
        grid_spec=pltpu.PrefetchScalarGridSpec(
            num_scalar_prefetch=len(tables), grid=(n_tiles + 1,),
            in_specs=[
                pl.BlockSpec((TOK_TILE, D_MODEL), b_map),
                pl.BlockSpec((TOK_TILE, SUBLANES), a_map),
                pl.BlockSpec((TOK_TILE, PLE_DIM), b_map),
                pl.BlockSpec(memory_space=pl.ANY),
                pl.BlockSpec(wple.shape, cmap),
                pl.BlockSpec(wgate.shape, cmap),
                pl.BlockSpec(gple.shape, cmap),
                pl.BlockSpec(gfin.shape, cmap),
            ],
            out_specs=pl.BlockSpec((TOK_TILE, D_MODEL), b_map),
            scratch_shapes=[pltpu.VMEM((2, LOC_ROWS * ROW_CHUNKS, LANES), U32),
                            pltpu.VMEM((2, LOC_ROWS, D_MODEL), BF16),
                            pltpu.VMEM((2, TOK_TILE, LOC_ROWS), BF16),
                            pltpu.SemaphoreType.DMA((2,))]),
        out_shape=jax.ShapeDtypeStruct((n_rows, D_MODEL), F32),
        compiler_params=pltpu.CompilerParams(
            dimension_semantics=("arbitrary",), vmem_limit_bytes=VMEM_LIMIT),
        name="final",
    )(*tables, h, rt, p, ys, wple, wgate, gple, gfin)


def _ceil_to(x, m):
    return ((x + m - 1) // m) * m


def _excl_cumsum(a, axis):
    return jnp.cumsum(a, axis=axis) - a


def _copy_tables(n_copies, first_local, first_sorted, step, cap):
    ends = jnp.cumsum(n_copies, axis=1)
    j = jnp.arange(cap, dtype=I32)[None, :, None]
    e_of = jnp.minimum(jnp.sum(ends[:, None, :] <= j, axis=-1), N_EXPERTS - 1)
    pick = e_of[:, :, None] == jnp.arange(N_EXPERTS, dtype=I32)[None, None, :]
    take = lambda a: jnp.sum(jnp.where(pick, a[:, None, :], 0), axis=-1)
    c = jnp.arange(cap, dtype=I32)[None, :] - (take(ends) - take(n_copies))
    flat = lambda a: a.reshape(-1).astype(I32)
    return flat(take(first_local) + step * c), flat(take(first_sorted) + step * c), ends[:, -1]


def kernel(x_prompt, x_sample, state_pool, p_prompt, p_sample, g_mix, w_in, w_pool, pool_scale,
           g_v, b_v, w_spatial, b_spatial, w_out, g_ffn, w_rg, b_rg, w_re, b_re, w_eg, w_eu, w_ed,
           w_ple, g_ple, w_ple_gate, g_final):
    batch, seq, _ = x_prompt.shape
    dec_batch, dec_seq, _ = x_sample.shape
    assert dec_seq == SUBLANES and g_mix.shape[0] == 1
    n_p = batch * seq
    n_s = dec_batch * dec_seq
    n_tok = n_p + n_s
    past_len = 16384

    row = lambda a: a.reshape(1, -1).astype(F32)
    wr = jnp.concatenate(
        [w_re[0], w_rg[0], jnp.zeros((D_MODEL, LANES - N_EXPERTS - N_EXPERT_GROUPS), F32)], axis=1)
    br = jnp.concatenate(
        [b_re[0], b_rg[0], jnp.zeros((LANES - N_EXPERTS - N_EXPERT_GROUPS,), F32)]).reshape(1, LANES)
    bsp_p = jnp.repeat(jnp.transpose(b_spatial[0]), GMLP_HEAD, axis=1)
    bsp_s = jnp.tile(bsp_p[:dec_seq], (CHUNK // dec_seq, 1))
    wsp_p = w_spatial[0].astype(BF16)
    wsp_s = jnp.tile(w_spatial[0][:, :dec_seq, :dec_seq].astype(BF16),
                     (1, CHUNK // dec_seq, CHUNK // dec_seq))

    def pool_band(first_chunk):
        t = jnp.arange(CHUNK, dtype=I32)[:, None]
        k = jnp.arange(POOL_SPAN, dtype=I32)[None, :]
        mats = []
        for w in POOL_WINDOWS:
            cnt = jnp.minimum(w, t + 1) if first_chunk else jnp.full_like(t, w)
            in_window = (k >= t + 17 - w) & (k <= t + 16)
            mats.append(jnp.where(in_window, 1.0 / cnt.astype(F32), 0.0) - (k == t + 16).astype(F32))
        return jnp.stack(mats).astype(BF16)

    def mixer_weights(wsp, bsp):
        return (row(g_mix[0]), w_in[0].astype(BF16), w_pool[0].astype(BF16), row(pool_scale[0]),
                row(g_v[0]), row(b_v[0]), wsp, bsp, w_out[0].astype(BF16), row(g_ffn[0]),
                wr.astype(BF16), br, pool_band(False), pool_band(True))

    h_p, hn_p, rt_p, cnt_p, hist_p = _mixer_call(
        x_prompt, None, mixer_weights(wsp_p, bsp_p), sample=False, pos0=0)
    h_s, hn_s, rt_s, cnt_s, hist_s, v_s = _mixer_call(
        x_sample.reshape(n_s, D_MODEL), state_pool[0], mixer_weights(wsp_s, bsp_s),
        sample=True, pos0=past_len)

    n_run = jnp.concatenate([cnt_p, cnt_s], axis=0)[:, 0, :N_EXPERTS].astype(I32)
    n_even = _ceil_to(n_run, RUN_ALIGN)
    l_start = _excl_cumsum(n_even, 1)
    tile_rows = jnp.sum(n_even, axis=1)
    base = _excl_cumsum(n_even, 0)
    counts = jnp.sum(n_even, axis=0)
    padded = _ceil_to(counts, MOE_TILE)
    ends = jnp.cumsum(padded)
    offs = ends - padded
    g_start = offs[None, :] + base
    n8 = n_even // 8
    f4 = (n_even // 4) % 2
    f2 = (n_even // 2) % 2
    l8, g8, c8 = _copy_tables(n8, l_start, g_start, 8, COPY_CAP[0])
    l4, g4, c4 = _copy_tables(f4, l_start + 8 * n8, g_start + 8 * n8, 0, COPY_CAP[1])
    l2, g2, c2 = _copy_tables(f2, l_start + 8 * n8 + 4 * f4, g_start + 8 * n8 + 4 * f4, 0, COPY_CAP[2])
    copy_cnt = jnp.stack([c8, c4, c2], axis=1).reshape(-1).astype(I32)
    tile_rows = tile_rows.astype(I32)
    z_start = (offs + counts).astype(I32)
    z_big = ((padded - counts) // ZERO_ROWS).astype(I32)
    z_blocks = (((padded - counts) % ZERO_ROWS + 7) // 8).astype(I32)

    max_rows = 2 * n_tok + n_run.shape[0] * N_EXPERTS * (RUN_ALIGN - 1)
    n_tiles = max_rows // MOE_TILE + N_EXPERTS + 1
    n_used = (ends[-1] // MOE_TILE).astype(I32).reshape(1)
    tile_id = jnp.minimum(jnp.arange(n_tiles, dtype=I32), n_used - 1)
    tile_expert = jnp.sum((tile_id[:, None] * MOE_TILE) >= ends[None, :], axis=1).astype(I32)
    tile_expert = jnp.minimum(tile_expert, N_EXPERTS - 1)
    pick_e = tile_expert[:, None] == jnp.arange(N_EXPERTS, dtype=I32)[None, :]
    data_end = jnp.sum(jnp.where(pick_e, (offs + counts)[None, :], 0), axis=1)
    n_groups_used = jnp.clip((data_end - tile_id * MOE_TILE + ROW_GROUP - 1) // ROW_GROUP,
                             1, MOE_TILE // ROW_GROUP).astype(I32)
    e_ids = jnp.arange(N_EXPERTS, dtype=I32)
    nonempty = padded > 0
    seg_index = jnp.cumsum(nonempty.astype(I32)) - 1
    later = nonempty[None, :] & (e_ids[None, :] > e_ids[:, None])
    next_of = jnp.min(jnp.where(later, e_ids[None, :], N_EXPERTS), axis=1)
    next_of = jnp.where(next_of == N_EXPERTS, -1, next_of)
    buf_parity = (jnp.sum(jnp.where(pick_e, seg_index[None, :], 0), axis=1) % 2).astype(I32)
    next_expert = jnp.sum(jnp.where(pick_e, next_of[None, :], 0), axis=1).astype(I32)

    copy_tabs = (l8, g8, l4, g4, l2, g2, copy_cnt, tile_rows)
    xs = _dispatch_call(copy_tabs + (z_start, z_big, z_blocks, n_used), hn_p, hn_s, rt_p, rt_s,
                        n_tiles * MOE_TILE)
    ys = _moe_call(tile_expert, n_used, buf_parity, next_expert, n_groups_used, xs, n_tiles,
                   w_eg[0], w_eu[0], w_ed[0])
    fin_w = (w_ple[0].astype(BF16), w_ple_gate[0].astype(BF16), row(g_ple[0]), row(g_final))
    y_p = _final_call(copy_tabs, h_p, rt_p, p_prompt[0].reshape(n_p, PLE_DIM), ys, *fin_w, tile_base=0)
    y_s = _final_call(copy_tabs, h_s, rt_s, p_sample[0].reshape(n_s, PLE_DIM), ys, *fin_w,
                      tile_base=n_p // TOK_TILE)

    return (y_p.reshape(batch, seq, D_MODEL),
            y_s.reshape(dec_batch, dec_seq, D_MODEL),
            hist_p[None],
            hist_s[None],
            v_s.reshape(1, dec_batch, dec_seq, GMLP_WIDTH))
```

```python
import functools

import jax
import jax.numpy as jnp
from jax import lax
from jax.experimental import pallas as pl
from jax.experimental.pallas import tpu as pltpu

D_MODEL = 1024
POOL_WIDTH = 512
GMLP_WIDTH = 512
IN_WIDTH = POOL_WIDTH + 2 * GMLP_WIDTH
POOL_WINDOWS = (2, 4, 8, 16)
POOL_GROUP = 128
POOL_HIST = 15
CHUNK = 128
POOL_SPAN = CHUNK + 16
N_GMLP_HEADS = 4
GMLP_HEAD = 128
N_EXPERT_GROUPS = 4
EXPERTS_PER_GROUP = 8
N_EXPERTS = 32
D_EXPERT = 512
PLE_DIM = 256
EPS = 1e-6

LANES = 128
SUBLANES = 8
HALF = D_MODEL // 2
ROW_CHUNKS = HALF // LANES

TOK_TILE = 256
MIX_TILE = 512
MOE_TILE = 640
ROW_GROUP = 128
RUN_ALIGN = 2
COPY_ROWS = (8, 4, 2)
ZERO_ROWS = 64
LOC_ROWS = 2 * TOK_TILE + N_EXPERTS * (RUN_ALIGN - 1)
COPY_CAP = (LOC_ROWS // 8 + 1, N_EXPERTS, N_EXPERTS)
VMEM_LIMIT = 56 * 1024 * 1024

F32 = jnp.float32
BF16 = jnp.bfloat16
I32 = jnp.int32
U32 = jnp.uint32


def _rms(x, g):
    return x * lax.rsqrt(jnp.mean(x * x, axis=-1, keepdims=True) + EPS) * g


def _gelu_tanh(x):
    inner = x * (0.7978845608028654 + (0.7978845608028654 * 0.044715) * (x * x))
    half = 0.5 * x
    return half + half * jnp.tanh(inner)


def _store_packed(ref, val, n_rows):
    packed = pltpu.pack_elementwise([val[:, :HALF], val[:, HALF:]], packed_dtype=BF16)
    for k in range(ROW_CHUNKS):
        ref[pl.ds(k, n_rows, stride=ROW_CHUNKS), :] = packed[:, k * LANES:(k + 1) * LANES]


def _load_packed(ref, n_rows):
    packed = jnp.concatenate(
        [ref[pl.ds(k, n_rows, stride=ROW_CHUNKS), :] for k in range(ROW_CHUNKS)], axis=-1)
    lo = pltpu.unpack_elementwise(packed, index=0, packed_dtype=BF16, unpacked_dtype=F32)
    hi = pltpu.unpack_elementwise(packed, index=1, packed_dtype=BF16, unpacked_dtype=F32)
    return lo, hi


def _rows(start, n):
    return pl.ds(pl.multiple_of(start * ROW_CHUNKS, SUBLANES), n * ROW_CHUNKS)


def _route(hn, wr_ref, br_ref, n_tok):
    logits = jnp.dot(hn, wr_ref[...], preferred_element_type=F32) + br_ref[...]

    lane = lax.broadcasted_iota(I32, (n_tok, LANES), 1).astype(F32)
    neg = jnp.float32(-jnp.inf)
    gmask = (lane >= N_EXPERTS) & (lane < N_EXPERTS + N_EXPERT_GROUPS)
    gl = jnp.where(gmask, logits, neg)
    gmax = jnp.max(gl, axis=-1, keepdims=True)
    gidx = jnp.argmax(gl, axis=-1, keepdims=True).astype(F32) - N_EXPERTS
    gden = jnp.sum(jnp.exp(gl - gmax), axis=-1, keepdims=True)
    gprob = 1.0 / gden

    lo = gidx * EXPERTS_PER_GROUP
    emask = (lane >= lo) & (lane < lo + EXPERTS_PER_GROUP)
    el = jnp.where(emask, logits, neg)
    m1 = jnp.max(el, axis=-1, keepdims=True)
    i1 = jnp.argmax(el, axis=-1, keepdims=True).astype(F32)
    el2 = jnp.where(lane == i1, neg, el)
    m2 = jnp.max(el2, axis=-1, keepdims=True)
    i2 = jnp.argmax(el2, axis=-1, keepdims=True).astype(F32)
    t = jnp.exp(m2 - m1)
    w1 = gprob / (1.0 + t)
    w2 = gprob * t / (1.0 + t)

    sel1 = lane == i1
    sel2 = lane == i2
    oh = jnp.where(sel1 | sel2, 1.0, 0.0).astype(BF16)
    r = lax.broadcasted_iota(I32, (n_tok, n_tok), 0)
    c = lax.broadcasted_iota(I32, (n_tok, n_tok), 1)
    tri = jnp.where(c < r, 1.0, 0.0).astype(BF16)
    before = jnp.dot(tri, oh, preferred_element_type=F32)
    counts = jnp.sum(oh.astype(F32), axis=0, keepdims=True)
    assert RUN_ALIGN == 2
    c_even = counts + (counts - 2.0 * jnp.floor(counts * 0.5))
    c_hi = jnp.floor(c_even * (1.0 / 256.0))
    piece = lax.broadcasted_iota(I32, (SUBLANES, LANES), 0)
    pieces = jnp.where(piece == 0, c_hi, jnp.where(piece == 1, c_even - 256.0 * c_hi, 0.0))
    er = lax.broadcasted_iota(I32, (LANES, LANES), 0)
    ec = lax.broadcasted_iota(I32, (LANES, LANES), 1)
    upper = jnp.where(er < ec, 1.0, 0.0).astype(BF16)
    prefix = jnp.dot(pieces.astype(BF16), upper, preferred_element_type=F32)
    lstart = 256.0 * prefix[0:1, :] + prefix[1:2, :]
    lpos1 = jnp.sum(jnp.where(sel1, before + lstart, 0.0), axis=-1, keepdims=True)
    lpos2 = jnp.sum(jnp.where(sel2, before + lstart, 0.0), axis=-1, keepdims=True)

    out = jnp.where(lane == 0, i1, 0.0)
    for k, val in enumerate((i2, lpos1, lpos2, w1, w2), start=1):
        out = jnp.where(lane == k, val, out)
    return out, counts


def _mixer_kernel(*refs, n_tok, sample, pos0):
    if sample:
        (x_ref, hist_ref, gmix_ref, win_ref, wpool_ref, pscale_ref, gv_ref, bv_ref, wsp_ref,
         bsp_ref, wout_ref, gffn_ref, wr_ref, br_ref, band_ref, band0_ref,
         h_ref, hn_ref, rt_ref, cnt_ref, hist_out_ref, v_ref, ext_scr) = refs
    else:
        (x_ref, gmix_ref, win_ref, wpool_ref, pscale_ref, gv_ref, bv_ref, wsp_ref,
         bsp_ref, wout_ref, gffn_ref, wr_ref, br_ref, band_ref, band0_ref,
         h_ref, hn_ref, rt_ref, cnt_ref, hist_out_ref, ext_scr) = refs

    x = x_ref[...]
    xn = _rms(x, gmix_ref[...])
    z = jnp.dot(xn.astype(BF16), win_ref[...], preferred_element_type=F32)
    a = z[:, :POOL_WIDTH]
    uv = _gelu_tanh(z[:, POOL_WIDTH:])
    u = uv[:, :GMLP_WIDTH]
    v = uv[:, GMLP_WIDTH:]
    mu = jnp.mean(v, axis=-1, keepdims=True)
    vc = v - mu
    v = vc * lax.rsqrt(jnp.mean(vc * vc, axis=-1, keepdims=True) + EPS) * gv_ref[...] + bv_ref[...]

    pooled = []
    if sample:
        n_seq = n_tok // SUBLANES
        ext_scr[:, 1:1 + POOL_HIST, :] = hist_ref[...]
        ext_scr[:, 1 + POOL_HIST:, :] = a.reshape(n_seq, SUBLANES, POOL_WIDTH)
        hist_out_ref[...] = ext_scr[:, 1 + SUBLANES:, :]
        pos = pos0 + lax.broadcasted_iota(I32, (n_seq, SUBLANES, 1), 1)
        for gi, w in enumerate(POOL_WINDOWS):
            cols = slice(gi * POOL_GROUP, (gi + 1) * POOL_GROUP)
            win = ext_scr[:, 16:16 + SUBLANES, cols]
            for j in range(1, w):
                win = win + ext_scr[:, 16 - j:16 - j + SUBLANES, cols]
            cnt = jnp.minimum(w, pos + 1).astype(F32)
            pooled.append((win / cnt).reshape(n_tok, POOL_GROUP) - a[:, cols])
    else:
        l = pl.program_id(1)

        @pl.when(l == 0)
        def _():
            ext_scr[0:16, :] = jnp.zeros((16, POOL_WIDTH), F32)

        ext_scr[16:16 + n_tok, :] = a
        chunks = []
        for ci in range(n_tok // CHUNK):
            ext_c = ext_scr[ci * CHUNK:ci * CHUNK + POOL_SPAN, :].astype(BF16)
            groups = []
            for gi in range(len(POOL_WINDOWS)):
                band = band_ref[gi]
                if ci == 0:
                    band = jnp.where(l == 0, band0_ref[gi], band)
                groups.append(jnp.dot(band, ext_c[:, gi * POOL_GROUP:(gi + 1) * POOL_GROUP],
                                      preferred_element_type=F32))
            chunks.append(groups)
        pooled = [jnp.concatenate([chunks[ci][gi] for ci in range(n_tok // CHUNK)], axis=0)
                  for gi in range(len(POOL_WINDOWS))]
        tail = ext_scr[n_tok + 1:n_tok + 16, :]
        hist_out_ref[...] = tail
        ext_scr[1:16, :] = tail

    pool_out = jnp.concatenate(
        [jnp.dot(pooled[gi].astype(BF16), wpool_ref[gi], preferred_element_type=F32)
         for gi in range(len(POOL_WINDOWS))], axis=-1) * pscale_ref[...]

    blk = SUBLANES if sample else CHUNK
    tr = lax.broadcasted_iota(I32, (CHUNK, CHUNK), 0)
    sc = lax.broadcasted_iota(I32, (CHUNK, CHUNK), 1)
    mask = (sc <= tr) & ((sc // blk) == (tr // blk))
    wsp = [jnp.where(mask, wsp_ref[hh], jnp.zeros((), BF16)) for hh in range(N_GMLP_HEADS)]
    v_bf = v.astype(BF16)
    mixed_chunks = []
    for ci in range(n_tok // CHUNK):
        rows = slice(ci * CHUNK, (ci + 1) * CHUNK)
        heads = [jnp.dot(wsp[hh], v_bf[rows, hh * GMLP_HEAD:(hh + 1) * GMLP_HEAD],
                         preferred_element_type=F32) for hh in range(N_GMLP_HEADS)]
        mixed_chunks.append(jnp.concatenate(heads, axis=-1) + bsp_ref[...])
    mixed = jnp.concatenate(mixed_chunks, axis=0)
    gmlp_out = u * mixed

    cat = jnp.concatenate([pool_out, gmlp_out], axis=-1)
    h = x + jnp.dot(cat.astype(BF16), wout_ref[...], preferred_element_type=F32)
    hn = _rms(h, gffn_ref[...]).astype(BF16)

    h_ref[...] = h
    hn_ref[...] = hn
    if sample:
        v_ref[...] = v
    for s in range(n_tok // TOK_TILE):
        rows = slice(s * TOK_TILE, (s + 1) * TOK_TILE)
        rt, counts = _route(hn[rows], wr_ref, br_ref, TOK_TILE)
        rt_ref[rows, :] = rt[:, :SUBLANES]
        cnt_ref[s] = counts


def _full(shape):
    return pl.BlockSpec(shape, lambda *_: (0,) * len(shape))


def _mixer_call(x, hist, weights, *, sample, pos0):
    w_specs = [_full(w.shape) for w in weights]
    n_tok = MIX_TILE
    if sample:
        n_rows = x.shape[0]
        n_seq = n_tok // SUBLANES
        grid = (n_rows // n_tok,)
        tok_map = lambda i: (i, 0)
        tile_map = lambda i: (i, 0, 0)
        in_specs = [pl.BlockSpec((n_tok, D_MODEL), tok_map),
                    pl.BlockSpec((n_seq, POOL_HIST, POOL_WIDTH), lambda i: (i, 0, 0))] + w_specs
        hist_shape = jax.ShapeDtypeStruct(hist.shape, F32)
        hist_spec = pl.BlockSpec((n_seq, POOL_HIST, POOL_WIDTH), lambda i: (i, 0, 0))
        ext = pltpu.VMEM((n_seq, 16 + SUBLANES, POOL_WIDTH), F32)
        args = (x, hist)
    else:
        b, seq, _ = x.shape
        n_rows = b * seq
        n_l = seq // n_tok
        grid = (b, n_l)
        tok_map = lambda bi, li: (bi * n_l + li, 0)
        tile_map = lambda bi, li: (bi * n_l + li, 0, 0)
        in_specs = [pl.BlockSpec((None, n_tok, D_MODEL), lambda bi, li: (bi, li, 0))] + w_specs
        hist_shape = jax.ShapeDtypeStruct((b, POOL_HIST, POOL_WIDTH), F32)
        hist_spec = pl.BlockSpec((None, POOL_HIST, POOL_WIDTH), lambda bi, li: (bi, 0, 0))
        ext = pltpu.VMEM((16 + n_tok, POOL_WIDTH), F32)
        args = (x,)
    out_shape = [jax.ShapeDtypeStruct((n_rows, D_MODEL), F32),
                 jax.ShapeDtypeStruct((n_rows, D_MODEL), BF16),
                 jax.ShapeDtypeStruct((n_rows, SUBLANES), F32),
                 jax.ShapeDtypeStruct((n_rows // TOK_TILE, 1, LANES), F32),
                 hist_shape]
    out_specs = [pl.BlockSpec((n_tok, D_MODEL), tok_map),
                 pl.BlockSpec((n_tok, D_MODEL), tok_map),
                 pl.BlockSpec((n_tok, SUBLANES), tok_map),
                 pl.BlockSpec((n_tok // TOK_TILE, 1, LANES), tile_map),
                 hist_spec]
    if sample:
        out_shape.append(jax.ShapeDtypeStruct((n_rows, GMLP_WIDTH), F32))
        out_specs.append(pl.BlockSpec((n_tok, GMLP_WIDTH), tok_map))
    return pl.pallas_call(
        functools.partial(_mixer_kernel, n_tok=n_tok, sample=sample, pos0=pos0),
        grid=grid, in_specs=in_specs, out_specs=out_specs, out_shape=out_shape,
        scratch_shapes=[ext],
        compiler_params=pltpu.CompilerParams(
            dimension_semantics=("arbitrary",) * len(grid), vmem_limit_bytes=VMEM_LIMIT),
        name="mixer_sample" if sample else "mixer_prompt",
    )(*args, *weights)


def _for_each_copy(tile, cnt_ref, fn, active=True):
    for s, n_rows in enumerate(COPY_ROWS):
        def body(j, carry, s=s, n_rows=n_rows):
            fn(s, n_rows, tile * COPY_CAP[s] + j)
            return carry
        lax.fori_loop(0, jnp.where(active, cnt_ref[tile * len(COPY_ROWS) + s], 0), body, 0)


def _wait_rows(n_rows_total, make_wait):
    units = n_rows_total // RUN_ALIGN
    bit = 0
    while (RUN_ALIGN << bit) <= LOC_ROWS:
        @pl.when((units >> bit) & 1 == 1)
        def _(bit=bit):
            make_wait(RUN_ALIGN << bit).wait()
        bit += 1


def _as_rows(*cols):
    n = cols[0].shape[0]
    lane = lax.broadcasted_iota(I32, (n, LANES), 1)
    packed = jnp.zeros((n, LANES), F32)
    for k, col in enumerate(cols):
        packed = jnp.where(lane == k, col, packed)
    rows = packed.T
    return [rows[k:k + 1, :] for k in range(len(cols))]


def _dispatch_kernel(l8, g8, l4, g4, l2, g2, cnt_ref, trow_ref, zs_ref, zbig_ref, zb_ref, nu_ref,
                     hnp_ref, hns_ref, rtp_ref, rts_ref, xs_ref, loc, zbuf, sem, zsem,
                     *, n_prompt_steps):
    i = pl.program_id(0)
    n_steps = pl.num_programs(0)
    slot = i % 2
    local_tabs, sorted_tabs = (l8, l4, l2), (g8, g4, g2)

    @pl.when(i == 0)
    def _():
        zbuf[...] = jnp.zeros(zbuf.shape, U32)
        n_tiles = xs_ref.shape[0] // (MOE_TILE * ROW_CHUNKS)

        def zero_copy(e, c, n_rows, first):
            return pltpu.make_async_copy(zbuf.at[pl.ds(0, n_rows * ROW_CHUNKS)],
                                         xs_ref.at[_rows(first + c * n_rows, n_rows)], zsem)

        def tail_copy(t):
            return pltpu.make_async_copy(zbuf, xs_ref.at[_rows(t * MOE_TILE, MOE_TILE)], zsem)

        def per_expert(e, totals):
            def big(c, carry):
                zero_copy(e, c, ZERO_ROWS, zs_ref[e]).start()
                return carry

            def small(c, carry):
                zero_copy(e, c, 8, zs_ref[e] + zbig_ref[e] * ZERO_ROWS).start()
                return carry

            lax.fori_loop(0, zbig_ref[e], big, 0)
            lax.fori_loop(0, zb_ref[e], small, 0)
            return totals[0] + zbig_ref[e], totals[1] + zb_ref[e]

        n_big, n_small = lax.fori_loop(0, N_EXPERTS, per_expert, (0, 0))

        def wait_big(_, carry):
            zero_copy(0, 0, ZERO_ROWS, 0).wait()
            return carry

        def wait_small(_, carry):
            zero_copy(0, 0, 8, 0).wait()
            return carry

        lax.fori_loop(0, n_big, wait_big, 0)
        lax.fori_loop(0, n_small, wait_small, 0)

        def tail_start(t, carry):
            tail_copy(t).start()
            return carry

        def tail_wait(t, carry):
            tail_copy(t).wait()
            return carry

        spill = pltpu.make_async_copy(zbuf.at[pl.ds(0, 8 * ROW_CHUNKS)],
                                      xs_ref.at[_rows(n_tiles * MOE_TILE, 8)], zsem)
        lax.fori_loop(nu_ref[0], n_tiles, tail_start, 0)
        spill.start()
        lax.fori_loop(nu_ref[0], n_tiles, tail_wait, 0)
        spill.wait()

    is_prompt = i < n_prompt_steps
    rt = jnp.where(is_prompt, rtp_ref[...], rts_ref[...])
    row1, row2 = _as_rows(rt[:, 2:3], rt[:, 3:4])
    grow = lax.broadcasted_iota(I32, (LOC_ROWS, TOK_TILE), 0).astype(F32)
    perm = jnp.where((grow == row1) | (grow == row2), 1.0, 0.0).astype(BF16)
    hn = jnp.where(is_prompt, hnp_ref[...], hns_ref[...])
    grouped = jnp.dot(perm, hn, preferred_element_type=F32)
    _store_packed(loc.at[slot], grouped, LOC_ROWS)

    def run_copy(buf_slot, s, n_rows, k):
        src = loc.at[buf_slot, _rows(local_tabs[s][k], n_rows)]
        dst = xs_ref.at[_rows(sorted_tabs[s][k], n_rows)]
        return pltpu.make_async_copy(src, dst, sem.at[buf_slot])

    _for_each_copy(i, cnt_ref, lambda s, n_rows, k: run_copy(slot, s, n_rows, k).start())

    def drain(tile, buf_slot):
        def make_wait(n):
            return pltpu.make_async_copy(loc.at[buf_slot, pl.ds(0, n * ROW_CHUNKS)],
                                         xs_ref.at[pl.ds(0, n * ROW_CHUNKS)], sem.at[buf_slot])
        _wait_rows(trow_ref[tile], make_wait)

    @pl.when(i > 0)
    def _():
        drain(i - 1, 1 - slot)

    @pl.when(i == n_steps - 1)
    def _():
        drain(i, slot)


def _clamped_maps(np_steps):
    pmap = lambda i, *_: (jnp.minimum(i, np_steps - 1), 0)
    smap = lambda i, *_: (jnp.maximum(i - np_steps, 0), 0)
    return pmap, smap


def _dispatch_call(tables, hn_p, hn_s, rt_p, rt_s, n_sorted_rows):
    n_p, n_s = hn_p.shape[0], hn_s.shape[0]
    np_steps = n_p // TOK_TILE
    n_steps = np_steps + n_s // TOK_TILE
    pmap, smap = _clamped_maps(np_steps)
    out_rows = (n_sorted_rows + 8) * ROW_CHUNKS
    return pl.pallas_call(
        functools.partial(_dispatch_kernel, n_prompt_steps=np_steps),
        grid_spec=pltpu.PrefetchScalarGridSpec(
            num_scalar_prefetch=len(tables), grid=(n_steps,),
            in_specs=[pl.BlockSpec((TOK_TILE, D_MODEL), pmap),
                      pl.BlockSpec((TOK_TILE, D_MODEL), smap),
                      pl.BlockSpec((TOK_TILE, SUBLANES), pmap),
                      pl.BlockSpec((TOK_TILE, SUBLANES), smap)],
            out_specs=pl.BlockSpec(memory_space=pl.ANY),
            scratch_shapes=[pltpu.VMEM((2, LOC_ROWS * ROW_CHUNKS, LANES), U32),
                            pltpu.VMEM((MOE_TILE * ROW_CHUNKS, LANES), U32),
                            pltpu.SemaphoreType.DMA((2,)),
                            pltpu.SemaphoreType.DMA(())]),
        out_shape=jax.ShapeDtypeStruct((out_rows, LANES), U32),
        compiler_params=pltpu.CompilerParams(
            dimension_semantics=("arbitrary",), vmem_limit_bytes=VMEM_LIMIT),
        name="dispatch",
    )(*tables, hn_p, hn_s, rt_p, rt_s)


def _moe_kernel(te_ref, nu_ref, par_ref, nxt_ref, ngrp_ref, xs_ref, wg_hbm, wu_hbm, wd_hbm, ys_ref,
                wg_f32, wu_f32, wd_f32, wg_bf, wu_bf, wd_bf, wsem):
    i = pl.program_id(0)
    used = i < nu_ref[0]

    def store_zeros(ref, n_rows):
        ij = (lax.broadcasted_iota(I32, (n_rows, D_MODEL), 0)
              + lax.broadcasted_iota(I32, (n_rows, D_MODEL), 1))
        _store_packed(ref, jnp.where(ij < jnp.minimum(nu_ref[0], 0), 1.0, 0.0), n_rows)

    def weight_copies(expert, buf):
        return [pltpu.make_async_copy(hbm.at[expert], vmem.at[buf], wsem.at[buf])
                for hbm, vmem in ((wg_hbm, wg_f32), (wu_hbm, wu_f32), (wd_hbm, wd_f32))]

    @pl.when(used)
    def _():
        prev = te_ref[jnp.maximum(i - 1, 0)]

        @pl.when((i == 0) | (te_ref[i] != prev))
        def _():
            buf = par_ref[i]

            @pl.when(i == 0)
            def _():
                for cp in weight_copies(te_ref[0], 0):
                    cp.start()

            for cp in weight_copies(te_ref[i], buf):
                cp.wait()
            wg_bf[...] = wg_f32[buf].astype(BF16)
            wu_bf[...] = wu_f32[buf].astype(BF16)
            wd_bf[...] = wd_f32[buf].astype(BF16)

            @pl.when(nxt_ref[i] >= 0)
            def _():
                for cp in weight_copies(nxt_ref[i], 1 - buf):
                    cp.start()

        def expert_rows(n_rows):
            lo, hi = _load_packed(xs_ref, n_rows)
            x = jnp.concatenate([lo, hi], axis=-1).astype(BF16)
            g = jnp.dot(x, wg_bf[...], preferred_element_type=F32)
            u = jnp.dot(x, wu_bf[...], preferred_element_type=F32)
            hh = (g * jax.nn.sigmoid(g)) * u
            y = jnp.dot(hh.astype(BF16), wd_bf[...], preferred_element_type=F32)
            _store_packed(ys_ref, y, n_rows)

        for n_groups in range(1, MOE_TILE // ROW_GROUP + 1):
            @pl.when(ngrp_ref[i] == n_groups)
            def _(n_rows=n_groups * ROW_GROUP):
                expert_rows(n_rows)
                if n_rows < MOE_TILE:
                    rest = MOE_TILE - n_rows
                    store_zeros(ys_ref.at[pl.ds(n_rows * ROW_CHUNKS, rest * ROW_CHUNKS)], rest)

    @pl.when(jnp.logical_not(used))
    def _():
        store_zeros(ys_ref, MOE_TILE)


def _moe_call(tile_expert, n_used, buf_parity, next_expert, n_groups_used, xs, n_tiles, w_eg, w_eu, w_ed):
    blk = MOE_TILE * ROW_CHUNKS
    any_spec = pl.BlockSpec(memory_space=pl.ANY)
    return pl.pallas_call(
        _moe_kernel,
        grid_spec=pltpu.PrefetchScalarGridSpec(
            num_scalar_prefetch=5, grid=(n_tiles,),
            in_specs=[
                pl.BlockSpec((blk, LANES), lambda i, te, nu, *_: (jnp.minimum(i, nu[0] - 1), 0)),
                any_spec, any_spec, any_spec,
            ],
            out_specs=pl.BlockSpec((blk, LANES), lambda i, *_: (i, 0)),
            scratch_shapes=[pltpu.VMEM((2, D_MODEL, D_EXPERT), F32),
                            pltpu.VMEM((2, D_MODEL, D_EXPERT), F32),
                            pltpu.VMEM((2, D_EXPERT, D_MODEL), F32),
                            pltpu.VMEM((D_MODEL, D_EXPERT), BF16),
                            pltpu.VMEM((D_MODEL, D_EXPERT), BF16),
                            pltpu.VMEM((D_EXPERT, D_MODEL), BF16),
                            pltpu.SemaphoreType.DMA((2,))]),
        out_shape=jax.ShapeDtypeStruct((n_tiles * blk, LANES), U32),
        compiler_params=pltpu.CompilerParams(
            dimension_semantics=("arbitrary",), vmem_limit_bytes=VMEM_LIMIT),
        name="experts",
    )(tile_expert, n_used, buf_parity, next_expert, n_groups_used, xs, w_eg, w_eu, w_ed)


def _final_kernel(l8, g8, l4, g4, l2, g2, cnt_ref, trow_ref,
                  h_ref, rt_ref, p_ref, ys_ref, wple_ref, wgate_ref, gple_ref, gfin_ref, out_ref,
                  ybuf, yb_s, perm_s, sem, *, tile_base, n_tiles):
    s = pl.program_id(0)
    a_slot = s % 2
    b_slot = 1 - a_slot
    local_tabs, sorted_tabs = (l8, l4, l2), (g8, g4, g2)

    def run_copy(buf_slot, sz, n_rows, k):
        src = ys_ref.at[_rows(sorted_tabs[sz][k], n_rows)]
        dst = ybuf.at[buf_slot, _rows(local_tabs[sz][k], n_rows)]
        return pltpu.make_async_copy(src, dst, sem.at[buf_slot])

    def issue(tile, buf_slot, active):
        _for_each_copy(tile_base + jnp.minimum(tile, n_tiles - 1), cnt_ref,
                       lambda sz, n_rows, k: run_copy(buf_slot, sz, n_rows, k).start(), active)

    @pl.when(s == 0)
    def _():
        ybuf[...] = jnp.zeros(ybuf.shape, U32)
        yb_s[...] = jnp.zeros(yb_s.shape, BF16)
        perm_s[...] = jnp.zeros(perm_s.shape, BF16)
        issue(0, 0, True)

    issue(s + 1, b_slot, s + 1 < n_tiles)

    def make_wait(n):
        return pltpu.make_async_copy(ys_ref.at[pl.ds(0, n * ROW_CHUNKS)],
                                     ybuf.at[a_slot, pl.ds(0, n * ROW_CHUNKS)], sem.at[a_slot])
    a_rows = jnp.where(s < n_tiles, trow_ref[tile_base + jnp.minimum(s, n_tiles - 1)], 0)
    _wait_rows(a_rows, make_wait)

    h = h_ref[...]
    h = h + jnp.dot(perm_s[b_slot], yb_s[b_slot], preferred_element_type=F32)
    r = _rms(h, gple_ref[...])
    gate = jax.nn.sigmoid(jnp.dot(r.astype(BF16), wgate_ref[...], preferred_element_type=F32))
    h = h + jnp.dot(p_ref[...].astype(BF16), wple_ref[...], preferred_element_type=F32) * gate
    out_ref[...] = _rms(h, gfin_ref[...])

    rt = rt_ref[...]
    lpos1, lpos2 = rt[:, 2:3], rt[:, 3:4]
    lane = lax.broadcasted_iota(I32, (TOK_TILE, LOC_ROWS), 1).astype(F32)
    perm_s[a_slot] = jnp.where((lane == lpos1) | (lane == lpos2), 1.0, 0.0).astype(BF16)
    row1, row2, w1_row, w2_row = _as_rows(lpos1, lpos2, rt[:, 4:5], rt[:, 5:6])
    grow = lax.broadcasted_iota(I32, (LOC_ROWS, TOK_TILE), 0).astype(F32)
    wrow = jnp.sum(jnp.where(grow == row1, w1_row, 0.0) + jnp.where(grow == row2, w2_row, 0.0),
                   axis=-1, keepdims=True)
    lo, hi = _load_packed(ybuf.at[a_slot], LOC_ROWS)
    yb_s[a_slot] = (jnp.concatenate([lo, hi], axis=-1) * wrow).astype(BF16)


def _final_call(tables, h, rt, p, ys, wple, wgate, gple, gfin, *, tile_base):
    n_rows = h.shape[0]
    n_tiles = n_rows // TOK_TILE
    b_map = lambda s, *_: (jnp.maximum(s - 1, 0), 0)
    a_map = lambda s, *_: (jnp.minimum(s, n_tiles - 1), 0)
    cmap = lambda s, *_: (0, 0)
    return pl.pallas_call(
        functools.partial(_final_kernel, tile_base=tile_base, n_tiles=n_tiles),
        grid_spec=pltpu.PrefetchScalarGridSpec(
            num_scalar_prefetch=len(tables), grid=(n_tiles + 1,),
            in_specs=[
                pl.BlockSpec((TOK_TILE, D_MODEL), b_map),
                pl.BlockSpec((TOK_TILE, SUBLANES), a_map),
                pl.BlockSpec((TOK_TILE, PLE_DIM), b_map),
                pl.BlockSpec(memory_space=pl.ANY),
                pl.BlockSpec(wple.shape, cmap),
                pl.BlockSpec(wgate.shape, cmap),
                pl.BlockSpec(gple.shape, cmap),
                pl.BlockSpec(gfin.shape, cmap),
            ],
            out_specs=pl.BlockSpec((TOK_TILE, D_MODEL), b_map),
            scratch_shapes=[pltpu.VMEM((2, LOC_ROWS * ROW_CHUNKS, LANES), U32),
                            pltpu.VMEM((2, LOC_ROWS, D_MODEL), BF16),
                            pltpu.VMEM((2, TOK_TILE, LOC_ROWS), BF16),
                            pltpu.SemaphoreType.DMA((2,))]),
        out_shape=jax.ShapeDtypeStruct((n_rows, D_MODEL), F32),
        compiler_params=pltpu.CompilerParams(
            dimension_semantics=("arbitrary",), vmem_limit_bytes=VMEM_LIMIT),
        name="final",
    )(*tables, h, rt, p, ys, wple, wgate, gple, gfin)


def _ceil_to(x, m):
    return ((x + m - 1) // m) * m


def _excl_cumsum(a, axis):
    return jnp.cumsum(a, axis=axis) - a


def _copy_tables(n_copies, first_local, first_sorted, step, cap):
    ends = jnp.cumsum(n_copies, axis=1)
    j = jnp.arange(cap, dtype=I32)[None, :, None]
    e_of = jnp.minimum(jnp.sum(ends[:, None, :] <= j, axis=-1), N_EXPERTS - 1)
    pick = e_of[:, :, None] == jnp.arange(N_EXPERTS, dtype=I32)[None, None, :]
    take = lambda a: jnp.sum(jnp.where(pick, a[:, None, :], 0), axis=-1)
    c = jnp.arange(cap, dtype=I32)[None, :] - (take(ends) - take(n_copies))
    flat = lambda a: a.reshape(-1).astype(I32)
    return flat(take(first_local) + step * c), flat(take(first_sorted) + step * c), ends[:, -1]


def kernel(x_prompt, x_sample, state_pool, p_prompt, p_sample, g_mix, w_in, w_pool, pool_scale,
           g_v, b_v, w_spatial, b_spatial, w_out, g_ffn, w_rg, b_rg, w_re, b_re, w_eg, w_eu, w_ed,
           w_ple, g_ple, w_ple_gate, g_final):
    batch, seq, _ = x_prompt.shape
    dec_batch, dec_seq, _ = x_sample.shape
    assert dec_seq == SUBLANES and g_mix.shape[0] == 1
    n_p = batch * seq
    n_s = dec_batch * dec_seq
    n_tok = n_p + n_s
    past_len = 16384

    row = lambda a: a.reshape(1, -1).astype(F32)
    wr = jnp.concatenate(
        [w_re[0], w_rg[0], jnp.zeros((D_MODEL, LANES - N_EXPERTS - N_EXPERT_GROUPS), F32)], axis=1)
    br = jnp.concatenate(
        [b_re[0], b_rg[0], jnp.zeros((LANES - N_EXPERTS - N_EXPERT_GROUPS,), F32)]).reshape(1, LANES)
    bsp_p = jnp.repeat(jnp.transpose(b_spatial[0]), GMLP_HEAD, axis=1)
    bsp_s = jnp.tile(bsp_p[:dec_seq], (CHUNK // dec_seq, 1))
    wsp_p = w_spatial[0].astype(BF16)
    wsp_s = jnp.tile(w_spatial[0][:, :dec_seq, :dec_seq].astype(BF16),
                     (1, CHUNK // dec_seq, CHUNK // dec_seq))

    def pool_band(first_chunk):
        t = jnp.arange(CHUNK, dtype=I32)[:, None]
        k = jnp.arange(POOL_SPAN, dtype=I32)[None, :]
        mats = []
        for w in POOL_WINDOWS:
            cnt = jnp.minimum(w, t + 1) if first_chunk else jnp.full_like(t, w)
            in_window = (k >= t + 17 - w) & (k <= t + 16)
            mats.append(jnp.where(in_window, 1.0 / cnt.astype(F32), 0.0) - (k == t + 16).astype(F32))
        return jnp.stack(mats).astype(BF16)

    def mixer_weights(wsp, bsp):
        return (row(g_mix[0]), w_in[0].astype(BF16), w_pool[0].astype(BF16), row(pool_scale[0]),
                row(g_v[0]), row(b_v[0]), wsp, bsp, w_out[0].astype(BF16), row(g_ffn[0]),
                wr.astype(BF16), br, pool_band(False), pool_band(True))

    h_p, hn_p, rt_p, cnt_p, hist_p = _mixer_call(
        x_prompt, None, mixer_weights(wsp_p, bsp_p), sample=False, pos0=0)
    h_s, hn_s, rt_s, cnt_s, hist_s, v_s = _mixer_call(
        x_sample.reshape(n_s, D_MODEL), state_pool[0], mixer_weights(wsp_s, bsp_s),
        sample=True, pos0=past_len)

    n_run = jnp.concatenate([cnt_p, cnt_s], axis=0)[:, 0, :N_EXPERTS].astype(I32)
    n_even = _ceil_to(n_run, RUN_ALIGN)
    l_start = _excl_cumsum(n_even, 1)
    tile_rows = jnp.sum(n_even, axis=1)
    base = _excl_cumsum(n_even, 0)
    counts = jnp.sum(n_even, axis=0)
    padded = _ceil_to(counts, MOE_TILE)
    ends = jnp.cumsum(padded)
    offs = ends - padded
    g_start = offs[None, :] + base
    n8 = n_even // 8
    f4 = (n_even // 4) % 2
    f2 = (n_even // 2) % 2
    l8, g8, c8 = _copy_tables(n8, l_start, g_start, 8, COPY_CAP[0])
    l4, g4, c4 = _copy_tables(f4, l_start + 8 * n8, g_start + 8 * n8, 0, COPY_CAP[1])
    l2, g2, c2 = _copy_tables(f2, l_start + 8 * n8 + 4 * f4, g_start + 8 * n8 + 4 * f4, 0, COPY_CAP[2])
    copy_cnt = jnp.stack([c8, c4, c2], axis=1).reshape(-1).astype(I32)
    tile_rows = tile_rows.astype(I32)
    z_start = (offs + counts).astype(I32)
    z_big = ((padded - counts) // ZERO_ROWS).astype(I32)
    z_blocks = (((padded - counts) % ZERO_ROWS + 7) // 8).astype(I32)

    max_rows = 2 * n_tok + n_run.shape[0] * N_EXPERTS * (RUN_ALIGN - 1)
    n_tiles = max_rows // MOE_TILE + N_EXPERTS + 1
    n_used = (ends[-1] // MOE_TILE).astype(I32).reshape(1)
    tile_id = jnp.minimum(jnp.arange(n_tiles, dtype=I32), n_used - 1)
    tile_expert = jnp.sum((tile_id[:, None] * MOE_TILE) >= ends[None, :], axis=1).astype(I32)
    tile_expert = jnp.minimum(tile_expert, N_EXPERTS - 1)
    pick_e = tile_expert[:, None] == jnp.arange(N_EXPERTS, dtype=I32)[None, :]
    data_end = jnp.sum(jnp.where(pick_e, (offs + counts)[None, :], 0), axis=1)
    n_groups_used = jnp.clip((data_end - tile_id * MOE_TILE + ROW_GROUP - 1) // ROW_GROUP,
                             1, MOE_TILE // ROW_GROUP).astype(I32)
    e_ids = jnp.arange(N_EXPERTS, dtype=I32)
    nonempty = padded > 0
    seg_index = jnp.cumsum(nonempty.astype(I32)) - 1
    later = nonempty[None, :] & (e_ids[None, :] > e_ids[:, None])
    next_of = jnp.min(jnp.where(later, e_ids[None, :], N_EXPERTS), axis=1)
    next_of = jnp.where(next_of == N_EXPERTS, -1, next_of)
    buf_parity = (jnp.sum(jnp.where(pick_e, seg_index[None, :], 0), axis=1) % 2).astype(I32)
    next_expert = jnp.sum(jnp.where(pick_e, next_of[None, :], 0), axis=1).astype(I32)

    copy_tabs = (l8, g8, l4, g4, l2, g2, copy_cnt, tile_rows)
    xs = _dispatch_call(copy_tabs + (z_start, z_big, z_blocks, n_used), hn_p, hn_s, rt_p, rt_s,
                        n_tiles * MOE_TILE)
    ys = _moe_call(tile_expert, n_used, buf_parity, next_expert, n_groups_used, xs, n_tiles,
                   w_eg[0], w_eu[0], w_ed[0])
    fin_w = (w_ple[0].astype(BF16), w_ple_gate[0].astype(BF16), row(g_ple[0]), row(g_final))
    y_p = _final_call(copy_tabs, h_p, rt_p, p_prompt[0].reshape(n_p, PLE_DIM), ys, *fin_w, tile_base=0)
    y_s = _final_call(copy_tabs, h_s, rt_s, p_sample[0].reshape(n_s, PLE_DIM), ys, *fin_w,
                      tile_base=n_p // TOK_TILE)

    return (y_p.reshape(batch, seq, D_MODEL),
            y_s.reshape(dec_batch, dec_seq, D_MODEL),
            hist_p[None],
            hist_s[None],
            v_s.reshape(1, dec_batch, dec_seq, GMLP_WIDTH))
```

```python
import functools

import jax
import jax.numpy as jnp
from jax import lax
from jax.experimental import pallas as pl
from jax.experimental.pallas import tpu as pltpu

D_MODEL = 1024
POOL_WIDTH = 512
GMLP_WIDTH = 512
IN_WIDTH = POOL_WIDTH + 2 * GMLP_WIDTH
POOL_WINDOWS = (2, 4, 8, 16)
POOL_GROUP = 128
POOL_HIST = 15
CHUNK = 128
POOL_SPAN = CHUNK + 16
N_GMLP_HEADS = 4
GMLP_HEAD = 128
N_EXPERT_GROUPS = 4
EXPERTS_PER_GROUP = 8
N_EXPERTS = 32
D_EXPERT = 512
PLE_DIM = 256
EPS = 1e-6

LANES = 128
SUBLANES = 8
HALF = D_MODEL // 2
ROW_CHUNKS = HALF // LANES

TOK_TILE = 256
MIX_TILE = 512
MOE_TILE = 640
ROW_GROUP = 128
RUN_ALIGN = 2
COPY_ROWS = (8, 4, 2)
ZERO_ROWS = 64
LOC_ROWS = 2 * TOK_TILE + N_EXPERTS * (RUN_ALIGN - 1)
COPY_CAP = (LOC_ROWS // 8 + 1, N_EXPERTS, N_EXPERTS)
VMEM_LIMIT = 56 * 1024 * 1024

F32 = jnp.float32
BF16 = jnp.bfloat16
I32 = jnp.int32
U32 = jnp.uint32


def _rms(x, g):
    return x * lax.rsqrt(jnp.mean(x * x, axis=-1, keepdims=True) + EPS) * g


def _gelu_tanh(x):
    inner = x * (0.7978845608028654 + (0.7978845608028654 * 0.044715) * (x * x))
    half = 0.5 * x
    return half + half * jnp.tanh(inner)


def _store_packed(ref, val, n_rows):
    packed = pltpu.pack_elementwise([val[:, :HALF], val[:, HALF:]], packed_dtype=BF16)
    for k in range(ROW_CHUNKS):
        ref[pl.ds(k, n_rows, stride=ROW_CHUNKS), :] = packed[:, k * LANES:(k + 1) * LANES]


def _load_packed(ref, n_rows):
    packed = jnp.concatenate(
        [ref[pl.ds(k, n_rows, stride=ROW_CHUNKS), :] for k in range(ROW_CHUNKS)], axis=-1)
    lo = pltpu.unpack_elementwise(packed, index=0, packed_dtype=BF16, unpacked_dtype=F32)
    hi = pltpu.unpack_elementwise(packed, index=1, packed_dtype=BF16, unpacked_dtype=F32)
    return lo, hi


def _rows(start, n):
    return pl.ds(pl.multiple_of(start * ROW_CHUNKS, SUBLANES), n * ROW_CHUNKS)


def _route(hn, wrt_ref, brc_ref, n_tok):
    logits = lax.dot_general(wrt_ref[...], hn, (((1,), (1,)), ((), ())),
                             preferred_element_type=F32) + brc_ref[...]
    neg = jnp.float32(-jnp.inf)
    big = jnp.float32(1e9)
    sub = lax.broadcasted_iota(I32, (SUBLANES, n_tok), 0).astype(F32)
    gl = jnp.where(sub < N_EXPERT_GROUPS, logits[N_EXPERTS:N_EXPERTS + SUBLANES, :], neg)
    gmax = jnp.max(gl, axis=0, keepdims=True)
    gidx = jnp.min(jnp.where(gl == gmax, sub, big), axis=0, keepdims=True)
    gprob = 1.0 / jnp.sum(jnp.exp(gl - gmax), axis=0, keepdims=True)
    assert EXPERTS_PER_GROUP == SUBLANES
    el = logits[0:SUBLANES, :]
    for g in range(1, N_EXPERT_GROUPS):
        el = jnp.where(gidx == g, logits[g * SUBLANES:(g + 1) * SUBLANES, :], el)
    m1 = jnp.max(el, axis=0, keepdims=True)
    o1 = jnp.min(jnp.where(el == m1, sub, big), axis=0, keepdims=True)
    el2 = jnp.where(sub == o1, neg, el)
    m2 = jnp.max(el2, axis=0, keepdims=True)
    o2 = jnp.min(jnp.where(el2 == m2, sub, big), axis=0, keepdims=True)
    i1 = gidx * EXPERTS_PER_GROUP + o1
    i2 = gidx * EXPERTS_PER_GROUP + o2
    t = jnp.exp(m2 - m1)
    w1 = gprob / (1.0 + t)
    w2 = gprob * t / (1.0 + t)

    erow = lax.broadcasted_iota(I32, (LANES, n_tok), 0).astype(F32)
    sel1 = erow == i1
    sel2 = erow == i2
    oh = jnp.where(sel1 | sel2, 1.0, 0.0).astype(BF16)
    r = lax.broadcasted_iota(I32, (n_tok, n_tok), 0)
    c = lax.broadcasted_iota(I32, (n_tok, n_tok), 1)
    earlier = jnp.where(r < c, 1.0, 0.0).astype(BF16)
    before = jnp.dot(oh, earlier, preferred_element_type=F32)
    counts_b = jnp.dot(oh, jnp.ones((n_tok, LANES), BF16), preferred_element_type=F32)
    assert RUN_ALIGN == 2
    c_even = counts_b + (counts_b - 2.0 * jnp.floor(counts_b * 0.5))
    c_hi = jnp.floor(c_even * (1.0 / 256.0))
    er = lax.broadcasted_iota(I32, (LANES, LANES), 0)
    ec = lax.broadcasted_iota(I32, (LANES, LANES), 1)
    lower = jnp.where(ec < er, 1.0, 0.0).astype(BF16)
    pieces = jnp.concatenate([c_hi, c_even - 256.0 * c_hi], axis=1).astype(BF16)
    prefix = jnp.dot(lower, pieces, preferred_element_type=F32)
    lstart = 256.0 * prefix[:, :LANES] + prefix[:, LANES:]
    placed = before + jnp.concatenate([lstart] * (n_tok // LANES), axis=1)
    lpos1 = jnp.sum(jnp.where(sel1, placed, 0.0), axis=0, keepdims=True)
    lpos2 = jnp.sum(jnp.where(sel2, placed, 0.0), axis=0, keepdims=True)

    stacked = jnp.where(erow == 0.0, i1, 0.0)
    for k, val in enumerate((i2, lpos1, lpos2, w1, w2), start=1):
        stacked = jnp.where(erow == float(k), val, stacked)
    return stacked.T[:, :SUBLANES], counts_b.T[0:1, :]


def _mixer_kernel(*refs, n_tok, sample, pos0):
    if sample:
        (x_ref, hist_ref, gmix_ref, win_ref, wpool_ref, pscale_ref, gv_ref, bv_ref, wsp_ref,
         bsp_ref, wout_ref, gffn_ref, wr_ref, br_ref, band_ref, band0_ref,
         h_ref, hn_ref, rt_ref, cnt_ref, hist_out_ref, v_ref, ext_scr) = refs
    else:
        (x_ref, gmix_ref, win_ref, wpool_ref, pscale_ref, gv_ref, bv_ref, wsp_ref,
         bsp_ref, wout_ref, gffn_ref, wr_ref, br_ref, band_ref, band0_ref,
         h_ref, hn_ref, rt_ref, cnt_ref, hist_out_ref, ext_scr) = refs

    x = x_ref[...]
    xn = _rms(x, gmix_ref[...])
    z = jnp.dot(xn.astype(BF16), win_ref[...], preferred_element_type=F32)
    a = z[:, :POOL_WIDTH]
    uv = _gelu_tanh(z[:, POOL_WIDTH:])
    u = uv[:, :GMLP_WIDTH]
    v = uv[:, GMLP_WIDTH:]
    mu = jnp.mean(v, axis=-1, keepdims=True)
    vc = v - mu
    v = vc * lax.rsqrt(jnp.mean(vc * vc, axis=-1, keepdims=True) + EPS) * gv_ref[...] + bv_ref[...]

    pooled = []
    if sample:
        n_seq = n_tok // SUBLANES
        ext_scr[:, 1:1 + POOL_HIST, :] = hist_ref[...]
        ext_scr[:, 1 + POOL_HIST:, :] = a.reshape(n_seq, SUBLANES, POOL_WIDTH)
        hist_out_ref[...] = ext_scr[:, 1 + SUBLANES:, :]
        pos = pos0 + lax.broadcasted_iota(I32, (n_seq, SUBLANES, 1), 1)
        for gi, w in enumerate(POOL_WINDOWS):
            cols = slice(gi * POOL_GROUP, (gi + 1) * POOL_GROUP)
            win = ext_scr[:, 16:16 + SUBLANES, cols]
            for j in range(1, w):
                win = win + ext_scr[:, 16 - j:16 - j + SUBLANES, cols]
            cnt = jnp.minimum(w, pos + 1).astype(F32)
            pooled.append((win / cnt).reshape(n_tok, POOL_GROUP) - a[:, cols])
    else:
        l = pl.program_id(1)

        @pl.when(l == 0)
        def _():
            ext_scr[0:16, :] = jnp.zeros((16, POOL_WIDTH), F32)

        ext_scr[16:16 + n_tok, :] = a
        chunks = []
        for ci in range(n_tok // CHUNK):
            ext_c = ext_scr[ci * CHUNK:ci * CHUNK + POOL_SPAN, :].astype(BF16)
            groups = []
            for gi in range(len(POOL_WINDOWS)):
                band = band_ref[gi]
                if ci == 0:
                    band = jnp.where(l == 0, band0_ref[gi], band)
                groups.append(jnp.dot(band, ext_c[:, gi * POOL_GROUP:(gi + 1) * POOL_GROUP],
                                      preferred_element_type=F32))
            chunks.append(groups)
        pooled = [jnp.concatenate([chunks[ci][gi] for ci in range(n_tok // CHUNK)], axis=0)
                  for gi in range(len(POOL_WINDOWS))]
        tail = ext_scr[n_tok + 1:n_tok + 16, :]
        hist_out_ref[...] = tail
        ext_scr[1:16, :] = tail

    pool_out = jnp.concatenate(
        [jnp.dot(pooled[gi].astype(BF16), wpool_ref[gi], preferred_element_type=F32)
         for gi in range(len(POOL_WINDOWS))], axis=-1) * pscale_ref[...]

    blk = SUBLANES if sample else CHUNK
    tr = lax.broadcasted_iota(I32, (CHUNK, CHUNK), 0)
    sc = lax.broadcasted_iota(I32, (CHUNK, CHUNK), 1)
    mask = (sc <= tr) & ((sc // blk) == (tr // blk))
    wsp = [jnp.where(mask, wsp_ref[hh], jnp.zeros((), BF16)) for hh in range(N_GMLP_HEADS)]
    v_bf = v.astype(BF16)
    mixed_chunks = []
    for ci in range(n_tok // CHUNK):
        rows = slice(ci * CHUNK, (ci + 1) * CHUNK)
        heads = [jnp.dot(wsp[hh], v_bf[rows, hh * GMLP_HEAD:(hh + 1) * GMLP_HEAD],
                         preferred_element_type=F32) for hh in range(N_GMLP_HEADS)]
        mixed_chunks.append(jnp.concatenate(heads, axis=-1) + bsp_ref[...])
    mixed = jnp.concatenate(mixed_chunks, axis=0)
    gmlp_out = u * mixed

    cat = jnp.concatenate([pool_out, gmlp_out], axis=-1)
    h = x + jnp.dot(cat.astype(BF16), wout_ref[...], preferred_element_type=F32)
    hn = _rms(h, gffn_ref[...]).astype(BF16)

    h_ref[...] = h
    hn_ref[...] = hn
    if sample:
        v_ref[...] = v
    for s in range(n_tok // TOK_TILE):
        rows = slice(s * TOK_TILE, (s + 1) * TOK_TILE)
        rt, counts = _route(hn[rows], wr_ref, br_ref, TOK_TILE)
        rt_ref[rows, :] = rt
        cnt_ref[s] = counts


def _full(shape):
    return pl.BlockSpec(shape, lambda *_: (0,) * len(shape))


def _mixer_call(x, hist, weights, *, sample, pos0):
    w_specs = [_full(w.shape) for w in weights]
    n_tok = MIX_TILE
    if sample:
        n_rows = x.shape[0]
        n_seq = n_tok // SUBLANES
        grid = (n_rows // n_tok,)
        tok_map = lambda i: (i, 0)
        tile_map = lambda i: (i, 0, 0)
        in_specs = [pl.BlockSpec((n_tok, D_MODEL), tok_map),
                    pl.BlockSpec((n_seq, POOL_HIST, POOL_WIDTH), lambda i: (i, 0, 0))] + w_specs
        hist_shape = jax.ShapeDtypeStruct(hist.shape, F32)
        hist_spec = pl.BlockSpec((n_seq, POOL_HIST, POOL_WIDTH), lambda i: (i, 0, 0))
        ext = pltpu.VMEM((n_seq, 16 + SUBLANES, POOL_WIDTH), F32)
        args = (x, hist)
    else:
        b, seq, _ = x.shape
        n_rows = b * seq
        n_l = seq // n_tok
        grid = (b, n_l)
        tok_map = lambda bi, li: (bi * n_l + li, 0)
        tile_map = lambda bi, li: (bi * n_l + li, 0, 0)
        in_specs = [pl.BlockSpec((None, n_tok, D_MODEL), lambda bi, li: (bi, li, 0))] + w_specs
        hist_shape = jax.ShapeDtypeStruct((b, POOL_HIST, POOL_WIDTH), F32)
        hist_spec = pl.BlockSpec((None, POOL_HIST, POOL_WIDTH), lambda bi, li: (bi, 0, 0))
        ext = pltpu.VMEM((16 + n_tok, POOL_WIDTH), F32)
        args = (x,)
    out_shape = [jax.ShapeDtypeStruct((n_rows, D_MODEL), F32),
                 jax.ShapeDtypeStruct((n_rows, D_MODEL), BF16),
                 jax.ShapeDtypeStruct((n_rows, SUBLANES), F32),
                 jax.ShapeDtypeStruct((n_rows // TOK_TILE, 1, LANES), F32),
                 hist_shape]
    out_specs = [pl.BlockSpec((n_tok, D_MODEL), tok_map),
                 pl.BlockSpec((n_tok, D_MODEL), tok_map),
                 pl.BlockSpec((n_tok, SUBLANES), tok_map),
                 pl.BlockSpec((n_tok // TOK_TILE, 1, LANES), tile_map),
                 hist_spec]
    if sample:
        out_shape.append(jax.ShapeDtypeStruct((n_rows, GMLP_WIDTH), F32))
        out_specs.append(pl.BlockSpec((n_tok, GMLP_WIDTH), tok_map))
    return pl.pallas_call(
        functools.partial(_mixer_kernel, n_tok=n_tok, sample=sample, pos0=pos0),
        grid=grid, in_specs=in_specs, out_specs=out_specs, out_shape=out_shape,
        scratch_shapes=[ext],
        compiler_params=pltpu.CompilerParams(
            dimension_semantics=("arbitrary",) * len(grid), vmem_limit_bytes=VMEM_LIMIT),
        name="mixer_sample" if sample else "mixer_prompt",
    )(*args, *weights)


def _for_each_copy(tile, cnt_ref, fn, active=True):
    for s, n_rows in enumerate(COPY_ROWS):
        def body(j, carry, s=s, n_rows=n_rows):
            fn(s, n_rows, tile * COPY_CAP[s] + j)
            return carry
        lax.fori_loop(0, jnp.where(active, cnt_ref[tile * len(COPY_ROWS) + s], 0), body, 0)


def _wait_rows(n_rows_total, make_wait):
    units = n_rows_total // RUN_ALIGN
    bit = 0
    while (RUN_ALIGN << bit) <= LOC_ROWS:
        @pl.when((units >> bit) & 1 == 1)
        def _(bit=bit):
            make_wait(RUN_ALIGN << bit).wait()
        bit += 1


def _as_rows(*cols):
    n = cols[0].shape[0]
    lane = lax.broadcasted_iota(I32, (n, LANES), 1)
    packed = jnp.zeros((n, LANES), F32)
    for k, col in enumerate(cols):
        packed = jnp.where(lane == k, col, packed)
    rows = packed.T
    return [rows[k:k + 1, :] for k in range(len(cols))]


def _dispatch_kernel(l8, g8, l4, g4, l2, g2, cnt_ref, trow_ref, zs_ref, zbig_ref, zb_ref, nu_ref,
                     hnp_ref, hns_ref, rtp_ref, rts_ref, xs_ref, loc, zbuf, sem, zsem,
                     *, n_prompt_steps):
    i = pl.program_id(0)
    n_steps = pl.num_programs(0)
    slot = i % 2
    local_tabs, sorted_tabs = (l8, l4, l2), (g8, g4, g2)

    @pl.when(i == 0)
    def _():
        zbuf[...] = jnp.zeros(zbuf.shape, U32)
        n_tiles = xs_ref.shape[0] // (MOE_TILE * ROW_CHUNKS)

        def zero_copy(e, c, n_rows, first):
            return pltpu.make_async_copy(zbuf.at[pl.ds(0, n_rows * ROW_CHUNKS)],
                                         xs_ref.at[_rows(first + c * n_rows, n_rows)], zsem)

        def tail_copy(t):
            return pltpu.make_async_copy(zbuf, xs_ref.at[_rows(t * MOE_TILE, MOE_TILE)], zsem)

        def per_expert(e, totals):
            def big(c, carry):
                zero_copy(e, c, ZERO_ROWS, zs_ref[e]).start()
                return carry

            def small(c, carry):
                zero_copy(e, c, 8, zs_ref[e] + zbig_ref[e] * ZERO_ROWS).start()
                return carry

            lax.fori_loop(0, zbig_ref[e], big, 0)
            lax.fori_loop(0, zb_ref[e], small, 0)
            return totals[0] + zbig_ref[e], totals[1] + zb_ref[e]

        n_big, n_small = lax.fori_loop(0, N_EXPERTS, per_expert, (0, 0))

        def wait_big(_, carry):
            zero_copy(0, 0, ZERO_ROWS, 0).wait()
            return carry

        def wait_small(_, carry):
            zero_copy(0, 0, 8, 0).wait()
            return carry

        lax.fori_loop(0, n_big, wait_big, 0)
        lax.fori_loop(0, n_small, wait_small, 0)

        def tail_start(t, carry):
            tail_copy(t).start()
            return carry

        def tail_wait(t, carry):
            tail_copy(t).wait()
            return carry

        spill = pltpu.make_async_copy(zbuf.at[pl.ds(0, 8 * ROW_CHUNKS)],
                                      xs_ref.at[_rows(n_tiles * MOE_TILE, 8)], zsem)
        lax.fori_loop(nu_ref[0], n_tiles, tail_start, 0)
        spill.start()
        lax.fori_loop(nu_ref[0], n_tiles, tail_wait, 0)
        spill.wait()

    is_prompt = i < n_prompt_steps
    rt = jnp.where(is_prompt, rtp_ref[...], rts_ref[...])
    row1, row2 = _as_rows(rt[:, 2:3], rt[:, 3:4])
    grow = lax.broadcasted_iota(I32, (LOC_ROWS, TOK_TILE), 0).astype(F32)
    perm = jnp.where((grow == row1) | (grow == row2), 1.0, 0.0).astype(BF16)
    hn = jnp.where(is_prompt, hnp_ref[...], hns_ref[...])
    grouped = jnp.dot(perm, hn, preferred_element_type=F32)
    _store_packed(loc.at[slot], grouped, LOC_ROWS)

    def run_copy(buf_slot, s, n_rows, k):
        src = loc.at[buf_slot, _rows(local_tabs[s][k], n_rows)]
        dst = xs_ref.at[_rows(sorted_tabs[s][k], n_rows)]
        return pltpu.make_async_copy(src, dst, sem.at[buf_slot])

    _for_each_copy(i, cnt_ref, lambda s, n_rows, k: run_copy(slot, s, n_rows, k).start())

    def drain(tile, buf_slot):
        def make_wait(n):
            return pltpu.make_async_copy(loc.at[buf_slot, pl.ds(0, n * ROW_CHUNKS)],
                                         xs_ref.at[pl.ds(0, n * ROW_CHUNKS)], sem.at[buf_slot])
        _wait_rows(trow_ref[tile], make_wait)

    @pl.when(i > 0)
    def _():
        drain(i - 1, 1 - slot)

    @pl.when(i == n_steps - 1)
    def _():
        drain(i, slot)


def _clamped_maps(np_steps):
    pmap = lambda i, *_: (jnp.minimum(i, np_steps - 1), 0)
    smap = lambda i, *_: (jnp.maximum(i - np_steps, 0), 0)
    return pmap, smap


def _dispatch_call(tables, hn_p, hn_s, rt_p, rt_s, n_sorted_rows):
    n_p, n_s = hn_p.shape[0], hn_s.shape[0]
    np_steps = n_p // TOK_TILE
    n_steps = np_steps + n_s // TOK_TILE
    pmap, smap = _clamped_maps(np_steps)
    out_rows = (n_sorted_rows + 8) * ROW_CHUNKS
    return pl.pallas_call(
        functools.partial(_dispatch_kernel, n_prompt_steps=np_steps),
        grid_spec=pltpu.PrefetchScalarGridSpec(
            num_scalar_prefetch=len(tables), grid=(n_steps,),
            in_specs=[pl.BlockSpec((TOK_TILE, D_MODEL), pmap),
                      pl.BlockSpec((TOK_TILE, D_MODEL), smap),
                      pl.BlockSpec((TOK_TILE, SUBLANES), pmap),
                      pl.BlockSpec((TOK_TILE, SUBLANES), smap)],
            out_specs=pl.BlockSpec(memory_space=pl.ANY),
            scratch_shapes=[pltpu.VMEM((2, LOC_ROWS * ROW_CHUNKS, LANES), U32),
                            pltpu.VMEM((MOE_TILE * ROW_CHUNKS, LANES), U32),
                            pltpu.SemaphoreType.DMA((2,)),
                            pltpu.SemaphoreType.DMA(())]),
        out_shape=jax.ShapeDtypeStruct((out_rows, LANES), U32),
        compiler_params=pltpu.CompilerParams(
            dimension_semantics=("arbitrary",), vmem_limit_bytes=VMEM_LIMIT),
        name="dispatch",
    )(*tables, hn_p, hn_s, rt_p, rt_s)


def _moe_kernel(te_ref, nu_ref, par_ref, nxt_ref, ngrp_ref, xs_ref, wg_hbm, wu_hbm, wd_hbm, ys_ref,
                wg_f32, wu_f32, wd_f32, wg_bf, wu_bf, wd_bf, wsem):
    i = pl.program_id(0)
    used = i < nu_ref[0]

    def store_zeros(ref, n_rows):
        ij = (lax.broadcasted_iota(I32, (n_rows, D_MODEL), 0)
              + lax.broadcasted_iota(I32, (n_rows, D_MODEL), 1))
        _store_packed(ref, jnp.where(ij < jnp.minimum(nu_ref[0], 0), 1.0, 0.0), n_rows)

    def weight_copies(expert, buf):
        return [pltpu.make_async_copy(hbm.at[expert], vmem.at[buf], wsem.at[buf])
                for hbm, vmem in ((wg_hbm, wg_f32), (wu_hbm, wu_f32), (wd_hbm, wd_f32))]

    @pl.when(used)
    def _():
        prev = te_ref[jnp.maximum(i - 1, 0)]

        @pl.when((i == 0) | (te_ref[i] != prev))
        def _():
            buf = par_ref[i]

            @pl.when(i == 0)
            def _():
                for cp in weight_copies(te_ref[0], 0):
                    cp.start()

            for cp in weight_copies(te_ref[i], buf):
                cp.wait()
            wg_bf[...] = wg_f32[buf].astype(BF16)
            wu_bf[...] = wu_f32[buf].astype(BF16)
            wd_bf[...] = wd_f32[buf].astype(BF16)

            @pl.when(nxt_ref[i] >= 0)
            def _():
                for cp in weight_copies(nxt_ref[i], 1 - buf):
                    cp.start()

        def expert_rows(n_rows):
            lo, hi = _load_packed(xs_ref, n_rows)
            x = jnp.concatenate([lo, hi], axis=-1).astype(BF16)
            g = jnp.dot(x, wg_bf[...], preferred_element_type=F32)
            u = jnp.dot(x, wu_bf[...], preferred_element_type=F32)
            hh = (g * jax.nn.sigmoid(g)) * u
            y = jnp.dot(hh.astype(BF16), wd_bf[...], preferred_element_type=F32)
            _store_packed(ys_ref, y, n_rows)

        for n_groups in range(1, MOE_TILE // ROW_GROUP + 1):
            @pl.when(ngrp_ref[i] == n_groups)
            def _(n_rows=n_groups * ROW_GROUP):
                expert_rows(n_rows)
                if n_rows < MOE_TILE:
                    rest = MOE_TILE - n_rows
                    store_zeros(ys_ref.at[pl.ds(n_rows * ROW_CHUNKS, rest * ROW_CHUNKS)], rest)

    @pl.when(jnp.logical_not(used))
    def _():
        store_zeros(ys_ref, MOE_TILE)


def _moe_call(tile_expert, n_used, buf_parity, next_expert, n_groups_used, xs, n_tiles, w_eg, w_eu, w_ed):
    blk = MOE_TILE * ROW_CHUNKS
    any_spec = pl.BlockSpec(memory_space=pl.ANY)
    return pl.pallas_call(
        _moe_kernel,
        grid_spec=pltpu.PrefetchScalarGridSpec(
            num_scalar_prefetch=5, grid=(n_tiles,),
            in_specs=[
                pl.BlockSpec((blk, LANES), lambda i, te, nu, *_: (jnp.minimum(i, nu[0] - 1), 0)),
                any_spec, any_spec, any_spec,
            ],
            out_specs=pl.BlockSpec((blk, LANES), lambda i, *_: (i, 0)),
            scratch_shapes=[pltpu.VMEM((2, D_MODEL, D_EXPERT), F32),
                            pltpu.VMEM((2, D_MODEL, D_EXPERT), F32),
                            pltpu.VMEM((2, D_EXPERT, D_MODEL), F32),
                            pltpu.VMEM((D_MODEL, D_EXPERT), BF16),
                            pltpu.VMEM((D_MODEL, D_EXPERT), BF16),
                            pltpu.VMEM((D_EXPERT, D_MODEL), BF16),
                            pltpu.SemaphoreType.DMA((2,))]),
        out_shape=jax.ShapeDtypeStruct((n_tiles * blk, LANES), U32),
        compiler_params=pltpu.CompilerParams(
            dimension_semantics=("arbitrary",), vmem_limit_bytes=VMEM_LIMIT),
        name="experts",
    )(tile_expert, n_used, buf_parity, next_expert, n_groups_used, xs, w_eg, w_eu, w_ed)


def _final_kernel(l8, g8, l4, g4, l2, g2, cnt_ref, trow_ref,
                  h_ref, rt_ref, p_ref, ys_ref, wple_ref, wgate_ref, gple_ref, gfin_ref, out_ref,
                  ybuf, yb_s, perm_s, sem, *, tile_base, n_tiles):
    s = pl.program_id(0)
    a_slot = s % 2
    b_slot = 1 - a_slot
    local_tabs, sorted_tabs = (l8, l4, l2), (g8, g4, g2)

    def run_copy(buf_slot, sz, n_rows, k):
        src = ys_ref.at[_rows(sorted_tabs[sz][k], n_rows)]
        dst = ybuf.at[buf_slot, _rows(local_tabs[sz][k], n_rows)]
        return pltpu.make_async_copy(src, dst, sem.at[buf_slot])

    def issue(tile, buf_slot, active):
        _for_each_copy(tile_base + jnp.minimum(tile, n_tiles - 1), cnt_ref,
                       lambda sz, n_rows, k: run_copy(buf_slot, sz, n_rows, k).start(), active)

    @pl.when(s == 0)
    def _():
        ybuf[...] = jnp.zeros(ybuf.shape, U32)
        yb_s[...] = jnp.zeros(yb_s.shape, BF16)
        perm_s[...] = jnp.zeros(perm_s.shape, BF16)
        issue(0, 0, True)

    issue(s + 1, b_slot, s + 1 < n_tiles)

    def make_wait(n):
        return pltpu.make_async_copy(ys_ref.at[pl.ds(0, n * ROW_CHUNKS)],
                                     ybuf.at[a_slot, pl.ds(0, n * ROW_CHUNKS)], sem.at[a_slot])
    a_rows = jnp.where(s < n_tiles, trow_ref[tile_base + jnp.minimum(s, n_tiles - 1)], 0)
    _wait_rows(a_rows, make_wait)

    h = h_ref[...]
    h = h + jnp.dot(perm_s[b_slot], yb_s[b_slot], preferred_element_type=F32)
    r = _rms(h, gple_ref[...])
    gate = jax.nn.sigmoid(jnp.dot(r.astype(BF16), wgate_ref[...], preferred_element_type=F32))
    h = h + jnp.dot(p_ref[...].astype(BF16), wple_ref[...], preferred_element_type=F32) * gate
    out_ref[...] = _rms(h, gfin_ref[...])

    rt = rt_ref[...]
    lpos1, lpos2 = rt[:, 2:3], rt[:, 3:4]
    lane = lax.broadcasted_iota(I32, (TOK_TILE, LOC_ROWS), 1).astype(F32)
    perm_s[a_slot] = jnp.where((lane == lpos1) | (lane == lpos2), 1.0, 0.0).astype(BF16)
    row1, row2, w1_row, w2_row = _as_rows(lpos1, lpos2, rt[:, 4:5], rt[:, 5:6])
    grow = lax.broadcasted_iota(I32, (LOC_ROWS, TOK_TILE), 0).astype(F32)
    wrow = jnp.sum(jnp.where(grow == row1, w1_row, 0.0) + jnp.where(grow == row2, w2_row, 0.0),
                   axis=-1, keepdims=True)
    lo, hi = _load_packed(ybuf.at[a_slot], LOC_ROWS)
    yb_s[a_slot] = (jnp.concatenate([lo, hi], axis=-1) * wrow).astype(BF16)


def _final_call(tables, h, rt, p, ys, wple, wgate, gple, gfin, *, tile_base):
    n_rows = h.shape[0]
    n_tiles = n_rows // TOK_TILE
    b_map = lambda s, *_: (jnp.maximum(s - 1, 0), 0)
    a_map = lambda s, *_: (jnp.minimum(s, n_tiles - 1), 0)
    cmap = lambda s, *_: (0, 0)
    return pl.pallas_call(
        functools.partial(_final_kernel, tile_base=tile_base, n_tiles=n_tiles),
        grid_spec=pltpu.PrefetchScalarGridSpec(
            num_scalar_prefetch=len(tables), grid=(n_tiles + 1,),
            in_specs=[
                pl.BlockSpec((TOK_TILE, D_MODEL), b_map),
                pl.BlockSpec((TOK_TILE, SUBLANES), a_map),
                pl.BlockSpec((TOK_TILE, PLE_DIM), b_map),
                pl.BlockSpec(memory_space=pl.ANY),
                pl.BlockSpec(wple.shape, cmap),
                pl.BlockSpec(wgate.shape, cmap),
                pl.BlockSpec(gple.shape, cmap),
                pl.BlockSpec(gfin.shape, cmap),
            ],
            out_specs=pl.BlockSpec((TOK_TILE, D_MODEL), b_map),
            scratch_shapes=[pltpu.VMEM((2, LOC_ROWS * ROW_CHUNKS, LANES), U32),
                            pltpu.VMEM((2, LOC_ROWS, D_MODEL), BF16),
                            pltpu.VMEM((2, TOK_TILE, LOC_ROWS), BF16),
                            pltpu.SemaphoreType.DMA((2,))]),
        out_shape=jax.ShapeDtypeStruct((n_rows, D_MODEL), F32),
        compiler_params=pltpu.CompilerParams(
            dimension_semantics=("arbitrary",), vmem_limit_bytes=VMEM_LIMIT),
        name="final",
    )(*tables, h, rt, p, ys, wple, wgate, gple, gfin)


def _ceil_to(x, m):
    return ((x + m - 1) // m) * m


def _excl_cumsum(a, axis):
    return jnp.cumsum(a, axis=axis) - a


def _copy_tables(n_copies, first_local, first_sorted, step, cap):
    ends = jnp.cumsum(n_copies, axis=1)
    j = jnp.arange(cap, dtype=I32)[None, :, None]
    e_of = jnp.minimum(jnp.sum(ends[:, None, :] <= j, axis=-1), N_EXPERTS - 1)
    pick = e_of[:, :, None] == jnp.arange(N_EXPERTS, dtype=I32)[None, None, :]
    take = lambda a: jnp.sum(jnp.where(pick, a[:, None, :], 0), axis=-1)
    c = jnp.arange(cap, dtype=I32)[None, :] - (take(ends) - take(n_copies))
    flat = lambda a: a.reshape(-1).astype(I32)
    return flat(take(first_local) + step * c), flat(take(first_sorted) + step * c), ends[:, -1]


def kernel(x_prompt, x_sample, state_pool, p_prompt, p_sample, g_mix, w_in, w_pool, pool_scale,
           g_v, b_v, w_spatial, b_spatial, w_out, g_ffn, w_rg, b_rg, w_re, b_re, w_eg, w_eu, w_ed,
           w_ple, g_ple, w_ple_gate, g_final):
    batch, seq, _ = x_prompt.shape
    dec_batch, dec_seq, _ = x_sample.shape
    assert dec_seq == SUBLANES and g_mix.shape[0] == 1
    n_p = batch * seq
    n_s = dec_batch * dec_seq
    n_tok = n_p + n_s
    past_len = 16384

    row = lambda a: a.reshape(1, -1).astype(F32)
    wr = jnp.concatenate(
        [w_re[0], w_rg[0], jnp.zeros((D_MODEL, LANES - N_EXPERTS - N_EXPERT_GROUPS), F32)], axis=1)
    br = jnp.concatenate(
        [b_re[0], b_rg[0], jnp.zeros((LANES - N_EXPERTS - N_EXPERT_GROUPS,), F32)]).reshape(1, LANES)
    bsp_p = jnp.repeat(jnp.transpose(b_spatial[0]), GMLP_HEAD, axis=1)
    bsp_s = jnp.tile(bsp_p[:dec_seq], (CHUNK // dec_seq, 1))
    wsp_p = w_spatial[0].astype(BF16)
    wsp_s = jnp.tile(w_spatial[0][:, :dec_seq, :dec_seq].astype(BF16),
                     (1, CHUNK // dec_seq, CHUNK // dec_seq))

    def pool_band(first_chunk):
        t = jnp.arange(CHUNK, dtype=I32)[:, None]
        k = jnp.arange(POOL_SPAN, dtype=I32)[None, :]
        mats = []
        for w in POOL_WINDOWS:
            cnt = jnp.minimum(w, t + 1) if first_chunk else jnp.full_like(t, w)
            in_window = (k >= t + 17 - w) & (k <= t + 16)
            mats.append(jnp.where(in_window, 1.0 / cnt.astype(F32), 0.0) - (k == t + 16).astype(F32))
        return jnp.stack(mats).astype(BF16)

    def mixer_weights(wsp, bsp):
        return (row(g_mix[0]), w_in[0].astype(BF16), w_pool[0].astype(BF16), row(pool_scale[0]),
                row(g_v[0]), row(b_v[0]), wsp, bsp, w_out[0].astype(BF16), row(g_ffn[0]),
                jnp.transpose(wr).astype(BF16), br.reshape(LANES, 1), pool_band(False), pool_band(True))

    h_p, hn_p, rt_p, cnt_p, hist_p = _mixer_call(
        x_prompt, None, mixer_weights(wsp_p, bsp_p), sample=False, pos0=0)
    h_s, hn_s, rt_s, cnt_s, hist_s, v_s = _mixer_call(
        x_sample.reshape(n_s, D_MODEL), state_pool[0], mixer_weights(wsp_s, bsp_s),
        sample=True, pos0=past_len)

    n_run = jnp.concatenate([cnt_p, cnt_s], axis=0)[:, 0, :N_EXPERTS].astype(I32)
    n_even = _ceil_to(n_run, RUN_ALIGN)
    l_start = _excl_cumsum(n_even, 1)
    tile_rows = jnp.sum(n_even, axis=1)
    base = _excl_cumsum(n_even, 0)
    counts = jnp.sum(n_even, axis=0)
    padded = _ceil_to(counts, MOE_TILE)
    ends = jnp.cumsum(padded)
    offs = ends - padded
    g_start = offs[None, :] + base
    n8 = n_even // 8
    f4 = (n_even // 4) % 2
    f2 = (n_even // 2) % 2
    l8, g8, c8 = _copy_tables(n8, l_start, g_start, 8, COPY_CAP[0])
    l4, g4, c4 = _copy_tables(f4, l_start + 8 * n8, g_start + 8 * n8, 0, COPY_CAP[1])
    l2, g2, c2 = _copy_tables(f2, l_start + 8 * n8 + 4 * f4, g_start + 8 * n8 + 4 * f4, 0, COPY_CAP[2])
    copy_cnt = jnp.stack([c8, c4, c2], axis=1).reshape(-1).astype(I32)
    tile_rows = tile_rows.astype(I32)
    z_start = (offs + counts).astype(I32)
    z_big = ((padded - counts) // ZERO_ROWS).astype(I32)
    z_blocks = (((padded - counts) % ZERO_ROWS + 7) // 8).astype(I32)

    max_rows = 2 * n_tok + n_run.shape[0] * N_EXPERTS * (RUN_ALIGN - 1)
    n_tiles = max_rows // MOE_TILE + N_EXPERTS + 1
    n_used = (ends[-1] // MOE_TILE).astype(I32).reshape(1)
    tile_id = jnp.minimum(jnp.arange(n_tiles, dtype=I32), n_used - 1)
    tile_expert = jnp.sum((tile_id[:, None] * MOE_TILE) >= ends[None, :], axis=1).astype(I32)
    tile_expert = jnp.minimum(tile_expert, N_EXPERTS - 1)
    pick_e = tile_expert[:, None] == jnp.arange(N_EXPERTS, dtype=I32)[None, :]
    data_end = jnp.sum(jnp.where(pick_e, (offs + counts)[None, :], 0), axis=1)
    n_groups_used = jnp.clip((data_end - tile_id * MOE_TILE + ROW_GROUP - 1) // ROW_GROUP,
                             1, MOE_TILE // ROW_GROUP).astype(I32)
    e_ids = jnp.arange(N_EXPERTS, dtype=I32)
    nonempty = padded > 0
    seg_index = jnp.cumsum(nonempty.astype(I32)) - 1
    later = nonempty[None, :] & (e_ids[None, :] > e_ids[:, None])
    next_of = jnp.min(jnp.where(later, e_ids[None, :], N_EXPERTS), axis=1)
    next_of = jnp.where(next_of == N_EXPERTS, -1, next_of)
    buf_parity = (jnp.sum(jnp.where(pick_e, seg_index[None, :], 0), axis=1) % 2).astype(I32)
    next_expert = jnp.sum(jnp.where(pick_e, next_of[None, :], 0), axis=1).astype(I32)

    copy_tabs = (l8, g8, l4, g4, l2, g2, copy_cnt, tile_rows)
    xs = _dispatch_call(copy_tabs + (z_start, z_big, z_blocks, n_used), hn_p, hn_s, rt_p, rt_s,
                        n_tiles * MOE_TILE)
    ys = _moe_call(tile_expert, n_used, buf_parity, next_expert, n_groups_used, xs, n_tiles,
                   w_eg[0], w_eu[0], w_ed[0])
    fin_w = (w_ple[0].astype(BF16), w_ple_gate[0].astype(BF16), row(g_ple[0]), row(g_final))
    y_p = _final_call(copy_tabs, h_p, rt_p, p_prompt[0].reshape(n_p, PLE_DIM), ys, *fin_w, tile_base=0)
    y_s = _final_call(copy_tabs, h_s, rt_s, p_sample[0].reshape(n_s, PLE_DIM), ys, *fin_w,
                      tile_base=n_p // TOK_TILE)

    return (y_p.reshape(batch, seq, D_MODEL),
            y_s.reshape(dec_batch, dec_seq, D_MODEL),
            hist_p[None],
            hist_s[None],
            v_s.reshape(1, dec_batch, dec_seq, GMLP_WIDTH))
```

```python
import functools

import jax
import jax.numpy as jnp
from jax import lax
from jax.experimental import pallas as pl
from jax.experimental.pallas import tpu as pltpu

D_MODEL = 1024
POOL_WIDTH = 512
GMLP_WIDTH = 512
IN_WIDTH = POOL_WIDTH + 2 * GMLP_WIDTH
POOL_WINDOWS = (2, 4, 8, 16)
POOL_GROUP = 128
POOL_HIST = 15
CHUNK = 128
POOL_SPAN = CHUNK + 16
N_GMLP_HEADS = 4
GMLP_HEAD = 128
N_EXPERT_GROUPS = 4
EXPERTS_PER_GROUP = 8
N_EXPERTS = 32
D_EXPERT = 512
PLE_DIM = 256
EPS = 1e-6

LANES = 128
SUBLANES = 8
HALF = D_MODEL // 2
ROW_CHUNKS = HALF // LANES

TOK_TILE = 256
MIX_TILE = 512
MOE_TILE = 640
ROW_GROUP = 128
RUN_ALIGN = 2
COPY_ROWS = (8, 4, 2)
ZERO_ROWS = 64
LOC_ROWS = 2 * TOK_TILE + N_EXPERTS * (RUN_ALIGN - 1)
COPY_SLOTS = LOC_ROWS // 8 + 1 + 2 * N_EXPERTS
VMEM_LIMIT = 56 * 1024 * 1024

F32 = jnp.float32
BF16 = jnp.bfloat16
I32 = jnp.int32
U32 = jnp.uint32


def _rms(x, g):
    return x * lax.rsqrt(jnp.mean(x * x, axis=-1, keepdims=True) + EPS) * g


def _gelu_tanh(x):
    inner = x * (0.7978845608028654 + (0.7978845608028654 * 0.044715) * (x * x))
    half = 0.5 * x
    return half + half * jnp.tanh(inner)


def _store_packed(ref, val, n_rows):
    packed = pltpu.pack_elementwise([val[:, :HALF], val[:, HALF:]], packed_dtype=BF16)
    for k in range(ROW_CHUNKS):
        ref[pl.ds(k, n_rows, stride=ROW_CHUNKS), :] = packed[:, k * LANES:(k + 1) * LANES]


def _load_packed(ref, n_rows):
    packed = jnp.concatenate(
        [ref[pl.ds(k, n_rows, stride=ROW_CHUNKS), :] for k in range(ROW_CHUNKS)], axis=-1)
    lo = pltpu.unpack_elementwise(packed, index=0, packed_dtype=BF16, unpacked_dtype=F32)
    hi = pltpu.unpack_elementwise(packed, index=1, packed_dtype=BF16, unpacked_dtype=F32)
    return lo, hi


def _rows(start, n):
    return pl.ds(pl.multiple_of(start * ROW_CHUNKS, SUBLANES), n * ROW_CHUNKS)


def _route(hn, wrt_ref, brc_ref, n_tok):
    logits = lax.dot_general(wrt_ref[...], hn, (((1,), (1,)), ((), ())),
                             preferred_element_type=F32) + brc_ref[...]
    neg = jnp.float32(-jnp.inf)
    big = jnp.float32(1e9)
    sub = lax.broadcasted_iota(I32, (SUBLANES, n_tok), 0).astype(F32)
    gl = jnp.where(sub < N_EXPERT_GROUPS, logits[N_EXPERTS:N_EXPERTS + SUBLANES, :], neg)
    gmax = jnp.max(gl, axis=0, keepdims=True)
    gidx = jnp.min(jnp.where(gl == gmax, sub, big), axis=0, keepdims=True)
    gprob = 1.0 / jnp.sum(jnp.exp(gl - gmax), axis=0, keepdims=True)
    assert EXPERTS_PER_GROUP == SUBLANES
    el = logits[0:SUBLANES, :]
    for g in range(1, N_EXPERT_GROUPS):
        el = jnp.where(gidx == g, logits[g * SUBLANES:(g + 1) * SUBLANES, :], el)
    m1 = jnp.max(el, axis=0, keepdims=True)
    o1 = jnp.min(jnp.where(el == m1, sub, big), axis=0, keepdims=True)
    el2 = jnp.where(sub == o1, neg, el)
    m2 = jnp.max(el2, axis=0, keepdims=True)
    o2 = jnp.min(jnp.where(el2 == m2, sub, big), axis=0, keepdims=True)
    i1 = gidx * EXPERTS_PER_GROUP + o1
    i2 = gidx * EXPERTS_PER_GROUP + o2
    t = jnp.exp(m2 - m1)
    w1 = gprob / (1.0 + t)
    w2 = gprob * t / (1.0 + t)

    erow = lax.broadcasted_iota(I32, (LANES, n_tok), 0).astype(F32)
    sel1 = erow == i1
    sel2 = erow == i2
    oh = jnp.where(sel1 | sel2, 1.0, 0.0).astype(BF16)
    r = lax.broadcasted_iota(I32, (n_tok, n_tok), 0)
    c = lax.broadcasted_iota(I32, (n_tok, n_tok), 1)
    earlier = jnp.where(r < c, 1.0, 0.0).astype(BF16)
    before = jnp.dot(oh, earlier, preferred_element_type=F32)
    counts_b = jnp.dot(oh, jnp.ones((n_tok, LANES), BF16), preferred_element_type=F32)
    assert RUN_ALIGN == 2
    c_even = counts_b + (counts_b - 2.0 * jnp.floor(counts_b * 0.5))
    c_hi = jnp.floor(c_even * (1.0 / 256.0))
    er = lax.broadcasted_iota(I32, (LANES, LANES), 0)
    ec = lax.broadcasted_iota(I32, (LANES, LANES), 1)
    lower = jnp.where(ec < er, 1.0, 0.0).astype(BF16)
    pieces = jnp.concatenate([c_hi, c_even - 256.0 * c_hi], axis=1).astype(BF16)
    prefix = jnp.dot(lower, pieces, preferred_element_type=F32)
    lstart = 256.0 * prefix[:, :LANES] + prefix[:, LANES:]
    placed = before + jnp.concatenate([lstart] * (n_tok // LANES), axis=1)
    lpos1 = jnp.sum(jnp.where(sel1, placed, 0.0), axis=0, keepdims=True)
    lpos2 = jnp.sum(jnp.where(sel2, placed, 0.0), axis=0, keepdims=True)

    stacked = jnp.where(erow == 0.0, i1, 0.0)
    for k, val in enumerate((i2, lpos1, lpos2, w1, w2), start=1):
        stacked = jnp.where(erow == float(k), val, stacked)
    return stacked.T[:, :SUBLANES], counts_b.T[0:1, :]


def _mixer_kernel(*refs, n_tok, sample, pos0):
    if sample:
        (x_ref, hist_ref, gmix_ref, win_ref, wpool_ref, pscale_ref, gv_ref, bv_ref, wsp_ref,
         bsp_ref, wout_ref, gffn_ref, wr_ref, br_ref, band_ref, band0_ref,
         h_ref, hn_ref, rt_ref, cnt_ref, hist_out_ref, v_ref, ext_scr) = refs
    else:
        (x_ref, gmix_ref, win_ref, wpool_ref, pscale_ref, gv_ref, bv_ref, wsp_ref,
         bsp_ref, wout_ref, gffn_ref, wr_ref, br_ref, band_ref, band0_ref,
         h_ref, hn_ref, rt_ref, cnt_ref, hist_out_ref, ext_scr) = refs

    x = x_ref[...]
    xn = _rms(x, gmix_ref[...])
    z = jnp.dot(xn.astype(BF16), win_ref[...], preferred_element_type=F32)
    a = z[:, :POOL_WIDTH]
    uv = _gelu_tanh(z[:, POOL_WIDTH:])
    u = uv[:, :GMLP_WIDTH]
    v = uv[:, GMLP_WIDTH:]
    mu = jnp.mean(v, axis=-1, keepdims=True)
    vc = v - mu
    v = vc * lax.rsqrt(jnp.mean(vc * vc, axis=-1, keepdims=True) + EPS) * gv_ref[...] + bv_ref[...]

    pooled = []
    if sample:
        n_seq = n_tok // SUBLANES
        ext_scr[:, 1:1 + POOL_HIST, :] = hist_ref[...]
        ext_scr[:, 1 + POOL_HIST:, :] = a.reshape(n_seq, SUBLANES, POOL_WIDTH)
        hist_out_ref[...] = ext_scr[:, 1 + SUBLANES:, :]
        pos = pos0 + lax.broadcasted_iota(I32, (n_seq, SUBLANES, 1), 1)
        for gi, w in enumerate(POOL_WINDOWS):
            cols = slice(gi * POOL_GROUP, (gi + 1) * POOL_GROUP)
            win = ext_scr[:, 16:16 + SUBLANES, cols]
            for j in range(1, w):
                win = win + ext_scr[:, 16 - j:16 - j + SUBLANES, cols]
            cnt = jnp.minimum(w, pos + 1).astype(F32)
            pooled.append((win / cnt).reshape(n_tok, POOL_GROUP) - a[:, cols])
    else:
        l = pl.program_id(1)

        @pl.when(l == 0)
        def _():
            ext_scr[0:16, :] = jnp.zeros((16, POOL_WIDTH), F32)

        ext_scr[16:16 + n_tok, :] = a
        chunks = []
        for ci in range(n_tok // CHUNK):
            ext_c = ext_scr[ci * CHUNK:ci * CHUNK + POOL_SPAN, :].astype(BF16)
            groups = []
            for gi in range(len(POOL_WINDOWS)):
                band = band_ref[gi]
                if ci == 0:
                    band = jnp.where(l == 0, band0_ref[gi], band)
                groups.append(jnp.dot(band, ext_c[:, gi * POOL_GROUP:(gi + 1) * POOL_GROUP],
                                      preferred_element_type=F32))
            chunks.append(groups)
        pooled = [jnp.concatenate([chunks[ci][gi] for ci in range(n_tok // CHUNK)], axis=0)
                  for gi in range(len(POOL_WINDOWS))]
        tail = ext_scr[n_tok + 1:n_tok + 16, :]
        hist_out_ref[...] = tail
        ext_scr[1:16, :] = tail

    pool_out = jnp.concatenate(
        [jnp.dot(pooled[gi].astype(BF16), wpool_ref[gi], preferred_element_type=F32)
         for gi in range(len(POOL_WINDOWS))], axis=-1) * pscale_ref[...]

    blk = SUBLANES if sample else CHUNK
    tr = lax.broadcasted_iota(I32, (CHUNK, CHUNK), 0)
    sc = lax.broadcasted_iota(I32, (CHUNK, CHUNK), 1)
    mask = (sc <= tr) & ((sc // blk) == (tr // blk))
    wsp = [jnp.where(mask, wsp_ref[hh], jnp.zeros((), BF16)) for hh in range(N_GMLP_HEADS)]
    v_bf = v.astype(BF16)
    mixed_chunks = []
    for ci in range(n_tok // CHUNK):
        rows = slice(ci * CHUNK, (ci + 1) * CHUNK)
        heads = [jnp.dot(wsp[hh], v_bf[rows, hh * GMLP_HEAD:(hh + 1) * GMLP_HEAD],
                         preferred_element_type=F32) for hh in range(N_GMLP_HEADS)]
        mixed_chunks.append(jnp.concatenate(heads, axis=-1) + bsp_ref[...])
    mixed = jnp.concatenate(mixed_chunks, axis=0)
    gmlp_out = u * mixed

    cat = jnp.concatenate([pool_out, gmlp_out], axis=-1)
    h = x + jnp.dot(cat.astype(BF16), wout_ref[...], preferred_element_type=F32)
    hn = _rms(h, gffn_ref[...]).astype(BF16)

    h_ref[...] = h
    hn_ref[...] = hn
    if sample:
        v_ref[...] = v
    for s in range(n_tok // TOK_TILE):
        rows = slice(s * TOK_TILE, (s + 1) * TOK_TILE)
        rt, counts = _route(hn[rows], wr_ref, br_ref, TOK_TILE)
        rt_ref[rows, :] = rt
        cnt_ref[s] = counts


def _full(shape):
    return pl.BlockSpec(shape, lambda *_: (0,) * len(shape))


def _mixer_call(x, hist, weights, *, sample, pos0):
    w_specs = [_full(w.shape) for w in weights]
    n_tok = MIX_TILE
    if sample:
        n_rows = x.shape[0]
        n_seq = n_tok // SUBLANES
        grid = (n_rows // n_tok,)
        tok_map = lambda i: (i, 0)
        tile_map = lambda i: (i, 0, 0)
        in_specs = [pl.BlockSpec((n_tok, D_MODEL), tok_map),
                    pl.BlockSpec((n_seq, POOL_HIST, POOL_WIDTH), lambda i: (i, 0, 0))] + w_specs
        hist_shape = jax.ShapeDtypeStruct(hist.shape, F32)
        hist_spec = pl.BlockSpec((n_seq, POOL_HIST, POOL_WIDTH), lambda i: (i, 0, 0))
        ext = pltpu.VMEM((n_seq, 16 + SUBLANES, POOL_WIDTH), F32)
        args = (x, hist)
    else:
        b, seq, _ = x.shape
        n_rows = b * seq
        n_l = seq // n_tok
        grid = (b, n_l)
        tok_map = lambda bi, li: (bi * n_l + li, 0)
        tile_map = lambda bi, li: (bi * n_l + li, 0, 0)
        in_specs = [pl.BlockSpec((None, n_tok, D_MODEL), lambda bi, li: (bi, li, 0))] + w_specs
        hist_shape = jax.ShapeDtypeStruct((b, POOL_HIST, POOL_WIDTH), F32)
        hist_spec = pl.BlockSpec((None, POOL_HIST, POOL_WIDTH), lambda bi, li: (bi, 0, 0))
        ext = pltpu.VMEM((16 + n_tok, POOL_WIDTH), F32)
        args = (x,)
    out_shape = [jax.ShapeDtypeStruct((n_rows, D_MODEL), F32),
                 jax.ShapeDtypeStruct((n_rows, D_MODEL), BF16),
                 jax.ShapeDtypeStruct((n_rows, SUBLANES), F32),
                 jax.ShapeDtypeStruct((n_rows // TOK_TILE, 1, LANES), F32),
                 hist_shape]
    out_specs = [pl.BlockSpec((n_tok, D_MODEL), tok_map),
                 pl.BlockSpec((n_tok, D_MODEL), tok_map),
                 pl.BlockSpec((n_tok, SUBLANES), tok_map),
                 pl.BlockSpec((n_tok // TOK_TILE, 1, LANES), tile_map),
                 hist_spec]
    if sample:
        out_shape.append(jax.ShapeDtypeStruct((n_rows, GMLP_WIDTH), F32))
        out_specs.append(pl.BlockSpec((n_tok, GMLP_WIDTH), tok_map))
    return pl.pallas_call(
        functools.partial(_mixer_kernel, n_tok=n_tok, sample=sample, pos0=pos0),
        grid=grid, in_specs=in_specs, out_specs=out_specs, out_shape=out_shape,
        scratch_shapes=[ext],
        compiler_params=pltpu.CompilerParams(
            dimension_semantics=("arbitrary",) * len(grid), vmem_limit_bytes=VMEM_LIMIT),
        name="mixer_sample" if sample else "mixer_prompt",
    )(*args, *weights)


def _for_each_copy(tile, cnt_ref, fn, active=True):
    start = 0
    for s, n_rows in enumerate(COPY_ROWS):
        def body(j, carry, n_rows=n_rows):
            fn(n_rows, tile * COPY_SLOTS + j)
            return carry
        end = jnp.where(active, cnt_ref[tile * len(COPY_ROWS) + s], 0)
        lax.fori_loop(start, end, body, 0)
        start = end


def _wait_rows(n_rows_total, make_wait):
    units = n_rows_total // RUN_ALIGN
    bit = 0
    while (RUN_ALIGN << bit) <= LOC_ROWS:
        @pl.when((units >> bit) & 1 == 1)
        def _(bit=bit):
            make_wait(RUN_ALIGN << bit).wait()
        bit += 1


def _as_rows(*cols):
    n = cols[0].shape[0]
    lane = lax.broadcasted_iota(I32, (n, LANES), 1)
    packed = jnp.zeros((n, LANES), F32)
    for k, col in enumerate(cols):
        packed = jnp.where(lane == k, col, packed)
    rows = packed.T
    return [rows[k:k + 1, :] for k in range(len(cols))]


def _dispatch_kernel(ltab, gtab, cnt_ref, trow_ref, zs_ref, zbig_ref, zb_ref, nu_ref,
                     hnp_ref, hns_ref, rtp_ref, rts_ref, xs_ref, loc, zbuf, sem, zsem,
                     *, n_prompt_steps):
    i = pl.program_id(0)
    n_steps = pl.num_programs(0)
    slot = i % 2

    @pl.when(i == 0)
    def _():
        zbuf[...] = jnp.zeros(zbuf.shape, U32)
        n_tiles = xs_ref.shape[0] // (MOE_TILE * ROW_CHUNKS)

        def zero_copy(e, c, n_rows, first):
            return pltpu.make_async_copy(zbuf.at[pl.ds(0, n_rows * ROW_CHUNKS)],
                                         xs_ref.at[_rows(first + c * n_rows, n_rows)], zsem)

        def tail_copy(t):
            return pltpu.make_async_copy(zbuf, xs_ref.at[_rows(t * MOE_TILE, MOE_TILE)], zsem)

        def per_expert(e, totals):
            def big(c, carry):
                zero_copy(e, c, ZERO_ROWS, zs_ref[e]).start()
                return carry

            def small(c, carry):
                zero_copy(e, c, 8, zs_ref[e] + zbig_ref[e] * ZERO_ROWS).start()
                return carry

            lax.fori_loop(0, zbig_ref[e], big, 0)
            lax.fori_loop(0, zb_ref[e], small, 0)
            return totals[0] + zbig_ref[e], totals[1] + zb_ref[e]

        n_big, n_small = lax.fori_loop(0, N_EXPERTS, per_expert, (0, 0))

        def wait_big(_, carry):
            zero_copy(0, 0, ZERO_ROWS, 0).wait()
            return carry

        def wait_small(_, carry):
            zero_copy(0, 0, 8, 0).wait()
            return carry

        lax.fori_loop(0, n_big, wait_big, 0)
        lax.fori_loop(0, n_small, wait_small, 0)

        def tail_start(t, carry):
            tail_copy(t).start()
            return carry

        def tail_wait(t, carry):
            tail_copy(t).wait()
            return carry

        spill = pltpu.make_async_copy(zbuf.at[pl.ds(0, 8 * ROW_CHUNKS)],
                                      xs_ref.at[_rows(n_tiles * MOE_TILE, 8)], zsem)
        lax.fori_loop(nu_ref[0], n_tiles, tail_start, 0)
        spill.start()
        lax.fori_loop(nu_ref[0], n_tiles, tail_wait, 0)
        spill.wait()

    is_prompt = i < n_prompt_steps
    rt = jnp.where(is_prompt, rtp_ref[...], rts_ref[...])
    row1, row2 = _as_rows(rt[:, 2:3], rt[:, 3:4])
    grow = lax.broadcasted_iota(I32, (LOC_ROWS, TOK_TILE), 0).astype(F32)
    perm = jnp.where((grow == row1) | (grow == row2), 1.0, 0.0).astype(BF16)
    hn = jnp.where(is_prompt, hnp_ref[...], hns_ref[...])
    grouped = jnp.dot(perm, hn, preferred_element_type=F32)
    _store_packed(loc.at[slot], grouped, LOC_ROWS)

    def run_copy(buf_slot, n_rows, k):
        src = loc.at[buf_slot, _rows(ltab[k], n_rows)]
        dst = xs_ref.at[_rows(gtab[k], n_rows)]
        return pltpu.make_async_copy(src, dst, sem.at[buf_slot])

    _for_each_copy(i, cnt_ref, lambda n_rows, k: run_copy(slot, n_rows, k).start())

    def drain(tile, buf_slot):
        def make_wait(n):
            return pltpu.make_async_copy(loc.at[buf_slot, pl.ds(0, n * ROW_CHUNKS)],
                                         xs_ref.at[pl.ds(0, n * ROW_CHUNKS)], sem.at[buf_slot])
        _wait_rows(trow_ref[tile], make_wait)

    @pl.when(i > 0)
    def _():
        drain(i - 1, 1 - slot)

    @pl.when(i == n_steps - 1)
    def _():
        drain(i, slot)


def _clamped_maps(np_steps):
    pmap = lambda i, *_: (jnp.minimum(i, np_steps - 1), 0)
    smap = lambda i, *_: (jnp.maximum(i - np_steps, 0), 0)
    return pmap, smap


def _dispatch_call(tables, hn_p, hn_s, rt_p, rt_s, n_sorted_rows):
    n_p, n_s = hn_p.shape[0], hn_s.shape[0]
    np_steps = n_p // TOK_TILE
    n_steps = np_steps + n_s // TOK_TILE
    pmap, smap = _clamped_maps(np_steps)
    out_rows = (n_sorted_rows + 8) * ROW_CHUNKS
    return pl.pallas_call(
        functools.partial(_dispatch_kernel, n_prompt_steps=np_steps),
        grid_spec=pltpu.PrefetchScalarGridSpec(
            num_scalar_prefetch=len(tables), grid=(n_steps,),
            in_specs=[pl.BlockSpec((TOK_TILE, D_MODEL), pmap),
                      pl.BlockSpec((TOK_TILE, D_MODEL), smap),
                      pl.BlockSpec((TOK_TILE, SUBLANES), pmap),
                      pl.BlockSpec((TOK_TILE, SUBLANES), smap)],
            out_specs=pl.BlockSpec(memory_space=pl.ANY),
            scratch_shapes=[pltpu.VMEM((2, LOC_ROWS * ROW_CHUNKS, LANES), U32),
                            pltpu.VMEM((MOE_TILE * ROW_CHUNKS, LANES), U32),
                            pltpu.SemaphoreType.DMA((2,)),
                            pltpu.SemaphoreType.DMA(())]),
        out_shape=jax.ShapeDtypeStruct((out_rows, LANES), U32),
        compiler_params=pltpu.CompilerParams(
            dimension_semantics=("arbitrary",), vmem_limit_bytes=VMEM_LIMIT),
        name="dispatch",
    )(*tables, hn_p, hn_s, rt_p, rt_s)


def _moe_kernel(te_ref, nu_ref, par_ref, nxt_ref, ngrp_ref, xs_ref, wg_hbm, wu_hbm, wd_hbm, ys_ref,
                wg_f32, wu_f32, wd_f32, wg_bf, wu_bf, wd_bf, wsem):
    i = pl.program_id(0)
    used = i < nu_ref[0]

    def store_zeros(ref, n_rows):
        ij = (lax.broadcasted_iota(I32, (n_rows, D_MODEL), 0)
              + lax.broadcasted_iota(I32, (n_rows, D_MODEL), 1))
        _store_packed(ref, jnp.where(ij < jnp.minimum(nu_ref[0], 0), 1.0, 0.0), n_rows)

    def weight_copies(expert, buf):
        return [pltpu.make_async_copy(hbm.at[expert], vmem.at[buf], wsem.at[buf])
                for hbm, vmem in ((wg_hbm, wg_f32), (wu_hbm, wu_f32), (wd_hbm, wd_f32))]

    @pl.when(used)
    def _():
        prev = te_ref[jnp.maximum(i - 1, 0)]

        @pl.when((i == 0) | (te_ref[i] != prev))
        def _():
            buf = par_ref[i]

            @pl.when(i == 0)
            def _():
                for cp in weight_copies(te_ref[0], 0):
                    cp.start()

            for cp in weight_copies(te_ref[i], buf):
                cp.wait()
            wg_bf[...] = wg_f32[buf].astype(BF16)
            wu_bf[...] = wu_f32[buf].astype(BF16)
            wd_bf[...] = wd_f32[buf].astype(BF16)

            @pl.when(nxt_ref[i] >= 0)
            def _():
                for cp in weight_copies(nxt_ref[i], 1 - buf):
                    cp.start()

        def expert_rows(n_rows):
            lo, hi = _load_packed(xs_ref, n_rows)
            x = jnp.concatenate([lo, hi], axis=-1).astype(BF16)
            g = jnp.dot(x, wg_bf[...], preferred_element_type=F32)
            u = jnp.dot(x, wu_bf[...], preferred_element_type=F32)
            hh = (g * jax.nn.sigmoid(g)) * u
            y = jnp.dot(hh.astype(BF16), wd_bf[...], preferred_element_type=F32)
            _store_packed(ys_ref, y, n_rows)

        for n_groups in range(1, MOE_TILE // ROW_GROUP + 1):
            @pl.when(ngrp_ref[i] == n_groups)
            def _(n_rows=n_groups * ROW_GROUP):
                expert_rows(n_rows)
                if n_rows < MOE_TILE:
                    rest = MOE_TILE - n_rows
                    store_zeros(ys_ref.at[pl.ds(n_rows * ROW_CHUNKS, rest * ROW_CHUNKS)], rest)

    @pl.when(jnp.logical_not(used))
    def _():
        store_zeros(ys_ref, MOE_TILE)


def _moe_call(tile_expert, n_used, buf_parity, next_expert, n_groups_used, xs, n_tiles, w_eg, w_eu, w_ed):
    blk = MOE_TILE * ROW_CHUNKS
    any_spec = pl.BlockSpec(memory_space=pl.ANY)
    return pl.pallas_call(
        _moe_kernel,
        grid_spec=pltpu.PrefetchScalarGridSpec(
            num_scalar_prefetch=5, grid=(n_tiles,),
            in_specs=[
                pl.BlockSpec((blk, LANES), lambda i, te, nu, *_: (jnp.minimum(i, nu[0] - 1), 0)),
                any_spec, any_spec, any_spec,
            ],
            out_specs=pl.BlockSpec((blk, LANES), lambda i, *_: (i, 0)),
            scratch_shapes=[pltpu.VMEM((2, D_MODEL, D_EXPERT), F32),
                            pltpu.VMEM((2, D_MODEL, D_EXPERT), F32),
                            pltpu.VMEM((2, D_EXPERT, D_MODEL), F32),
                            pltpu.VMEM((D_MODEL, D_EXPERT), BF16),
                            pltpu.VMEM((D_MODEL, D_EXPERT), BF16),
                            pltpu.VMEM((D_EXPERT, D_MODEL), BF16),
                            pltpu.SemaphoreType.DMA((2,))]),
        out_shape=jax.ShapeDtypeStruct((n_tiles * blk, LANES), U32),
        compiler_params=pltpu.CompilerParams(
            dimension_semantics=("arbitrary",), vmem_limit_bytes=VMEM_LIMIT),
        name="experts",
    )(tile_expert, n_used, buf_parity, next_expert, n_groups_used, xs, w_eg, w_eu, w_ed)


def _final_kernel(ltab, gtab, cnt_ref, trow_ref,
                  h_ref, rt_ref, p_ref, ys_ref, wple_ref, wgate_ref, gple_ref, gfin_ref, out_ref,
                  ybuf, yb_s, perm_s, sem, *, tile_base, n_tiles):
    s = pl.program_id(0)
    a_slot = s % 2
    b_slot = 1 - a_slot

    def run_copy(buf_slot, n_rows, k):
        src = ys_ref.at[_rows(gtab[k], n_rows)]
        dst = ybuf.at[buf_slot, _rows(ltab[k], n_rows)]
        return pltpu.make_async_copy(src, dst, sem.at[buf_slot])

    def issue(tile, buf_slot, active):
        _for_each_copy(tile_base + jnp.minimum(tile, n_tiles - 1), cnt_ref,
                       lambda n_rows, k: run_copy(buf_slot, n_rows, k).start(), active)

    @pl.when(s == 0)
    def _():
        ybuf[...] = jnp.zeros(ybuf.shape, U32)
        yb_s[...] = jnp.zeros(yb_s.shape, BF16)
        perm_s[...] = jnp.zeros(perm_s.shape, BF16)
        issue(0, 0, True)

    issue(s + 1, b_slot, s + 1 < n_tiles)

    def make_wait(n):
        return pltpu.make_async_copy(ys_ref.at[pl.ds(0, n * ROW_CHUNKS)],
                                     ybuf.at[a_slot, pl.ds(0, n * ROW_CHUNKS)], sem.at[a_slot])
    a_rows = jnp.where(s < n_tiles, trow_ref[tile_base + jnp.minimum(s, n_tiles - 1)], 0)
    _wait_rows(a_rows, make_wait)

    h = h_ref[...]
    h = h + jnp.dot(perm_s[b_slot], yb_s[b_slot], preferred_element_type=F32)
    r = _rms(h, gple_ref[...])
    gate = jax.nn.sigmoid(jnp.dot(r.astype(BF16), wgate_ref[...], preferred_element_type=F32))
    h = h + jnp.dot(p_ref[...].astype(BF16), wple_ref[...], preferred_element_type=F32) * gate
    out_ref[...] = _rms(h, gfin_ref[...])

    rt = rt_ref[...]
    lpos1, lpos2 = rt[:, 2:3], rt[:, 3:4]
    lane = lax.broadcasted_iota(I32, (TOK_TILE, LOC_ROWS), 1).astype(F32)
    perm_s[a_slot] = jnp.where((lane == lpos1) | (lane == lpos2), 1.0, 0.0).astype(BF16)
    row1, row2, w1_row, w2_row = _as_rows(lpos1, lpos2, rt[:, 4:5], rt[:, 5:6])
    grow = lax.broadcasted_iota(I32, (LOC_ROWS, TOK_TILE), 0).astype(F32)
    wrow = jnp.sum(jnp.where(grow == row1, w1_row, 0.0) + jnp.where(grow == row2, w2_row, 0.0),
                   axis=-1, keepdims=True)
    lo, hi = _load_packed(ybuf.at[a_slot], LOC_ROWS)
    yb_s[a_slot] = (jnp.concatenate([lo, hi], axis=-1) * wrow).astype(BF16)


def _final_call(tables, h, rt, p, ys, wple, wgate, gple, gfin, *, tile_base):
    n_rows = h.shape[0]
    n_tiles = n_rows // TOK_TILE
    b_map = lambda s, *_: (jnp.maximum(s - 1, 0), 0)
    a_map = lambda s, *_: (jnp.minimum(s, n_tiles - 1), 0)
    cmap = lambda s, *_: (0, 0)
    return pl.pallas_call(
        functools.partial(_final_kernel, tile_base=tile_base, n_tiles=n_tiles),
        grid_spec=pltpu.PrefetchScalarGridSpec(
            num_scalar_prefetch=len(tables), grid=(n_tiles + 1,),
            in_specs=[
                pl.BlockSpec((TOK_TILE, D_MODEL), b_map),
                pl.BlockSpec((TOK_TILE, SUBLANES), a_map),
                pl.BlockSpec((TOK_TILE, PLE_DIM), b_map),
                pl.BlockSpec(memory_space=pl.ANY),
                pl.BlockSpec(wple.shape, cmap),
                pl.BlockSpec(wgate.shape, cmap),
                pl.BlockSpec(gple.shape, cmap),
                pl.BlockSpec(gfin.shape, cmap),
            ],
            out_specs=pl.BlockSpec((TOK_TILE, D_MODEL), b_map),
            scratch_shapes=[pltpu.VMEM((2, LOC_ROWS * ROW_CHUNKS, LANES), U32),
                            pltpu.VMEM((2, LOC_ROWS, D_MODEL), BF16),
                            pltpu.VMEM((2, TOK_TILE, LOC_ROWS), BF16),
                            pltpu.SemaphoreType.DMA((2,))]),
        out_shape=jax.ShapeDtypeStruct((n_rows, D_MODEL), F32),
        compiler_params=pltpu.CompilerParams(
            dimension_semantics=("arbitrary",), vmem_limit_bytes=VMEM_LIMIT),
        name="final",
    )(*tables, h, rt, p, ys, wple, wgate, gple, gfin)


def _ceil_to(x, m):
    return ((x + m - 1) // m) * m


def _excl_cumsum(a, axis):
    return jnp.cumsum(a, axis=axis) - a


def _copy_tables(n_even, l_start, g_start):
    n8 = n_even // 8
    f4 = (n_even // 4) % 2
    f2 = (n_even // 2) % 2
    zero = jnp.zeros_like(n8)
    n_copies = jnp.concatenate([n8, f4, f2], axis=1)
    skip = jnp.concatenate([zero, 8 * n8, 8 * n8 + 4 * f4], axis=1)
    step = jnp.concatenate([zero + 8, zero, zero], axis=1)
    first_local = jnp.concatenate([l_start] * 3, axis=1) + skip
    first_sorted = jnp.concatenate([g_start] * 3, axis=1) + skip
    n_virtual = n_copies.shape[1]
    ends = jnp.cumsum(n_copies, axis=1)
    j = jnp.arange(COPY_SLOTS, dtype=I32)[None, :, None]
    v_of = jnp.minimum(jnp.sum(ends[:, None, :] <= j, axis=-1), n_virtual - 1)
    pick = v_of[:, :, None] == jnp.arange(n_virtual, dtype=I32)[None, None, :]
    take = lambda a: jnp.sum(jnp.where(pick, a[:, None, :], 0), axis=-1)
    c = jnp.arange(COPY_SLOTS, dtype=I32)[None, :] - (take(ends) - take(n_copies))
    flat = lambda a: a.reshape(-1).astype(I32)
    size_ends = ends[:, N_EXPERTS - 1::N_EXPERTS]
    return (flat(take(first_local) + take(step) * c), flat(take(first_sorted) + take(step) * c),
            flat(size_ends))


def kernel(x_prompt, x_sample, state_pool, p_prompt, p_sample, g_mix, w_in, w_pool, pool_scale,
           g_v, b_v, w_spatial, b_spatial, w_out, g_ffn, w_rg, b_rg, w_re, b_re, w_eg, w_eu, w_ed,
           w_ple, g_ple, w_ple_gate, g_final):
    batch, seq, _ = x_prompt.shape
    dec_batch, dec_seq, _ = x_sample.shape
    assert dec_seq == SUBLANES and g_mix.shape[0] == 1
    n_p = batch * seq
    n_s = dec_batch * dec_seq
    n_tok = n_p + n_s
    past_len = 16384

    row = lambda a: a.reshape(1, -1).astype(F32)
    wr = jnp.concatenate(
        [w_re[0], w_rg[0], jnp.zeros((D_MODEL, LANES - N_EXPERTS - N_EXPERT_GROUPS), F32)], axis=1)
    br = jnp.concatenate(
        [b_re[0], b_rg[0], jnp.zeros((LANES - N_EXPERTS - N_EXPERT_GROUPS,), F32)]).reshape(1, LANES)
    bsp_p = jnp.repeat(jnp.transpose(b_spatial[0]), GMLP_HEAD, axis=1)
    bsp_s = jnp.tile(bsp_p[:dec_seq], (CHUNK // dec_seq, 1))
    wsp_p = w_spatial[0].astype(BF16)
    wsp_s = jnp.tile(w_spatial[0][:, :dec_seq, :dec_seq].astype(BF16),
                     (1, CHUNK // dec_seq, CHUNK // dec_seq))

    def pool_band(first_chunk):
        t = jnp.arange(CHUNK, dtype=I32)[:, None]
        k = jnp.arange(POOL_SPAN, dtype=I32)[None, :]
        mats = []
        for w in POOL_WINDOWS:
            cnt = jnp.minimum(w, t + 1) if first_chunk else jnp.full_like(t, w)
            in_window = (k >= t + 17 - w) & (k <= t + 16)
            mats.append(jnp.where(in_window, 1.0 / cnt.astype(F32), 0.0) - (k == t + 16).astype(F32))
        return jnp.stack(mats).astype(BF16)

    def mixer_weights(wsp, bsp):
        return (row(g_mix[0]), w_in[0].astype(BF16), w_pool[0].astype(BF16), row(pool_scale[0]),
                row(g_v[0]), row(b_v[0]), wsp, bsp, w_out[0].astype(BF16), row(g_ffn[0]),
                jnp.transpose(wr).astype(BF16), br.reshape(LANES, 1), pool_band(False), pool_band(True))

    h_p, hn_p, rt_p, cnt_p, hist_p = _mixer_call(
        x_prompt, None, mixer_weights(wsp_p, bsp_p), sample=False, pos0=0)
    h_s, hn_s, rt_s, cnt_s, hist_s, v_s = _mixer_call(
        x_sample.reshape(n_s, D_MODEL), state_pool[0], mixer_weights(wsp_s, bsp_s),
        sample=True, pos0=past_len)

    n_run = jnp.concatenate([cnt_p, cnt_s], axis=0)[:, 0, :N_EXPERTS].astype(I32)
    n_even = _ceil_to(n_run, RUN_ALIGN)
    l_start = _excl_cumsum(n_even, 1)
    tile_rows = jnp.sum(n_even, axis=1)
    base = _excl_cumsum(n_even, 0)
    counts = jnp.sum(n_even, axis=0)
    padded = _ceil_to(counts, MOE_TILE)
    ends = jnp.cumsum(padded)
    offs = ends - padded
    g_start = offs[None, :] + base
    ltab, gtab, copy_cnt = _copy_tables(n_even, l_start, g_start)
    tile_rows = tile_rows.astype(I32)
    z_start = (offs + counts).astype(I32)
    z_big = ((padded - counts) // ZERO_ROWS).astype(I32)
    z_blocks = (((padded - counts) % ZERO_ROWS + 7) // 8).astype(I32)

    max_rows = 2 * n_tok + n_run.shape[0] * N_EXPERTS * (RUN_ALIGN - 1)
    n_tiles = max_rows // MOE_TILE + N_EXPERTS + 1
    n_used = (ends[-1] // MOE_TILE).astype(I32).reshape(1)
    tile_id = jnp.minimum(jnp.arange(n_tiles, dtype=I32), n_used - 1)
    tile_expert = jnp.sum((tile_id[:, None] * MOE_TILE) >= ends[None, :], axis=1).astype(I32)
    tile_expert = jnp.minimum(tile_expert, N_EXPERTS - 1)
    pick_e = tile_expert[:, None] == jnp.arange(N_EXPERTS, dtype=I32)[None, :]
    data_end = jnp.sum(jnp.where(pick_e, (offs + counts)[None, :], 0), axis=1)
    n_groups_used = jnp.clip((data_end - tile_id * MOE_TILE + ROW_GROUP - 1) // ROW_GROUP,
                             1, MOE_TILE // ROW_GROUP).astype(I32)
    e_ids = jnp.arange(N_EXPERTS, dtype=I32)
    nonempty = padded > 0
    seg_index = jnp.cumsum(nonempty.astype(I32)) - 1
    later = nonempty[None, :] & (e_ids[None, :] > e_ids[:, None])
    next_of = jnp.min(jnp.where(later, e_ids[None, :], N_EXPERTS), axis=1)
    next_of = jnp.where(next_of == N_EXPERTS, -1, next_of)
    buf_parity = (jnp.sum(jnp.where(pick_e, seg_index[None, :], 0), axis=1) % 2).astype(I32)
    next_expert = jnp.sum(jnp.where(pick_e, next_of[None, :], 0), axis=1).astype(I32)

    copy_tabs = (ltab, gtab, copy_cnt, tile_rows)
    xs = _dispatch_call(copy_tabs + (z_start, z_big, z_blocks, n_used), hn_p, hn_s, rt_p, rt_s,
                        n_tiles * MOE_TILE)
    ys = _moe_call(tile_expert, n_used, buf_parity, next_expert, n_groups_used, xs, n_tiles,
                   w_eg[0], w_eu[0], w_ed[0])
    fin_w = (w_ple[0].astype(BF16), w_ple_gate[0].astype(BF16), row(g_ple[0]), row(g_final))
    y_p = _final_call(copy_tabs, h_p, rt_p, p_prompt[0].reshape(n_p, PLE_DIM), ys, *fin_w, tile_base=0)
    y_s = _final_call(copy_tabs, h_s, rt_s, p_sample[0].reshape(n_s, PLE_DIM), ys, *fin_w,
                      tile_base=n_p // TOK_TILE)

    return (y_p.reshape(batch, seq, D_MODEL),
            y_s.reshape(dec_batch, dec_seq, D_MODEL),
            hist_p[None],
            hist_s[None],
            v_s.reshape(1, dec_batch, dec_seq, GMLP_WIDTH))
```

```python
import functools

import jax
import jax.numpy as jnp
from jax import lax
from jax.experimental import pallas as pl
from jax.experimental.pallas import tpu as pltpu

D_MODEL = 1024
POOL_WIDTH = 512
GMLP_WIDTH = 512
IN_WIDTH = POOL_WIDTH + 2 * GMLP_WIDTH
POOL_WINDOWS = (2, 4, 8, 16)
POOL_GROUP = 128
POOL_HIST = 15
CHUNK = 128
POOL_SPAN = CHUNK + 16
N_GMLP_HEADS = 4
GMLP_HEAD = 128
N_EXPERT_GROUPS = 4
EXPERTS_PER_GROUP = 8
N_EXPERTS = 32
D_EXPERT = 512
PLE_DIM = 256
EPS = 1e-6

LANES = 128
SUBLANES = 8
HALF = D_MODEL // 2
ROW_CHUNKS = HALF // LANES

TOK_TILE = 256
MIX_TILE = 512
MOE_TILE = 640
ROW_GROUP = 128
RUN_ALIGN = 2
COPY_ROWS = (8, 4, 2)
ZERO_ROWS = 64
LOC_ROWS = 2 * TOK_TILE + N_EXPERTS * (RUN_ALIGN - 1)
COPY_CAP = (LOC_ROWS // 8 + 1, N_EXPERTS, N_EXPERTS)
VMEM_LIMIT = 56 * 1024 * 1024

F32 = jnp.float32
BF16 = jnp.bfloat16
I32 = jnp.int32
U32 = jnp.uint32


def _rms(x, g):
    return x * lax.rsqrt(jnp.mean(x * x, axis=-1, keepdims=True) + EPS) * g


def _gelu_tanh(x):
    inner = x * (0.7978845608028654 + (0.7978845608028654 * 0.044715) * (x * x))
    half = 0.5 * x
    return half + half * jnp.tanh(inner)


def _store_packed(ref, val, n_rows):
    packed = pltpu.pack_elementwise([val[:, :HALF], val[:, HALF:]], packed_dtype=BF16)
    for k in range(ROW_CHUNKS):
        ref[pl.ds(k, n_rows, stride=ROW_CHUNKS), :] = packed[:, k * LANES:(k + 1) * LANES]


def _load_packed(ref, n_rows):
    packed = jnp.concatenate(
        [ref[pl.ds(k, n_rows, stride=ROW_CHUNKS), :] for k in range(ROW_CHUNKS)], axis=-1)
    lo = pltpu.unpack_elementwise(packed, index=0, packed_dtype=BF16, unpacked_dtype=F32)
    hi = pltpu.unpack_elementwise(packed, index=1, packed_dtype=BF16, unpacked_dtype=F32)
    return lo, hi


def _rows(start, n):
    return pl.ds(pl.multiple_of(start * ROW_CHUNKS, SUBLANES), n * ROW_CHUNKS)


def _route(hn, wrt_ref, brc_ref, n_tok):
    logits = lax.dot_general(wrt_ref[...], hn, (((1,), (1,)), ((), ())),
                             preferred_element_type=F32) + brc_ref[...]
    neg = jnp.float32(-jnp.inf)
    big = jnp.float32(1e9)
    sub = lax.broadcasted_iota(I32, (SUBLANES, n_tok), 0).astype(F32)
    gl = jnp.where(sub < N_EXPERT_GROUPS, logits[N_EXPERTS:N_EXPERTS + SUBLANES, :], neg)
    gmax = jnp.max(gl, axis=0, keepdims=True)
    gidx = jnp.min(jnp.where(gl == gmax, sub, big), axis=0, keepdims=True)
    gprob = 1.0 / jnp.sum(jnp.exp(gl - gmax), axis=0, keepdims=True)
    assert EXPERTS_PER_GROUP == SUBLANES
    el = logits[0:SUBLANES, :]
    for g in range(1, N_EXPERT_GROUPS):
        el = jnp.where(gidx == g, logits[g * SUBLANES:(g + 1) * SUBLANES, :], el)
    m1 = jnp.max(el, axis=0, keepdims=True)
    o1 = jnp.min(jnp.where(el == m1, sub, big), axis=0, keepdims=True)
    el2 = jnp.where(sub == o1, neg, el)
    m2 = jnp.max(el2, axis=0, keepdims=True)
    o2 = jnp.min(jnp.where(el2 == m2, sub, big), axis=0, keepdims=True)
    i1 = gidx * EXPERTS_PER_GROUP + o1
    i2 = gidx * EXPERTS_PER_GROUP + o2
    t = jnp.exp(m2 - m1)
    w1 = gprob / (1.0 + t)
    w2 = gprob * t / (1.0 + t)

    erow = lax.broadcasted_iota(I32, (LANES, n_tok), 0).astype(F32)
    sel1 = erow == i1
    sel2 = erow == i2
    oh = jnp.where(sel1 | sel2, 1.0, 0.0).astype(BF16)
    r = lax.broadcasted_iota(I32, (n_tok, n_tok), 0)
    c = lax.broadcasted_iota(I32, (n_tok, n_tok), 1)
    earlier = jnp.where(r < c, 1.0, 0.0).astype(BF16)
    before = jnp.dot(oh, earlier, preferred_element_type=F32)
    counts_b = jnp.dot(oh, jnp.ones((n_tok, LANES), BF16), preferred_element_type=F32)
    assert RUN_ALIGN == 2
    c_even = counts_b + (counts_b - 2.0 * jnp.floor(counts_b * 0.5))
    c_hi = jnp.floor(c_even * (1.0 / 256.0))
    er = lax.broadcasted_iota(I32, (LANES, LANES), 0)
    ec = lax.broadcasted_iota(I32, (LANES, LANES), 1)
    lower = jnp.where(ec < er, 1.0, 0.0).astype(BF16)
    pieces = jnp.concatenate([c_hi, c_even - 256.0 * c_hi], axis=1).astype(BF16)
    prefix = jnp.dot(lower, pieces, preferred_element_type=F32)
    lstart = 256.0 * prefix[:, :LANES] + prefix[:, LANES:]
    placed = before + jnp.concatenate([lstart] * (n_tok // LANES), axis=1)
    lpos1 = jnp.sum(jnp.where(sel1, placed, 0.0), axis=0, keepdims=True)
    lpos2 = jnp.sum(jnp.where(sel2, placed, 0.0), axis=0, keepdims=True)

    stacked = jnp.where(erow == 0.0, i1, 0.0)
    for k, val in enumerate((i2, lpos1, lpos2, w1, w2), start=1):
        stacked = jnp.where(erow == float(k), val, stacked)
    return stacked.T[:, :SUBLANES], counts_b.T[0:1, :]


def _mixer_kernel(*refs, n_tok, sample, pos0):
    if sample:
        (x_ref, hist_ref, gmix_ref, win_ref, wpool_ref, pscale_ref, gv_ref, bv_ref, wsp_ref,
         bsp_ref, wout_ref, gffn_ref, wr_ref, br_ref, band_ref, band0_ref,
         h_ref, hn_ref, rt_ref, cnt_ref, hist_out_ref, v_ref, ext_scr, pool_scr) = refs
    else:
        (x_ref, gmix_ref, win_ref, wpool_ref, pscale_ref, gv_ref, bv_ref, wsp_ref,
         bsp_ref, wout_ref, gffn_ref, wr_ref, br_ref, band_ref, band0_ref,
         h_ref, hn_ref, rt_ref, cnt_ref, hist_out_ref, ext_scr) = refs

    x = x_ref[...]
    xn = _rms(x, gmix_ref[...])
    z = jnp.dot(xn.astype(BF16), win_ref[...], preferred_element_type=F32)
    a = z[:, :POOL_WIDTH]
    uv = _gelu_tanh(z[:, POOL_WIDTH:])
    u = uv[:, :GMLP_WIDTH]
    v = uv[:, GMLP_WIDTH:]
    mu = jnp.mean(v, axis=-1, keepdims=True)
    vc = v - mu
    v = vc * lax.rsqrt(jnp.mean(vc * vc, axis=-1, keepdims=True) + EPS) * gv_ref[...] + bv_ref[...]

    pooled = []
    if sample:
        n_seq = n_tok // SUBLANES
        n_grp = len(POOL_WINDOWS)
        gcols = [slice(gi * POOL_GROUP, (gi + 1) * POOL_GROUP) for gi in range(n_grp)]
        ext_scr[1:1 + POOL_HIST] = hist_ref[...]
        for gi in range(n_grp):
            pool_scr[gi] = a[:, gcols[gi]]
        for i in range(SUBLANES):
            ext_scr[1 + POOL_HIST + i] = jnp.concatenate(
                [pool_scr[gi, pl.ds(i, n_seq, stride=SUBLANES), :] for gi in range(n_grp)], axis=-1)
        hist_out_ref[...] = ext_scr[1 + SUBLANES:]
        for i in range(SUBLANES):
            back = [ext_scr[1 + POOL_HIST + i - j] for j in range(max(POOL_WINDOWS))]
            for gi, w in enumerate(POOL_WINDOWS):
                win = back[0][:, gcols[gi]]
                for j in range(1, w):
                    win = win + back[j][:, gcols[gi]]
                pool_scr[gi, pl.ds(i, n_seq, stride=SUBLANES), :] = (
                    win * (1.0 / min(w, pos0 + i + 1)) - back[0][:, gcols[gi]])
        pooled = [pool_scr[gi] for gi in range(n_grp)]
    else:
        l = pl.program_id(1)

        @pl.when(l == 0)
        def _():
            ext_scr[0:16, :] = jnp.zeros((16, POOL_WIDTH), F32)

        ext_scr[16:16 + n_tok, :] = a
        chunks = []
        for ci in range(n_tok // CHUNK):
            ext_c = ext_scr[ci * CHUNK:ci * CHUNK + POOL_SPAN, :].astype(BF16)
            groups = []
            for gi in range(len(POOL_WINDOWS)):
                band = band_ref[gi]
                if ci == 0:
                    band = jnp.where(l == 0, band0_ref[gi], band)
                groups.append(jnp.dot(band, ext_c[:, gi * POOL_GROUP:(gi + 1) * POOL_GROUP],
                                      preferred_element_type=F32))
            chunks.append(groups)
        pooled = [jnp.concatenate([chunks[ci][gi] for ci in range(n_tok // CHUNK)], axis=0)
                  for gi in range(len(POOL_WINDOWS))]
        tail = ext_scr[n_tok + 1:n_tok + 16, :]
        hist_out_ref[...] = tail
        ext_scr[1:16, :] = tail

    pool_out = jnp.concatenate(
        [jnp.dot(pooled[gi].astype(BF16), wpool_ref[gi], preferred_element_type=F32)
         for gi in range(len(POOL_WINDOWS))], axis=-1) * pscale_ref[...]

    blk = SUBLANES if sample else CHUNK
    tr = lax.broadcasted_iota(I32, (CHUNK, CHUNK), 0)
    sc = lax.broadcasted_iota(I32, (CHUNK, CHUNK), 1)
    mask = (sc <= tr) & ((sc // blk) == (tr // blk))
    wsp = [jnp.where(mask, wsp_ref[hh], jnp.zeros((), BF16)) for hh in range(N_GMLP_HEADS)]
    v_bf = v.astype(BF16)
    mixed_chunks = []
    for ci in range(n_tok // CHUNK):
        rows = slice(ci * CHUNK, (ci + 1) * CHUNK)
        heads = [jnp.dot(wsp[hh], v_bf[rows, hh * GMLP_HEAD:(hh + 1) * GMLP_HEAD],
                         preferred_element_type=F32) for hh in range(N_GMLP_HEADS)]
        mixed_chunks.append(jnp.concatenate(heads, axis=-1) + bsp_ref[...])
    mixed = jnp.concatenate(mixed_chunks, axis=0)
    gmlp_out = u * mixed

    cat = jnp.concatenate([pool_out, gmlp_out], axis=-1)
    h = x + jnp.dot(cat.astype(BF16), wout_ref[...], preferred_element_type=F32)
    hn = _rms(h, gffn_ref[...]).astype(BF16)

    h_ref[...] = h
    hn_ref[...] = hn
    if sample:
        v_ref[...] = v
    for s in range(n_tok // TOK_TILE):
        rows = slice(s * TOK_TILE, (s + 1) * TOK_TILE)
        rt, counts = _route(hn[rows], wr_ref, br_ref, TOK_TILE)
        rt_ref[rows, :] = rt
        cnt_ref[s] = counts


def _full(shape):
    return pl.BlockSpec(shape, lambda *_: (0,) * len(shape))


def _mixer_call(x, hist, weights, *, sample, pos0):
    w_specs = [_full(w.shape) for w in weights]
    n_tok = MIX_TILE
    if sample:
        n_rows = x.shape[0]
        n_seq = n_tok // SUBLANES
        grid = (n_rows // n_tok,)
        tok_map = lambda i: (i, 0)
        tile_map = lambda i: (i, 0, 0)
        hist_spec = pl.BlockSpec((POOL_HIST, n_seq, POOL_WIDTH), lambda i: (0, i, 0))
        in_specs = [pl.BlockSpec((n_tok, D_MODEL), tok_map), hist_spec] + w_specs
        hist_shape = jax.ShapeDtypeStruct(hist.shape, F32)
        scratch = [pltpu.VMEM((1 + POOL_HIST + SUBLANES, n_seq, POOL_WIDTH), F32),
                   pltpu.VMEM((len(POOL_WINDOWS), n_tok, POOL_GROUP), F32)]
        args = (x, hist)
    else:
        b, seq, _ = x.shape
        n_rows = b * seq
        n_l = seq // n_tok
        grid = (b, n_l)
        tok_map = lambda bi, li: (bi * n_l + li, 0)
        tile_map = lambda bi, li: (bi * n_l + li, 0, 0)
        in_specs = [pl.BlockSpec((None, n_tok, D_MODEL), lambda bi, li: (bi, li, 0))] + w_specs
        hist_shape = jax.ShapeDtypeStruct((b, POOL_HIST, POOL_WIDTH), F32)
        hist_spec = pl.BlockSpec((None, POOL_HIST, POOL_WIDTH), lambda bi, li: (bi, 0, 0))
        scratch = [pltpu.VMEM((16 + n_tok, POOL_WIDTH), F32)]
        args = (x,)
    out_shape = [jax.ShapeDtypeStruct((n_rows, D_MODEL), F32),
                 jax.ShapeDtypeStruct((n_rows, D_MODEL), BF16),
                 jax.ShapeDtypeStruct((n_rows, SUBLANES), F32),
                 jax.ShapeDtypeStruct((n_rows // TOK_TILE, 1, LANES), F32),
                 hist_shape]
    out_specs = [pl.BlockSpec((n_tok, D_MODEL), tok_map),
                 pl.BlockSpec((n_tok, D_MODEL), tok_map),
                 pl.BlockSpec((n_tok, SUBLANES), tok_map),
                 pl.BlockSpec((n_tok // TOK_TILE, 1, LANES), tile_map),
                 hist_spec]
    if sample:
        out_shape.append(jax.ShapeDtypeStruct((n_rows, GMLP_WIDTH), F32))
        out_specs.append(pl.BlockSpec((n_tok, GMLP_WIDTH), tok_map))
    return pl.pallas_call(
        functools.partial(_mixer_kernel, n_tok=n_tok, sample=sample, pos0=pos0),
        grid=grid, in_specs=in_specs, out_specs=out_specs, out_shape=out_shape,
        scratch_shapes=scratch,
        compiler_params=pltpu.CompilerParams(
            dimension_semantics=("arbitrary",) * len(grid), vmem_limit_bytes=VMEM_LIMIT),
        name="mixer_sample" if sample else "mixer_prompt",
    )(*args, *weights)


def _for_each_copy(tile, cnt_ref, fn, active=True):
    for s, n_rows in enumerate(COPY_ROWS):
        def body(j, carry, s=s, n_rows=n_rows):
            fn(s, n_rows, tile * COPY_CAP[s] + j)
            return carry
        lax.fori_loop(0, jnp.where(active, cnt_ref[tile * len(COPY_ROWS) + s], 0), body, 0)


def _wait_rows(n_rows_total, make_wait):
    units = n_rows_total // RUN_ALIGN
    bit = 0
    while (RUN_ALIGN << bit) <= LOC_ROWS:
        @pl.when((units >> bit) & 1 == 1)
        def _(bit=bit):
            make_wait(RUN_ALIGN << bit).wait()
        bit += 1


def _as_rows(*cols):
    n = cols[0].shape[0]
    lane = lax.broadcasted_iota(I32, (n, LANES), 1)
    packed = jnp.zeros((n, LANES), F32)
    for k, col in enumerate(cols):
        packed = jnp.where(lane == k, col, packed)
    rows = packed.T
    return [rows[k:k + 1, :] for k in range(len(cols))]


def _dispatch_kernel(l8, g8, l4, g4, l2, g2, cnt_ref, trow_ref, zs_ref, zbig_ref, zb_ref, nu_ref,
                     hnp_ref, hns_ref, rtp_ref, rts_ref, xs_ref, loc, zbuf, sem, zsem,
                     *, n_prompt_steps):
    i = pl.program_id(0)
    n_steps = pl.num_programs(0)
    slot = i % 2
    local_tabs, sorted_tabs = (l8, l4, l2), (g8, g4, g2)

    @pl.when(i == 0)
    def _():
        zbuf[...] = jnp.zeros(zbuf.shape, U32)
        n_tiles = xs_ref.shape[0] // (MOE_TILE * ROW_CHUNKS)

        def zero_copy(e, c, n_rows, first):
            return pltpu.make_async_copy(zbuf.at[pl.ds(0, n_rows * ROW_CHUNKS)],
                                         xs_ref.at[_rows(first + c * n_rows, n_rows)], zsem)

        def tail_copy(t):
            return pltpu.make_async_copy(zbuf, xs_ref.at[_rows(t * MOE_TILE, MOE_TILE)], zsem)

        def per_expert(e, totals):
            def big(c, carry):
                zero_copy(e, c, ZERO_ROWS, zs_ref[e]).start()
                return carry

            def small(c, carry):
                zero_copy(e, c, 8, zs_ref[e] + zbig_ref[e] * ZERO_ROWS).start()
                return carry

            lax.fori_loop(0, zbig_ref[e], big, 0)
            lax.fori_loop(0, zb_ref[e], small, 0)
            return totals[0] + zbig_ref[e], totals[1] + zb_ref[e]

        n_big, n_small = lax.fori_loop(0, N_EXPERTS, per_expert, (0, 0))

        def wait_big(_, carry):
            zero_copy(0, 0, ZERO_ROWS, 0).wait()
            return carry

        def wait_small(_, carry):
            zero_copy(0, 0, 8, 0).wait()
            return carry

        lax.fori_loop(0, n_big, wait_big, 0)
        lax.fori_loop(0, n_small, wait_small, 0)

        def tail_start(t, carry):
            tail_copy(t).start()
            return carry

        def tail_wait(t, carry):
            tail_copy(t).wait()
            return carry

        spill = pltpu.make_async_copy(zbuf.at[pl.ds(0, 8 * ROW_CHUNKS)],
                                      xs_ref.at[_rows(n_tiles * MOE_TILE, 8)], zsem)
        lax.fori_loop(nu_ref[0], n_tiles, tail_start, 0)
        spill.start()
        lax.fori_loop(nu_ref[0], n_tiles, tail_wait, 0)
        spill.wait()

    is_prompt = i < n_prompt_steps
    rt = jnp.where(is_prompt, rtp_ref[...], rts_ref[...])
    row1, row2 = _as_rows(rt[:, 2:3], rt[:, 3:4])
    grow = lax.broadcasted_iota(I32, (LOC_ROWS, TOK_TILE), 0).astype(F32)
    perm = jnp.where((grow == row1) | (grow == row2), 1.0, 0.0).astype(BF16)
    hn = jnp.where(is_prompt, hnp_ref[...], hns_ref[...])
    grouped = jnp.dot(perm, hn, preferred_element_type=F32)
    _store_packed(loc.at[slot], grouped, LOC_ROWS)

    def run_copy(buf_slot, s, n_rows, k):
        src = loc.at[buf_slot, _rows(local_tabs[s][k], n_rows)]
        dst = xs_ref.at[_rows(sorted_tabs[s][k], n_rows)]
        return pltpu.make_async_copy(src, dst, sem.at[buf_slot])

    _for_each_copy(i, cnt_ref, lambda s, n_rows, k: run_copy(slot, s, n_rows, k).start())

    def drain(tile, buf_slot):
        def make_wait(n):
            return pltpu.make_async_copy(loc.at[buf_slot, pl.ds(0, n * ROW_CHUNKS)],
                                         xs_ref.at[pl.ds(0, n * ROW_CHUNKS)], sem.at[buf_slot])
        _wait_rows(trow_ref[tile], make_wait)

    @pl.when(i > 0)
    def _():
        drain(i - 1, 1 - slot)

    @pl.when(i == n_steps - 1)
    def _():
        drain(i, slot)


def _clamped_maps(np_steps):
    pmap = lambda i, *_: (jnp.minimum(i, np_steps - 1), 0)
    smap = lambda i, *_: (jnp.maximum(i - np_steps, 0), 0)
    return pmap, smap


def _dispatch_call(tables, hn_p, hn_s, rt_p, rt_s, n_sorted_rows):
    n_p, n_s = hn_p.shape[0], hn_s.shape[0]
    np_steps = n_p // TOK_TILE
    n_steps = np_steps + n_s // TOK_TILE
    pmap, smap = _clamped_maps(np_steps)
    out_rows = (n_sorted_rows + 8) * ROW_CHUNKS
    return pl.pallas_call(
        functools.partial(_dispatch_kernel, n_prompt_steps=np_steps),
        grid_spec=pltpu.PrefetchScalarGridSpec(
            num_scalar_prefetch=len(tables), grid=(n_steps,),
            in_specs=[pl.BlockSpec((TOK_TILE, D_MODEL), pmap),
                      pl.BlockSpec((TOK_TILE, D_MODEL), smap),
                      pl.BlockSpec((TOK_TILE, SUBLANES), pmap),
                      pl.BlockSpec((TOK_TILE, SUBLANES), smap)],
            out_specs=pl.BlockSpec(memory_space=pl.ANY),
            scratch_shapes=[pltpu.VMEM((2, LOC_ROWS * ROW_CHUNKS, LANES), U32),
                            pltpu.VMEM((MOE_TILE * ROW_CHUNKS, LANES), U32),
                            pltpu.SemaphoreType.DMA((2,)),
                            pltpu.SemaphoreType.DMA(())]),
        out_shape=jax.ShapeDtypeStruct((out_rows, LANES), U32),
        compiler_params=pltpu.CompilerParams(
            dimension_semantics=("arbitrary",), vmem_limit_bytes=VMEM_LIMIT),
        name="dispatch",
    )(*tables, hn_p, hn_s, rt_p, rt_s)


def _moe_kernel(te_ref, nu_ref, par_ref, nxt_ref, ngrp_ref, xs_ref, wg_hbm, wu_hbm, wd_hbm, ys_ref,
                wg_f32, wu_f32, wd_f32, wg_bf, wu_bf, wd_bf, wsem):
    i = pl.program_id(0)
    used = i < nu_ref[0]

    def store_zeros(ref, n_rows):
        ij = (lax.broadcasted_iota(I32, (n_rows, D_MODEL), 0)
              + lax.broadcasted_iota(I32, (n_rows, D_MODEL), 1))
        _store_packed(ref, jnp.where(ij < jnp.minimum(nu_ref[0], 0), 1.0, 0.0), n_rows)

    def weight_copies(expert, buf):
        return [pltpu.make_async_copy(hbm.at[expert], vmem.at[buf], wsem.at[buf])
                for hbm, vmem in ((wg_hbm, wg_f32), (wu_hbm, wu_f32), (wd_hbm, wd_f32))]

    @pl.when(used)
    def _():
        prev = te_ref[jnp.maximum(i - 1, 0)]

        @pl.when((i == 0) | (te_ref[i] != prev))
        def _():
            buf = par_ref[i]

            @pl.when(i == 0)
            def _():
                for cp in weight_copies(te_ref[0], 0):
                    cp.start()

            for cp in weight_copies(te_ref[i], buf):
                cp.wait()
            wg_bf[...] = wg_f32[buf].astype(BF16)
            wu_bf[...] = wu_f32[buf].astype(BF16)
            wd_bf[...] = wd_f32[buf].astype(BF16)

            @pl.when(nxt_ref[i] >= 0)
            def _():
                for cp in weight_copies(nxt_ref[i], 1 - buf):
                    cp.start()

        def expert_rows(n_rows):
            lo, hi = _load_packed(xs_ref, n_rows)
            x = jnp.concatenate([lo, hi], axis=-1).astype(BF16)
            g = jnp.dot(x, wg_bf[...], preferred_element_type=F32)
            u = jnp.dot(x, wu_bf[...], preferred_element_type=F32)
            hh = (g * jax.nn.sigmoid(g)) * u
            y = jnp.dot(hh.astype(BF16), wd_bf[...], preferred_element_type=F32)
            _store_packed(ys_ref, y, n_rows)

        for n_groups in range(1, MOE_TILE // ROW_GROUP + 1):
            @pl.when(ngrp_ref[i] == n_groups)
            def _(n_rows=n_groups * ROW_GROUP):
                expert_rows(n_rows)
                if n_rows < MOE_TILE:
                    rest = MOE_TILE - n_rows
                    store_zeros(ys_ref.at[pl.ds(n_rows * ROW_CHUNKS, rest * ROW_CHUNKS)], rest)

    @pl.when(jnp.logical_not(used))
    def _():
        store_zeros(ys_ref, MOE_TILE)


def _moe_call(tile_expert, n_used, buf_parity, next_expert, n_groups_used, xs, n_tiles, w_eg, w_eu, w_ed):
    blk = MOE_TILE * ROW_CHUNKS
    any_spec = pl.BlockSpec(memory_space=pl.ANY)
    return pl.pallas_call(
        _moe_kernel,
        grid_spec=pltpu.PrefetchScalarGridSpec(
            num_scalar_prefetch=5, grid=(n_tiles,),
            in_specs=[
                pl.BlockSpec((blk, LANES), lambda i, te, nu, *_: (jnp.minimum(i, nu[0] - 1), 0)),
                any_spec, any_spec, any_spec,
            ],
            out_specs=pl.BlockSpec((blk, LANES), lambda i, *_: (i, 0)),
            scratch_shapes=[pltpu.VMEM((2, D_MODEL, D_EXPERT), F32),
                            pltpu.VMEM((2, D_MODEL, D_EXPERT), F32),
                            pltpu.VMEM((2, D_EXPERT, D_MODEL), F32),
                            pltpu.VMEM((D_MODEL, D_EXPERT), BF16),
                            pltpu.VMEM((D_MODEL, D_EXPERT), BF16),
                            pltpu.VMEM((D_EXPERT, D_MODEL), BF16),
                            pltpu.SemaphoreType.DMA((2,))]),
        out_shape=jax.ShapeDtypeStruct((n_tiles * blk, LANES), U32),
        compiler_params=pltpu.CompilerParams(
            dimension_semantics=("arbitrary",), vmem_limit_bytes=VMEM_LIMIT),
        name="experts",
    )(tile_expert, n_used, buf_parity, next_expert, n_groups_used, xs, w_eg, w_eu, w_ed)


def _final_kernel(l8, g8, l4, g4, l2, g2, cnt_ref, trow_ref,
                  h_ref, rt_ref, p_ref, ys_ref, wple_ref, wgate_ref, gple_ref, gfin_ref, out_ref,
                  ybuf, yb_s, perm_s, sem, *, tile_base, n_tiles):
    s = pl.program_id(0)
    a_slot = s % 2
    b_slot = 1 - a_slot
    local_tabs, sorted_tabs = (l8, l4, l2), (g8, g4, g2)

    def run_copy(buf_slot, sz, n_rows, k):
        src = ys_ref.at[_rows(sorted_tabs[sz][k], n_rows)]
        dst = ybuf.at[buf_slot, _rows(local_tabs[sz][k], n_rows)]
        return pltpu.make_async_copy(src, dst, sem.at[buf_slot])

    def issue(tile, buf_slot, active):
        _for_each_copy(tile_base + jnp.minimum(tile, n_tiles - 1), cnt_ref,
                       lambda sz, n_rows, k: run_copy(buf_slot, sz, n_rows, k).start(), active)

    @pl.when(s == 0)
    def _():
        ybuf[...] = jnp.zeros(ybuf.shape, U32)
        yb_s[...] = jnp.zeros(yb_s.shape, BF16)
        perm_s[...] = jnp.zeros(perm_s.shape, BF16)
        issue(0, 0, True)

    issue(s + 1, b_slot, s + 1 < n_tiles)

    def make_wait(n):
        return pltpu.make_async_copy(ys_ref.at[pl.ds(0, n * ROW_CHUNKS)],
                                     ybuf.at[a_slot, pl.ds(0, n * ROW_CHUNKS)], sem.at[a_slot])
    a_rows = jnp.where(s < n_tiles, trow_ref[tile_base + jnp.minimum(s, n_tiles - 1)], 0)
    _wait_rows(a_rows, make_wait)

    h = h_ref[...]
    h = h + jnp.dot(perm_s[b_slot], yb_s[b_slot], preferred_element_type=F32)
    r = _rms(h, gple_ref[...])
    gate = jax.nn.sigmoid(jnp.dot(r.astype(BF16), wgate_ref[...], preferred_element_type=F32))
    h = h + jnp.dot(p_ref[...].astype(BF16), wple_ref[...], preferred_element_type=F32) * gate
    out_ref[...] = _rms(h, gfin_ref[...])

    rt = rt_ref[...]
    lpos1, lpos2 = rt[:, 2:3], rt[:, 3:4]
    lane = lax.broadcasted_iota(I32, (TOK_TILE, LOC_ROWS), 1).astype(F32)
    perm_s[a_slot] = jnp.where((lane == lpos1) | (lane == lpos2), 1.0, 0.0).astype(BF16)
    row1, row2, w1_row, w2_row = _as_rows(lpos1, lpos2, rt[:, 4:5], rt[:, 5:6])
    grow = lax.broadcasted_iota(I32, (LOC_ROWS, TOK_TILE), 0).astype(F32)
    wrow = jnp.sum(jnp.where(grow == row1, w1_row, 0.0) + jnp.where(grow == row2, w2_row, 0.0),
                   axis=-1, keepdims=True)
    lo, hi = _load_packed(ybuf.at[a_slot], LOC_ROWS)
    yb_s[a_slot] = (jnp.concatenate([lo, hi], axis=-1) * wrow).astype(BF16)


def _final_call(tables, h, rt, p, ys, wple, wgate, gple, gfin, *, tile_base):
    n_rows = h.shape[0]
    n_tiles = n_rows // TOK_TILE
    b_map = lambda s, *_: (jnp.maximum(s - 1, 0), 0)
    a_map = lambda s, *_: (jnp.minimum(s, n_tiles - 1), 0)
    cmap = lambda s, *_: (0, 0)
    return pl.pallas_call(
        functools.partial(_final_kernel, tile_base=tile_base, n_tiles=n_tiles),
        grid_spec=pltpu.PrefetchScalarGridSpec(
            num_scalar_prefetch=len(tables), grid=(n_tiles + 1,),
            in_specs=[
                pl.BlockSpec((TOK_TILE, D_MODEL), b_map),
                pl.BlockSpec((TOK_TILE, SUBLANES), a_map),
                pl.BlockSpec((TOK_TILE, PLE_DIM), b_map),
                pl.BlockSpec(memory_space=pl.ANY),
                pl.BlockSpec(wple.shape, cmap),
                pl.BlockSpec(wgate.shape, cmap),
                pl.BlockSpec(gple.shape, cmap),
                pl.BlockSpec(gfin.shape, cmap),
            ],
            out_specs=pl.BlockSpec((TOK_TILE, D_MODEL), b_map),
            scratch_shapes=[pltpu.VMEM((2, LOC_ROWS * ROW_CHUNKS, LANES), U32),
                            pltpu.VMEM((2, LOC_ROWS, D_MODEL), BF16),
                            pltpu.VMEM((2, TOK_TILE, LOC_ROWS), BF16),
                            pltpu.SemaphoreType.DMA((2,))]),
        out_shape=jax.ShapeDtypeStruct((n_rows, D_MODEL), F32),
        compiler_params=pltpu.CompilerParams(
            dimension_semantics=("arbitrary",), vmem_limit_bytes=VMEM_LIMIT),
        name="final",
    )(*tables, h, rt, p, ys, wple, wgate, gple, gfin)


def _ceil_to(x, m):
    return ((x + m - 1) // m) * m


def _excl_cumsum(a, axis):
    return jnp.cumsum(a, axis=axis) - a


def _copy_tables(n_copies, first_local, first_sorted, step, cap):
    ends = jnp.cumsum(n_copies, axis=1)
    j = jnp.arange(cap, dtype=I32)[None, :, None]
    e_of = jnp.minimum(jnp.sum(ends[:, None, :] <= j, axis=-1), N_EXPERTS - 1)
    pick = e_of[:, :, None] == jnp.arange(N_EXPERTS, dtype=I32)[None, None, :]
    take = lambda a: jnp.sum(jnp.where(pick, a[:, None, :], 0), axis=-1)
    c = jnp.arange(cap, dtype=I32)[None, :] - (take(ends) - take(n_copies))
    flat = lambda a: a.reshape(-1).astype(I32)
    return flat(take(first_local) + step * c), flat(take(first_sorted) + step * c), ends[:, -1]


def kernel(x_prompt, x_sample, state_pool, p_prompt, p_sample, g_mix, w_in, w_pool, pool_scale,
           g_v, b_v, w_spatial, b_spatial, w_out, g_ffn, w_rg, b_rg, w_re, b_re, w_eg, w_eu, w_ed,
           w_ple, g_ple, w_ple_gate, g_final):
    batch, seq, _ = x_prompt.shape
    dec_batch, dec_seq, _ = x_sample.shape
    assert dec_seq == SUBLANES and g_mix.shape[0] == 1
    n_p = batch * seq
    n_s = dec_batch * dec_seq
    n_tok = n_p + n_s
    past_len = 16384

    row = lambda a: a.reshape(1, -1).astype(F32)
    wr = jnp.concatenate(
        [w_re[0], w_rg[0], jnp.zeros((D_MODEL, LANES - N_EXPERTS - N_EXPERT_GROUPS), F32)], axis=1)
    br = jnp.concatenate(
        [b_re[0], b_rg[0], jnp.zeros((LANES - N_EXPERTS - N_EXPERT_GROUPS,), F32)]).reshape(1, LANES)
    bsp_p = jnp.repeat(jnp.transpose(b_spatial[0]), GMLP_HEAD, axis=1)
    bsp_s = jnp.tile(bsp_p[:dec_seq], (CHUNK // dec_seq, 1))
    wsp_p = w_spatial[0].astype(BF16)
    wsp_s = jnp.tile(w_spatial[0][:, :dec_seq, :dec_seq].astype(BF16),
                     (1, CHUNK // dec_seq, CHUNK // dec_seq))

    def pool_band(first_chunk):
        t = jnp.arange(CHUNK, dtype=I32)[:, None]
        k = jnp.arange(POOL_SPAN, dtype=I32)[None, :]
        mats = []
        for w in POOL_WINDOWS:
            cnt = jnp.minimum(w, t + 1) if first_chunk else jnp.full_like(t, w)
            in_window = (k >= t + 17 - w) & (k <= t + 16)
            mats.append(jnp.where(in_window, 1.0 / cnt.astype(F32), 0.0) - (k == t + 16).astype(F32))
        return jnp.stack(mats).astype(BF16)

    def mixer_weights(wsp, bsp):
        return (row(g_mix[0]), w_in[0].astype(BF16), w_pool[0].astype(BF16), row(pool_scale[0]),
                row(g_v[0]), row(b_v[0]), wsp, bsp, w_out[0].astype(BF16), row(g_ffn[0]),
                jnp.transpose(wr).astype(BF16), br.reshape(LANES, 1), pool_band(False), pool_band(True))

    h_p, hn_p, rt_p, cnt_p, hist_p = _mixer_call(
        x_prompt, None, mixer_weights(wsp_p, bsp_p), sample=False, pos0=0)
    h_s, hn_s, rt_s, cnt_s, hist_s, v_s = _mixer_call(
        x_sample.reshape(n_s, D_MODEL), jnp.transpose(state_pool[0], (1, 0, 2)),
        mixer_weights(wsp_s, bsp_s),
        sample=True, pos0=past_len)

    n_run = jnp.concatenate([cnt_p, cnt_s], axis=0)[:, 0, :N_EXPERTS].astype(I32)
    n_even = _ceil_to(n_run, RUN_ALIGN)
    l_start = _excl_cumsum(n_even, 1)
    tile_rows = jnp.sum(n_even, axis=1)
    base = _excl_cumsum(n_even, 0)
    counts = jnp.sum(n_even, axis=0)
    padded = _ceil_to(counts, MOE_TILE)
    ends = jnp.cumsum(padded)
    offs = ends - padded
    g_start = offs[None, :] + base
    n8 = n_even // 8
    f4 = (n_even // 4) % 2
    f2 = (n_even // 2) % 2
    l8, g8, c8 = _copy_tables(n8, l_start, g_start, 8, COPY_CAP[0])
    l4, g4, c4 = _copy_tables(f4, l_start + 8 * n8, g_start + 8 * n8, 0, COPY_CAP[1])
    l2, g2, c2 = _copy_tables(f2, l_start + 8 * n8 + 4 * f4, g_start + 8 * n8 + 4 * f4, 0, COPY_CAP[2])
    copy_cnt = jnp.stack([c8, c4, c2], axis=1).reshape(-1).astype(I32)
    tile_rows = tile_rows.astype(I32)
    z_start = (offs + counts).astype(I32)
    z_big = ((padded - counts) // ZERO_ROWS).astype(I32)
    z_blocks = (((padded - counts) % ZERO_ROWS + 7) // 8).astype(I32)

    max_rows = 2 * n_tok + n_run.shape[0] * N_EXPERTS * (RUN_ALIGN - 1)
    n_tiles = max_rows // MOE_TILE + N_EXPERTS + 1
    n_used = (ends[-1] // MOE_TILE).astype(I32).reshape(1)
    tile_id = jnp.minimum(jnp.arange(n_tiles, dtype=I32), n_used - 1)
    tile_expert = jnp.sum((tile_id[:, None] * MOE_TILE) >= ends[None, :], axis=1).astype(I32)
    tile_expert = jnp.minimum(tile_expert, N_EXPERTS - 1)
    pick_e = tile_expert[:, None] == jnp.arange(N_EXPERTS, dtype=I32)[None, :]
    data_end = jnp.sum(jnp.where(pick_e, (offs + counts)[None, :], 0), axis=1)
    n_groups_used = jnp.clip((data_end - tile_id * MOE_TILE + ROW_GROUP - 1) // ROW_GROUP,
                             1, MOE_TILE // ROW_GROUP).astype(I32)
    e_ids = jnp.arange(N_EXPERTS, dtype=I32)
    nonempty = padded > 0
    seg_index = jnp.cumsum(nonempty.astype(I32)) - 1
    later = nonempty[None, :] & (e_ids[None, :] > e_ids[:, None])
    next_of = jnp.min(jnp.where(later, e_ids[None, :], N_EXPERTS), axis=1)
    next_of = jnp.where(next_of == N_EXPERTS, -1, next_of)
    buf_parity = (jnp.sum(jnp.where(pick_e, seg_index[None, :], 0), axis=1) % 2).astype(I32)
    next_expert = jnp.sum(jnp.where(pick_e, next_of[None, :], 0), axis=1).astype(I32)

    copy_tabs = (l8, g8, l4, g4, l2, g2, copy_cnt, tile_rows)
    xs = _dispatch_call(copy_tabs + (z_start, z_big, z_blocks, n_used), hn_p, hn_s, rt_p, rt_s,
                        n_tiles * MOE_TILE)
    ys = _moe_call(tile_expert, n_used, buf_parity, next_expert, n_groups_used, xs, n_tiles,
                   w_eg[0], w_eu[0], w_ed[0])
    fin_w = (w_ple[0].astype(BF16), w_ple_gate[0].astype(BF16), row(g_ple[0]), row(g_final))
    y_p = _final_call(copy_tabs, h_p, rt_p, p_prompt[0].reshape(n_p, PLE_DIM), ys, *fin_w, tile_base=0)
    y_s = _final_call(copy_tabs, h_s, rt_s, p_sample[0].reshape(n_s, PLE_DIM), ys, *fin_w,
                      tile_base=n_p // TOK_TILE)

    return (y_p.reshape(batch, seq, D_MODEL),
            y_s.reshape(dec_batch, dec_seq, D_MODEL),
            hist_p[None],
            jnp.transpose(hist_s, (1, 0, 2))[None],
            v_s.reshape(1, dec_batch, dec_seq, GMLP_WIDTH))
```

```python
import functools

import jax
import jax.numpy as jnp
from jax import lax
from jax.experimental import pallas as pl
from jax.experimental.pallas import tpu as pltpu

D_MODEL = 1024
POOL_WIDTH = 512
GMLP_WIDTH = 512
IN_WIDTH = POOL_WIDTH + 2 * GMLP_WIDTH
POOL_WINDOWS = (2, 4, 8, 16)
POOL_GROUP = 128
POOL_HIST = 15
CHUNK = 128
POOL_SPAN = CHUNK + 16
N_GMLP_HEADS = 4
GMLP_HEAD = 128
N_EXPERT_GROUPS = 4
EXPERTS_PER_GROUP = 8
N_EXPERTS = 32
D_EXPERT = 512
PLE_DIM = 256
EPS = 1e-6

LANES = 128
SUBLANES = 8
HALF = D_MODEL // 2
ROW_CHUNKS = HALF // LANES

TOK_TILE = 256
MIX_TILE = 1024
MOE_TILE = 640
ROW_GROUP = 128
WEIGHT_BUFS = 3
RUN_ALIGN = 2
COPY_ROWS = (8, 4, 2)
ZERO_ROWS = 64
LOC_ROWS = 2 * TOK_TILE + N_EXPERTS * (RUN_ALIGN - 1)
COPY_CAP = (LOC_ROWS // 8 + 1, N_EXPERTS, N_EXPERTS)
VMEM_LIMIT = 56 * 1024 * 1024

F32 = jnp.float32
BF16 = jnp.bfloat16
I32 = jnp.int32
U32 = jnp.uint32


def _rms(x, g):
    return x * lax.rsqrt(jnp.mean(x * x, axis=-1, keepdims=True) + EPS) * g


def _gelu_tanh(x):
    inner = x * (0.7978845608028654 + (0.7978845608028654 * 0.044715) * (x * x))
    half = 0.5 * x
    return half + half * jnp.tanh(inner)


def _store_packed(ref, val, n_rows):
    packed = pltpu.pack_elementwise([val[:, :HALF], val[:, HALF:]], packed_dtype=BF16)
    for k in range(ROW_CHUNKS):
        ref[pl.ds(k, n_rows, stride=ROW_CHUNKS), :] = packed[:, k * LANES:(k + 1) * LANES]


def _load_packed(ref, n_rows):
    packed = jnp.concatenate(
        [ref[pl.ds(k, n_rows, stride=ROW_CHUNKS), :] for k in range(ROW_CHUNKS)], axis=-1)
    lo = pltpu.unpack_elementwise(packed, index=0, packed_dtype=BF16, unpacked_dtype=F32)
    hi = pltpu.unpack_elementwise(packed, index=1, packed_dtype=BF16, unpacked_dtype=F32)
    return lo, hi


def _rows(start, n):
    return pl.ds(pl.multiple_of(start * ROW_CHUNKS, SUBLANES), n * ROW_CHUNKS)


def _route(hn, wrt_ref, brc_ref, n_tok):
    logits = lax.dot_general(wrt_ref[...], hn, (((1,), (1,)), ((), ())),
                             preferred_element_type=F32) + brc_ref[...]
    neg = jnp.float32(-jnp.inf)
    big = jnp.float32(1e9)
    sub = lax.broadcasted_iota(I32, (SUBLANES, n_tok), 0).astype(F32)
    gl = jnp.where(sub < N_EXPERT_GROUPS, logits[N_EXPERTS:N_EXPERTS + SUBLANES, :], neg)
    gmax = jnp.max(gl, axis=0, keepdims=True)
    gidx = jnp.min(jnp.where(gl == gmax, sub, big), axis=0, keepdims=True)
    gprob = 1.0 / jnp.sum(jnp.exp(gl - gmax), axis=0, keepdims=True)
    assert EXPERTS_PER_GROUP == SUBLANES
    el = logits[0:SUBLANES, :]
    for g in range(1, N_EXPERT_GROUPS):
        el = jnp.where(gidx == g, logits[g * SUBLANES:(g + 1) * SUBLANES, :], el)
    m1 = jnp.max(el, axis=0, keepdims=True)
    o1 = jnp.min(jnp.where(el == m1, sub, big), axis=0, keepdims=True)
    el2 = jnp.where(sub == o1, neg, el)
    m2 = jnp.max(el2, axis=0, keepdims=True)
    o2 = jnp.min(jnp.where(el2 == m2, sub, big), axis=0, keepdims=True)
    i1 = gidx * EXPERTS_PER_GROUP + o1
    i2 = gidx * EXPERTS_PER_GROUP + o2
    t = jnp.exp(m2 - m1)
    w1 = gprob / (1.0 + t)
    w2 = gprob * t / (1.0 + t)

    erow = lax.broadcasted_iota(I32, (LANES, n_tok), 0).astype(F32)
    sel1 = erow == i1
    sel2 = erow == i2
    oh = jnp.where(sel1 | sel2, 1.0, 0.0).astype(BF16)
    r = lax.broadcasted_iota(I32, (n_tok, n_tok), 0)
    c = lax.broadcasted_iota(I32, (n_tok, n_tok), 1)
    earlier = jnp.where(r < c, 1.0, 0.0).astype(BF16)
    before = jnp.dot(oh, earlier, preferred_element_type=F32)
    counts_b = jnp.dot(oh, jnp.ones((n_tok, LANES), BF16), preferred_element_type=F32)
    assert RUN_ALIGN == 2
    c_even = counts_b + (counts_b - 2.0 * jnp.floor(counts_b * 0.5))
    c_hi = jnp.floor(c_even * (1.0 / 256.0))
    er = lax.broadcasted_iota(I32, (LANES, LANES), 0)
    ec = lax.broadcasted_iota(I32, (LANES, LANES), 1)
    lower = jnp.where(ec < er, 1.0, 0.0).astype(BF16)
    pieces = jnp.concatenate([c_hi, c_even - 256.0 * c_hi], axis=1).astype(BF16)
    prefix = jnp.dot(lower, pieces, preferred_element_type=F32)
    lstart = 256.0 * prefix[:, :LANES] + prefix[:, LANES:]
    placed = before + jnp.concatenate([lstart] * (n_tok // LANES), axis=1)
    lpos1 = jnp.sum(jnp.where(sel1, placed, 0.0), axis=0, keepdims=True)
    lpos2 = jnp.sum(jnp.where(sel2, placed, 0.0), axis=0, keepdims=True)

    stacked = jnp.where(erow == 0.0, i1, 0.0)
    for k, val in enumerate((i2, lpos1, lpos2, w1, w2), start=1):
        stacked = jnp.where(erow == float(k), val, stacked)
    return stacked.T[:, :SUBLANES], counts_b.T[0:1, :]


def _mixer_kernel(*refs, n_tok, sample, pos0):
    if sample:
        (x_ref, hist_ref, gmix_ref, win_ref, wpool_ref, pscale_ref, gv_ref, bv_ref, wsp_ref,
         bsp_ref, wout_ref, gffn_ref, wr_ref, br_ref, band_ref, band0_ref,
         h_ref, hn_ref, rt_ref, cnt_ref, hist_out_ref, v_ref, ext_scr, pool_scr) = refs
    else:
        (x_ref, gmix_ref, win_ref, wpool_ref, pscale_ref, gv_ref, bv_ref, wsp_ref,
         bsp_ref, wout_ref, gffn_ref, wr_ref, br_ref, band_ref, band0_ref,
         h_ref, hn_ref, rt_ref, cnt_ref, hist_out_ref, ext_scr) = refs

    x = x_ref[...]
    xn = _rms(x, gmix_ref[...])
    z = jnp.dot(xn.astype(BF16), win_ref[...], preferred_element_type=F32)
    a = z[:, :POOL_WIDTH]
    uv = _gelu_tanh(z[:, POOL_WIDTH:])
    u = uv[:, :GMLP_WIDTH]
    v = uv[:, GMLP_WIDTH:]
    mu = jnp.mean(v, axis=-1, keepdims=True)
    vc = v - mu
    v = vc * lax.rsqrt(jnp.mean(vc * vc, axis=-1, keepdims=True) + EPS) * gv_ref[...] + bv_ref[...]

    pooled = []
    if sample:
        n_seq = n_tok // SUBLANES
        n_grp = len(POOL_WINDOWS)
        gcols = [slice(gi * POOL_GROUP, (gi + 1) * POOL_GROUP) for gi in range(n_grp)]
        ext_scr[1:1 + POOL_HIST] = hist_ref[...]
        for gi in range(n_grp):
            pool_scr[gi] = a[:, gcols[gi]]
        for i in range(SUBLANES):
            ext_scr[1 + POOL_HIST + i] = jnp.concatenate(
                [pool_scr[gi, pl.ds(i, n_seq, stride=SUBLANES), :] for gi in range(n_grp)], axis=-1)
        hist_out_ref[...] = ext_scr[1 + SUBLANES:]
        for i in range(SUBLANES):
            back = [ext_scr[1 + POOL_HIST + i - j] for j in range(max(POOL_WINDOWS))]
            for gi, w in enumerate(POOL_WINDOWS):
                win = back[0][:, gcols[gi]]
                for j in range(1, w):
                    win = win + back[j][:, gcols[gi]]
                pool_scr[gi, pl.ds(i, n_seq, stride=SUBLANES), :] = (
                    win * (1.0 / min(w, pos0 + i + 1)) - back[0][:, gcols[gi]])
        pooled = [pool_scr[gi] for gi in range(n_grp)]
    else:
        l = pl.program_id(1)

        @pl.when(l == 0)
        def _():
            ext_scr[0:16, :] = jnp.zeros((16, POOL_WIDTH), F32)

        ext_scr[16:16 + n_tok, :] = a
        chunks = []
        for ci in range(n_tok // CHUNK):
            ext_c = ext_scr[ci * CHUNK:ci * CHUNK + POOL_SPAN, :].astype(BF16)
            groups = []
            for gi in range(len(POOL_WINDOWS)):
                band = band_ref[gi]
                if ci == 0:
                    band = jnp.where(l == 0, band0_ref[gi], band)
                groups.append(jnp.dot(band, ext_c[:, gi * POOL_GROUP:(gi + 1) * POOL_GROUP],
                                      preferred_element_type=F32))
            chunks.append(groups)
        pooled = [jnp.concatenate([chunks[ci][gi] for ci in range(n_tok // CHUNK)], axis=0)
                  for gi in range(len(POOL_WINDOWS))]
        tail = ext_scr[n_tok + 1:n_tok + 16, :]
        hist_out_ref[...] = tail
        ext_scr[1:16, :] = tail

    pool_out = jnp.concatenate(
        [jnp.dot(pooled[gi].astype(BF16), wpool_ref[gi], preferred_element_type=F32)
         for gi in range(len(POOL_WINDOWS))], axis=-1) * pscale_ref[...]

    blk = SUBLANES if sample else CHUNK
    tr = lax.broadcasted_iota(I32, (CHUNK, CHUNK), 0)
    sc = lax.broadcasted_iota(I32, (CHUNK, CHUNK), 1)
    mask = (sc <= tr) & ((sc // blk) == (tr // blk))
    wsp = [jnp.where(mask, wsp_ref[hh], jnp.zeros((), BF16)) for hh in range(N_GMLP_HEADS)]
    v_bf = v.astype(BF16)
    mixed_chunks = []
    for ci in range(n_tok // CHUNK):
        rows = slice(ci * CHUNK, (ci + 1) * CHUNK)
        heads = [jnp.dot(wsp[hh], v_bf[rows, hh * GMLP_HEAD:(hh + 1) * GMLP_HEAD],
                         preferred_element_type=F32) for hh in range(N_GMLP_HEADS)]
        mixed_chunks.append(jnp.concatenate(heads, axis=-1) + bsp_ref[...])
    mixed = jnp.concatenate(mixed_chunks, axis=0)
    gmlp_out = u * mixed

    cat = jnp.concatenate([pool_out, gmlp_out], axis=-1)
    h = x + jnp.dot(cat.astype(BF16), wout_ref[...], preferred_element_type=F32)
    hn = _rms(h, gffn_ref[...]).astype(BF16)

    h_ref[...] = h
    hn_ref[...] = hn
    if sample:
        v_ref[...] = v
    for s in range(n_tok // TOK_TILE):
        rows = slice(s * TOK_TILE, (s + 1) * TOK_TILE)
        rt, counts = _route(hn[rows], wr_ref, br_ref, TOK_TILE)
        rt_ref[rows, :] = rt
        cnt_ref[s] = counts


def _full(shape):
    return pl.BlockSpec(shape, lambda *_: (0,) * len(shape))


def _mixer_call(x, hist, weights, *, sample, pos0):
    w_specs = [_full(w.shape) for w in weights]
    n_tok = MIX_TILE
    if sample:
        n_rows = x.shape[0]
        n_seq = n_tok // SUBLANES
        grid = (n_rows // n_tok,)
        tok_map = lambda i: (i, 0)
        tile_map = lambda i: (i, 0, 0)
        hist_spec = pl.BlockSpec((POOL_HIST, n_seq, POOL_WIDTH), lambda i: (0, i, 0))
        in_specs = [pl.BlockSpec((n_tok, D_MODEL), tok_map), hist_spec] + w_specs
        hist_shape = jax.ShapeDtypeStruct(hist.shape, F32)
        scratch = [pltpu.VMEM((1 + POOL_HIST + SUBLANES, n_seq, POOL_WIDTH), F32),
                   pltpu.VMEM((len(POOL_WINDOWS), n_tok, POOL_GROUP), F32)]
        args = (x, hist)
    else:
        b, seq, _ = x.shape
        n_rows = b * seq
        n_l = seq // n_tok
        grid = (b, n_l)
        tok_map = lambda bi, li: (bi * n_l + li, 0)
        tile_map = lambda bi, li: (bi * n_l + li, 0, 0)
        in_specs = [pl.BlockSpec((None, n_tok, D_MODEL), lambda bi, li: (bi, li, 0))] + w_specs
        hist_shape = jax.ShapeDtypeStruct((b, POOL_HIST, POOL_WIDTH), F32)
        hist_spec = pl.BlockSpec((None, POOL_HIST, POOL_WIDTH), lambda bi, li: (bi, 0, 0))
        scratch = [pltpu.VMEM((16 + n_tok, POOL_WIDTH), F32)]
        args = (x,)
    out_shape = [jax.ShapeDtypeStruct((n_rows, D_MODEL), F32),
                 jax.ShapeDtypeStruct((n_rows, D_MODEL), BF16),
                 jax.ShapeDtypeStruct((n_rows, SUBLANES), F32),
                 jax.ShapeDtypeStruct((n_rows // TOK_TILE, 1, LANES), F32),
                 hist_shape]
    out_specs = [pl.BlockSpec((n_tok, D_MODEL), tok_map),
                 pl.BlockSpec((n_tok, D_MODEL), tok_map),
                 pl.BlockSpec((n_tok, SUBLANES), tok_map),
                 pl.BlockSpec((n_tok // TOK_TILE, 1, LANES), tile_map),
                 hist_spec]
    if sample:
        out_shape.append(jax.ShapeDtypeStruct((n_rows, GMLP_WIDTH), F32))
        out_specs.append(pl.BlockSpec((n_tok, GMLP_WIDTH), tok_map))
    return pl.pallas_call(
        functools.partial(_mixer_kernel, n_tok=n_tok, sample=sample, pos0=pos0),
        grid=grid, in_specs=in_specs, out_specs=out_specs, out_shape=out_shape,
        scratch_shapes=scratch,
        compiler_params=pltpu.CompilerParams(
            dimension_semantics=("arbitrary",) * len(grid), vmem_limit_bytes=VMEM_LIMIT),
        name="mixer_sample" if sample else "mixer_prompt",
    )(*args, *weights)


def _for_each_copy(tile, cnt_ref, fn, active=True):
    for s, n_rows in enumerate(COPY_ROWS):
        def body(j, carry, s=s, n_rows=n_rows):
            fn(s, n_rows, tile * COPY_CAP[s] + j)
            return carry
        lax.fori_loop(0, jnp.where(active, cnt_ref[tile * len(COPY_ROWS) + s], 0), body, 0)


def _wait_rows(n_rows_total, make_wait):
    units = n_rows_total // RUN_ALIGN
    bit = 0
    while (RUN_ALIGN << bit) <= LOC_ROWS:
        @pl.when((units >> bit) & 1 == 1)
        def _(bit=bit):
            make_wait(RUN_ALIGN << bit).wait()
        bit += 1


def _as_rows(*cols):
    n = cols[0].shape[0]
    lane = lax.broadcasted_iota(I32, (n, LANES), 1)
    packed = jnp.zeros((n, LANES), F32)
    for k, col in enumerate(cols):
        packed = jnp.where(lane == k, col, packed)
    rows = packed.T
    return [rows[k:k + 1, :] for k in range(len(cols))]


def _dispatch_kernel(l8, g8, l4, g4, l2, g2, cnt_ref, trow_ref, zs_ref, zbig_ref, zb_ref, nu_ref,
                     hnp_ref, hns_ref, rtp_ref, rts_ref, xs_ref, loc, zbuf, sem, zsem,
                     *, n_prompt_steps):
    i = pl.program_id(0)
    n_steps = pl.num_programs(0)
    slot = i % 2
    local_tabs, sorted_tabs = (l8, l4, l2), (g8, g4, g2)

    @pl.when(i == 0)
    def _():
        zbuf[...] = jnp.zeros(zbuf.shape, U32)
        n_tiles = xs_ref.shape[0] // (MOE_TILE * ROW_CHUNKS)

        def zero_copy(e, c, n_rows, first):
            return pltpu.make_async_copy(zbuf.at[pl.ds(0, n_rows * ROW_CHUNKS)],
                                         xs_ref.at[_rows(first + c * n_rows, n_rows)], zsem)

        def tail_copy(t):
            return pltpu.make_async_copy(zbuf, xs_ref.at[_rows(t * MOE_TILE, MOE_TILE)], zsem)

        def per_expert(e, totals):
            def big(c, carry):
                zero_copy(e, c, ZERO_ROWS, zs_ref[e]).start()
                return carry

            def small(c, carry):
                zero_copy(e, c, 8, zs_ref[e] + zbig_ref[e] * ZERO_ROWS).start()
                return carry

            lax.fori_loop(0, zbig_ref[e], big, 0)
            lax.fori_loop(0, zb_ref[e], small, 0)
            return totals[0] + zbig_ref[e], totals[1] + zb_ref[e]

        n_big, n_small = lax.fori_loop(0, N_EXPERTS, per_expert, (0, 0))

        def wait_big(_, carry):
            zero_copy(0, 0, ZERO_ROWS, 0).wait()
            return carry

        def wait_small(_, carry):
            zero_copy(0, 0, 8, 0).wait()
            return carry

        lax.fori_loop(0, n_big, wait_big, 0)
        lax.fori_loop(0, n_small, wait_small, 0)

        def tail_start(t, carry):
            tail_copy(t).start()
            return carry

        def tail_wait(t, carry):
            tail_copy(t).wait()
            return carry

        spill = pltpu.make_async_copy(zbuf.at[pl.ds(0, 8 * ROW_CHUNKS)],
                                      xs_ref.at[_rows(n_tiles * MOE_TILE, 8)], zsem)
        lax.fori_loop(nu_ref[0], n_tiles, tail_start, 0)
        spill.start()
        lax.fori_loop(nu_ref[0], n_tiles, tail_wait, 0)
        spill.wait()

    is_prompt = i < n_prompt_steps
    rt = jnp.where(is_prompt, rtp_ref[...], rts_ref[...])
    row1, row2 = _as_rows(rt[:, 2:3], rt[:, 3:4])
    grow = lax.broadcasted_iota(I32, (LOC_ROWS, TOK_TILE), 0).astype(F32)
    perm = jnp.where((grow == row1) | (grow == row2), 1.0, 0.0).astype(BF16)
    hn = jnp.where(is_prompt, hnp_ref[...], hns_ref[...])
    grouped = jnp.dot(perm, hn, preferred_element_type=F32)
    _store_packed(loc.at[slot], grouped, LOC_ROWS)

    def run_copy(buf_slot, s, n_rows, k):
        src = loc.at[buf_slot, _rows(local_tabs[s][k], n_rows)]
        dst = xs_ref.at[_rows(sorted_tabs[s][k], n_rows)]
        return pltpu.make_async_copy(src, dst, sem.at[buf_slot])

    _for_each_copy(i, cnt_ref, lambda s, n_rows, k: run_copy(slot, s, n_rows, k).start())

    def drain(tile, buf_slot):
        def make_wait(n):
            return pltpu.make_async_copy(loc.at[buf_slot, pl.ds(0, n * ROW_CHUNKS)],
                                         xs_ref.at[pl.ds(0, n * ROW_CHUNKS)], sem.at[buf_slot])
        _wait_rows(trow_ref[tile], make_wait)

    @pl.when(i > 0)
    def _():
        drain(i - 1, 1 - slot)

    @pl.when(i == n_steps - 1)
    def _():
        drain(i, slot)


def _clamped_maps(np_steps):
    pmap = lambda i, *_: (jnp.minimum(i, np_steps - 1), 0)
    smap = lambda i, *_: (jnp.maximum(i - np_steps, 0), 0)
    return pmap, smap


def _dispatch_call(tables, hn_p, hn_s, rt_p, rt_s, n_sorted_rows):
    n_p, n_s = hn_p.shape[0], hn_s.shape[0]
    np_steps = n_p // TOK_TILE
    n_steps = np_steps + n_s // TOK_TILE
    pmap, smap = _clamped_maps(np_steps)
    out_rows = (n_sorted_rows + 8) * ROW_CHUNKS
    return pl.pallas_call(
        functools.partial(_dispatch_kernel, n_prompt_steps=np_steps),
        grid_spec=pltpu.PrefetchScalarGridSpec(
            num_scalar_prefetch=len(tables), grid=(n_steps,),
            in_specs=[pl.BlockSpec((TOK_TILE, D_MODEL), pmap),
                      pl.BlockSpec((TOK_TILE, D_MODEL), smap),
                      pl.BlockSpec((TOK_TILE, SUBLANES), pmap),
                      pl.BlockSpec((TOK_TILE, SUBLANES), smap)],
            out_specs=pl.BlockSpec(memory_space=pl.ANY),
            scratch_shapes=[pltpu.VMEM((2, LOC_ROWS * ROW_CHUNKS, LANES), U32),
                            pltpu.VMEM((MOE_TILE * ROW_CHUNKS, LANES), U32),
                            pltpu.SemaphoreType.DMA((2,)),
                            pltpu.SemaphoreType.DMA(())]),
        out_shape=jax.ShapeDtypeStruct((out_rows, LANES), U32),
        compiler_params=pltpu.CompilerParams(
            dimension_semantics=("arbitrary",), vmem_limit_bytes=VMEM_LIMIT),
        name="dispatch",
    )(*tables, hn_p, hn_s, rt_p, rt_s)


def _moe_kernel(te_ref, nu_ref, par_ref, nxt_ref, nxt2_ref, ngrp_ref, xs_ref, wg_hbm, wu_hbm, wd_hbm, ys_ref,
                wg_f32, wu_f32, wd_f32, wg_bf, wu_bf, wd_bf, wsem):
    i = pl.program_id(0)
    used = i < nu_ref[0]

    def store_zeros(ref, n_rows):
        ij = (lax.broadcasted_iota(I32, (n_rows, D_MODEL), 0)
              + lax.broadcasted_iota(I32, (n_rows, D_MODEL), 1))
        _store_packed(ref, jnp.where(ij < jnp.minimum(nu_ref[0], 0), 1.0, 0.0), n_rows)

    def weight_copies(expert, buf):
        return [pltpu.make_async_copy(hbm.at[expert], vmem.at[buf], wsem.at[buf])
                for hbm, vmem in ((wg_hbm, wg_f32), (wu_hbm, wu_f32), (wd_hbm, wd_f32))]

    @pl.when(used)
    def _():
        prev = te_ref[jnp.maximum(i - 1, 0)]

        @pl.when((i == 0) | (te_ref[i] != prev))
        def _():
            buf = par_ref[i]

            @pl.when(i == 0)
            def _():
                for cp in weight_copies(te_ref[0], 0):
                    cp.start()

                @pl.when(nxt_ref[0] >= 0)
                def _():
                    for cp in weight_copies(nxt_ref[0], 1):
                        cp.start()

            for cp in weight_copies(te_ref[i], buf):
                cp.wait()
            wg_bf[...] = wg_f32[buf].astype(BF16)
            wu_bf[...] = wu_f32[buf].astype(BF16)
            wd_bf[...] = wd_f32[buf].astype(BF16)

            @pl.when(nxt2_ref[i] >= 0)
            def _():
                for cp in weight_copies(nxt2_ref[i], (buf + 2) % WEIGHT_BUFS):
                    cp.start()

        def expert_rows(n_rows):
            lo, hi = _load_packed(xs_ref, n_rows)
            x = jnp.concatenate([lo, hi], axis=-1).astype(BF16)
            g = jnp.dot(x, wg_bf[...], preferred_element_type=F32)
            u = jnp.dot(x, wu_bf[...], preferred_element_type=F32)
            hh = (g * jax.nn.sigmoid(g)) * u
            y = jnp.dot(hh.astype(BF16), wd_bf[...], preferred_element_type=F32)
            _store_packed(ys_ref, y, n_rows)

        for n_groups in range(1, MOE_TILE // ROW_GROUP + 1):
            @pl.when(ngrp_ref[i] == n_groups)
            def _(n_rows=n_groups * ROW_GROUP):
                expert_rows(n_rows)
                if n_rows < MOE_TILE:
                    rest = MOE_TILE - n_rows
                    store_zeros(ys_ref.at[pl.ds(n_rows * ROW_CHUNKS, rest * ROW_CHUNKS)], rest)

    @pl.when(jnp.logical_not(used))
    def _():
        store_zeros(ys_ref, MOE_TILE)


def _moe_call(tile_expert, n_used, weight_buf, next_expert, next2_expert, n_groups_used, xs, n_tiles,
              w_eg, w_eu, w_ed):
    blk = MOE_TILE * ROW_CHUNKS
    any_spec = pl.BlockSpec(memory_space=pl.ANY)
    return pl.pallas_call(
        _moe_kernel,
        grid_spec=pltpu.PrefetchScalarGridSpec(
            num_scalar_prefetch=6, grid=(n_tiles,),
            in_specs=[
                pl.BlockSpec((blk, LANES), lambda i, te, nu, *_: (jnp.minimum(i, nu[0] - 1), 0)),
                any_spec, any_spec, any_spec,
            ],
            out_specs=pl.BlockSpec((blk, LANES), lambda i, *_: (i, 0)),
            scratch_shapes=[pltpu.VMEM((WEIGHT_BUFS, D_MODEL, D_EXPERT), F32),
                            pltpu.VMEM((WEIGHT_BUFS, D_MODEL, D_EXPERT), F32),
                            pltpu.VMEM((WEIGHT_BUFS, D_EXPERT, D_MODEL), F32),
                            pltpu.VMEM((D_MODEL, D_EXPERT), BF16),
                            pltpu.VMEM((D_MODEL, D_EXPERT), BF16),
                            pltpu.VMEM((D_EXPERT, D_MODEL), BF16),
                            pltpu.SemaphoreType.DMA((WEIGHT_BUFS,))]),
        out_shape=jax.ShapeDtypeStruct((n_tiles * blk, LANES), U32),
        compiler_params=pltpu.CompilerParams(
            dimension_semantics=("arbitrary",), vmem_limit_bytes=VMEM_LIMIT),
        name="experts",
    )(tile_expert, n_used, weight_buf, next_expert, next2_expert, n_groups_used, xs, w_eg, w_eu, w_ed)


def _final_kernel(l8, g8, l4, g4, l2, g2, cnt_ref, trow_ref,
                  h_ref, rt_ref, p_ref, ys_ref, wple_ref, wgate_ref, gple_ref, gfin_ref, out_ref,
                  ybuf, yb_s, perm_s, sem, *, tile_base, n_tiles):
    s = pl.program_id(0)
    a_slot = s % 2
    b_slot = 1 - a_slot
    local_tabs, sorted_tabs = (l8, l4, l2), (g8, g4, g2)

    def run_copy(buf_slot, sz, n_rows, k):
        src = ys_ref.at[_rows(sorted_tabs[sz][k], n_rows)]
        dst = ybuf.at[buf_slot, _rows(local_tabs[sz][k], n_rows)]
        return pltpu.make_async_copy(src, dst, sem.at[buf_slot])

    def issue(tile, buf_slot, active):
        _for_each_copy(tile_base + jnp.minimum(tile, n_tiles - 1), cnt_ref,
                       lambda sz, n_rows, k: run_copy(buf_slot, sz, n_rows, k).start(), active)

    @pl.when(s == 0)
    def _():
        ybuf[...] = jnp.zeros(ybuf.shape, U32)
        yb_s[...] = jnp.zeros(yb_s.shape, BF16)
        perm_s[...] = jnp.zeros(perm_s.shape, BF16)
        issue(0, 0, True)

    issue(s + 1, b_slot, s + 1 < n_tiles)

    def make_wait(n):
        return pltpu.make_async_copy(ys_ref.at[pl.ds(0, n * ROW_CHUNKS)],
                                     ybuf.at[a_slot, pl.ds(0, n * ROW_CHUNKS)], sem.at[a_slot])
    a_rows = jnp.where(s < n_tiles, trow_ref[tile_base + jnp.minimum(s, n_tiles - 1)], 0)
    _wait_rows(a_rows, make_wait)

    h = h_ref[...]
    h = h + jnp.dot(perm_s[b_slot], yb_s[b_slot], preferred_element_type=F32)
    r = _rms(h, gple_ref[...])
    gate = jax.nn.sigmoid(jnp.dot(r.astype(BF16), wgate_ref[...], preferred_element_type=F32))
    h = h + jnp.dot(p_ref[...].astype(BF16), wple_ref[...], preferred_element_type=F32) * gate
    out_ref[...] = _rms(h, gfin_ref[...])

    rt = rt_ref[...]
    lpos1, lpos2 = rt[:, 2:3], rt[:, 3:4]
    lane = lax.broadcasted_iota(I32, (TOK_TILE, LOC_ROWS), 1).astype(F32)
    perm_s[a_slot] = jnp.where((lane == lpos1) | (lane == lpos2), 1.0, 0.0).astype(BF16)
    row1, row2, w1_row, w2_row = _as_rows(lpos1, lpos2, rt[:, 4:5], rt[:, 5:6])
    grow = lax.broadcasted_iota(I32, (LOC_ROWS, TOK_TILE), 0).astype(F32)
    wrow = jnp.sum(jnp.where(grow == row1, w1_row, 0.0) + jnp.where(grow == row2, w2_row, 0.0),
                   axis=-1, keepdims=True)
    lo, hi = _load_packed(ybuf.at[a_slot], LOC_ROWS)
    yb_s[a_slot] = (jnp.concatenate([lo, hi], axis=-1) * wrow).astype(BF16)


def _final_call(tables, h, rt, p, ys, wple, wgate, gple, gfin, *, tile_base):
    n_rows = h.shape[0]
    n_tiles = n_rows // TOK_TILE
    b_map = lambda s, *_: (jnp.maximum(s - 1, 0), 0)
    a_map = lambda s, *_: (jnp.minimum(s, n_tiles - 1), 0)
    cmap = lambda s, *_: (0, 0)
    return pl.pallas_call(
        functools.partial(_final_kernel, tile_base=tile_base, n_tiles=n_tiles),
        grid_spec=pltpu.PrefetchScalarGridSpec(
            num_scalar_prefetch=len(tables), grid=(n_tiles + 1,),
            in_specs=[
                pl.BlockSpec((TOK_TILE, D_MODEL), b_map),
                pl.BlockSpec((TOK_TILE, SUBLANES), a_map),
                pl.BlockSpec((TOK_TILE, PLE_DIM), b_map),
                pl.BlockSpec(memory_space=pl.ANY),
                pl.BlockSpec(wple.shape, cmap),
                pl.BlockSpec(wgate.shape, cmap),
                pl.BlockSpec(gple.shape, cmap),
                pl.BlockSpec(gfin.shape, cmap),
            ],
            out_specs=pl.BlockSpec((TOK_TILE, D_MODEL), b_map),
            scratch_shapes=[pltpu.VMEM((2, LOC_ROWS * ROW_CHUNKS, LANES), U32),
                            pltpu.VMEM((2, LOC_ROWS, D_MODEL), BF16),
                            pltpu.VMEM((2, TOK_TILE, LOC_ROWS), BF16),
                            pltpu.SemaphoreType.DMA((2,))]),
        out_shape=jax.ShapeDtypeStruct((n_rows, D_MODEL), F32),
        compiler_params=pltpu.CompilerParams(
            dimension_semantics=("arbitrary",), vmem_limit_bytes=VMEM_LIMIT),
        name="final",
    )(*tables, h, rt, p, ys, wple, wgate, gple, gfin)


def _ceil_to(x, m):
    return ((x + m - 1) // m) * m


def _excl_cumsum(a, axis):
    return jnp.cumsum(a, axis=axis) - a


def _copy_tables(n_copies, first_local, first_sorted, step, cap):
    ends = jnp.cumsum(n_copies, axis=1)
    j = jnp.arange(cap, dtype=I32)[None, :, None]
    e_of = jnp.minimum(jnp.sum(ends[:, None, :] <= j, axis=-1), N_EXPERTS - 1)
    pick = e_of[:, :, None] == jnp.arange(N_EXPERTS, dtype=I32)[None, None, :]
    take = lambda a: jnp.sum(jnp.where(pick, a[:, None, :], 0), axis=-1)
    c = jnp.arange(cap, dtype=I32)[None, :] - (take(ends) - take(n_copies))
    flat = lambda a: a.reshape(-1).astype(I32)
    return flat(take(first_local) + step * c), flat(take(first_sorted) + step * c), ends[:, -1]


def kernel(x_prompt, x_sample, state_pool, p_prompt, p_sample, g_mix, w_in, w_pool, pool_scale,
           g_v, b_v, w_spatial, b_spatial, w_out, g_ffn, w_rg, b_rg, w_re, b_re, w_eg, w_eu, w_ed,
           w_ple, g_ple, w_ple_gate, g_final):
    batch, seq, _ = x_prompt.shape
    dec_batch, dec_seq, _ = x_sample.shape
    assert dec_seq == SUBLANES and g_mix.shape[0] == 1
    n_p = batch * seq
    n_s = dec_batch * dec_seq
    n_tok = n_p + n_s
    past_len = 16384

    row = lambda a: a.reshape(1, -1).astype(F32)
    wr = jnp.concatenate(
        [w_re[0], w_rg[0], jnp.zeros((D_MODEL, LANES - N_EXPERTS - N_EXPERT_GROUPS), F32)], axis=1)
    br = jnp.concatenate(
        [b_re[0], b_rg[0], jnp.zeros((LANES - N_EXPERTS - N_EXPERT_GROUPS,), F32)]).reshape(1, LANES)
    bsp_p = jnp.repeat(jnp.transpose(b_spatial[0]), GMLP_HEAD, axis=1)
    bsp_s = jnp.tile(bsp_p[:dec_seq], (CHUNK // dec_seq, 1))
    wsp_p = w_spatial[0].astype(BF16)
    wsp_s = jnp.tile(w_spatial[0][:, :dec_seq, :dec_seq].astype(BF16),
                     (1, CHUNK // dec_seq, CHUNK // dec_seq))

    def pool_band(first_chunk):
        t = jnp.arange(CHUNK, dtype=I32)[:, None]
        k = jnp.arange(POOL_SPAN, dtype=I32)[None, :]
        mats = []
        for w in POOL_WINDOWS:
            cnt = jnp.minimum(w, t + 1) if first_chunk else jnp.full_like(t, w)
            in_window = (k >= t + 17 - w) & (k <= t + 16)
            mats.append(jnp.where(in_window, 1.0 / cnt.astype(F32), 0.0) - (k == t + 16).astype(F32))
        return jnp.stack(mats).astype(BF16)

    def mixer_weights(wsp, bsp):
        return (row(g_mix[0]), w_in[0].astype(BF16), w_pool[0].astype(BF16), row(pool_scale[0]),
                row(g_v[0]), row(b_v[0]), wsp, bsp, w_out[0].astype(BF16), row(g_ffn[0]),
                jnp.transpose(wr).astype(BF16), br.reshape(LANES, 1), pool_band(False), pool_band(True))

    h_p, hn_p, rt_p, cnt_p, hist_p = _mixer_call(
        x_prompt, None, mixer_weights(wsp_p, bsp_p), sample=False, pos0=0)
    h_s, hn_s, rt_s, cnt_s, hist_s, v_s = _mixer_call(
        x_sample.reshape(n_s, D_MODEL), jnp.transpose(state_pool[0], (1, 0, 2)),
        mixer_weights(wsp_s, bsp_s),
        sample=True, pos0=past_len)

    n_run = jnp.concatenate([cnt_p, cnt_s], axis=0)[:, 0, :N_EXPERTS].astype(I32)
    n_even = _ceil_to(n_run, RUN_ALIGN)
    l_start = _excl_cumsum(n_even, 1)
    tile_rows = jnp.sum(n_even, axis=1)
    base = _excl_cumsum(n_even, 0)
    counts = jnp.sum(n_even, axis=0)
    padded = _ceil_to(counts, MOE_TILE)
    ends = jnp.cumsum(padded)
    offs = ends - padded
    g_start = offs[None, :] + base
    n8 = n_even // 8
    f4 = (n_even // 4) % 2
    f2 = (n_even // 2) % 2
    l8, g8, c8 = _copy_tables(n8, l_start, g_start, 8, COPY_CAP[0])
    l4, g4, c4 = _copy_tables(f4, l_start + 8 * n8, g_start + 8 * n8, 0, COPY_CAP[1])
    l2, g2, c2 = _copy_tables(f2, l_start + 8 * n8 + 4 * f4, g_start + 8 * n8 + 4 * f4, 0, COPY_CAP[2])
    copy_cnt = jnp.stack([c8, c4, c2], axis=1).reshape(-1).astype(I32)
    tile_rows = tile_rows.astype(I32)
    z_start = (offs + counts).astype(I32)
    z_big = ((padded - counts) // ZERO_ROWS).astype(I32)
    z_blocks = (((padded - counts) % ZERO_ROWS + 7) // 8).astype(I32)

    max_rows = 2 * n_tok + n_run.shape[0] * N_EXPERTS * (RUN_ALIGN - 1)
    n_tiles = max_rows // MOE_TILE + N_EXPERTS + 1
    n_used = (ends[-1] // MOE_TILE).astype(I32).reshape(1)
    tile_id = jnp.minimum(jnp.arange(n_tiles, dtype=I32), n_used - 1)
    tile_expert = jnp.sum((tile_id[:, None] * MOE_TILE) >= ends[None, :], axis=1).astype(I32)
    tile_expert = jnp.minimum(tile_expert, N_EXPERTS - 1)
    pick_e = tile_expert[:, None] == jnp.arange(N_EXPERTS, dtype=I32)[None, :]
    data_end = jnp.sum(jnp.where(pick_e, (offs + counts)[None, :], 0), axis=1)
    n_groups_used = jnp.clip((data_end - tile_id * MOE_TILE + ROW_GROUP - 1) // ROW_GROUP,
                             1, MOE_TILE // ROW_GROUP).astype(I32)
    e_ids = jnp.arange(N_EXPERTS, dtype=I32)
    nonempty = padded > 0
    seg_index = jnp.cumsum(nonempty.astype(I32)) - 1
    later = nonempty[None, :] & (e_ids[None, :] > e_ids[:, None])
    next_of = jnp.min(jnp.where(later, e_ids[None, :], N_EXPERTS), axis=1)
    next_of = jnp.where(next_of == N_EXPERTS, -1, next_of)
    next2_of = jnp.sum(jnp.where(next_of[:, None] == e_ids[None, :], next_of[None, :], 0), axis=1)
    next2_of = jnp.where(next_of >= 0, next2_of, -1)
    weight_buf = (jnp.sum(jnp.where(pick_e, seg_index[None, :], 0), axis=1) % WEIGHT_BUFS).astype(I32)
    next_expert = jnp.sum(jnp.where(pick_e, next_of[None, :], 0), axis=1).astype(I32)
    next2_expert = jnp.sum(jnp.where(pick_e, next2_of[None, :], 0), axis=1).astype(I32)

    copy_tabs = (l8, g8, l4, g4, l2, g2, copy_cnt, tile_rows)
    xs = _dispatch_call(copy_tabs + (z_start, z_big, z_blocks, n_used), hn_p, hn_s, rt_p, rt_s,
                        n_tiles * MOE_TILE)
    ys = _moe_call(tile_expert, n_used, weight_buf, next_expert, next2_expert, n_groups_used, xs, n_tiles,
                   w_eg[0], w_eu[0], w_ed[0])
    fin_w = (w_ple[0].astype(BF16), w_ple_gate[0].astype(BF16), row(g_ple[0]), row(g_final))
    y_p = _final_call(copy_tabs, h_p, rt_p, p_prompt[0].reshape(n_p, PLE_DIM), ys, *fin_w, tile_base=0)
    y_s = _final_call(copy_tabs, h_s, rt_s, p_sample[0].reshape(n_s, PLE_DIM), ys, *fin_w,
                      tile_base=n_p // TOK_TILE)

    return (y_p.reshape(batch, seq, D_MODEL),
            y_s.reshape(dec_batch, dec_seq, D_MODEL),
            hist_p[None],
            jnp.transpose(hist_s, (1, 0, 2))[None],
            v_s.reshape(1, dec_batch, dec_seq, GMLP_WIDTH))
```

```python
import functools

import jax
import jax.numpy as jnp
from jax import lax
from jax.experimental import pallas as pl
from jax.experimental.pallas import tpu as pltpu

D_MODEL = 1024
POOL_WIDTH = 512
GMLP_WIDTH = 512
IN_WIDTH = POOL_WIDTH + 2 * GMLP_WIDTH
POOL_WINDOWS = (2, 4, 8, 16)
POOL_GROUP = 128
POOL_HIST = 15
CHUNK = 128
POOL_SPAN = CHUNK + 16
N_GMLP_HEADS = 4
GMLP_HEAD = 128
N_EXPERT_GROUPS = 4
EXPERTS_PER_GROUP = 8
N_EXPERTS = 32
D_EXPERT = 512
PLE_DIM = 256
EPS = 1e-6

LANES = 128
SUBLANES = 8
HALF = D_MODEL // 2
ROW_CHUNKS = HALF // LANES

TOK_TILE = 256
STEP_TILES = 2
MIX_TILE = 1024
MOE_TILE = 640
ROW_GROUP = 128
WEIGHT_BUFS = 3
X_BUFS = 3
RUN_ALIGN = 2
COPY_ROWS = (8, 4, 2)
ZERO_ROWS = 64
LOC_ROWS = 2 * TOK_TILE + N_EXPERTS * (RUN_ALIGN - 1)
COPY_CAP = (LOC_ROWS // 8 + 1, N_EXPERTS, N_EXPERTS)
VMEM_LIMIT = 56 * 1024 * 1024

F32 = jnp.float32
BF16 = jnp.bfloat16
I32 = jnp.int32
U32 = jnp.uint32


def _rms(x, g):
    return x * lax.rsqrt(jnp.mean(x * x, axis=-1, keepdims=True) + EPS) * g


def _gelu_tanh(x):
    inner = x * (0.7978845608028654 + (0.7978845608028654 * 0.044715) * (x * x))
    half = 0.5 * x
    return half + half * jnp.tanh(inner)


def _store_packed(ref, val, n_rows):
    packed = pltpu.pack_elementwise([val[:, :HALF], val[:, HALF:]], packed_dtype=BF16)
    for k in range(ROW_CHUNKS):
        ref[pl.ds(k, n_rows, stride=ROW_CHUNKS), :] = packed[:, k * LANES:(k + 1) * LANES]


def _load_packed(ref, n_rows):
    packed = jnp.concatenate(
        [ref[pl.ds(k, n_rows, stride=ROW_CHUNKS), :] for k in range(ROW_CHUNKS)], axis=-1)
    lo = pltpu.unpack_elementwise(packed, index=0, packed_dtype=BF16, unpacked_dtype=F32)
    hi = pltpu.unpack_elementwise(packed, index=1, packed_dtype=BF16, unpacked_dtype=F32)
    return lo, hi


def _rows(start, n):
    return pl.ds(pl.multiple_of(start * ROW_CHUNKS, SUBLANES), n * ROW_CHUNKS)


def _route(hn, wrt_ref, brc_ref, n_tok):
    logits = lax.dot_general(wrt_ref[...], hn, (((1,), (1,)), ((), ())),
                             preferred_element_type=F32) + brc_ref[...]
    neg = jnp.float32(-jnp.inf)
    big = jnp.float32(1e9)
    sub = lax.broadcasted_iota(I32, (SUBLANES, n_tok), 0).astype(F32)
    gl = jnp.where(sub < N_EXPERT_GROUPS, logits[N_EXPERTS:N_EXPERTS + SUBLANES, :], neg)
    gmax = jnp.max(gl, axis=0, keepdims=True)
    gidx = jnp.min(jnp.where(gl == gmax, sub, big), axis=0, keepdims=True)
    gprob = 1.0 / jnp.sum(jnp.exp(gl - gmax), axis=0, keepdims=True)
    assert EXPERTS_PER_GROUP == SUBLANES
    el = logits[0:SUBLANES, :]
    for g in range(1, N_EXPERT_GROUPS):
        el = jnp.where(gidx == g, logits[g * SUBLANES:(g + 1) * SUBLANES, :], el)
    m1 = jnp.max(el, axis=0, keepdims=True)
    o1 = jnp.min(jnp.where(el == m1, sub, big), axis=0, keepdims=True)
    el2 = jnp.where(sub == o1, neg, el)
    m2 = jnp.max(el2, axis=0, keepdims=True)
    o2 = jnp.min(jnp.where(el2 == m2, sub, big), axis=0, keepdims=True)
    i1 = gidx * EXPERTS_PER_GROUP + o1
    i2 = gidx * EXPERTS_PER_GROUP + o2
    t = jnp.exp(m2 - m1)
    w1 = gprob / (1.0 + t)
    w2 = gprob * t / (1.0 + t)

    erow = lax.broadcasted_iota(I32, (LANES, n_tok), 0).astype(F32)
    sel1 = erow == i1
    sel2 = erow == i2
    oh = jnp.where(sel1 | sel2, 1.0, 0.0).astype(BF16)
    r = lax.broadcasted_iota(I32, (n_tok, n_tok), 0)
    c = lax.broadcasted_iota(I32, (n_tok, n_tok), 1)
    earlier = jnp.where(r < c, 1.0, 0.0).astype(BF16)
    before = jnp.dot(oh, earlier, preferred_element_type=F32)
    counts_b = jnp.dot(oh, jnp.ones((n_tok, LANES), BF16), preferred_element_type=F32)
    assert RUN_ALIGN == 2
    c_even = counts_b + (counts_b - 2.0 * jnp.floor(counts_b * 0.5))
    c_hi = jnp.floor(c_even * (1.0 / 256.0))
    er = lax.broadcasted_iota(I32, (LANES, LANES), 0)
    ec = lax.broadcasted_iota(I32, (LANES, LANES), 1)
    lower = jnp.where(ec < er, 1.0, 0.0).astype(BF16)
    pieces = jnp.concatenate([c_hi, c_even - 256.0 * c_hi], axis=1).astype(BF16)
    prefix = jnp.dot(lower, pieces, preferred_element_type=F32)
    lstart = 256.0 * prefix[:, :LANES] + prefix[:, LANES:]
    placed = before + jnp.concatenate([lstart] * (n_tok // LANES), axis=1)
    lpos1 = jnp.sum(jnp.where(sel1, placed, 0.0), axis=0, keepdims=True)
    lpos2 = jnp.sum(jnp.where(sel2, placed, 0.0), axis=0, keepdims=True)

    stacked = jnp.where(erow == 0.0, i1, 0.0)
    for k, val in enumerate((i2, lpos1, lpos2, w1, w2), start=1):
        stacked = jnp.where(erow == float(k), val, stacked)
    return stacked.T[:, :SUBLANES], counts_b.T[0:1, :]


def _mixer_kernel(*refs, n_tok, sample, pos0):
    if sample:
        (x_ref, hist_ref, gmix_ref, win_ref, wpool_ref, pscale_ref, gv_ref, bv_ref, wsp_ref,
         bsp_ref, wout_ref, gffn_ref, wr_ref, br_ref, band_ref, band0_ref,
         h_ref, hn_ref, rt_ref, cnt_ref, hist_out_ref, v_ref, ext_scr, pool_scr) = refs
    else:
        (x_ref, gmix_ref, win_ref, wpool_ref, pscale_ref, gv_ref, bv_ref, wsp_ref,
         bsp_ref, wout_ref, gffn_ref, wr_ref, br_ref, band_ref, band0_ref,
         h_ref, hn_ref, rt_ref, cnt_ref, hist_out_ref, ext_scr) = refs

    x = x_ref[...]
    xn = _rms(x, gmix_ref[...])
    z = jnp.dot(xn.astype(BF16), win_ref[...], preferred_element_type=F32)
    a = z[:, :POOL_WIDTH]
    uv = _gelu_tanh(z[:, POOL_WIDTH:])
    u = uv[:, :GMLP_WIDTH]
    v = uv[:, GMLP_WIDTH:]
    mu = jnp.mean(v, axis=-1, keepdims=True)
    vc = v - mu
    v = vc * lax.rsqrt(jnp.mean(vc * vc, axis=-1, keepdims=True) + EPS) * gv_ref[...] + bv_ref[...]

    pooled = []
    if sample:
        n_seq = n_tok // SUBLANES
        n_grp = len(POOL_WINDOWS)
        gcols = [slice(gi * POOL_GROUP, (gi + 1) * POOL_GROUP) for gi in range(n_grp)]
        ext_scr[1:1 + POOL_HIST] = hist_ref[...]
        for gi in range(n_grp):
            pool_scr[gi] = a[:, gcols[gi]]
        for i in range(SUBLANES):
            ext_scr[1 + POOL_HIST + i] = jnp.concatenate(
                [pool_scr[gi, pl.ds(i, n_seq, stride=SUBLANES), :] for gi in range(n_grp)], axis=-1)
        hist_out_ref[...] = ext_scr[1 + SUBLANES:]
        for i in range(SUBLANES):
            back = [ext_scr[1 + POOL_HIST + i - j] for j in range(max(POOL_WINDOWS))]
            for gi, w in enumerate(POOL_WINDOWS):
                win = back[0][:, gcols[gi]]
                for j in range(1, w):
                    win = win + back[j][:, gcols[gi]]
                pool_scr[gi, pl.ds(i, n_seq, stride=SUBLANES), :] = (
                    win * (1.0 / min(w, pos0 + i + 1)) - back[0][:, gcols[gi]])
        pooled = [pool_scr[gi] for gi in range(n_grp)]
    else:
        l = pl.program_id(1)

        @pl.when(l == 0)
        def _():
            ext_scr[0:16, :] = jnp.zeros((16, POOL_WIDTH), F32)

        ext_scr[16:16 + n_tok, :] = a
        chunks = []
        for ci in range(n_tok // CHUNK):
            ext_c = ext_scr[ci * CHUNK:ci * CHUNK + POOL_SPAN, :].astype(BF16)
            groups = []
            for gi in range(len(POOL_WINDOWS)):
                band = band_ref[gi]
                if ci == 0:
                    band = jnp.where(l == 0, band0_ref[gi], band)
                groups.append(jnp.dot(band, ext_c[:, gi * POOL_GROUP:(gi + 1) * POOL_GROUP],
                                      preferred_element_type=F32))
            chunks.append(groups)
        pooled = [jnp.concatenate([chunks[ci][gi] for ci in range(n_tok // CHUNK)], axis=0)
                  for gi in range(len(POOL_WINDOWS))]
        tail = ext_scr[n_tok + 1:n_tok + 16, :]
        hist_out_ref[...] = tail
        ext_scr[1:16, :] = tail

    pool_out = jnp.concatenate(
        [jnp.dot(pooled[gi].astype(BF16), wpool_ref[gi], preferred_element_type=F32)
         for gi in range(len(POOL_WINDOWS))], axis=-1) * pscale_ref[...]

    blk = SUBLANES if sample else CHUNK
    tr = lax.broadcasted_iota(I32, (CHUNK, CHUNK), 0)
    sc = lax.broadcasted_iota(I32, (CHUNK, CHUNK), 1)
    mask = (sc <= tr) & ((sc // blk) == (tr // blk))
    wsp = [jnp.where(mask, wsp_ref[hh], jnp.zeros((), BF16)) for hh in range(N_GMLP_HEADS)]
    v_bf = v.astype(BF16)
    mixed_chunks = []
    for ci in range(n_tok // CHUNK):
        rows = slice(ci * CHUNK, (ci + 1) * CHUNK)
        heads = [jnp.dot(wsp[hh], v_bf[rows, hh * GMLP_HEAD:(hh + 1) * GMLP_HEAD],
                         preferred_element_type=F32) for hh in range(N_GMLP_HEADS)]
        mixed_chunks.append(jnp.concatenate(heads, axis=-1) + bsp_ref[...])
    mixed = jnp.concatenate(mixed_chunks, axis=0)
    gmlp_out = u * mixed

    cat = jnp.concatenate([pool_out, gmlp_out], axis=-1)
    h = x + jnp.dot(cat.astype(BF16), wout_ref[...], preferred_element_type=F32)
    hn = _rms(h, gffn_ref[...]).astype(BF16)

    h_ref[...] = h
    hn_ref[...] = hn
    if sample:
        v_ref[...] = v
    for s in range(n_tok // TOK_TILE):
        rows = slice(s * TOK_TILE, (s + 1) * TOK_TILE)
        rt, counts = _route(hn[rows], wr_ref, br_ref, TOK_TILE)
        rt_ref[rows, :] = rt
        cnt_ref[s] = counts


def _full(shape):
    return pl.BlockSpec(shape, lambda *_: (0,) * len(shape))


def _mixer_call(x, hist, weights, *, sample, pos0):
    w_specs = [_full(w.shape) for w in weights]
    n_tok = MIX_TILE
    if sample:
        n_rows = x.shape[0]
        n_seq = n_tok // SUBLANES
        grid = (n_rows // n_tok,)
        tok_map = lambda i: (i, 0)
        tile_map = lambda i: (i, 0, 0)
        hist_spec = pl.BlockSpec((POOL_HIST, n_seq, POOL_WIDTH), lambda i: (0, i, 0))
        in_specs = [pl.BlockSpec((n_tok, D_MODEL), tok_map), hist_spec] + w_specs
        hist_shape = jax.ShapeDtypeStruct(hist.shape, F32)
        scratch = [pltpu.VMEM((1 + POOL_HIST + SUBLANES, n_seq, POOL_WIDTH), F32),
                   pltpu.VMEM((len(POOL_WINDOWS), n_tok, POOL_GROUP), F32)]
        args = (x, hist)
    else:
        b, seq, _ = x.shape
        n_rows = b * seq
        n_l = seq // n_tok
        grid = (b, n_l)
        tok_map = lambda bi, li: (bi * n_l + li, 0)
        tile_map = lambda bi, li: (bi * n_l + li, 0, 0)
        in_specs = [pl.BlockSpec((None, n_tok, D_MODEL), lambda bi, li: (bi, li, 0))] + w_specs
        hist_shape = jax.ShapeDtypeStruct((b, POOL_HIST, POOL_WIDTH), F32)
        hist_spec = pl.BlockSpec((None, POOL_HIST, POOL_WIDTH), lambda bi, li: (bi, 0, 0))
        scratch = [pltpu.VMEM((16 + n_tok, POOL_WIDTH), F32)]
        args = (x,)
    out_shape = [jax.ShapeDtypeStruct((n_rows, D_MODEL), F32),
                 jax.ShapeDtypeStruct((n_rows, D_MODEL), BF16),
                 jax.ShapeDtypeStruct((n_rows, SUBLANES), F32),
                 jax.ShapeDtypeStruct((n_rows // TOK_TILE, 1, LANES), F32),
                 hist_shape]
    out_specs = [pl.BlockSpec((n_tok, D_MODEL), tok_map),
                 pl.BlockSpec((n_tok, D_MODEL), tok_map),
                 pl.BlockSpec((n_tok, SUBLANES), tok_map),
                 pl.BlockSpec((n_tok // TOK_TILE, 1, LANES), tile_map),
                 hist_spec]
    if sample:
        out_shape.append(jax.ShapeDtypeStruct((n_rows, GMLP_WIDTH), F32))
        out_specs.append(pl.BlockSpec((n_tok, GMLP_WIDTH), tok_map))
    return pl.pallas_call(
        functools.partial(_mixer_kernel, n_tok=n_tok, sample=sample, pos0=pos0),
        grid=grid, in_specs=in_specs, out_specs=out_specs, out_shape=out_shape,
        scratch_shapes=scratch,
        compiler_params=pltpu.CompilerParams(
            dimension_semantics=("arbitrary",) * len(grid), vmem_limit_bytes=VMEM_LIMIT),
        name="mixer_sample" if sample else "mixer_prompt",
    )(*args, *weights)


def _for_each_copy(tile, cnt_ref, fn, active=True):
    for s, n_rows in enumerate(COPY_ROWS):
        def body(j, carry, s=s, n_rows=n_rows):
            fn(s, n_rows, tile * COPY_CAP[s] + j)
            return carry
        lax.fori_loop(0, jnp.where(active, cnt_ref[tile * len(COPY_ROWS) + s], 0), body, 0)


def _wait_rows(n_rows_total, make_wait):
    units = n_rows_total // RUN_ALIGN
    bit = 0
    while (RUN_ALIGN << bit) <= LOC_ROWS:
        @pl.when((units >> bit) & 1 == 1)
        def _(bit=bit):
            make_wait(RUN_ALIGN << bit).wait()
        bit += 1


def _as_rows(*cols):
    n = cols[0].shape[0]
    lane = lax.broadcasted_iota(I32, (n, LANES), 1)
    packed = jnp.zeros((n, LANES), F32)
    for k, col in enumerate(cols):
        packed = jnp.where(lane == k, col, packed)
    rows = packed.T
    return [rows[k:k + 1, :] for k in range(len(cols))]


def _dispatch_kernel(l8, g8, l4, g4, l2, g2, cnt_ref, trow_ref, zs_ref, zbig_ref, zb_ref, nu_ref,
                     hnp_hbm, hns_hbm, rtp_ref, rts_ref, xs_ref, hbuf, loc, zbuf, hsem, sem, zsem,
                     *, n_prompt_steps):
    i = pl.program_id(0)
    n_steps = pl.num_programs(0)
    slot = i % 2

    def hn_start(tile):
        dst, dsem = hbuf.at[tile % X_BUFS], hsem.at[tile % X_BUFS]

        @pl.when(tile < n_prompt_steps)
        def _():
            pltpu.make_async_copy(hnp_hbm.at[pl.ds(pl.multiple_of(tile * TOK_TILE, TOK_TILE), TOK_TILE)],
                                  dst, dsem).start()

        @pl.when(tile >= n_prompt_steps)
        def _():
            row0 = pl.multiple_of((tile - n_prompt_steps) * TOK_TILE, TOK_TILE)
            pltpu.make_async_copy(hns_hbm.at[pl.ds(row0, TOK_TILE)], dst, dsem).start()

    @pl.when(i == 0)
    def _():
        for ahead in range(X_BUFS - 1):
            hn_start(jnp.int32(ahead))

    @pl.when(i + (X_BUFS - 1) < n_steps)
    def _():
        hn_start(i + (X_BUFS - 1))
    local_tabs, sorted_tabs = (l8, l4, l2), (g8, g4, g2)

    @pl.when(i == 0)
    def _():
        zbuf[...] = jnp.zeros(zbuf.shape, U32)
        n_tiles = xs_ref.shape[0] // (MOE_TILE * ROW_CHUNKS)

        def zero_copy(e, c, n_rows, first):
            return pltpu.make_async_copy(zbuf.at[pl.ds(0, n_rows * ROW_CHUNKS)],
                                         xs_ref.at[_rows(first + c * n_rows, n_rows)], zsem)

        def tail_copy(t):
            return pltpu.make_async_copy(zbuf, xs_ref.at[_rows(t * MOE_TILE, MOE_TILE)], zsem)

        def per_expert(e, totals):
            def big(c, carry):
                zero_copy(e, c, ZERO_ROWS, zs_ref[e]).start()
                return carry

            def small(c, carry):
                zero_copy(e, c, 8, zs_ref[e] + zbig_ref[e] * ZERO_ROWS).start()
                return carry

            lax.fori_loop(0, zbig_ref[e], big, 0)
            lax.fori_loop(0, zb_ref[e], small, 0)
            return totals[0] + zbig_ref[e], totals[1] + zb_ref[e]

        n_big, n_small = lax.fori_loop(0, N_EXPERTS, per_expert, (0, 0))

        def wait_big(_, carry):
            zero_copy(0, 0, ZERO_ROWS, 0).wait()
            return carry

        def wait_small(_, carry):
            zero_copy(0, 0, 8, 0).wait()
            return carry

        lax.fori_loop(0, n_big, wait_big, 0)
        lax.fori_loop(0, n_small, wait_small, 0)

        def tail_start(t, carry):
            tail_copy(t).start()
            return carry

        def tail_wait(t, carry):
            tail_copy(t).wait()
            return carry

        spill = pltpu.make_async_copy(zbuf.at[pl.ds(0, 8 * ROW_CHUNKS)],
                                      xs_ref.at[_rows(n_tiles * MOE_TILE, 8)], zsem)
        lax.fori_loop(nu_ref[0], n_tiles, tail_start, 0)
        spill.start()
        lax.fori_loop(nu_ref[0], n_tiles, tail_wait, 0)
        spill.wait()

    is_prompt = i < n_prompt_steps
    rt = jnp.where(is_prompt, rtp_ref[...], rts_ref[...])
    row1, row2 = _as_rows(rt[:, 2:3], rt[:, 3:4])
    grow = lax.broadcasted_iota(I32, (LOC_ROWS, TOK_TILE), 0).astype(F32)
    perm = jnp.where((grow == row1) | (grow == row2), 1.0, 0.0).astype(BF16)
    pltpu.make_async_copy(hnp_hbm.at[pl.ds(0, TOK_TILE)], hbuf.at[i % X_BUFS],
                          hsem.at[i % X_BUFS]).wait()
    hn = hbuf[i % X_BUFS]
    grouped = jnp.dot(perm, hn, preferred_element_type=F32)
    _store_packed(loc.at[slot], grouped, LOC_ROWS)

    def run_copy(buf_slot, s, n_rows, k):
        src = loc.at[buf_slot, _rows(local_tabs[s][k], n_rows)]
        dst = xs_ref.at[_rows(sorted_tabs[s][k], n_rows)]
        return pltpu.make_async_copy(src, dst, sem.at[buf_slot])

    _for_each_copy(i, cnt_ref, lambda s, n_rows, k: run_copy(slot, s, n_rows, k).start())

    def drain(tile, buf_slot):
        def make_wait(n):
            return pltpu.make_async_copy(loc.at[buf_slot, pl.ds(0, n * ROW_CHUNKS)],
                                         xs_ref.at[pl.ds(0, n * ROW_CHUNKS)], sem.at[buf_slot])
        _wait_rows(trow_ref[tile], make_wait)

    @pl.when(i > 0)
    def _():
        drain(i - 1, 1 - slot)

    @pl.when(i == n_steps - 1)
    def _():
        drain(i, slot)


def _clamped_maps(np_steps):
    pmap = lambda i, *_: (jnp.minimum(i, np_steps - 1), 0)
    smap = lambda i, *_: (jnp.maximum(i - np_steps, 0), 0)
    return pmap, smap


def _dispatch_call(tables, hn_p, hn_s, rt_p, rt_s, n_sorted_rows):
    n_p, n_s = hn_p.shape[0], hn_s.shape[0]
    np_steps = n_p // TOK_TILE
    n_steps = np_steps + n_s // TOK_TILE
    pmap, smap = _clamped_maps(np_steps)
    out_rows = (n_sorted_rows + 8) * ROW_CHUNKS
    return pl.pallas_call(
        functools.partial(_dispatch_kernel, n_prompt_steps=np_steps),
        grid_spec=pltpu.PrefetchScalarGridSpec(
            num_scalar_prefetch=len(tables), grid=(n_steps,),
            in_specs=[pl.BlockSpec(memory_space=pl.ANY),
                      pl.BlockSpec(memory_space=pl.ANY),
                      pl.BlockSpec((TOK_TILE, SUBLANES), pmap),
                      pl.BlockSpec((TOK_TILE, SUBLANES), smap)],
            out_specs=pl.BlockSpec(memory_space=pl.ANY),
            scratch_shapes=[pltpu.VMEM((X_BUFS, TOK_TILE, D_MODEL), BF16),
                            pltpu.VMEM((2, LOC_ROWS * ROW_CHUNKS, LANES), U32),
                            pltpu.VMEM((MOE_TILE * ROW_CHUNKS, LANES), U32),
                            pltpu.SemaphoreType.DMA((X_BUFS,)),
                            pltpu.SemaphoreType.DMA((2,)),
                            pltpu.SemaphoreType.DMA(())]),
        out_shape=jax.ShapeDtypeStruct((out_rows, LANES), U32),
        compiler_params=pltpu.CompilerParams(
            dimension_semantics=("arbitrary",), vmem_limit_bytes=VMEM_LIMIT),
        name="dispatch",
    )(*tables, hn_p, hn_s, rt_p, rt_s)


def _moe_kernel(te_ref, nu_ref, par_ref, nxt_ref, nxt2_ref, ngrp_ref, xs_hbm, wg_hbm, wu_hbm, wd_hbm, ys_ref,
                xbuf, wg_f32, wu_f32, wd_f32, wg_bf, wu_bf, wd_bf, xsem, wsem):
    i = pl.program_id(0)
    n_used = nu_ref[0]
    used = i < n_used
    x_slot = i % X_BUFS

    def x_copy(tile):
        slot = tile % X_BUFS
        return pltpu.make_async_copy(xs_hbm.at[_rows(tile * MOE_TILE, MOE_TILE)], xbuf.at[slot],
                                     xsem.at[slot])

    def store_zeros(ref, n_rows):
        ij = (lax.broadcasted_iota(I32, (n_rows, D_MODEL), 0)
              + lax.broadcasted_iota(I32, (n_rows, D_MODEL), 1))
        _store_packed(ref, jnp.where(ij < jnp.minimum(nu_ref[0], 0), 1.0, 0.0), n_rows)

    def weight_copies(expert, buf):
        return [pltpu.make_async_copy(hbm.at[expert], vmem.at[buf], wsem.at[buf])
                for hbm, vmem in ((wg_hbm, wg_f32), (wu_hbm, wu_f32), (wd_hbm, wd_f32))]

    @pl.when(used)
    def _():
        @pl.when(i == 0)
        def _():
            for ahead in range(X_BUFS - 1):
                @pl.when(ahead < n_used)
                def _(ahead=ahead):
                    x_copy(ahead).start()

        @pl.when(i + (X_BUFS - 1) < n_used)
        def _():
            x_copy(i + (X_BUFS - 1)).start()

        prev = te_ref[jnp.maximum(i - 1, 0)]

        @pl.when((i == 0) | (te_ref[i] != prev))
        def _():
            buf = par_ref[i]

            @pl.when(i == 0)
            def _():
                for cp in weight_copies(te_ref[0], 0):
                    cp.start()

                @pl.when(nxt_ref[0] >= 0)
                def _():
                    for cp in weight_copies(nxt_ref[0], 1):
                        cp.start()

            for cp in weight_copies(te_ref[i], buf):
                cp.wait()
            wg_bf[...] = wg_f32[buf].astype(BF16)
            wu_bf[...] = wu_f32[buf].astype(BF16)
            wd_bf[...] = wd_f32[buf].astype(BF16)

            @pl.when(nxt2_ref[i] >= 0)
            def _():
                for cp in weight_copies(nxt2_ref[i], (buf + 2) % WEIGHT_BUFS):
                    cp.start()

        x_copy(i).wait()

        def expert_rows(n_rows):
            lo, hi = _load_packed(xbuf.at[x_slot], n_rows)
            x = jnp.concatenate([lo, hi], axis=-1).astype(BF16)
            g = jnp.dot(x, wg_bf[...], preferred_element_type=F32)
            u = jnp.dot(x, wu_bf[...], preferred_element_type=F32)
            hh = (g * jax.nn.sigmoid(g)) * u
            y = jnp.dot(hh.astype(BF16), wd_bf[...], preferred_element_type=F32)
            _store_packed(ys_ref, y, n_rows)

        for n_groups in range(1, MOE_TILE // ROW_GROUP + 1):
            @pl.when(ngrp_ref[i] == n_groups)
            def _(n_rows=n_groups * ROW_GROUP):
                expert_rows(n_rows)
                if n_rows < MOE_TILE:
                    rest = MOE_TILE - n_rows
                    store_zeros(ys_ref.at[pl.ds(n_rows * ROW_CHUNKS, rest * ROW_CHUNKS)], rest)

    @pl.when(jnp.logical_not(used))
    def _():
        store_zeros(ys_ref, MOE_TILE)


def _moe_call(tile_expert, n_used, weight_buf, next_expert, next2_expert, n_groups_used, xs, n_tiles,
              w_eg, w_eu, w_ed):
    blk = MOE_TILE * ROW_CHUNKS
    any_spec = pl.BlockSpec(memory_space=pl.ANY)
    return pl.pallas_call(
        _moe_kernel,
        grid_spec=pltpu.PrefetchScalarGridSpec(
            num_scalar_prefetch=6, grid=(n_tiles,),
            in_specs=[
                any_spec, any_spec, any_spec, any_spec,
            ],
            out_specs=pl.BlockSpec((blk, LANES), lambda i, *_: (i, 0)),
            scratch_shapes=[pltpu.VMEM((X_BUFS, blk, LANES), U32),
                            pltpu.VMEM((WEIGHT_BUFS, D_MODEL, D_EXPERT), F32),
                            pltpu.VMEM((WEIGHT_BUFS, D_MODEL, D_EXPERT), F32),
                            pltpu.VMEM((WEIGHT_BUFS, D_EXPERT, D_MODEL), F32),
                            pltpu.VMEM((D_MODEL, D_EXPERT), BF16),
                            pltpu.VMEM((D_MODEL, D_EXPERT), BF16),
                            pltpu.VMEM((D_EXPERT, D_MODEL), BF16),
                            pltpu.SemaphoreType.DMA((X_BUFS,)),
                            pltpu.SemaphoreType.DMA((WEIGHT_BUFS,))]),
        out_shape=jax.ShapeDtypeStruct((n_tiles * blk, LANES), U32),
        compiler_params=pltpu.CompilerParams(
            dimension_semantics=("arbitrary",), vmem_limit_bytes=VMEM_LIMIT),
        name="experts",
    )(tile_expert, n_used, weight_buf, next_expert, next2_expert, n_groups_used, xs, w_eg, w_eu, w_ed)


def _final_kernel(l8, g8, l4, g4, l2, g2, cnt_ref, trow_ref,
                  h_ref, rt_ref, p_ref, ys_ref, wple_ref, wgate_ref, gple_ref, gfin_ref, out_ref,
                  ybuf, yb_s, perm_s, sem, *, tile_base, n_tiles):
    s = pl.program_id(0)
    n_groups = n_tiles // STEP_TILES
    a_par = s % 2
    local_tabs, sorted_tabs = (l8, l4, l2), (g8, g4, g2)

    def slot_of(parity, sub):
        return parity * STEP_TILES + sub

    def run_copy(buf_slot, sz, n_rows, k):
        src = ys_ref.at[_rows(sorted_tabs[sz][k], n_rows)]
        dst = ybuf.at[buf_slot, _rows(local_tabs[sz][k], n_rows)]
        return pltpu.make_async_copy(src, dst, sem.at[buf_slot])

    def issue(group, parity, active):
        for sub in range(STEP_TILES):
            tile = jnp.minimum(group, n_groups - 1) * STEP_TILES + sub
            buf_slot = slot_of(parity, sub)
            _for_each_copy(tile_base + tile, cnt_ref,
                           lambda sz, n_rows, k, b=buf_slot: run_copy(b, sz, n_rows, k).start(),
                           active)

    @pl.when(s == 0)
    def _():
        ybuf[...] = jnp.zeros(ybuf.shape, U32)
        yb_s[...] = jnp.zeros(yb_s.shape, BF16)
        perm_s[...] = jnp.zeros(perm_s.shape, BF16)
        issue(0, 0, True)

    issue(s + 1, 1 - a_par, s + 1 < n_groups)

    for sub in range(STEP_TILES):
        a_slot = slot_of(a_par, sub)

        def make_wait(n, a_slot=a_slot):
            return pltpu.make_async_copy(ys_ref.at[pl.ds(0, n * ROW_CHUNKS)],
                                         ybuf.at[a_slot, pl.ds(0, n * ROW_CHUNKS)], sem.at[a_slot])
        tile = tile_base + jnp.minimum(s, n_groups - 1) * STEP_TILES + sub
        _wait_rows(jnp.where(s < n_groups, trow_ref[tile], 0), make_wait)

    for sub in range(STEP_TILES):
        rows = slice(sub * TOK_TILE, (sub + 1) * TOK_TILE)
        b_slot = slot_of(1 - a_par, sub)
        h = h_ref[rows, :]
        h = h + jnp.dot(perm_s[b_slot], yb_s[b_slot], preferred_element_type=F32)
        r = _rms(h, gple_ref[...])
        gate = jax.nn.sigmoid(jnp.dot(r.astype(BF16), wgate_ref[...], preferred_element_type=F32))
        h = h + jnp.dot(p_ref[rows, :].astype(BF16), wple_ref[...],
                        preferred_element_type=F32) * gate
        out_ref[rows, :] = _rms(h, gfin_ref[...])

    for sub in range(STEP_TILES):
        rows = slice(sub * TOK_TILE, (sub + 1) * TOK_TILE)
        a_slot = slot_of(a_par, sub)
        rt = rt_ref[rows, :]
        lpos1, lpos2 = rt[:, 2:3], rt[:, 3:4]
        lane = lax.broadcasted_iota(I32, (TOK_TILE, LOC_ROWS), 1).astype(F32)
        perm_s[a_slot] = jnp.where((lane == lpos1) | (lane == lpos2), 1.0, 0.0).astype(BF16)
        row1, row2, w1_row, w2_row = _as_rows(lpos1, lpos2, rt[:, 4:5], rt[:, 5:6])
        grow = lax.broadcasted_iota(I32, (LOC_ROWS, TOK_TILE), 0).astype(F32)
        wrow = jnp.sum(jnp.where(grow == row1, w1_row, 0.0) + jnp.where(grow == row2, w2_row, 0.0),
                       axis=-1, keepdims=True)
        lo, hi = _load_packed(ybuf.at[a_slot], LOC_ROWS)
        yb_s[a_slot] = (jnp.concatenate([lo, hi], axis=-1) * wrow).astype(BF16)


def _final_call(tables, h, rt, p, ys, wple, wgate, gple, gfin, *, tile_base):
    n_rows = h.shape[0]
    n_tiles = n_rows // TOK_TILE
    n_groups = n_tiles // STEP_TILES
    blk = STEP_TILES * TOK_TILE
    b_map = lambda s, *_: (jnp.maximum(s - 1, 0), 0)
    a_map = lambda s, *_: (jnp.minimum(s, n_groups - 1), 0)
    cmap = lambda s, *_: (0, 0)
    n_slots = 2 * STEP_TILES
    return pl.pallas_call(
        functools.partial(_final_kernel, tile_base=tile_base, n_tiles=n_tiles),
        grid_spec=pltpu.PrefetchScalarGridSpec(
            num_scalar_prefetch=len(tables), grid=(n_groups + 1,),
            in_specs=[
                pl.BlockSpec((blk, D_MODEL), b_map),
                pl.BlockSpec((blk, SUBLANES), a_map),
                pl.BlockSpec((blk, PLE_DIM), b_map),
                pl.BlockSpec(memory_space=pl.ANY),
                pl.BlockSpec(wple.shape, cmap),
                pl.BlockSpec(wgate.shape, cmap),
                pl.BlockSpec(gple.shape, cmap),
                pl.BlockSpec(gfin.shape, cmap),
            ],
            out_specs=pl.BlockSpec((blk, D_MODEL), b_map),
            scratch_shapes=[pltpu.VMEM((n_slots, LOC_ROWS * ROW_CHUNKS, LANES), U32),
                            pltpu.VMEM((n_slots, LOC_ROWS, D_MODEL), BF16),
                            pltpu.VMEM((n_slots, TOK_TILE, LOC_ROWS), BF16),
                            pltpu.SemaphoreType.DMA((n_slots,))]),
        out_shape=jax.ShapeDtypeStruct((n_rows, D_MODEL), F32),
        compiler_params=pltpu.CompilerParams(
            dimension_semantics=("arbitrary",), vmem_limit_bytes=VMEM_LIMIT),
        name="final",
    )(*tables, h, rt, p, ys, wple, wgate, gple, gfin)


def _ceil_to(x, m):
    return ((x + m - 1) // m) * m


def _excl_cumsum(a, axis):
    return jnp.cumsum(a, axis=axis) - a


def _copy_tables(n_copies, first_local, first_sorted, step, cap):
    ends = jnp.cumsum(n_copies, axis=1)
    j = jnp.arange(cap, dtype=I32)[None, :, None]
    e_of = jnp.minimum(jnp.sum(ends[:, None, :] <= j, axis=-1), N_EXPERTS - 1)
    pick = e_of[:, :, None] == jnp.arange(N_EXPERTS, dtype=I32)[None, None, :]
    take = lambda a: jnp.sum(jnp.where(pick, a[:, None, :], 0), axis=-1)
    c = jnp.arange(cap, dtype=I32)[None, :] - (take(ends) - take(n_copies))
    flat = lambda a: a.reshape(-1).astype(I32)
    return flat(take(first_local) + step * c), flat(take(first_sorted) + step * c), ends[:, -1]


def kernel(x_prompt, x_sample, state_pool, p_prompt, p_sample, g_mix, w_in, w_pool, pool_scale,
           g_v, b_v, w_spatial, b_spatial, w_out, g_ffn, w_rg, b_rg, w_re, b_re, w_eg, w_eu, w_ed,
           w_ple, g_ple, w_ple_gate, g_final):
    batch, seq, _ = x_prompt.shape
    dec_batch, dec_seq, _ = x_sample.shape
    assert dec_seq == SUBLANES and g_mix.shape[0] == 1
    n_p = batch * seq
    n_s = dec_batch * dec_seq
    n_tok = n_p + n_s
    past_len = 16384

    row = lambda a: a.reshape(1, -1).astype(F32)
    wr = jnp.concatenate(
        [w_re[0], w_rg[0], jnp.zeros((D_MODEL, LANES - N_EXPERTS - N_EXPERT_GROUPS), F32)], axis=1)
    br = jnp.concatenate(
        [b_re[0], b_rg[0], jnp.zeros((LANES - N_EXPERTS - N_EXPERT_GROUPS,), F32)]).reshape(1, LANES)
    bsp_p = jnp.repeat(jnp.transpose(b_spatial[0]), GMLP_HEAD, axis=1)
    bsp_s = jnp.tile(bsp_p[:dec_seq], (CHUNK // dec_seq, 1))
    wsp_p = w_spatial[0].astype(BF16)
    wsp_s = jnp.tile(w_spatial[0][:, :dec_seq, :dec_seq].astype(BF16),
                     (1, CHUNK // dec_seq, CHUNK // dec_seq))

    def pool_band(first_chunk):
        t = jnp.arange(CHUNK, dtype=I32)[:, None]
        k = jnp.arange(POOL_SPAN, dtype=I32)[None, :]
        mats = []
        for w in POOL_WINDOWS:
            cnt = jnp.minimum(w, t + 1) if first_chunk else jnp.full_like(t, w)
            in_window = (k >= t + 17 - w) & (k <= t + 16)
            mats.append(jnp.where(in_window, 1.0 / cnt.astype(F32), 0.0) - (k == t + 16).astype(F32))
        return jnp.stack(mats).astype(BF16)

    def mixer_weights(wsp, bsp):
        return (row(g_mix[0]), w_in[0].astype(BF16), w_pool[0].astype(BF16), row(pool_scale[0]),
                row(g_v[0]), row(b_v[0]), wsp, bsp, w_out[0].astype(BF16), row(g_ffn[0]),
                jnp.transpose(wr).astype(BF16), br.reshape(LANES, 1), pool_band(False), pool_band(True))

    h_p, hn_p, rt_p, cnt_p, hist_p = _mixer_call(
        x_prompt, None, mixer_weights(wsp_p, bsp_p), sample=False, pos0=0)
    h_s, hn_s, rt_s, cnt_s, hist_s, v_s = _mixer_call(
        x_sample.reshape(n_s, D_MODEL), jnp.transpose(state_pool[0], (1, 0, 2)),
        mixer_weights(wsp_s, bsp_s),
        sample=True, pos0=past_len)

    n_run = jnp.concatenate([cnt_p, cnt_s], axis=0)[:, 0, :N_EXPERTS].astype(I32)
    n_even = _ceil_to(n_run, RUN_ALIGN)
    l_start = _excl_cumsum(n_even, 1)
    tile_rows = jnp.sum(n_even, axis=1)
    base = _excl_cumsum(n_even, 0)
    counts = jnp.sum(n_even, axis=0)
    padded = _ceil_to(counts, MOE_TILE)
    ends = jnp.cumsum(padded)
    offs = ends - padded
    g_start = offs[None, :] + base
    n8 = n_even // 8
    f4 = (n_even // 4) % 2
    f2 = (n_even // 2) % 2
    l8, g8, c8 = _copy_tables(n8, l_start, g_start, 8, COPY_CAP[0])
    l4, g4, c4 = _copy_tables(f4, l_start + 8 * n8, g_start + 8 * n8, 0, COPY_CAP[1])
    l2, g2, c2 = _copy_tables(f2, l_start + 8 * n8 + 4 * f4, g_start + 8 * n8 + 4 * f4, 0, COPY_CAP[2])
    copy_cnt = jnp.stack([c8, c4, c2], axis=1).reshape(-1).astype(I32)
    tile_rows = tile_rows.astype(I32)
    z_start = (offs + counts).astype(I32)
    z_big = ((padded - counts) // ZERO_ROWS).astype(I32)
    z_blocks = (((padded - counts) % ZERO_ROWS + 7) // 8).astype(I32)

    max_rows = 2 * n_tok + n_run.shape[0] * N_EXPERTS * (RUN_ALIGN - 1)
    n_tiles = max_rows // MOE_TILE + N_EXPERTS + 1
    n_used = (ends[-1] // MOE_TILE).astype(I32).reshape(1)
    tile_id = jnp.minimum(jnp.arange(n_tiles, dtype=I32), n_used - 1)
    tile_expert = jnp.sum((tile_id[:, None] * MOE_TILE) >= ends[None, :], axis=1).astype(I32)
    tile_expert = jnp.minimum(tile_expert, N_EXPERTS - 1)
    pick_e = tile_expert[:, None] == jnp.arange(N_EXPERTS, dtype=I32)[None, :]
    data_end = jnp.sum(jnp.where(pick_e, (offs + counts)[None, :], 0), axis=1)
    n_groups_used = jnp.clip((data_end - tile_id * MOE_TILE + ROW_GROUP - 1) // ROW_GROUP,
                             1, MOE_TILE // ROW_GROUP).astype(I32)
    e_ids = jnp.arange(N_EXPERTS, dtype=I32)
    nonempty = padded > 0
    seg_index = jnp.cumsum(nonempty.astype(I32)) - 1
    later = nonempty[None, :] & (e_ids[None, :] > e_ids[:, None])
    next_of = jnp.min(jnp.where(later, e_ids[None, :], N_EXPERTS), axis=1)
    next_of = jnp.where(next_of == N_EXPERTS, -1, next_of)
    next2_of = jnp.sum(jnp.where(next_of[:, None] == e_ids[None, :], next_of[None, :], 0), axis=1)
    next2_of = jnp.where(next_of >= 0, next2_of, -1)
    weight_buf = (jnp.sum(jnp.where(pick_e, seg_index[None, :], 0), axis=1) % WEIGHT_BUFS).astype(I32)
    next_expert = jnp.sum(jnp.where(pick_e, next_of[None, :], 0), axis=1).astype(I32)
    next2_expert = jnp.sum(jnp.where(pick_e, next2_of[None, :], 0), axis=1).astype(I32)

    copy_tabs = (l8, g8, l4, g4, l2, g2, copy_cnt, tile_rows)
    xs = _dispatch_call(copy_tabs + (z_start, z_big, z_blocks, n_used), hn_p, hn_s, rt_p, rt_s,
                        n_tiles * MOE_TILE)
    ys = _moe_call(tile_expert, n_used, weight_buf, next_expert, next2_expert, n_groups_used, xs, n_tiles,
                   w_eg[0], w_eu[0], w_ed[0])
    fin_w = (w_ple[0].astype(BF16), w_ple_gate[0].astype(BF16), row(g_ple[0]), row(g_final))
    y_p = _final_call(copy_tabs, h_p, rt_p, p_prompt[0].reshape(n_p, PLE_DIM), ys, *fin_w, tile_base=0)
    y_s = _final_call(copy_tabs, h_s, rt_s, p_sample[0].reshape(n_s, PLE_DIM), ys, *fin_w,
                      tile_base=n_p // TOK_TILE)

    return (y_p.reshape(batch, seq, D_MODEL),
            y_s.reshape(dec_batch, dec_seq, D_MODEL),
            hist_p[None],
            jnp.transpose(hist_s, (1, 0, 2))[None],
            v_s.reshape(1, dec_batch, dec_seq, GMLP_WIDTH))
```

```python
import functools

import jax
import jax.numpy as jnp
from jax import lax
from jax.experimental import pallas as pl
from jax.experimental.pallas import tpu as pltpu

D_MODEL = 1024
POOL_WIDTH = 512
GMLP_WIDTH = 512
IN_WIDTH = POOL_WIDTH + 2 * GMLP_WIDTH
POOL_WINDOWS = (2, 4, 8, 16)
POOL_GROUP = 128
POOL_HIST = 15
CHUNK = 128
POOL_SPAN = CHUNK + 16
N_GMLP_HEADS = 4
GMLP_HEAD = 128
N_EXPERT_GROUPS = 4
EXPERTS_PER_GROUP = 8
N_EXPERTS = 32
D_EXPERT = 512
PLE_DIM = 256
EPS = 1e-6

LANES = 128
SUBLANES = 8
HALF = D_MODEL // 2
ROW_CHUNKS = HALF // LANES

TOK_TILE = 256
STEP_TILES = 2
MIX_TILE = 1024
MOE_TILE = 640
ROW_GROUP = 128
WEIGHT_BUFS = 3
X_BUFS = 3
RUN_ALIGN = 2
COPY_ROWS = (8, 4, 2)
ZERO_ROWS = 64
LOC_ROWS = 2 * TOK_TILE + N_EXPERTS * (RUN_ALIGN - 1)
COPY_CAP = (LOC_ROWS // 8 + 1, N_EXPERTS, N_EXPERTS)
VMEM_LIMIT = 56 * 1024 * 1024

F32 = jnp.float32
BF16 = jnp.bfloat16
I32 = jnp.int32
U32 = jnp.uint32


def _rms(x, g):
    return x * lax.rsqrt(jnp.mean(x * x, axis=-1, keepdims=True) + EPS) * g


def _gelu_tanh(x):
    inner = x * (0.7978845608028654 + (0.7978845608028654 * 0.044715) * (x * x))
    half = 0.5 * x
    return half + half * jnp.tanh(inner)


def _store_packed(ref, val, n_rows):
    packed = pltpu.pack_elementwise([val[:, :HALF], val[:, HALF:]], packed_dtype=BF16)
    for k in range(ROW_CHUNKS):
        ref[pl.ds(k, n_rows, stride=ROW_CHUNKS), :] = packed[:, k * LANES:(k + 1) * LANES]


def _load_packed(ref, n_rows):
    packed = jnp.concatenate(
        [ref[pl.ds(k, n_rows, stride=ROW_CHUNKS), :] for k in range(ROW_CHUNKS)], axis=-1)
    lo = pltpu.unpack_elementwise(packed, index=0, packed_dtype=BF16, unpacked_dtype=F32)
    hi = pltpu.unpack_elementwise(packed, index=1, packed_dtype=BF16, unpacked_dtype=F32)
    return lo, hi


def _rows(start, n):
    return pl.ds(pl.multiple_of(start * ROW_CHUNKS, SUBLANES), n * ROW_CHUNKS)


def _route(hn, wrt_ref, brc_ref, n_tok):
    logits = lax.dot_general(wrt_ref[...], hn, (((1,), (1,)), ((), ())),
                             preferred_element_type=F32) + brc_ref[...]
    neg = jnp.float32(-jnp.inf)
    big = jnp.float32(1e9)
    sub = lax.broadcasted_iota(I32, (SUBLANES, n_tok), 0).astype(F32)
    gl = jnp.where(sub < N_EXPERT_GROUPS, logits[N_EXPERTS:N_EXPERTS + SUBLANES, :], neg)
    gmax = jnp.max(gl, axis=0, keepdims=True)
    gidx = jnp.min(jnp.where(gl == gmax, sub, big), axis=0, keepdims=True)
    gprob = 1.0 / jnp.sum(jnp.exp(gl - gmax), axis=0, keepdims=True)
    assert EXPERTS_PER_GROUP == SUBLANES
    el = logits[0:SUBLANES, :]
    for g in range(1, N_EXPERT_GROUPS):
        el = jnp.where(gidx == g, logits[g * SUBLANES:(g + 1) * SUBLANES, :], el)
    m1 = jnp.max(el, axis=0, keepdims=True)
    o1 = jnp.min(jnp.where(el == m1, sub, big), axis=0, keepdims=True)
    el2 = jnp.where(sub == o1, neg, el)
    m2 = jnp.max(el2, axis=0, keepdims=True)
    o2 = jnp.min(jnp.where(el2 == m2, sub, big), axis=0, keepdims=True)
    i1 = gidx * EXPERTS_PER_GROUP + o1
    i2 = gidx * EXPERTS_PER_GROUP + o2
    t = jnp.exp(m2 - m1)
    w1 = gprob / (1.0 + t)
    w2 = gprob * t / (1.0 + t)

    erow = lax.broadcasted_iota(I32, (LANES, n_tok), 0).astype(F32)
    sel1 = erow == i1
    sel2 = erow == i2
    oh = jnp.where(sel1 | sel2, 1.0, 0.0).astype(BF16)
    r = lax.broadcasted_iota(I32, (n_tok, n_tok), 0)
    c = lax.broadcasted_iota(I32, (n_tok, n_tok), 1)
    earlier = jnp.where(r < c, 1.0, 0.0).astype(BF16)
    before = jnp.dot(oh, earlier, preferred_element_type=F32)
    counts_b = jnp.dot(oh, jnp.ones((n_tok, LANES), BF16), preferred_element_type=F32)
    assert RUN_ALIGN == 2
    c_even = counts_b + (counts_b - 2.0 * jnp.floor(counts_b * 0.5))
    c_hi = jnp.floor(c_even * (1.0 / 256.0))
    er = lax.broadcasted_iota(I32, (LANES, LANES), 0)
    ec = lax.broadcasted_iota(I32, (LANES, LANES), 1)
    lower = jnp.where(ec < er, 1.0, 0.0).astype(BF16)
    pieces = jnp.concatenate([c_hi, c_even - 256.0 * c_hi], axis=1).astype(BF16)
    prefix = jnp.dot(lower, pieces, preferred_element_type=F32)
    lstart = 256.0 * prefix[:, :LANES] + prefix[:, LANES:]
    placed = before + jnp.concatenate([lstart] * (n_tok // LANES), axis=1)
    lpos1 = jnp.sum(jnp.where(sel1, placed, 0.0), axis=0, keepdims=True)
    lpos2 = jnp.sum(jnp.where(sel2, placed, 0.0), axis=0, keepdims=True)

    stacked = jnp.where(erow == 0.0, i1, 0.0)
    for k, val in enumerate((i2, lpos1, lpos2, w1, w2), start=1):
        stacked = jnp.where(erow == float(k), val, stacked)
    return stacked.T[:, :SUBLANES], stacked[0:SUBLANES, :], counts_b.T[0:1, :]


def _mixer_kernel(*refs, n_tok, sample, pos0):
    if sample:
        (x_ref, hist_ref, gmix_ref, win_ref, wpool_ref, pscale_ref, gv_ref, bv_ref, wsp_ref,
         bsp_ref, wout_ref, gffn_ref, wr_ref, br_ref, band_ref, band0_ref,
         h_ref, hn_ref, rt_ref, rtr_ref, cnt_ref, hist_out_ref, v_ref, ext_scr, pool_scr) = refs
    else:
        (x_ref, gmix_ref, win_ref, wpool_ref, pscale_ref, gv_ref, bv_ref, wsp_ref,
         bsp_ref, wout_ref, gffn_ref, wr_ref, br_ref, band_ref, band0_ref,
         h_ref, hn_ref, rt_ref, rtr_ref, cnt_ref, hist_out_ref, ext_scr) = refs

    x = x_ref[...]
    xn = _rms(x, gmix_ref[...])
    z = jnp.dot(xn.astype(BF16), win_ref[...], preferred_element_type=F32)
    a = z[:, :POOL_WIDTH]
    uv = _gelu_tanh(z[:, POOL_WIDTH:])
    u = uv[:, :GMLP_WIDTH]
    v = uv[:, GMLP_WIDTH:]
    mu = jnp.mean(v, axis=-1, keepdims=True)
    vc = v - mu
    v = vc * lax.rsqrt(jnp.mean(vc * vc, axis=-1, keepdims=True) + EPS) * gv_ref[...] + bv_ref[...]

    pooled = []
    if sample:
        n_seq = n_tok // SUBLANES
        n_grp = len(POOL_WINDOWS)
        gcols = [slice(gi * POOL_GROUP, (gi + 1) * POOL_GROUP) for gi in range(n_grp)]
        ext_scr[1:1 + POOL_HIST] = hist_ref[...]
        for gi in range(n_grp):
            pool_scr[gi] = a[:, gcols[gi]]
        for i in range(SUBLANES):
            ext_scr[1 + POOL_HIST + i] = jnp.concatenate(
                [pool_scr[gi, pl.ds(i, n_seq, stride=SUBLANES), :] for gi in range(n_grp)], axis=-1)
        hist_out_ref[...] = ext_scr[1 + SUBLANES:]
        for i in range(SUBLANES):
            back = [ext_scr[1 + POOL_HIST + i - j] for j in range(max(POOL_WINDOWS))]
            for gi, w in enumerate(POOL_WINDOWS):
                win = back[0][:, gcols[gi]]
                for j in range(1, w):
                    win = win + back[j][:, gcols[gi]]
                pool_scr[gi, pl.ds(i, n_seq, stride=SUBLANES), :] = (
                    win * (1.0 / min(w, pos0 + i + 1)) - back[0][:, gcols[gi]])
        pooled = [pool_scr[gi] for gi in range(n_grp)]
    else:
        l = pl.program_id(1)

        @pl.when(l == 0)
        def _():
            ext_scr[0:16, :] = jnp.zeros((16, POOL_WIDTH), F32)

        ext_scr[16:16 + n_tok, :] = a
        chunks = []
        for ci in range(n_tok // CHUNK):
            ext_c = ext_scr[ci * CHUNK:ci * CHUNK + POOL_SPAN, :].astype(BF16)
            groups = []
            for gi in range(len(POOL_WINDOWS)):
                band = band_ref[gi]
                if ci == 0:
                    band = jnp.where(l == 0, band0_ref[gi], band)
                groups.append(jnp.dot(band, ext_c[:, gi * POOL_GROUP:(gi + 1) * POOL_GROUP],
                                      preferred_element_type=F32))
            chunks.append(groups)
        pooled = [jnp.concatenate([chunks[ci][gi] for ci in range(n_tok // CHUNK)], axis=0)
                  for gi in range(len(POOL_WINDOWS))]
        tail = ext_scr[n_tok + 1:n_tok + 16, :]
        hist_out_ref[...] = tail
        ext_scr[1:16, :] = tail

    pool_out = jnp.concatenate(
        [jnp.dot(pooled[gi].astype(BF16), wpool_ref[gi], preferred_element_type=F32)
         for gi in range(len(POOL_WINDOWS))], axis=-1) * pscale_ref[...]

    blk = SUBLANES if sample else CHUNK
    tr = lax.broadcasted_iota(I32, (CHUNK, CHUNK), 0)
    sc = lax.broadcasted_iota(I32, (CHUNK, CHUNK), 1)
    mask = (sc <= tr) & ((sc // blk) == (tr // blk))
    wsp = [jnp.where(mask, wsp_ref[hh], jnp.zeros((), BF16)) for hh in range(N_GMLP_HEADS)]
    v_bf = v.astype(BF16)
    mixed_chunks = []
    for ci in range(n_tok // CHUNK):
        rows = slice(ci * CHUNK, (ci + 1) * CHUNK)
        heads = [jnp.dot(wsp[hh], v_bf[rows, hh * GMLP_HEAD:(hh + 1) * GMLP_HEAD],
                         preferred_element_type=F32) for hh in range(N_GMLP_HEADS)]
        mixed_chunks.append(jnp.concatenate(heads, axis=-1) + bsp_ref[...])
    mixed = jnp.concatenate(mixed_chunks, axis=0)
    gmlp_out = u * mixed

    cat = jnp.concatenate([pool_out, gmlp_out], axis=-1)
    h = x + jnp.dot(cat.astype(BF16), wout_ref[...], preferred_element_type=F32)
    hn = _rms(h, gffn_ref[...]).astype(BF16)

    h_ref[...] = h
    hn_ref[...] = hn
    if sample:
        v_ref[...] = v
    for s in range(n_tok // TOK_TILE):
        rows = slice(s * TOK_TILE, (s + 1) * TOK_TILE)
        rt, rt_rows, counts = _route(hn[rows], wr_ref, br_ref, TOK_TILE)
        rt_ref[rows, :] = rt
        rtr_ref[s * SUBLANES:(s + 1) * SUBLANES, :] = rt_rows
        cnt_ref[s] = counts


def _full(shape):
    return pl.BlockSpec(shape, lambda *_: (0,) * len(shape))


def _mixer_call(x, hist, weights, *, sample, pos0):
    w_specs = [_full(w.shape) for w in weights]
    n_tok = MIX_TILE
    if sample:
        n_rows = x.shape[0]
        n_seq = n_tok // SUBLANES
        grid = (n_rows // n_tok,)
        tok_map = lambda i: (i, 0)
        tile_map = lambda i: (i, 0, 0)
        hist_spec = pl.BlockSpec((POOL_HIST, n_seq, POOL_WIDTH), lambda i: (0, i, 0))
        in_specs = [pl.BlockSpec((n_tok, D_MODEL), tok_map), hist_spec] + w_specs
        hist_shape = jax.ShapeDtypeStruct(hist.shape, F32)
        scratch = [pltpu.VMEM((1 + POOL_HIST + SUBLANES, n_seq, POOL_WIDTH), F32),
                   pltpu.VMEM((len(POOL_WINDOWS), n_tok, POOL_GROUP), F32)]
        args = (x, hist)
    else:
        b, seq, _ = x.shape
        n_rows = b * seq
        n_l = seq // n_tok
        grid = (b, n_l)
        tok_map = lambda bi, li: (bi * n_l + li, 0)
        tile_map = lambda bi, li: (bi * n_l + li, 0, 0)
        in_specs = [pl.BlockSpec((None, n_tok, D_MODEL), lambda bi, li: (bi, li, 0))] + w_specs
        hist_shape = jax.ShapeDtypeStruct((b, POOL_HIST, POOL_WIDTH), F32)
        hist_spec = pl.BlockSpec((None, POOL_HIST, POOL_WIDTH), lambda bi, li: (bi, 0, 0))
        scratch = [pltpu.VMEM((16 + n_tok, POOL_WIDTH), F32)]
        args = (x,)
    out_shape = [jax.ShapeDtypeStruct((n_rows, D_MODEL), F32),
                 jax.ShapeDtypeStruct((n_rows, D_MODEL), BF16),
                 jax.ShapeDtypeStruct((n_rows, SUBLANES), F32),
                 jax.ShapeDtypeStruct((n_rows // TOK_TILE * SUBLANES, TOK_TILE), F32),
                 jax.ShapeDtypeStruct((n_rows // TOK_TILE, 1, LANES), F32),
                 hist_shape]
    out_specs = [pl.BlockSpec((n_tok, D_MODEL), tok_map),
                 pl.BlockSpec((n_tok, D_MODEL), tok_map),
                 pl.BlockSpec((n_tok, SUBLANES), tok_map),
                 pl.BlockSpec((n_tok // TOK_TILE * SUBLANES, TOK_TILE), tok_map),
                 pl.BlockSpec((n_tok // TOK_TILE, 1, LANES), tile_map),
                 hist_spec]
    if sample:
        out_shape.append(jax.ShapeDtypeStruct((n_rows, GMLP_WIDTH), F32))
        out_specs.append(pl.BlockSpec((n_tok, GMLP_WIDTH), tok_map))
    return pl.pallas_call(
        functools.partial(_mixer_kernel, n_tok=n_tok, sample=sample, pos0=pos0),
        grid=grid, in_specs=in_specs, out_specs=out_specs, out_shape=out_shape,
        scratch_shapes=scratch,
        compiler_params=pltpu.CompilerParams(
            dimension_semantics=("arbitrary",) * len(grid), vmem_limit_bytes=VMEM_LIMIT),
        name="mixer_sample" if sample else "mixer_prompt",
    )(*args, *weights)


def _for_each_copy(tile, cnt_ref, fn, active=True):
    for s, n_rows in enumerate(COPY_ROWS):
        def body(j, carry, s=s, n_rows=n_rows):
            fn(s, n_rows, tile * COPY_CAP[s] + j)
            return carry
        lax.fori_loop(0, jnp.where(active, cnt_ref[tile * len(COPY_ROWS) + s], 0), body, 0)


def _wait_rows(n_rows_total, make_wait):
    units = n_rows_total // RUN_ALIGN
    bit = 0
    while (RUN_ALIGN << bit) <= LOC_ROWS:
        @pl.when((units >> bit) & 1 == 1)
        def _(bit=bit):
            make_wait(RUN_ALIGN << bit).wait()
        bit += 1


def _dispatch_kernel(l8, g8, l4, g4, l2, g2, cnt_ref, trow_ref, zs_ref, zbig_ref, zb_ref, nu_ref,
                     hnp_ref, hns_ref, rtp_ref, rts_ref, xs_ref, loc, zbuf, sem, zsem,
                     *, n_prompt_steps):
    i = pl.program_id(0)
    n_steps = pl.num_programs(0)
    slot = i % 2
    local_tabs, sorted_tabs = (l8, l4, l2), (g8, g4, g2)

    @pl.when(i == 0)
    def _():
        zbuf[...] = jnp.zeros(zbuf.shape, U32)
        n_tiles = xs_ref.shape[0] // (MOE_TILE * ROW_CHUNKS)

        def zero_copy(e, c, n_rows, first):
            return pltpu.make_async_copy(zbuf.at[pl.ds(0, n_rows * ROW_CHUNKS)],
                                         xs_ref.at[_rows(first + c * n_rows, n_rows)], zsem)

        def tail_copy(t):
            return pltpu.make_async_copy(zbuf, xs_ref.at[_rows(t * MOE_TILE, MOE_TILE)], zsem)

        def per_expert(e, totals):
            def big(c, carry):
                zero_copy(e, c, ZERO_ROWS, zs_ref[e]).start()
                return carry

            def small(c, carry):
                zero_copy(e, c, 8, zs_ref[e] + zbig_ref[e] * ZERO_ROWS).start()
                return carry

            lax.fori_loop(0, zbig_ref[e], big, 0)
            lax.fori_loop(0, zb_ref[e], small, 0)
            return totals[0] + zbig_ref[e], totals[1] + zb_ref[e]

        n_big, n_small = lax.fori_loop(0, N_EXPERTS, per_expert, (0, 0))

        def wait_big(_, carry):
            zero_copy(0, 0, ZERO_ROWS, 0).wait()
            return carry

        def wait_small(_, carry):
            zero_copy(0, 0, 8, 0).wait()
            return carry

        lax.fori_loop(0, n_big, wait_big, 0)
        lax.fori_loop(0, n_small, wait_small, 0)

        def tail_start(t, carry):
            tail_copy(t).start()
            return carry

        def tail_wait(t, carry):
            tail_copy(t).wait()
            return carry

        spill = pltpu.make_async_copy(zbuf.at[pl.ds(0, 8 * ROW_CHUNKS)],
                                      xs_ref.at[_rows(n_tiles * MOE_TILE, 8)], zsem)
        lax.fori_loop(nu_ref[0], n_tiles, tail_start, 0)
        spill.start()
        lax.fori_loop(nu_ref[0], n_tiles, tail_wait, 0)
        spill.wait()

    is_prompt = i < n_prompt_steps
    rt_rows = jnp.where(is_prompt, rtp_ref[...], rts_ref[...])
    row1, row2 = rt_rows[2:3, :], rt_rows[3:4, :]
    grow = lax.broadcasted_iota(I32, (LOC_ROWS, TOK_TILE), 0).astype(F32)
    perm = jnp.where((grow == row1) | (grow == row2), 1.0, 0.0).astype(BF16)
    hn = jnp.where(is_prompt, hnp_ref[...], hns_ref[...])
    grouped = jnp.dot(perm, hn, preferred_element_type=F32)
    _store_packed(loc.at[slot], grouped, LOC_ROWS)

    def run_copy(buf_slot, s, n_rows, k):
        src = loc.at[buf_slot, _rows(local_tabs[s][k], n_rows)]
        dst = xs_ref.at[_rows(sorted_tabs[s][k], n_rows)]
        return pltpu.make_async_copy(src, dst, sem.at[buf_slot])

    _for_each_copy(i, cnt_ref, lambda s, n_rows, k: run_copy(slot, s, n_rows, k).start())

    def drain(tile, buf_slot):
        def make_wait(n):
            return pltpu.make_async_copy(loc.at[buf_slot, pl.ds(0, n * ROW_CHUNKS)],
                                         xs_ref.at[pl.ds(0, n * ROW_CHUNKS)], sem.at[buf_slot])
        _wait_rows(trow_ref[tile], make_wait)

    @pl.when(i > 0)
    def _():
        drain(i - 1, 1 - slot)

    @pl.when(i == n_steps - 1)
    def _():
        drain(i, slot)


def _clamped_maps(np_steps):
    pmap = lambda i, *_: (jnp.minimum(i, np_steps - 1), 0)
    smap = lambda i, *_: (jnp.maximum(i - np_steps, 0), 0)
    return pmap, smap


def _dispatch_call(tables, hn_p, hn_s, rtr_p, rtr_s, n_sorted_rows):
    n_p, n_s = hn_p.shape[0], hn_s.shape[0]
    np_steps = n_p // TOK_TILE
    n_steps = np_steps + n_s // TOK_TILE
    pmap, smap = _clamped_maps(np_steps)
    out_rows = (n_sorted_rows + 8) * ROW_CHUNKS
    return pl.pallas_call(
        functools.partial(_dispatch_kernel, n_prompt_steps=np_steps),
        grid_spec=pltpu.PrefetchScalarGridSpec(
            num_scalar_prefetch=len(tables), grid=(n_steps,),
            in_specs=[pl.BlockSpec((TOK_TILE, D_MODEL), pmap),
                      pl.BlockSpec((TOK_TILE, D_MODEL), smap),
                      pl.BlockSpec((SUBLANES, TOK_TILE), pmap),
                      pl.BlockSpec((SUBLANES, TOK_TILE), smap)],
            out_specs=pl.BlockSpec(memory_space=pl.ANY),
            scratch_shapes=[pltpu.VMEM((2, LOC_ROWS * ROW_CHUNKS, LANES), U32),
                            pltpu.VMEM((MOE_TILE * ROW_CHUNKS, LANES), U32),
                            pltpu.SemaphoreType.DMA((2,)),
                            pltpu.SemaphoreType.DMA(())]),
        out_shape=jax.ShapeDtypeStruct((out_rows, LANES), U32),
        compiler_params=pltpu.CompilerParams(
            dimension_semantics=("arbitrary",), vmem_limit_bytes=VMEM_LIMIT),
        name="dispatch",
    )(*tables, hn_p, hn_s, rtr_p, rtr_s)


def _moe_kernel(te_ref, nu_ref, par_ref, nxt_ref, nxt2_ref, ngrp_ref, xs_hbm, wg_hbm, wu_hbm, wd_hbm, ys_ref,
                xbuf, wg_f32, wu_f32, wd_f32, wg_bf, wu_bf, wd_bf, xsem, wsem):
    i = pl.program_id(0)
    n_used = nu_ref[0]
    used = i < n_used
    x_slot = i % X_BUFS

    def x_copy(tile):
        slot = tile % X_BUFS
        return pltpu.make_async_copy(xs_hbm.at[_rows(tile * MOE_TILE, MOE_TILE)], xbuf.at[slot],
                                     xsem.at[slot])

    def store_zeros(ref, n_rows):
        ij = (lax.broadcasted_iota(I32, (n_rows, D_MODEL), 0)
              + lax.broadcasted_iota(I32, (n_rows, D_MODEL), 1))
        _store_packed(ref, jnp.where(ij < jnp.minimum(nu_ref[0], 0), 1.0, 0.0), n_rows)

    def weight_copies(expert, buf):
        return [pltpu.make_async_copy(hbm.at[expert], vmem.at[buf], wsem.at[buf])
                for hbm, vmem in ((wg_hbm, wg_f32), (wu_hbm, wu_f32), (wd_hbm, wd_f32))]

    @pl.when(used)
    def _():
        @pl.when(i == 0)
        def _():
            for ahead in range(X_BUFS - 1):
                @pl.when(ahead < n_used)
                def _(ahead=ahead):
                    x_copy(ahead).start()

        @pl.when(i + (X_BUFS - 1) < n_used)
        def _():
            x_copy(i + (X_BUFS - 1)).start()

        prev = te_ref[jnp.maximum(i - 1, 0)]

        @pl.when((i == 0) | (te_ref[i] != prev))
        def _():
            buf = par_ref[i]

            @pl.when(i == 0)
            def _():
                for cp in weight_copies(te_ref[0], 0):
                    cp.start()

                @pl.when(nxt_ref[0] >= 0)
                def _():
                    for cp in weight_copies(nxt_ref[0], 1):
                        cp.start()

            for cp in weight_copies(te_ref[i], buf):
                cp.wait()
            wg_bf[...] = wg_f32[buf].astype(BF16)
            wu_bf[...] = wu_f32[buf].astype(BF16)
            wd_bf[...] = wd_f32[buf].astype(BF16)

            @pl.when(nxt2_ref[i] >= 0)
            def _():
                for cp in weight_copies(nxt2_ref[i], (buf + 2) % WEIGHT_BUFS):
                    cp.start()

        x_copy(i).wait()

        def expert_rows(n_rows):
            lo, hi = _load_packed(xbuf.at[x_slot], n_rows)
            x = jnp.concatenate([lo, hi], axis=-1).astype(BF16)
            g = jnp.dot(x, wg_bf[...], preferred_element_type=F32)
            u = jnp.dot(x, wu_bf[...], preferred_element_type=F32)
            hh = (g * jax.nn.sigmoid(g)) * u
            y = jnp.dot(hh.astype(BF16), wd_bf[...], preferred_element_type=F32)
            _store_packed(ys_ref, y, n_rows)

        for n_groups in range(1, MOE_TILE // ROW_GROUP + 1):
            @pl.when(ngrp_ref[i] == n_groups)
            def _(n_rows=n_groups * ROW_GROUP):
                expert_rows(n_rows)
                if n_rows < MOE_TILE:
                    rest = MOE_TILE - n_rows
                    store_zeros(ys_ref.at[pl.ds(n_rows * ROW_CHUNKS, rest * ROW_CHUNKS)], rest)

    @pl.when(jnp.logical_not(used))
    def _():
        store_zeros(ys_ref, MOE_TILE)


def _moe_call(tile_expert, n_used, weight_buf, next_expert, next2_expert, n_groups_used, xs, n_tiles,
              w_eg, w_eu, w_ed):
    blk = MOE_TILE * ROW_CHUNKS
    any_spec = pl.BlockSpec(memory_space=pl.ANY)
    return pl.pallas_call(
        _moe_kernel,
        grid_spec=pltpu.PrefetchScalarGridSpec(
            num_scalar_prefetch=6, grid=(n_tiles,),
            in_specs=[
                any_spec, any_spec, any_spec, any_spec,
            ],
            out_specs=pl.BlockSpec((blk, LANES), lambda i, *_: (i, 0)),
            scratch_shapes=[pltpu.VMEM((X_BUFS, blk, LANES), U32),
                            pltpu.VMEM((WEIGHT_BUFS, D_MODEL, D_EXPERT), F32),
                            pltpu.VMEM((WEIGHT_BUFS, D_MODEL, D_EXPERT), F32),
                            pltpu.VMEM((WEIGHT_BUFS, D_EXPERT, D_MODEL), F32),
                            pltpu.VMEM((D_MODEL, D_EXPERT), BF16),
                            pltpu.VMEM((D_MODEL, D_EXPERT), BF16),
                            pltpu.VMEM((D_EXPERT, D_MODEL), BF16),
                            pltpu.SemaphoreType.DMA((X_BUFS,)),
                            pltpu.SemaphoreType.DMA((WEIGHT_BUFS,))]),
        out_shape=jax.ShapeDtypeStruct((n_tiles * blk, LANES), U32),
        compiler_params=pltpu.CompilerParams(
            dimension_semantics=("arbitrary",), vmem_limit_bytes=VMEM_LIMIT),
        name="experts",
    )(tile_expert, n_used, weight_buf, next_expert, next2_expert, n_groups_used, xs, w_eg, w_eu, w_ed)


def _final_kernel(l8, g8, l4, g4, l2, g2, cnt_ref, trow_ref,
                  h_ref, rt_ref, rtr_ref, p_ref, ys_ref, wple_ref, wgate_ref, gple_ref, gfin_ref, out_ref,
                  ybuf, yb_s, perm_s, sem, *, tile_base, n_tiles):
    s = pl.program_id(0)
    n_groups = n_tiles // STEP_TILES
    a_par = s % 2
    local_tabs, sorted_tabs = (l8, l4, l2), (g8, g4, g2)

    def slot_of(parity, sub):
        return parity * STEP_TILES + sub

    def run_copy(buf_slot, sz, n_rows, k):
        src = ys_ref.at[_rows(sorted_tabs[sz][k], n_rows)]
        dst = ybuf.at[buf_slot, _rows(local_tabs[sz][k], n_rows)]
        return pltpu.make_async_copy(src, dst, sem.at[buf_slot])

    def issue(group, parity, active):
        for sub in range(STEP_TILES):
            tile = jnp.minimum(group, n_groups - 1) * STEP_TILES + sub
            buf_slot = slot_of(parity, sub)
            _for_each_copy(tile_base + tile, cnt_ref,
                           lambda sz, n_rows, k, b=buf_slot: run_copy(b, sz, n_rows, k).start(),
                           active)

    @pl.when(s == 0)
    def _():
        ybuf[...] = jnp.zeros(ybuf.shape, U32)
        yb_s[...] = jnp.zeros(yb_s.shape, BF16)
        perm_s[...] = jnp.zeros(perm_s.shape, BF16)
        issue(0, 0, True)

    issue(s + 1, 1 - a_par, s + 1 < n_groups)

    for sub in range(STEP_TILES):
        a_slot = slot_of(a_par, sub)

        def make_wait(n, a_slot=a_slot):
            return pltpu.make_async_copy(ys_ref.at[pl.ds(0, n * ROW_CHUNKS)],
                                         ybuf.at[a_slot, pl.ds(0, n * ROW_CHUNKS)], sem.at[a_slot])
        tile = tile_base + jnp.minimum(s, n_groups - 1) * STEP_TILES + sub
        _wait_rows(jnp.where(s < n_groups, trow_ref[tile], 0), make_wait)

    for sub in range(STEP_TILES):
        rows = slice(sub * TOK_TILE, (sub + 1) * TOK_TILE)
        b_slot = slot_of(1 - a_par, sub)
        h = h_ref[rows, :]
        h = h + jnp.dot(perm_s[b_slot], yb_s[b_slot], preferred_element_type=F32)
        r = _rms(h, gple_ref[...])
        gate = jax.nn.sigmoid(jnp.dot(r.astype(BF16), wgate_ref[...], preferred_element_type=F32))
        h = h + jnp.dot(p_ref[rows, :].astype(BF16), wple_ref[...],
                        preferred_element_type=F32) * gate
        out_ref[rows, :] = _rms(h, gfin_ref[...])

    for sub in range(STEP_TILES):
        rows = slice(sub * TOK_TILE, (sub + 1) * TOK_TILE)
        a_slot = slot_of(a_par, sub)
        rt = rt_ref[rows, :]
        lpos1, lpos2 = rt[:, 2:3], rt[:, 3:4]
        lane = lax.broadcasted_iota(I32, (TOK_TILE, LOC_ROWS), 1).astype(F32)
        perm_s[a_slot] = jnp.where((lane == lpos1) | (lane == lpos2), 1.0, 0.0).astype(BF16)
        rt_rows = rtr_ref[sub * SUBLANES:(sub + 1) * SUBLANES, :]
        row1, row2, w1_row, w2_row = (rt_rows[k:k + 1, :] for k in range(2, 6))
        grow = lax.broadcasted_iota(I32, (LOC_ROWS, TOK_TILE), 0).astype(F32)
        wrow = jnp.sum(jnp.where(grow == row1, w1_row, 0.0) + jnp.where(grow == row2, w2_row, 0.0),
                       axis=-1, keepdims=True)
        lo, hi = _load_packed(ybuf.at[a_slot], LOC_ROWS)
        yb_s[a_slot] = (jnp.concatenate([lo, hi], axis=-1) * wrow).astype(BF16)


def _final_call(tables, h, rt, rt_rows, p, ys, wple, wgate, gple, gfin, *, tile_base):
    n_rows = h.shape[0]
    n_tiles = n_rows // TOK_TILE
    n_groups = n_tiles // STEP_TILES
    blk = STEP_TILES * TOK_TILE
    b_map = lambda s, *_: (jnp.maximum(s - 1, 0), 0)
    a_map = lambda s, *_: (jnp.minimum(s, n_groups - 1), 0)
    cmap = lambda s, *_: (0, 0)
    n_slots = 2 * STEP_TILES
    return pl.pallas_call(
        functools.partial(_final_kernel, tile_base=tile_base, n_tiles=n_tiles),
        grid_spec=pltpu.PrefetchScalarGridSpec(
            num_scalar_prefetch=len(tables), grid=(n_groups + 1,),
            in_specs=[
                pl.BlockSpec((blk, D_MODEL), b_map),
                pl.BlockSpec((blk, SUBLANES), a_map),
                pl.BlockSpec((STEP_TILES * SUBLANES, TOK_TILE), a_map),
                pl.BlockSpec((blk, PLE_DIM), b_map),
                pl.BlockSpec(memory_space=pl.ANY),
                pl.BlockSpec(wple.shape, cmap),
                pl.BlockSpec(wgate.shape, cmap),
                pl.BlockSpec(gple.shape, cmap),
                pl.BlockSpec(gfin.shape, cmap),
            ],
            out_specs=pl.BlockSpec((blk, D_MODEL), b_map),
            scratch_shapes=[pltpu.VMEM((n_slots, LOC_ROWS * ROW_CHUNKS, LANES), U32),
                            pltpu.VMEM((n_slots, LOC_ROWS, D_MODEL), BF16),
                            pltpu.VMEM((n_slots, TOK_TILE, LOC_ROWS), BF16),
                            pltpu.SemaphoreType.DMA((n_slots,))]),
        out_shape=jax.ShapeDtypeStruct((n_rows, D_MODEL), F32),
        compiler_params=pltpu.CompilerParams(
            dimension_semantics=("arbitrary",), vmem_limit_bytes=VMEM_LIMIT),
        name="final",
    )(*tables, h, rt, rt_rows, p, ys, wple, wgate, gple, gfin)


def _ceil_to(x, m):
    return ((x + m - 1) // m) * m


def _excl_cumsum(a, axis):
    return jnp.cumsum(a, axis=axis) - a


def _copy_tables(n_copies, first_local, first_sorted, step, cap):
    ends = jnp.cumsum(n_copies, axis=1)
    j = jnp.arange(cap, dtype=I32)[None, :, None]
    e_of = jnp.minimum(jnp.sum(ends[:, None, :] <= j, axis=-1), N_EXPERTS - 1)
    pick = e_of[:, :, None] == jnp.arange(N_EXPERTS, dtype=I32)[None, None, :]
    take = lambda a: jnp.sum(jnp.where(pick, a[:, None, :], 0), axis=-1)
    c = jnp.arange(cap, dtype=I32)[None, :] - (take(ends) - take(n_copies))
    flat = lambda a: a.reshape(-1).astype(I32)
    return flat(take(first_local) + step * c), flat(take(first_sorted) + step * c), ends[:, -1]


def kernel(x_prompt, x_sample, state_pool, p_prompt, p_sample, g_mix, w_in, w_pool, pool_scale,
           g_v, b_v, w_spatial, b_spatial, w_out, g_ffn, w_rg, b_rg, w_re, b_re, w_eg, w_eu, w_ed,
           w_ple, g_ple, w_ple_gate, g_final):
    batch, seq, _ = x_prompt.shape
    dec_batch, dec_seq, _ = x_sample.shape
    assert dec_seq == SUBLANES and g_mix.shape[0] == 1
    n_p = batch * seq
    n_s = dec_batch * dec_seq
    n_tok = n_p + n_s
    past_len = 16384

    row = lambda a: a.reshape(1, -1).astype(F32)
    wr = jnp.concatenate(
        [w_re[0], w_rg[0], jnp.zeros((D_MODEL, LANES - N_EXPERTS - N_EXPERT_GROUPS), F32)], axis=1)
    br = jnp.concatenate(
        [b_re[0], b_rg[0], jnp.zeros((LANES - N_EXPERTS - N_EXPERT_GROUPS,), F32)]).reshape(1, LANES)
    bsp_p = jnp.repeat(jnp.transpose(b_spatial[0]), GMLP_HEAD, axis=1)
    bsp_s = jnp.tile(bsp_p[:dec_seq], (CHUNK // dec_seq, 1))
    wsp_p = w_spatial[0].astype(BF16)
    wsp_s = jnp.tile(w_spatial[0][:, :dec_seq, :dec_seq].astype(BF16),
                     (1, CHUNK // dec_seq, CHUNK // dec_seq))

    def pool_band(first_chunk):
        t = jnp.arange(CHUNK, dtype=I32)[:, None]
        k = jnp.arange(POOL_SPAN, dtype=I32)[None, :]
        mats = []
        for w in POOL_WINDOWS:
            cnt = jnp.minimum(w, t + 1) if first_chunk else jnp.full_like(t, w)
            in_window = (k >= t + 17 - w) & (k <= t + 16)
            mats.append(jnp.where(in_window, 1.0 / cnt.astype(F32), 0.0) - (k == t + 16).astype(F32))
        return jnp.stack(mats).astype(BF16)

    def mixer_weights(wsp, bsp):
        return (row(g_mix[0]), w_in[0].astype(BF16), w_pool[0].astype(BF16), row(pool_scale[0]),
                row(g_v[0]), row(b_v[0]), wsp, bsp, w_out[0].astype(BF16), row(g_ffn[0]),
                jnp.transpose(wr).astype(BF16), br.reshape(LANES, 1), pool_band(False), pool_band(True))

    h_p, hn_p, rt_p, rtr_p, cnt_p, hist_p = _mixer_call(
        x_prompt, None, mixer_weights(wsp_p, bsp_p), sample=False, pos0=0)
    h_s, hn_s, rt_s, rtr_s, cnt_s, hist_s, v_s = _mixer_call(
        x_sample.reshape(n_s, D_MODEL), jnp.transpose(state_pool[0], (1, 0, 2)),
        mixer_weights(wsp_s, bsp_s),
        sample=True, pos0=past_len)

    n_run = jnp.concatenate([cnt_p, cnt_s], axis=0)[:, 0, :N_EXPERTS].astype(I32)
    n_even = _ceil_to(n_run, RUN_ALIGN)
    l_start = _excl_cumsum(n_even, 1)
    tile_rows = jnp.sum(n_even, axis=1)
    base = _excl_cumsum(n_even, 0)
    counts = jnp.sum(n_even, axis=0)
    padded = _ceil_to(counts, MOE_TILE)
    ends = jnp.cumsum(padded)
    offs = ends - padded
    g_start = offs[None, :] + base
    n8 = n_even // 8
    f4 = (n_even // 4) % 2
    f2 = (n_even // 2) % 2
    l8, g8, c8 = _copy_tables(n8, l_start, g_start, 8, COPY_CAP[0])
    l4, g4, c4 = _copy_tables(f4, l_start + 8 * n8, g_start + 8 * n8, 0, COPY_CAP[1])
    l2, g2, c2 = _copy_tables(f2, l_start + 8 * n8 + 4 * f4, g_start + 8 * n8 + 4 * f4, 0, COPY_CAP[2])
    copy_cnt = jnp.stack([c8, c4, c2], axis=1).reshape(-1).astype(I32)
    tile_rows = tile_rows.astype(I32)
    z_start = (offs + counts).astype(I32)
    z_big = ((padded - counts) // ZERO_ROWS).astype(I32)
    z_blocks = (((padded - counts) % ZERO_ROWS + 7) // 8).astype(I32)

    max_rows = 2 * n_tok + n_run.shape[0] * N_EXPERTS * (RUN_ALIGN - 1)
    n_tiles = max_rows // MOE_TILE + N_EXPERTS + 1
    n_used = (ends[-1] // MOE_TILE).astype(I32).reshape(1)
    tile_id = jnp.minimum(jnp.arange(n_tiles, dtype=I32), n_used - 1)
    tile_expert = jnp.sum((tile_id[:, None] * MOE_TILE) >= ends[None, :], axis=1).astype(I32)
    tile_expert = jnp.minimum(tile_expert, N_EXPERTS - 1)
    pick_e = tile_expert[:, None] == jnp.arange(N_EXPERTS, dtype=I32)[None, :]
    data_end = jnp.sum(jnp.where(pick_e, (offs + counts)[None, :], 0), axis=1)
    n_groups_used = jnp.clip((data_end - tile_id * MOE_TILE + ROW_GROUP - 1) // ROW_GROUP,
                             1, MOE_TILE // ROW_GROUP).astype(I32)
    e_ids = jnp.arange(N_EXPERTS, dtype=I32)
    nonempty = padded > 0
    seg_index = jnp.cumsum(nonempty.astype(I32)) - 1
    later = nonempty[None, :] & (e_ids[None, :] > e_ids[:, None])
    next_of = jnp.min(jnp.where(later, e_ids[None, :], N_EXPERTS), axis=1)
    next_of = jnp.where(next_of == N_EXPERTS, -1, next_of)
    next2_of = jnp.sum(jnp.where(next_of[:, None] == e_ids[None, :], next_of[None, :], 0), axis=1)
    next2_of = jnp.where(next_of >= 0, next2_of, -1)
    weight_buf = (jnp.sum(jnp.where(pick_e, seg_index[None, :], 0), axis=1) % WEIGHT_BUFS).astype(I32)
    next_expert = jnp.sum(jnp.where(pick_e, next_of[None, :], 0), axis=1).astype(I32)
    next2_expert = jnp.sum(jnp.where(pick_e, next2_of[None, :], 0), axis=1).astype(I32)

    copy_tabs = (l8, g8, l4, g4, l2, g2, copy_cnt, tile_rows)
    xs = _dispatch_call(copy_tabs + (z_start, z_big, z_blocks, n_used), hn_p, hn_s, rtr_p, rtr_s,
                        n_tiles * MOE_TILE)
    ys = _moe_call(tile_expert, n_used, weight_buf, next_expert, next2_expert, n_groups_used, xs, n_tiles,
                   w_eg[0], w_eu[0], w_ed[0])
    fin_w = (w_ple[0].astype(BF16), w_ple_gate[0].astype(BF16), row(g_ple[0]), row(g_final))
    y_p = _final_call(copy_tabs, h_p, rt_p, rtr_p, p_prompt[0].reshape(n_p, PLE_DIM), ys, *fin_w,
                      tile_base=0)
    y_s = _final_call(copy_tabs, h_s, rt_s, rtr_s, p_sample[0].reshape(n_s, PLE_DIM), ys, *fin_w,
                      tile_base=n_p // TOK_TILE)

    return (y_p.reshape(batch, seq, D_MODEL),
            y_s.reshape(dec_batch, dec_seq, D_MODEL),
            hist_p[None],
            jnp.transpose(hist_s, (1, 0, 2))[None],
            v_s.reshape(1, dec_batch, dec_seq, GMLP_WIDTH))
```

```python
import functools

import jax
import jax.numpy as jnp
from jax import lax
from jax.experimental import pallas as pl
from jax.experimental.pallas import tpu as pltpu

D_MODEL = 1024
POOL_WIDTH = 512
GMLP_WIDTH = 512
IN_WIDTH = POOL_WIDTH + 2 * GMLP_WIDTH
POOL_WINDOWS = (2, 4, 8, 16)
POOL_GROUP = 128
POOL_HIST = 15
CHUNK = 128
POOL_SPAN = CHUNK + 16
N_GMLP_HEADS = 4
GMLP_HEAD = 128
N_EXPERT_GROUPS = 4
EXPERTS_PER_GROUP = 8
N_EXPERTS = 32
D_EXPERT = 512
PLE_DIM = 256
EPS = 1e-6

LANES = 128
SUBLANES = 8
HALF = D_MODEL // 2
ROW_CHUNKS = HALF // LANES

TOK_TILE = 256
STEP_TILES = 2
MIX_TILE = 1024
MOE_TILE = 640
ROW_GROUP = 128
WEIGHT_BUFS = 3
X_BUFS = 3
RUN_ALIGN = 2
COPY_ROWS = (8, 4, 2)
ZERO_ROWS = 64
LOC_ROWS = 2 * TOK_TILE + N_EXPERTS * (RUN_ALIGN - 1)
COPY_CAP = (LOC_ROWS // 8 + 1, N_EXPERTS, N_EXPERTS)
VMEM_LIMIT = 56 * 1024 * 1024

F32 = jnp.float32
BF16 = jnp.bfloat16
I32 = jnp.int32
U32 = jnp.uint32


def _rms(x, g):
    return x * lax.rsqrt(jnp.mean(x * x, axis=-1, keepdims=True) + EPS) * g


def _gelu_tanh(x):
    inner = x * (0.7978845608028654 + (0.7978845608028654 * 0.044715) * (x * x))
    half = 0.5 * x
    return half + half * jnp.tanh(inner)


def _store_packed(ref, val, n_rows):
    packed = pltpu.pack_elementwise([val[:, :HALF], val[:, HALF:]], packed_dtype=BF16)
    for k in range(ROW_CHUNKS):
        ref[pl.ds(k, n_rows, stride=ROW_CHUNKS), :] = packed[:, k * LANES:(k + 1) * LANES]


def _load_packed(ref, n_rows):
    packed = jnp.concatenate(
        [ref[pl.ds(k, n_rows, stride=ROW_CHUNKS), :] for k in range(ROW_CHUNKS)], axis=-1)
    lo = pltpu.unpack_elementwise(packed, index=0, packed_dtype=BF16, unpacked_dtype=F32)
    hi = pltpu.unpack_elementwise(packed, index=1, packed_dtype=BF16, unpacked_dtype=F32)
    return lo, hi


def _rows(start, n):
    return pl.ds(pl.multiple_of(start * ROW_CHUNKS, SUBLANES), n * ROW_CHUNKS)


def _route(hn, wrt_ref, brc_ref, n_tok):
    logits = lax.dot_general(wrt_ref[...], hn, (((1,), (1,)), ((), ())),
                             preferred_element_type=F32) + brc_ref[...]
    neg = jnp.float32(-jnp.inf)
    big = jnp.float32(1e9)
    sub = lax.broadcasted_iota(I32, (SUBLANES, n_tok), 0).astype(F32)
    gl = jnp.where(sub < N_EXPERT_GROUPS, logits[N_EXPERTS:N_EXPERTS + SUBLANES, :], neg)
    gmax = jnp.max(gl, axis=0, keepdims=True)
    gidx = jnp.min(jnp.where(gl == gmax, sub, big), axis=0, keepdims=True)
    gprob = 1.0 / jnp.sum(jnp.exp(gl - gmax), axis=0, keepdims=True)
    assert EXPERTS_PER_GROUP == SUBLANES
    el = logits[0:SUBLANES, :]
    for g in range(1, N_EXPERT_GROUPS):
        el = jnp.where(gidx == g, logits[g * SUBLANES:(g + 1) * SUBLANES, :], el)
    m1 = jnp.max(el, axis=0, keepdims=True)
    o1 = jnp.min(jnp.where(el == m1, sub, big), axis=0, keepdims=True)
    el2 = jnp.where(sub == o1, neg, el)
    m2 = jnp.max(el2, axis=0, keepdims=True)
    o2 = jnp.min(jnp.where(el2 == m2, sub, big), axis=0, keepdims=True)
    i1 = gidx * EXPERTS_PER_GROUP + o1
    i2 = gidx * EXPERTS_PER_GROUP + o2
    t = jnp.exp(m2 - m1)
    w1 = gprob / (1.0 + t)
    w2 = gprob * t / (1.0 + t)

    erow = lax.broadcasted_iota(I32, (LANES, n_tok), 0).astype(F32)
    sel1 = erow == i1
    sel2 = erow == i2
    oh = jnp.where(sel1 | sel2, 1.0, 0.0).astype(BF16)
    r = lax.broadcasted_iota(I32, (n_tok, n_tok), 0)
    c = lax.broadcasted_iota(I32, (n_tok, n_tok), 1)
    earlier = jnp.where(r < c, 1.0, 0.0).astype(BF16)
    before = jnp.dot(oh, earlier, preferred_element_type=F32)
    counts_b = jnp.dot(oh, jnp.ones((n_tok, LANES), BF16), preferred_element_type=F32)
    assert RUN_ALIGN == 2
    c_even = counts_b + (counts_b - 2.0 * jnp.floor(counts_b * 0.5))
    c_hi = jnp.floor(c_even * (1.0 / 256.0))
    er = lax.broadcasted_iota(I32, (LANES, LANES), 0)
    ec = lax.broadcasted_iota(I32, (LANES, LANES), 1)
    lower = jnp.where(ec < er, 1.0, 0.0).astype(BF16)
    pieces = jnp.concatenate([c_hi, c_even - 256.0 * c_hi], axis=1).astype(BF16)
    prefix = jnp.dot(lower, pieces, preferred_element_type=F32)
    lstart = 256.0 * prefix[:, :LANES] + prefix[:, LANES:]
    placed = before + jnp.concatenate([lstart] * (n_tok // LANES), axis=1)
    lpos1 = jnp.sum(jnp.where(sel1, placed, 0.0), axis=0, keepdims=True)
    lpos2 = jnp.sum(jnp.where(sel2, placed, 0.0), axis=0, keepdims=True)

    stacked = jnp.where(erow == 0.0, i1, 0.0)
    for k, val in enumerate((i2, lpos1, lpos2, w1, w2), start=1):
        stacked = jnp.where(erow == float(k), val, stacked)
    return stacked.T[:, :SUBLANES], stacked[0:SUBLANES, :], counts_b.T[0:1, :]


def _mixer_kernel(*refs, n_tok, sample, pos0):
    if sample:
        (x_ref, hist_ref, gmix_ref, win_ref, wpool_ref, pscale_ref, gv_ref, bv_ref, wsp_ref,
         bsp_ref, wout_ref, gffn_ref, wr_ref, br_ref, band_ref, band0_ref,
         h_ref, hn_ref, rt_ref, rtr_ref, cnt_ref, hist_out_ref, v_ref, ext_scr, pool_scr) = refs
    else:
        (x_ref, gmix_ref, win_ref, wpool_ref, pscale_ref, gv_ref, bv_ref, wsp_ref,
         bsp_ref, wout_ref, gffn_ref, wr_ref, br_ref, band_ref, band0_ref,
         h_ref, hn_ref, rt_ref, rtr_ref, cnt_ref, hist_out_ref, ext_scr) = refs

    x = x_ref[...]
    xn = _rms(x, gmix_ref[...])
    z = jnp.dot(xn.astype(BF16), win_ref[...], preferred_element_type=F32)
    a = z[:, :POOL_WIDTH]
    uv = _gelu_tanh(z[:, POOL_WIDTH:])
    u = uv[:, :GMLP_WIDTH]
    v = uv[:, GMLP_WIDTH:]
    mu = jnp.mean(v, axis=-1, keepdims=True)
    vc = v - mu
    v = vc * lax.rsqrt(jnp.mean(vc * vc, axis=-1, keepdims=True) + EPS) * gv_ref[...] + bv_ref[...]

    pooled = []
    if sample:
        n_seq = n_tok // SUBLANES
        n_grp = len(POOL_WINDOWS)
        gcols = [slice(gi * POOL_GROUP, (gi + 1) * POOL_GROUP) for gi in range(n_grp)]
        ext_scr[1:1 + POOL_HIST] = hist_ref[...]
        for gi in range(n_grp):
            pool_scr[gi] = a[:, gcols[gi]]
        for i in range(SUBLANES):
            ext_scr[1 + POOL_HIST + i] = jnp.concatenate(
                [pool_scr[gi, pl.ds(i, n_seq, stride=SUBLANES), :] for gi in range(n_grp)], axis=-1)
        hist_out_ref[...] = ext_scr[1 + SUBLANES:]
        for i in range(SUBLANES):
            back = [ext_scr[1 + POOL_HIST + i - j] for j in range(max(POOL_WINDOWS))]
            for gi, w in enumerate(POOL_WINDOWS):
                win = back[0][:, gcols[gi]]
                for j in range(1, w):
                    win = win + back[j][:, gcols[gi]]
                pool_scr[gi, pl.ds(i, n_seq, stride=SUBLANES), :] = (
                    win * (1.0 / min(w, pos0 + i + 1)) - back[0][:, gcols[gi]])
        pooled = [pool_scr[gi] for gi in range(n_grp)]
    else:
        l = pl.program_id(1)

        @pl.when(l == 0)
        def _():
            ext_scr[0:16, :] = jnp.zeros((16, POOL_WIDTH), F32)

        ext_scr[16:16 + n_tok, :] = a
        chunks = []
        for ci in range(n_tok // CHUNK):
            ext_c = ext_scr[ci * CHUNK:ci * CHUNK + POOL_SPAN, :].astype(BF16)
            groups = []
            for gi in range(len(POOL_WINDOWS)):
                band = band_ref[gi]
                if ci == 0:
                    band = jnp.where(l == 0, band0_ref[gi], band)
                groups.append(jnp.dot(band, ext_c[:, gi * POOL_GROUP:(gi + 1) * POOL_GROUP],
                                      preferred_element_type=F32))
            chunks.append(groups)
        pooled = [jnp.concatenate([chunks[ci][gi] for ci in range(n_tok // CHUNK)], axis=0)
                  for gi in range(len(POOL_WINDOWS))]
        tail = ext_scr[n_tok + 1:n_tok + 16, :]
        hist_out_ref[...] = tail
        ext_scr[1:16, :] = tail

    pool_out = jnp.concatenate(
        [jnp.dot(pooled[gi].astype(BF16), wpool_ref[gi], preferred_element_type=F32)
         for gi in range(len(POOL_WINDOWS))], axis=-1) * pscale_ref[...]

    blk = SUBLANES if sample else CHUNK
    tr = lax.broadcasted_iota(I32, (CHUNK, CHUNK), 0)
    sc = lax.broadcasted_iota(I32, (CHUNK, CHUNK), 1)
    mask = (sc <= tr) & ((sc // blk) == (tr // blk))
    wsp = [jnp.where(mask, wsp_ref[hh], jnp.zeros((), BF16)) for hh in range(N_GMLP_HEADS)]
    v_bf = v.astype(BF16)
    mixed_chunks = []
    for ci in range(n_tok // CHUNK):
        rows = slice(ci * CHUNK, (ci + 1) * CHUNK)
        heads = [jnp.dot(wsp[hh], v_bf[rows, hh * GMLP_HEAD:(hh + 1) * GMLP_HEAD],
                         preferred_element_type=F32) for hh in range(N_GMLP_HEADS)]
        mixed_chunks.append(jnp.concatenate(heads, axis=-1) + bsp_ref[...])
    mixed = jnp.concatenate(mixed_chunks, axis=0)
    gmlp_out = u * mixed

    cat = jnp.concatenate([pool_out, gmlp_out], axis=-1)
    h = x + jnp.dot(cat.astype(BF16), wout_ref[...], preferred_element_type=F32)
    hn = _rms(h, gffn_ref[...]).astype(BF16)

    h_ref[...] = h
    hn_ref[...] = hn
    if sample:
        v_ref[...] = v
    for s in range(n_tok // TOK_TILE):
        rows = slice(s * TOK_TILE, (s + 1) * TOK_TILE)
        rt, rt_rows, counts = _route(hn[rows], wr_ref, br_ref, TOK_TILE)
        rt_ref[rows, :] = rt
        rtr_ref[s * SUBLANES:(s + 1) * SUBLANES, :] = rt_rows
        cnt_ref[s] = counts


def _full(shape):
    return pl.BlockSpec(shape, lambda *_: (0,) * len(shape))


def _mixer_call(x, hist, weights, *, sample, pos0):
    w_specs = [_full(w.shape) for w in weights]
    n_tok = MIX_TILE
    if sample:
        n_rows = x.shape[0]
        n_seq = n_tok // SUBLANES
        grid = (n_rows // n_tok,)
        tok_map = lambda i: (i, 0)
        tile_map = lambda i: (i, 0, 0)
        hist_spec = pl.BlockSpec((POOL_HIST, n_seq, POOL_WIDTH), lambda i: (0, i, 0))
        in_specs = [pl.BlockSpec((n_tok, D_MODEL), tok_map), hist_spec] + w_specs
        hist_shape = jax.ShapeDtypeStruct(hist.shape, F32)
        scratch = [pltpu.VMEM((1 + POOL_HIST + SUBLANES, n_seq, POOL_WIDTH), F32),
                   pltpu.VMEM((len(POOL_WINDOWS), n_tok, POOL_GROUP), F32)]
        args = (x, hist)
    else:
        b, seq, _ = x.shape
        n_rows = b * seq
        n_l = seq // n_tok
        grid = (b, n_l)
        tok_map = lambda bi, li: (bi * n_l + li, 0)
        tile_map = lambda bi, li: (bi * n_l + li, 0, 0)
        in_specs = [pl.BlockSpec((None, n_tok, D_MODEL), lambda bi, li: (bi, li, 0))] + w_specs
        hist_shape = jax.ShapeDtypeStruct((b, POOL_HIST, POOL_WIDTH), F32)
        hist_spec = pl.BlockSpec((None, POOL_HIST, POOL_WIDTH), lambda bi, li: (bi, 0, 0))
        scratch = [pltpu.VMEM((16 + n_tok, POOL_WIDTH), F32)]
        args = (x,)
    out_shape = [jax.ShapeDtypeStruct((n_rows, D_MODEL), F32),
                 jax.ShapeDtypeStruct((n_rows, D_MODEL), BF16),
                 jax.ShapeDtypeStruct((n_rows, SUBLANES), F32),
                 jax.ShapeDtypeStruct((n_rows // TOK_TILE * SUBLANES, TOK_TILE), F32),
                 jax.ShapeDtypeStruct((n_rows // TOK_TILE, 1, LANES), F32),
                 hist_shape]
    out_specs = [pl.BlockSpec((n_tok, D_MODEL), tok_map),
                 pl.BlockSpec((n_tok, D_MODEL), tok_map),
                 pl.BlockSpec((n_tok, SUBLANES), tok_map),
                 pl.BlockSpec((n_tok // TOK_TILE * SUBLANES, TOK_TILE), tok_map),
                 pl.BlockSpec((n_tok // TOK_TILE, 1, LANES), tile_map),
                 hist_spec]
    if sample:
        out_shape.append(jax.ShapeDtypeStruct((n_rows, GMLP_WIDTH), F32))
        out_specs.append(pl.BlockSpec((n_tok, GMLP_WIDTH), tok_map))
    return pl.pallas_call(
        functools.partial(_mixer_kernel, n_tok=n_tok, sample=sample, pos0=pos0),
        grid=grid, in_specs=in_specs, out_specs=out_specs, out_shape=out_shape,
        scratch_shapes=scratch,
        compiler_params=pltpu.CompilerParams(
            dimension_semantics=("arbitrary",) * len(grid), vmem_limit_bytes=VMEM_LIMIT),
        name="mixer_sample" if sample else "mixer_prompt",
    )(*args, *weights)


def _for_each_copy(tile, cnt_ref, fn, active=True):
    for s, n_rows in enumerate(COPY_ROWS):
        def body(j, carry, s=s, n_rows=n_rows):
            fn(s, n_rows, tile * COPY_CAP[s] + j)
            return carry
        lax.fori_loop(0, jnp.where(active, cnt_ref[tile * len(COPY_ROWS) + s], 0), body, 0)


def _wait_rows(n_rows_total, make_wait):
    units = n_rows_total // RUN_ALIGN
    bit = 0
    while (RUN_ALIGN << bit) <= LOC_ROWS:
        @pl.when((units >> bit) & 1 == 1)
        def _(bit=bit):
            make_wait(RUN_ALIGN << bit).wait()
        bit += 1


def _dispatch_kernel(l8, g8, l4, g4, l2, g2, cnt_ref, trow_ref, zs_ref, zbig_ref, zb_ref, nu_ref,
                     hnp_ref, hns_ref, rtp_ref, rts_ref, xs_ref, loc, zbuf, sem, zsem,
                     *, n_prompt_steps):
    i = pl.program_id(0)
    n_steps = pl.num_programs(0)
    slot = i % 2
    local_tabs, sorted_tabs = (l8, l4, l2), (g8, g4, g2)

    @pl.when(i == 0)
    def _():
        zbuf[...] = jnp.zeros(zbuf.shape, U32)
        n_tiles = xs_ref.shape[0] // (MOE_TILE * ROW_CHUNKS)

        def zero_copy(e, c, n_rows, first):
            return pltpu.make_async_copy(zbuf.at[pl.ds(0, n_rows * ROW_CHUNKS)],
                                         xs_ref.at[_rows(first + c * n_rows, n_rows)], zsem)

        def tail_copy(t):
            return pltpu.make_async_copy(zbuf, xs_ref.at[_rows(t * MOE_TILE, MOE_TILE)], zsem)

        def per_expert(e, totals):
            def big(c, carry):
                zero_copy(e, c, ZERO_ROWS, zs_ref[e]).start()
                return carry

            def small(c, carry):
                zero_copy(e, c, 8, zs_ref[e] + zbig_ref[e] * ZERO_ROWS).start()
                return carry

            lax.fori_loop(0, zbig_ref[e], big, 0)
            lax.fori_loop(0, zb_ref[e], small, 0)
            return totals[0] + zbig_ref[e], totals[1] + zb_ref[e]

        n_big, n_small = lax.fori_loop(0, N_EXPERTS, per_expert, (0, 0))

        def wait_big(_, carry):
            zero_copy(0, 0, ZERO_ROWS, 0).wait()
            return carry

        def wait_small(_, carry):
            zero_copy(0, 0, 8, 0).wait()
            return carry

        lax.fori_loop(0, n_big, wait_big, 0)
        lax.fori_loop(0, n_small, wait_small, 0)

        def tail_start(t, carry):
            tail_copy(t).start()
            return carry

        def tail_wait(t, carry):
            tail_copy(t).wait()
            return carry

        spill = pltpu.make_async_copy(zbuf.at[pl.ds(0, 8 * ROW_CHUNKS)],
                                      xs_ref.at[_rows(n_tiles * MOE_TILE, 8)], zsem)
        lax.fori_loop(nu_ref[0], n_tiles, tail_start, 0)
        spill.start()
        lax.fori_loop(nu_ref[0], n_tiles, tail_wait, 0)
        spill.wait()

    is_prompt = i < n_prompt_steps
    rt_rows = jnp.where(is_prompt, rtp_ref[...], rts_ref[...])
    row1, row2 = rt_rows[2:3, :], rt_rows[3:4, :]
    grow = lax.broadcasted_iota(I32, (LOC_ROWS, TOK_TILE), 0).astype(F32)
    perm = jnp.where((grow == row1) | (grow == row2), 1.0, 0.0).astype(BF16)
    hn = jnp.where(is_prompt, hnp_ref[...], hns_ref[...])
    grouped = jnp.dot(perm, hn, preferred_element_type=F32)
    _store_packed(loc.at[slot], grouped, LOC_ROWS)

    def run_copy(buf_slot, s, n_rows, k):
        src = loc.at[buf_slot, _rows(local_tabs[s][k], n_rows)]
        dst = xs_ref.at[_rows(sorted_tabs[s][k], n_rows)]
        return pltpu.make_async_copy(src, dst, sem.at[buf_slot])

    _for_each_copy(i, cnt_ref, lambda s, n_rows, k: run_copy(slot, s, n_rows, k).start())

    def drain(tile, buf_slot):
        def make_wait(n):
            return pltpu.make_async_copy(loc.at[buf_slot, pl.ds(0, n * ROW_CHUNKS)],
                                         xs_ref.at[pl.ds(0, n * ROW_CHUNKS)], sem.at[buf_slot])
        _wait_rows(trow_ref[tile], make_wait)

    @pl.when(i > 0)
    def _():
        drain(i - 1, 1 - slot)

    @pl.when(i == n_steps - 1)
    def _():
        drain(i, slot)


def _clamped_maps(np_steps):
    pmap = lambda i, *_: (jnp.minimum(i, np_steps - 1), 0)
    smap = lambda i, *_: (jnp.maximum(i - np_steps, 0), 0)
    return pmap, smap


def _dispatch_call(tables, hn_p, hn_s, rtr_p, rtr_s, n_sorted_rows):
    n_p, n_s = hn_p.shape[0], hn_s.shape[0]
    np_steps = n_p // TOK_TILE
    n_steps = np_steps + n_s // TOK_TILE
    pmap, smap = _clamped_maps(np_steps)
    out_rows = (n_sorted_rows + 8) * ROW_CHUNKS
    return pl.pallas_call(
        functools.partial(_dispatch_kernel, n_prompt_steps=np_steps),
        grid_spec=pltpu.PrefetchScalarGridSpec(
            num_scalar_prefetch=len(tables), grid=(n_steps,),
            in_specs=[pl.BlockSpec((TOK_TILE, D_MODEL), pmap),
                      pl.BlockSpec((TOK_TILE, D_MODEL), smap),
                      pl.BlockSpec((SUBLANES, TOK_TILE), pmap),
                      pl.BlockSpec((SUBLANES, TOK_TILE), smap)],
            out_specs=pl.BlockSpec(memory_space=pl.ANY),
            scratch_shapes=[pltpu.VMEM((2, LOC_ROWS * ROW_CHUNKS, LANES), U32),
                            pltpu.VMEM((MOE_TILE * ROW_CHUNKS, LANES), U32),
                            pltpu.SemaphoreType.DMA((2,)),
                            pltpu.SemaphoreType.DMA(())]),
        out_shape=jax.ShapeDtypeStruct((out_rows, LANES), U32),
        compiler_params=pltpu.CompilerParams(
            dimension_semantics=("arbitrary",), vmem_limit_bytes=VMEM_LIMIT),
        name="dispatch",
    )(*tables, hn_p, hn_s, rtr_p, rtr_s)


def _moe_kernel(te_ref, nu_ref, par_ref, nxt_ref, nxt2_ref, ngrp_ref, xs_hbm, wg_hbm, wu_hbm, wd_hbm, ys_ref,
                xbuf, wg_f32, wu_f32, wd_f32, wg_bf, wu_bf, wd_bf, xsem, wsem):
    i = pl.program_id(0)
    n_used = nu_ref[0]
    used = i < n_used
    x_slot = i % X_BUFS

    def x_copy(tile):
        slot = tile % X_BUFS
        return pltpu.make_async_copy(xs_hbm.at[_rows(tile * MOE_TILE, MOE_TILE)], xbuf.at[slot],
                                     xsem.at[slot])

    def store_zeros(ref, n_rows):
        ij = (lax.broadcasted_iota(I32, (n_rows, D_MODEL), 0)
              + lax.broadcasted_iota(I32, (n_rows, D_MODEL), 1))
        _store_packed(ref, jnp.where(ij < jnp.minimum(nu_ref[0], 0), 1.0, 0.0), n_rows)

    def weight_copies(expert, buf):
        return [pltpu.make_async_copy(hbm.at[expert], vmem.at[buf], wsem.at[buf])
                for hbm, vmem in ((wg_hbm, wg_f32), (wu_hbm, wu_f32), (wd_hbm, wd_f32))]

    @pl.when(used)
    def _():
        @pl.when(i == 0)
        def _():
            for ahead in range(X_BUFS - 1):
                @pl.when(ahead < n_used)
                def _(ahead=ahead):
                    x_copy(ahead).start()

        @pl.when(i + (X_BUFS - 1) < n_used)
        def _():
            x_copy(i + (X_BUFS - 1)).start()

        prev = te_ref[jnp.maximum(i - 1, 0)]

        @pl.when((i == 0) | (te_ref[i] != prev))
        def _():
            buf = par_ref[i]

            @pl.when(i == 0)
            def _():
                for cp in weight_copies(te_ref[0], 0):
                    cp.start()

                @pl.when(nxt_ref[0] >= 0)
                def _():
                    for cp in weight_copies(nxt_ref[0], 1):
                        cp.start()

            for cp in weight_copies(te_ref[i], buf):
                cp.wait()
            wg_bf[...] = wg_f32[buf].astype(BF16)
            wu_bf[...] = wu_f32[buf].astype(BF16)
            wd_bf[...] = wd_f32[buf].astype(BF16)

            @pl.when(nxt2_ref[i] >= 0)
            def _():
                for cp in weight_copies(nxt2_ref[i], (buf + 2) % WEIGHT_BUFS):
                    cp.start()

        x_copy(i).wait()

        def expert_rows(n_rows):
            lo, hi = _load_packed(xbuf.at[x_slot], n_rows)
            x = jnp.concatenate([lo, hi], axis=-1).astype(BF16)
            g = jnp.dot(x, wg_bf[...], preferred_element_type=F32)
            u = jnp.dot(x, wu_bf[...], preferred_element_type=F32)
            hh = (g * jax.nn.sigmoid(g)) * u
            y = jnp.dot(hh.astype(BF16), wd_bf[...], preferred_element_type=F32)
            _store_packed(ys_ref, y, n_rows)

        for n_groups in range(1, MOE_TILE // ROW_GROUP + 1):
            @pl.when(ngrp_ref[i] == n_groups)
            def _(n_rows=n_groups * ROW_GROUP):
                expert_rows(n_rows)
                if n_rows < MOE_TILE:
                    rest = MOE_TILE - n_rows
                    store_zeros(ys_ref.at[pl.ds(n_rows * ROW_CHUNKS, rest * ROW_CHUNKS)], rest)

    @pl.when(jnp.logical_not(used))
    def _():
        store_zeros(ys_ref, MOE_TILE)


def _moe_call(tile_expert, n_used, weight_buf, next_expert, next2_expert, n_groups_used, xs, n_tiles,
              w_eg, w_eu, w_ed):
    blk = MOE_TILE * ROW_CHUNKS
    any_spec = pl.BlockSpec(memory_space=pl.ANY)
    return pl.pallas_call(
        _moe_kernel,
        grid_spec=pltpu.PrefetchScalarGridSpec(
            num_scalar_prefetch=6, grid=(n_tiles,),
            in_specs=[
                any_spec, any_spec, any_spec, any_spec,
            ],
            out_specs=pl.BlockSpec((blk, LANES), lambda i, *_: (i, 0)),
            scratch_shapes=[pltpu.VMEM((X_BUFS, blk, LANES), U32),
                            pltpu.VMEM((WEIGHT_BUFS, D_MODEL, D_EXPERT), F32),
                            pltpu.VMEM((WEIGHT_BUFS, D_MODEL, D_EXPERT), F32),
                            pltpu.VMEM((WEIGHT_BUFS, D_EXPERT, D_MODEL), F32),
                            pltpu.VMEM((D_MODEL, D_EXPERT), BF16),
                            pltpu.VMEM((D_MODEL, D_EXPERT), BF16),
                            pltpu.VMEM((D_EXPERT, D_MODEL), BF16),
                            pltpu.SemaphoreType.DMA((X_BUFS,)),
                            pltpu.SemaphoreType.DMA((WEIGHT_BUFS,))]),
        out_shape=jax.ShapeDtypeStruct((n_tiles * blk, LANES), U32),
        compiler_params=pltpu.CompilerParams(
            dimension_semantics=("arbitrary",), vmem_limit_bytes=VMEM_LIMIT),
        name="experts",
    )(tile_expert, n_used, weight_buf, next_expert, next2_expert, n_groups_used, xs, w_eg, w_eu, w_ed)


def _final_kernel(l8, g8, l4, g4, l2, g2, cnt_ref, trow_ref,
                  h_ref, rt_ref, rtr_ref, p_ref, ys_ref, wple_ref, wgate_ref, gple_ref, gfin_ref, out_ref,
                  ybuf, yb_s, perm_s, sem, *, tile_base, n_tiles):
    s = pl.program_id(0)
    n_groups = n_tiles // STEP_TILES
    a_par = s % 2
    local_tabs, sorted_tabs = (l8, l4, l2), (g8, g4, g2)

    def slot_of(parity, sub):
        return parity * STEP_TILES + sub

    def run_copy(buf_slot, sz, n_rows, k):
        src = ys_ref.at[_rows(sorted_tabs[sz][k], n_rows)]
        dst = ybuf.at[buf_slot, _rows(local_tabs[sz][k], n_rows)]
        return pltpu.make_async_copy(src, dst, sem.at[buf_slot])

    def issue(group, parity, active):
        for sub in range(STEP_TILES):
            tile = jnp.minimum(group, n_groups - 1) * STEP_TILES + sub
            buf_slot = slot_of(parity, sub)
            _for_each_copy(tile_base + tile, cnt_ref,
                           lambda sz, n_rows, k, b=buf_slot: run_copy(b, sz, n_rows, k).start(),
                           active)

    @pl.when(s == 0)
    def _():
        ybuf[...] = jnp.zeros(ybuf.shape, U32)
        yb_s[...] = jnp.zeros(yb_s.shape, BF16)
        perm_s[...] = jnp.zeros(perm_s.shape, BF16)
        issue(0, 0, True)

    issue(s + 1, 1 - a_par, s + 1 < n_groups)

    for sub in range(STEP_TILES):
        a_slot = slot_of(a_par, sub)

        def make_wait(n, a_slot=a_slot):
            return pltpu.make_async_copy(ys_ref.at[pl.ds(0, n * ROW_CHUNKS)],
                                         ybuf.at[a_slot, pl.ds(0, n * ROW_CHUNKS)], sem.at[a_slot])
        tile = tile_base + jnp.minimum(s, n_groups - 1) * STEP_TILES + sub
        _wait_rows(jnp.where(s < n_groups, trow_ref[tile], 0), make_wait)

    for sub in range(STEP_TILES):
        rows = slice(sub * TOK_TILE, (sub + 1) * TOK_TILE)
        b_slot = slot_of(1 - a_par, sub)
        h = h_ref[rows, :]
        h = h + jnp.dot(perm_s[b_slot], yb_s[b_slot], preferred_element_type=F32)
        r = _rms(h, gple_ref[...])
        gate = jax.nn.sigmoid(jnp.dot(r.astype(BF16), wgate_ref[...], preferred_element_type=F32))
        h = h + jnp.dot(p_ref[rows, :].astype(BF16), wple_ref[...],
                        preferred_element_type=F32) * gate
        out_ref[rows, :] = _rms(h, gfin_ref[...])

    for sub in range(STEP_TILES):
        rows = slice(sub * TOK_TILE, (sub + 1) * TOK_TILE)
        a_slot = slot_of(a_par, sub)
        rt = rt_ref[rows, :]
        lpos1, lpos2 = rt[:, 2:3], rt[:, 3:4]
        lane = lax.broadcasted_iota(I32, (TOK_TILE, LOC_ROWS), 1).astype(F32)
        perm_s[a_slot] = jnp.where((lane == lpos1) | (lane == lpos2), 1.0, 0.0).astype(BF16)
        rt_rows = rtr_ref[sub * SUBLANES:(sub + 1) * SUBLANES, :]
        row1, row2, w1_row, w2_row = (rt_rows[k:k + 1, :] for k in range(2, 6))
        grow = lax.broadcasted_iota(I32, (LOC_ROWS, TOK_TILE), 0).astype(F32)
        wrow = jnp.sum(jnp.where(grow == row1, w1_row, 0.0) + jnp.where(grow == row2, w2_row, 0.0),
                       axis=-1, keepdims=True)
        lo, hi = _load_packed(ybuf.at[a_slot], LOC_ROWS)
        yb_s[a_slot] = (jnp.concatenate([lo, hi], axis=-1) * wrow).astype(BF16)


def _final_call(tables, h, rt, rt_rows, p, ys, wple, wgate, gple, gfin, *, tile_base):
    n_rows = h.shape[0]
    n_tiles = n_rows // TOK_TILE
    n_groups = n_tiles // STEP_TILES
    blk = STEP_TILES * TOK_TILE
    b_map = lambda s, *_: (jnp.maximum(s - 1, 0), 0)
    a_map = lambda s, *_: (jnp.minimum(s, n_groups - 1), 0)
    cmap = lambda s, *_: (0, 0)
    n_slots = 2 * STEP_TILES
    return pl.pallas_call(
        functools.partial(_final_kernel, tile_base=tile_base, n_tiles=n_tiles),
        grid_spec=pltpu.PrefetchScalarGridSpec(
            num_scalar_prefetch=len(tables), grid=(n_groups + 1,),
            in_specs=[
                pl.BlockSpec((blk, D_MODEL), b_map),
                pl.BlockSpec((blk, SUBLANES), a_map),
                pl.BlockSpec((STEP_TILES * SUBLANES, TOK_TILE), a_map),
                pl.BlockSpec((blk, PLE_DIM), b_map),
                pl.BlockSpec(memory_space=pl.ANY),
                pl.BlockSpec(wple.shape, cmap),
                pl.BlockSpec(wgate.shape, cmap),
                pl.BlockSpec(gple.shape, cmap),
                pl.BlockSpec(gfin.shape, cmap),
            ],
            out_specs=pl.BlockSpec((blk, D_MODEL), b_map),
            scratch_shapes=[pltpu.VMEM((n_slots, LOC_ROWS * ROW_CHUNKS, LANES), U32),
                            pltpu.VMEM((n_slots, LOC_ROWS, D_MODEL), BF16),
                            pltpu.VMEM((n_slots, TOK_TILE, LOC_ROWS), BF16),
                            pltpu.SemaphoreType.DMA((n_slots,))]),
        out_shape=jax.ShapeDtypeStruct((n_rows, D_MODEL), F32),
        compiler_params=pltpu.CompilerParams(
            dimension_semantics=("arbitrary",), vmem_limit_bytes=VMEM_LIMIT),
        name="final",
    )(*tables, h, rt, rt_rows, p, ys, wple, wgate, gple, gfin)


def _ceil_to(x, m):
    return ((x + m - 1) // m) * m


def _excl_cumsum(a, axis):
    return jnp.cumsum(a, axis=axis) - a


def _copy_tables(n_copies, first_local, first_sorted, step, cap):
    ends = jnp.cumsum(n_copies, axis=1)
    j = jnp.arange(cap, dtype=I32)[None, :, None]
    e_of = jnp.minimum(jnp.sum(ends[:, None, :] <= j, axis=-1), N_EXPERTS - 1)
    pick = e_of[:, :, None] == jnp.arange(N_EXPERTS, dtype=I32)[None, None, :]
    take = lambda a: jnp.sum(jnp.where(pick, a[:, None, :], 0), axis=-1)
    back = step * (ends - n_copies)
    ahead = step * jnp.arange(cap, dtype=I32)[None, :]
    flat = lambda a: a.reshape(-1).astype(I32)
    return (flat(take(first_local - back) + ahead), flat(take(first_sorted - back) + ahead),
            ends[:, -1])


def kernel(x_prompt, x_sample, state_pool, p_prompt, p_sample, g_mix, w_in, w_pool, pool_scale,
           g_v, b_v, w_spatial, b_spatial, w_out, g_ffn, w_rg, b_rg, w_re, b_re, w_eg, w_eu, w_ed,
           w_ple, g_ple, w_ple_gate, g_final):
    batch, seq, _ = x_prompt.shape
    dec_batch, dec_seq, _ = x_sample.shape
    assert dec_seq == SUBLANES and g_mix.shape[0] == 1
    n_p = batch * seq
    n_s = dec_batch * dec_seq
    n_tok = n_p + n_s
    past_len = 16384

    row = lambda a: a.reshape(1, -1).astype(F32)
    wr = jnp.concatenate(
        [w_re[0], w_rg[0], jnp.zeros((D_MODEL, LANES - N_EXPERTS - N_EXPERT_GROUPS), F32)], axis=1)
    br = jnp.concatenate(
        [b_re[0], b_rg[0], jnp.zeros((LANES - N_EXPERTS - N_EXPERT_GROUPS,), F32)]).reshape(1, LANES)
    bsp_p = jnp.repeat(jnp.transpose(b_spatial[0]), GMLP_HEAD, axis=1)
    bsp_s = jnp.tile(bsp_p[:dec_seq], (CHUNK // dec_seq, 1))
    wsp_p = w_spatial[0].astype(BF16)
    wsp_s = jnp.tile(w_spatial[0][:, :dec_seq, :dec_seq].astype(BF16),
                     (1, CHUNK // dec_seq, CHUNK // dec_seq))

    def pool_band(first_chunk):
        t = jnp.arange(CHUNK, dtype=I32)[:, None]
        k = jnp.arange(POOL_SPAN, dtype=I32)[None, :]
        mats = []
        for w in POOL_WINDOWS:
            cnt = jnp.minimum(w, t + 1) if first_chunk else jnp.full_like(t, w)
            in_window = (k >= t + 17 - w) & (k <= t + 16)
            mats.append(jnp.where(in_window, 1.0 / cnt.astype(F32), 0.0) - (k == t + 16).astype(F32))
        return jnp.stack(mats).astype(BF16)

    def mixer_weights(wsp, bsp):
        return (row(g_mix[0]), w_in[0].astype(BF16), w_pool[0].astype(BF16), row(pool_scale[0]),
                row(g_v[0]), row(b_v[0]), wsp, bsp, w_out[0].astype(BF16), row(g_ffn[0]),
                jnp.transpose(wr).astype(BF16), br.reshape(LANES, 1), pool_band(False), pool_band(True))

    h_p, hn_p, rt_p, rtr_p, cnt_p, hist_p = _mixer_call(
        x_prompt, None, mixer_weights(wsp_p, bsp_p), sample=False, pos0=0)
    h_s, hn_s, rt_s, rtr_s, cnt_s, hist_s, v_s = _mixer_call(
        x_sample.reshape(n_s, D_MODEL), jnp.transpose(state_pool[0], (1, 0, 2)),
        mixer_weights(wsp_s, bsp_s),
        sample=True, pos0=past_len)

    n_run = jnp.concatenate([cnt_p, cnt_s], axis=0)[:, 0, :N_EXPERTS].astype(I32)
    n_even = _ceil_to(n_run, RUN_ALIGN)
    l_start = _excl_cumsum(n_even, 1)
    tile_rows = jnp.sum(n_even, axis=1)
    base = _excl_cumsum(n_even, 0)
    counts = jnp.sum(n_even, axis=0)
    padded = _ceil_to(counts, MOE_TILE)
    ends = jnp.cumsum(padded)
    offs = ends - padded
    g_start = offs[None, :] + base
    n8 = n_even // 8
    f4 = (n_even // 4) % 2
    f2 = (n_even // 2) % 2
    l8, g8, c8 = _copy_tables(n8, l_start, g_start, 8, COPY_CAP[0])
    l4, g4, c4 = _copy_tables(f4, l_start + 8 * n8, g_start + 8 * n8, 0, COPY_CAP[1])
    l2, g2, c2 = _copy_tables(f2, l_start + 8 * n8 + 4 * f4, g_start + 8 * n8 + 4 * f4, 0, COPY_CAP[2])
    copy_cnt = jnp.stack([c8, c4, c2], axis=1).reshape(-1).astype(I32)
    tile_rows = tile_rows.astype(I32)
    z_start = (offs + counts).astype(I32)
    z_big = ((padded - counts) // ZERO_ROWS).astype(I32)
    z_blocks = (((padded - counts) % ZERO_ROWS + 7) // 8).astype(I32)

    max_rows = 2 * n_tok + n_run.shape[0] * N_EXPERTS * (RUN_ALIGN - 1)
    n_tiles = max_rows // MOE_TILE + N_EXPERTS + 1
    n_used = (ends[-1] // MOE_TILE).astype(I32).reshape(1)
    tile_id = jnp.minimum(jnp.arange(n_tiles, dtype=I32), n_used - 1)
    tile_expert = jnp.sum((tile_id[:, None] * MOE_TILE) >= ends[None, :], axis=1).astype(I32)
    tile_expert = jnp.minimum(tile_expert, N_EXPERTS - 1)
    pick_e = tile_expert[:, None] == jnp.arange(N_EXPERTS, dtype=I32)[None, :]
    data_end = jnp.sum(jnp.where(pick_e, (offs + counts)[None, :], 0), axis=1)
    n_groups_used = jnp.clip((data_end - tile_id * MOE_TILE + ROW_GROUP - 1) // ROW_GROUP,
                             1, MOE_TILE // ROW_GROUP).astype(I32)
    e_ids = jnp.arange(N_EXPERTS, dtype=I32)
    nonempty = padded > 0
    seg_index = jnp.cumsum(nonempty.astype(I32)) - 1
    later = nonempty[None, :] & (e_ids[None, :] > e_ids[:, None])
    next_of = jnp.min(jnp.where(later, e_ids[None, :], N_EXPERTS), axis=1)
    next_of = jnp.where(next_of == N_EXPERTS, -1, next_of)
    next2_of = jnp.sum(jnp.where(next_of[:, None] == e_ids[None, :], next_of[None, :], 0), axis=1)
    next2_of = jnp.where(next_of >= 0, next2_of, -1)
    weight_buf = (jnp.sum(jnp.where(pick_e, seg_index[None, :], 0), axis=1) % WEIGHT_BUFS).astype(I32)
    next_expert = jnp.sum(jnp.where(pick_e, next_of[None, :], 0), axis=1).astype(I32)
    next2_expert = jnp.sum(jnp.where(pick_e, next2_of[None, :], 0), axis=1).astype(I32)

    copy_tabs = (l8, g8, l4, g4, l2, g2, copy_cnt, tile_rows)
    xs = _dispatch_call(copy_tabs + (z_start, z_big, z_blocks, n_used), hn_p, hn_s, rtr_p, rtr_s,
                        n_tiles * MOE_TILE)
    ys = _moe_call(tile_expert, n_used, weight_buf, next_expert, next2_expert, n_groups_used, xs, n_tiles,
                   w_eg[0], w_eu[0], w_ed[0])
    fin_w = (w_ple[0].astype(BF16), w_ple_gate[0].astype(BF16), row(g_ple[0]), row(g_final))
    y_p = _final_call(copy_tabs, h_p, rt_p, rtr_p, p_prompt[0].reshape(n_p, PLE_DIM), ys, *fin_w,
                      tile_base=0)
    y_s = _final_call(copy_tabs, h_s, rt_s, rtr_s, p_sample[0].reshape(n_s, PLE_DIM), ys, *fin_w,
                      tile_base=n_p // TOK_TILE)

    return (y_p.reshape(batch, seq, D_MODEL),
            y_s.reshape(dec_batch, dec_seq, D_MODEL),
            hist_p[None],
            jnp.transpose(hist_s, (1, 0, 2))[None],
            v_s.reshape(1, dec_batch, dec_seq, GMLP_WIDTH))
```

```python
import functools

import jax
import jax.numpy as jnp
from jax import lax
from jax.experimental import pallas as pl
from jax.experimental.pallas import tpu as pltpu

D_MODEL = 1024
POOL_WIDTH = 512
GMLP_WIDTH = 512
IN_WIDTH = POOL_WIDTH + 2 * GMLP_WIDTH
POOL_WINDOWS = (2, 4, 8, 16)
POOL_GROUP = 128
POOL_HIST = 15
CHUNK = 128
POOL_SPAN = CHUNK + 16
N_GMLP_HEADS = 4
GMLP_HEAD = 128
N_EXPERT_GROUPS = 4
EXPERTS_PER_GROUP = 8
N_EXPERTS = 32
D_EXPERT = 512
PLE_DIM = 256
EPS = 1e-6

LANES = 128
SUBLANES = 8
HALF = D_MODEL // 2
ROW_CHUNKS = HALF // LANES

TOK_TILE = 256
STEP_TILES = 2
MIX_TILE = 1024
MOE_TILE = 640
ROW_GROUP = 128
WEIGHT_BUFS = 3
X_BUFS = 3
RUN_ALIGN = 2
COPY_ROWS = (8, 4, 2)
ZERO_ROWS = 64
LOC_ROWS = 2 * TOK_TILE + N_EXPERTS * (RUN_ALIGN - 1)
COPY_CAP = (LOC_ROWS // 8 + 1, N_EXPERTS, N_EXPERTS)
VMEM_LIMIT = 56 * 1024 * 1024

F32 = jnp.float32
BF16 = jnp.bfloat16
I32 = jnp.int32
U32 = jnp.uint32


def _rms(x, g):
    return x * lax.rsqrt(jnp.mean(x * x, axis=-1, keepdims=True) + EPS) * g


def _gelu_tanh(x):
    inner = x * (0.7978845608028654 + (0.7978845608028654 * 0.044715) * (x * x))
    half = 0.5 * x
    return half + half * jnp.tanh(inner)


def _store_packed(ref, val, n_rows):
    packed = pltpu.pack_elementwise([val[:, :HALF], val[:, HALF:]], packed_dtype=BF16)
    for k in range(ROW_CHUNKS):
        ref[pl.ds(k, n_rows, stride=ROW_CHUNKS), :] = packed[:, k * LANES:(k + 1) * LANES]


def _load_packed(ref, n_rows):
    packed = jnp.concatenate(
        [ref[pl.ds(k, n_rows, stride=ROW_CHUNKS), :] for k in range(ROW_CHUNKS)], axis=-1)
    lo = pltpu.unpack_elementwise(packed, index=0, packed_dtype=BF16, unpacked_dtype=F32)
    hi = pltpu.unpack_elementwise(packed, index=1, packed_dtype=BF16, unpacked_dtype=F32)
    return lo, hi


def _rows(start, n):
    return pl.ds(pl.multiple_of(start * ROW_CHUNKS, SUBLANES), n * ROW_CHUNKS)


def _route(hn, wrt_ref, brc_ref, n_tok):
    logits = lax.dot_general(wrt_ref[...], hn, (((1,), (1,)), ((), ())),
                             preferred_element_type=F32) + brc_ref[...]
    neg = jnp.float32(-jnp.inf)
    big = jnp.float32(1e9)
    sub = lax.broadcasted_iota(I32, (SUBLANES, n_tok), 0).astype(F32)
    gl = jnp.where(sub < N_EXPERT_GROUPS, logits[N_EXPERTS:N_EXPERTS + SUBLANES, :], neg)
    gmax = jnp.max(gl, axis=0, keepdims=True)
    gidx = jnp.min(jnp.where(gl == gmax, sub, big), axis=0, keepdims=True)
    gprob = 1.0 / jnp.sum(jnp.exp(gl - gmax), axis=0, keepdims=True)
    assert EXPERTS_PER_GROUP == SUBLANES
    el = logits[0:SUBLANES, :]
    for g in range(1, N_EXPERT_GROUPS):
        el = jnp.where(gidx == g, logits[g * SUBLANES:(g + 1) * SUBLANES, :], el)
    m1 = jnp.max(el, axis=0, keepdims=True)
    o1 = jnp.min(jnp.where(el == m1, sub, big), axis=0, keepdims=True)
    el2 = jnp.where(sub == o1, neg, el)
    m2 = jnp.max(el2, axis=0, keepdims=True)
    o2 = jnp.min(jnp.where(el2 == m2, sub, big), axis=0, keepdims=True)
    i1 = gidx * EXPERTS_PER_GROUP + o1
    i2 = gidx * EXPERTS_PER_GROUP + o2
    t = jnp.exp(m2 - m1)
    w1 = gprob / (1.0 + t)
    w2 = gprob * t / (1.0 + t)

    erow = lax.broadcasted_iota(I32, (LANES, n_tok), 0).astype(F32)
    sel1 = erow == i1
    sel2 = erow == i2
    oh = jnp.where(sel1 | sel2, 1.0, 0.0).astype(BF16)
    r = lax.broadcasted_iota(I32, (n_tok, n_tok), 0)
    c = lax.broadcasted_iota(I32, (n_tok, n_tok), 1)
    earlier = jnp.where(r < c, 1.0, 0.0).astype(BF16)
    before = jnp.dot(oh, earlier, preferred_element_type=F32)
    counts_b = jnp.dot(oh, jnp.ones((n_tok, LANES), BF16), preferred_element_type=F32)
    assert RUN_ALIGN == 2
    c_even = counts_b + (counts_b - 2.0 * jnp.floor(counts_b * 0.5))
    c_hi = jnp.floor(c_even * (1.0 / 256.0))
    er = lax.broadcasted_iota(I32, (LANES, LANES), 0)
    ec = lax.broadcasted_iota(I32, (LANES, LANES), 1)
    lower = jnp.where(ec < er, 1.0, 0.0).astype(BF16)
    pieces = jnp.concatenate([c_hi, c_even - 256.0 * c_hi], axis=1).astype(BF16)
    prefix = jnp.dot(lower, pieces, preferred_element_type=F32)
    lstart = 256.0 * prefix[:, :LANES] + prefix[:, LANES:]
    placed = before + jnp.concatenate([lstart] * (n_tok // LANES), axis=1)
    lpos1 = jnp.sum(jnp.where(sel1, placed, 0.0), axis=0, keepdims=True)
    lpos2 = jnp.sum(jnp.where(sel2, placed, 0.0), axis=0, keepdims=True)

    stacked = jnp.where(erow == 0.0, i1, 0.0)
    for k, val in enumerate((i2, lpos1, lpos2, w1, w2), start=1):
        stacked = jnp.where(erow == float(k), val, stacked)
    return stacked.T[:, :SUBLANES], stacked[0:SUBLANES, :], counts_b.T[0:1, :]


def _mixer_kernel(*refs, n_tok, sample, pos0):
    if sample:
        (x_ref, hist_ref, gmix_ref, win_ref, wpool_ref, pscale_ref, gv_ref, bv_ref, wsp_ref,
         bsp_ref, wout_ref, gffn_ref, wr_ref, br_ref, band_ref, band0_ref,
         h_ref, hn_ref, rt_ref, rtr_ref, cnt_ref, hist_out_ref, v_ref, ext_scr, pool_scr) = refs
    else:
        (x_ref, gmix_ref, win_ref, wpool_ref, pscale_ref, gv_ref, bv_ref, wsp_ref,
         bsp_ref, wout_ref, gffn_ref, wr_ref, br_ref, band_ref, band0_ref,
         h_ref, hn_ref, rt_ref, rtr_ref, cnt_ref, hist_out_ref, ext_scr) = refs

    x = x_ref[...]
    xn = _rms(x, gmix_ref[...])
    z = jnp.dot(xn.astype(BF16), win_ref[...], preferred_element_type=F32)
    a = z[:, :POOL_WIDTH]
    uv = _gelu_tanh(z[:, POOL_WIDTH:])
    u = uv[:, :GMLP_WIDTH]
    v = uv[:, GMLP_WIDTH:]
    mu = jnp.mean(v, axis=-1, keepdims=True)
    vc = v - mu
    v = vc * lax.rsqrt(jnp.mean(vc * vc, axis=-1, keepdims=True) + EPS) * gv_ref[...] + bv_ref[...]

    pooled = []
    if sample:
        n_seq = n_tok // SUBLANES
        n_grp = len(POOL_WINDOWS)
        gcols = [slice(gi * POOL_GROUP, (gi + 1) * POOL_GROUP) for gi in range(n_grp)]
        ext_scr[1:1 + POOL_HIST] = hist_ref[...]
        for gi in range(n_grp):
            pool_scr[gi] = a[:, gcols[gi]]
        for i in range(SUBLANES):
            ext_scr[1 + POOL_HIST + i] = jnp.concatenate(
                [pool_scr[gi, pl.ds(i, n_seq, stride=SUBLANES), :] for gi in range(n_grp)], axis=-1)
        hist_out_ref[...] = ext_scr[1 + SUBLANES:]
        for i in range(SUBLANES):
            back = [ext_scr[1 + POOL_HIST + i - j] for j in range(max(POOL_WINDOWS))]
            for gi, w in enumerate(POOL_WINDOWS):
                win = back[0][:, gcols[gi]]
                for j in range(1, w):
                    win = win + back[j][:, gcols[gi]]
                pool_scr[gi, pl.ds(i, n_seq, stride=SUBLANES), :] = (
                    win * (1.0 / min(w, pos0 + i + 1)) - back[0][:, gcols[gi]])
        pooled = [pool_scr[gi] for gi in range(n_grp)]
    else:
        l = pl.program_id(1)

        @pl.when(l == 0)
        def _():
            ext_scr[0:16, :] = jnp.zeros((16, POOL_WIDTH), F32)

        ext_scr[16:16 + n_tok, :] = a
        chunks = []
        for ci in range(n_tok // CHUNK):
            ext_c = ext_scr[ci * CHUNK:ci * CHUNK + POOL_SPAN, :].astype(BF16)
            groups = []
            for gi in range(len(POOL_WINDOWS)):
                band = band_ref[gi]
                if ci == 0:
                    band = jnp.where(l == 0, band0_ref[gi], band)
                groups.append(jnp.dot(band, ext_c[:, gi * POOL_GROUP:(gi + 1) * POOL_GROUP],
                                      preferred_element_type=F32))
            chunks.append(groups)
        pooled = [jnp.concatenate([chunks[ci][gi] for ci in range(n_tok // CHUNK)], axis=0)
                  for gi in range(len(POOL_WINDOWS))]
        tail = ext_scr[n_tok + 1:n_tok + 16, :]
        hist_out_ref[...] = tail
        ext_scr[1:16, :] = tail

    pool_out = jnp.concatenate(
        [jnp.dot(pooled[gi].astype(BF16), wpool_ref[gi], preferred_element_type=F32)
         for gi in range(len(POOL_WINDOWS))], axis=-1) * pscale_ref[...]

    blk = SUBLANES if sample else CHUNK
    tr = lax.broadcasted_iota(I32, (CHUNK, CHUNK), 0)
    sc = lax.broadcasted_iota(I32, (CHUNK, CHUNK), 1)
    mask = (sc <= tr) & ((sc // blk) == (tr // blk))
    wsp = [jnp.where(mask, wsp_ref[hh], jnp.zeros((), BF16)) for hh in range(N_GMLP_HEADS)]
    v_bf = v.astype(BF16)
    mixed_chunks = []
    for ci in range(n_tok // CHUNK):
        rows = slice(ci * CHUNK, (ci + 1) * CHUNK)
        heads = [jnp.dot(wsp[hh], v_bf[rows, hh * GMLP_HEAD:(hh + 1) * GMLP_HEAD],
                         preferred_element_type=F32) for hh in range(N_GMLP_HEADS)]
        mixed_chunks.append(jnp.concatenate(heads, axis=-1) + bsp_ref[...])
    mixed = jnp.concatenate(mixed_chunks, axis=0)
    gmlp_out = u * mixed

    cat = jnp.concatenate([pool_out, gmlp_out], axis=-1)
    h = x + jnp.dot(cat.astype(BF16), wout_ref[...], preferred_element_type=F32)
    hn = _rms(h, gffn_ref[...]).astype(BF16)

    h_ref[...] = h
    hn_ref[...] = hn
    if sample:
        v_ref[...] = v
    for s in range(n_tok // TOK_TILE):
        rows = slice(s * TOK_TILE, (s + 1) * TOK_TILE)
        rt, rt_rows, counts = _route(hn[rows], wr_ref, br_ref, TOK_TILE)
        rt_ref[rows, :] = rt
        rtr_ref[s * SUBLANES:(s + 1) * SUBLANES, :] = rt_rows
        cnt_ref[s] = counts


def _full(shape):
    return pl.BlockSpec(shape, lambda *_: (0,) * len(shape))


def _mixer_call(x, hist, weights, *, sample, pos0):
    w_specs = [_full(w.shape) for w in weights]
    n_tok = MIX_TILE
    if sample:
        n_rows = x.shape[0]
        n_seq = n_tok // SUBLANES
        grid = (n_rows // n_tok,)
        tok_map = lambda i: (i, 0)
        tile_map = lambda i: (i, 0, 0)
        hist_spec = pl.BlockSpec((POOL_HIST, n_seq, POOL_WIDTH), lambda i: (0, i, 0))
        in_specs = [pl.BlockSpec((n_tok, D_MODEL), tok_map), hist_spec] + w_specs
        hist_shape = jax.ShapeDtypeStruct(hist.shape, F32)
        scratch = [pltpu.VMEM((1 + POOL_HIST + SUBLANES, n_seq, POOL_WIDTH), F32),
                   pltpu.VMEM((len(POOL_WINDOWS), n_tok, POOL_GROUP), F32)]
        args = (x, hist)
    else:
        b, seq, _ = x.shape
        n_rows = b * seq
        n_l = seq // n_tok
        grid = (b, n_l)
        tok_map = lambda bi, li: (bi * n_l + li, 0)
        tile_map = lambda bi, li: (bi * n_l + li, 0, 0)
        in_specs = [pl.BlockSpec((None, n_tok, D_MODEL), lambda bi, li: (bi, li, 0))] + w_specs
        hist_shape = jax.ShapeDtypeStruct((b, POOL_HIST, POOL_WIDTH), F32)
        hist_spec = pl.BlockSpec((None, POOL_HIST, POOL_WIDTH), lambda bi, li: (bi, 0, 0))
        scratch = [pltpu.VMEM((16 + n_tok, POOL_WIDTH), F32)]
        args = (x,)
    out_shape = [jax.ShapeDtypeStruct((n_rows, D_MODEL), F32),
                 jax.ShapeDtypeStruct((n_rows, D_MODEL), BF16),
                 jax.ShapeDtypeStruct((n_rows, SUBLANES), F32),
                 jax.ShapeDtypeStruct((n_rows // TOK_TILE * SUBLANES, TOK_TILE), F32),
                 jax.ShapeDtypeStruct((n_rows // TOK_TILE, 1, LANES), F32),
                 hist_shape]
    out_specs = [pl.BlockSpec((n_tok, D_MODEL), tok_map),
                 pl.BlockSpec((n_tok, D_MODEL), tok_map),
                 pl.BlockSpec((n_tok, SUBLANES), tok_map),
                 pl.BlockSpec((n_tok // TOK_TILE * SUBLANES, TOK_TILE), tok_map),
                 pl.BlockSpec((n_tok // TOK_TILE, 1, LANES), tile_map),
                 hist_spec]
    if sample:
        out_shape.append(jax.ShapeDtypeStruct((n_rows, GMLP_WIDTH), F32))
        out_specs.append(pl.BlockSpec((n_tok, GMLP_WIDTH), tok_map))
    return pl.pallas_call(
        functools.partial(_mixer_kernel, n_tok=n_tok, sample=sample, pos0=pos0),
        grid=grid, in_specs=in_specs, out_specs=out_specs, out_shape=out_shape,
        scratch_shapes=scratch,
        compiler_params=pltpu.CompilerParams(
            dimension_semantics=("arbitrary",) * len(grid), vmem_limit_bytes=VMEM_LIMIT),
        name="mixer_sample" if sample else "mixer_prompt",
    )(*args, *weights)


def _for_each_copy(tile, cnt_ref, fn, active=True):
    for s, n_rows in enumerate(COPY_ROWS):
        def body(j, carry, s=s, n_rows=n_rows):
            fn(s, n_rows, tile * COPY_CAP[s] + j)
            return carry
        lax.fori_loop(0, jnp.where(active, cnt_ref[tile * len(COPY_ROWS) + s], 0), body, 0)


def _wait_rows(n_rows_total, make_wait):
    units = n_rows_total // RUN_ALIGN
    bit = 0
    while (RUN_ALIGN << bit) <= LOC_ROWS:
        @pl.when((units >> bit) & 1 == 1)
        def _(bit=bit):
            make_wait(RUN_ALIGN << bit).wait()
        bit += 1


def _dispatch_kernel(l8, g8, l4, g4, l2, g2, cnt_ref, trow_ref, zs_ref, zbig_ref, zb_ref, nu_ref,
                     hnp_ref, hns_ref, rtp_ref, rts_ref, xs_ref, loc, zbuf, sem, zsem,
                     *, n_prompt_steps):
    i = pl.program_id(0)
    n_steps = pl.num_programs(0)
    slot = i % 2
    local_tabs, sorted_tabs = (l8, l4, l2), (g8, g4, g2)

    @pl.when(i == 0)
    def _():
        zbuf[...] = jnp.zeros(zbuf.shape, U32)
        n_tiles = xs_ref.shape[0] // (MOE_TILE * ROW_CHUNKS)

        def zero_copy(e, c, n_rows, first):
            return pltpu.make_async_copy(zbuf.at[pl.ds(0, n_rows * ROW_CHUNKS)],
                                         xs_ref.at[_rows(first + c * n_rows, n_rows)], zsem)

        def tail_copy(t):
            return pltpu.make_async_copy(zbuf, xs_ref.at[_rows(t * MOE_TILE, MOE_TILE)], zsem)

        def per_expert(e, totals):
            def big(c, carry):
                zero_copy(e, c, ZERO_ROWS, zs_ref[e]).start()
                return carry

            def small(c, carry):
                zero_copy(e, c, 8, zs_ref[e] + zbig_ref[e] * ZERO_ROWS).start()
                return carry

            lax.fori_loop(0, zbig_ref[e], big, 0)
            lax.fori_loop(0, zb_ref[e], small, 0)
            return totals[0] + zbig_ref[e], totals[1] + zb_ref[e]

        n_big, n_small = lax.fori_loop(0, N_EXPERTS, per_expert, (0, 0))

        def wait_big(_, carry):
            zero_copy(0, 0, ZERO_ROWS, 0).wait()
            return carry

        def wait_small(_, carry):
            zero_copy(0, 0, 8, 0).wait()
            return carry

        lax.fori_loop(0, n_big, wait_big, 0)
        lax.fori_loop(0, n_small, wait_small, 0)

        def tail_start(t, carry):
            tail_copy(t).start()
            return carry

        def tail_wait(t, carry):
            tail_copy(t).wait()
            return carry

        spill = pltpu.make_async_copy(zbuf.at[pl.ds(0, 8 * ROW_CHUNKS)],
                                      xs_ref.at[_rows(n_tiles * MOE_TILE, 8)], zsem)
        lax.fori_loop(nu_ref[0], n_tiles, tail_start, 0)
        spill.start()
        lax.fori_loop(nu_ref[0], n_tiles, tail_wait, 0)
        spill.wait()

    is_prompt = i < n_prompt_steps
    rt_rows = jnp.where(is_prompt, rtp_ref[...], rts_ref[...])
    row1, row2 = rt_rows[2:3, :], rt_rows[3:4, :]
    grow = lax.broadcasted_iota(I32, (LOC_ROWS, TOK_TILE), 0).astype(F32)
    perm = jnp.where((grow == row1) | (grow == row2), 1.0, 0.0).astype(BF16)
    hn = jnp.where(is_prompt, hnp_ref[...], hns_ref[...])
    grouped = jnp.dot(perm, hn, preferred_element_type=F32)
    _store_packed(loc.at[slot], grouped, LOC_ROWS)

    def run_copy(buf_slot, s, n_rows, k):
        src = loc.at[buf_slot, _rows(local_tabs[s][k], n_rows)]
        dst = xs_ref.at[_rows(sorted_tabs[s][k], n_rows)]
        return pltpu.make_async_copy(src, dst, sem.at[buf_slot])

    _for_each_copy(i, cnt_ref, lambda s, n_rows, k: run_copy(slot, s, n_rows, k).start())

    def drain(tile, buf_slot):
        def make_wait(n):
            return pltpu.make_async_copy(loc.at[buf_slot, pl.ds(0, n * ROW_CHUNKS)],
                                         xs_ref.at[pl.ds(0, n * ROW_CHUNKS)], sem.at[buf_slot])
        _wait_rows(trow_ref[tile], make_wait)

    @pl.when(i > 0)
    def _():
        drain(i - 1, 1 - slot)

    @pl.when(i == n_steps - 1)
    def _():
        drain(i, slot)


def _clamped_maps(np_steps):
    pmap = lambda i, *_: (jnp.minimum(i, np_steps - 1), 0)
    smap = lambda i, *_: (jnp.maximum(i - np_steps, 0), 0)
    return pmap, smap


def _dispatch_call(tables, hn_p, hn_s, rtr_p, rtr_s, n_sorted_rows):
    n_p, n_s = hn_p.shape[0], hn_s.shape[0]
    np_steps = n_p // TOK_TILE
    n_steps = np_steps + n_s // TOK_TILE
    pmap, smap = _clamped_maps(np_steps)
    out_rows = (n_sorted_rows + 8) * ROW_CHUNKS
    return pl.pallas_call(
        functools.partial(_dispatch_kernel, n_prompt_steps=np_steps),
        grid_spec=pltpu.PrefetchScalarGridSpec(
            num_scalar_prefetch=len(tables), grid=(n_steps,),
            in_specs=[pl.BlockSpec((TOK_TILE, D_MODEL), pmap),
                      pl.BlockSpec((TOK_TILE, D_MODEL), smap),
                      pl.BlockSpec((SUBLANES, TOK_TILE), pmap),
                      pl.BlockSpec((SUBLANES, TOK_TILE), smap)],
            out_specs=pl.BlockSpec(memory_space=pl.ANY),
            scratch_shapes=[pltpu.VMEM((2, LOC_ROWS * ROW_CHUNKS, LANES), U32),
                            pltpu.VMEM((MOE_TILE * ROW_CHUNKS, LANES), U32),
                            pltpu.SemaphoreType.DMA((2,)),
                            pltpu.SemaphoreType.DMA(())]),
        out_shape=jax.ShapeDtypeStruct((out_rows, LANES), U32),
        compiler_params=pltpu.CompilerParams(
            dimension_semantics=("arbitrary",), vmem_limit_bytes=VMEM_LIMIT),
        name="dispatch",
    )(*tables, hn_p, hn_s, rtr_p, rtr_s)


def _moe_kernel(te_ref, nu_ref, par_ref, nxt_ref, nxt2_ref, ngrp_ref, xs_hbm, wg_hbm, wu_hbm, wd_hbm, ys_ref,
                xbuf, zero_tile, wg_f32, wu_f32, wd_f32, wg_bf, wu_bf, wd_bf, xsem, wsem):
    i = pl.program_id(0)
    n_used = nu_ref[0]
    used = i < n_used
    x_slot = i % X_BUFS

    def x_copy(tile):
        slot = tile % X_BUFS
        return pltpu.make_async_copy(xs_hbm.at[_rows(tile * MOE_TILE, MOE_TILE)], xbuf.at[slot],
                                     xsem.at[slot])

    def store_zeros(ref, n_rows):
        ij = (lax.broadcasted_iota(I32, (n_rows, D_MODEL), 0)
              + lax.broadcasted_iota(I32, (n_rows, D_MODEL), 1))
        _store_packed(ref, jnp.where(ij < jnp.minimum(nu_ref[0], 0), 1.0, 0.0), n_rows)

    def weight_copies(expert, buf):
        return [pltpu.make_async_copy(hbm.at[expert], vmem.at[buf], wsem.at[buf])
                for hbm, vmem in ((wg_hbm, wg_f32), (wu_hbm, wu_f32), (wd_hbm, wd_f32))]

    @pl.when(used)
    def _():
        @pl.when(i == 0)
        def _():
            for ahead in range(X_BUFS - 1):
                @pl.when(ahead < n_used)
                def _(ahead=ahead):
                    x_copy(ahead).start()

        @pl.when(i + (X_BUFS - 1) < n_used)
        def _():
            x_copy(i + (X_BUFS - 1)).start()

        prev = te_ref[jnp.maximum(i - 1, 0)]

        @pl.when((i == 0) | (te_ref[i] != prev))
        def _():
            buf = par_ref[i]

            @pl.when(i == 0)
            def _():
                for cp in weight_copies(te_ref[0], 0):
                    cp.start()

                @pl.when(nxt_ref[0] >= 0)
                def _():
                    for cp in weight_copies(nxt_ref[0], 1):
                        cp.start()

            for cp in weight_copies(te_ref[i], buf):
                cp.wait()
            wg_bf[...] = wg_f32[buf].astype(BF16)
            wu_bf[...] = wu_f32[buf].astype(BF16)
            wd_bf[...] = wd_f32[buf].astype(BF16)

            @pl.when(nxt2_ref[i] >= 0)
            def _():
                for cp in weight_copies(nxt2_ref[i], (buf + 2) % WEIGHT_BUFS):
                    cp.start()

        x_copy(i).wait()

        def expert_rows(n_rows):
            lo, hi = _load_packed(xbuf.at[x_slot], n_rows)
            x = jnp.concatenate([lo, hi], axis=-1).astype(BF16)
            g = jnp.dot(x, wg_bf[...], preferred_element_type=F32)
            u = jnp.dot(x, wu_bf[...], preferred_element_type=F32)
            hh = (g * jax.nn.sigmoid(g)) * u
            y = jnp.dot(hh.astype(BF16), wd_bf[...], preferred_element_type=F32)
            _store_packed(ys_ref, y, n_rows)

        for n_groups in range(1, MOE_TILE // ROW_GROUP + 1):
            @pl.when(ngrp_ref[i] == n_groups)
            def _(n_rows=n_groups * ROW_GROUP):
                expert_rows(n_rows)
                if n_rows < MOE_TILE:
                    rest = MOE_TILE - n_rows
                    store_zeros(ys_ref.at[pl.ds(n_rows * ROW_CHUNKS, rest * ROW_CHUNKS)], rest)

    @pl.when(i == n_used)
    def _():
        store_zeros(zero_tile, MOE_TILE)

    @pl.when(jnp.logical_not(used))
    def _():
        ys_ref[...] = zero_tile[...]


def _moe_call(tile_expert, n_used, weight_buf, next_expert, next2_expert, n_groups_used, xs, n_tiles,
              w_eg, w_eu, w_ed):
    blk = MOE_TILE * ROW_CHUNKS
    any_spec = pl.BlockSpec(memory_space=pl.ANY)
    return pl.pallas_call(
        _moe_kernel,
        grid_spec=pltpu.PrefetchScalarGridSpec(
            num_scalar_prefetch=6, grid=(n_tiles,),
            in_specs=[
                any_spec, any_spec, any_spec, any_spec,
            ],
            out_specs=pl.BlockSpec((blk, LANES), lambda i, *_: (i, 0)),
            scratch_shapes=[pltpu.VMEM((X_BUFS, blk, LANES), U32),
                            pltpu.VMEM((blk, LANES), U32),
                            pltpu.VMEM((WEIGHT_BUFS, D_MODEL, D_EXPERT), F32),
                            pltpu.VMEM((WEIGHT_BUFS, D_MODEL, D_EXPERT), F32),
                            pltpu.VMEM((WEIGHT_BUFS, D_EXPERT, D_MODEL), F32),
                            pltpu.VMEM((D_MODEL, D_EXPERT), BF16),
                            pltpu.VMEM((D_MODEL, D_EXPERT), BF16),
                            pltpu.VMEM((D_EXPERT, D_MODEL), BF16),
                            pltpu.SemaphoreType.DMA((X_BUFS,)),
                            pltpu.SemaphoreType.DMA((WEIGHT_BUFS,))]),
        out_shape=jax.ShapeDtypeStruct((n_tiles * blk, LANES), U32),
        compiler_params=pltpu.CompilerParams(
            dimension_semantics=("arbitrary",), vmem_limit_bytes=VMEM_LIMIT),
        name="experts",
    )(tile_expert, n_used, weight_buf, next_expert, next2_expert, n_groups_used, xs, w_eg, w_eu, w_ed)


def _final_kernel(l8, g8, l4, g4, l2, g2, cnt_ref, trow_ref,
                  h_ref, rt_ref, rtr_ref, p_ref, ys_ref, wple_ref, wgate_ref, gple_ref, gfin_ref, out_ref,
                  ybuf, yb_s, perm_s, sem, *, tile_base, n_tiles):
    s = pl.program_id(0)
    n_groups = n_tiles // STEP_TILES
    a_par = s % 2
    local_tabs, sorted_tabs = (l8, l4, l2), (g8, g4, g2)

    def slot_of(parity, sub):
        return parity * STEP_TILES + sub

    def run_copy(buf_slot, sz, n_rows, k):
        src = ys_ref.at[_rows(sorted_tabs[sz][k], n_rows)]
        dst = ybuf.at[buf_slot, _rows(local_tabs[sz][k], n_rows)]
        return pltpu.make_async_copy(src, dst, sem.at[buf_slot])

    def issue(group, parity, active):
        for sub in range(STEP_TILES):
            tile = jnp.minimum(group, n_groups - 1) * STEP_TILES + sub
            buf_slot = slot_of(parity, sub)
            _for_each_copy(tile_base + tile, cnt_ref,
                           lambda sz, n_rows, k, b=buf_slot: run_copy(b, sz, n_rows, k).start(),
                           active)

    @pl.when(s == 0)
    def _():
        ybuf[...] = jnp.zeros(ybuf.shape, U32)
        yb_s[...] = jnp.zeros(yb_s.shape, BF16)
        perm_s[...] = jnp.zeros(perm_s.shape, BF16)
        issue(0, 0, True)

    issue(s + 1, 1 - a_par, s + 1 < n_groups)

    for sub in range(STEP_TILES):
        a_slot = slot_of(a_par, sub)

        def make_wait(n, a_slot=a_slot):
            return pltpu.make_async_copy(ys_ref.at[pl.ds(0, n * ROW_CHUNKS)],
                                         ybuf.at[a_slot, pl.ds(0, n * ROW_CHUNKS)], sem.at[a_slot])
        tile = tile_base + jnp.minimum(s, n_groups - 1) * STEP_TILES + sub
        _wait_rows(jnp.where(s < n_groups, trow_ref[tile], 0), make_wait)

    for sub in range(STEP_TILES):
        rows = slice(sub * TOK_TILE, (sub + 1) * TOK_TILE)
        b_slot = slot_of(1 - a_par, sub)
        h = h_ref[rows, :]
        h = h + jnp.dot(perm_s[b_slot], yb_s[b_slot], preferred_element_type=F32)
        r = _rms(h, gple_ref[...])
        gate = jax.nn.sigmoid(jnp.dot(r.astype(BF16), wgate_ref[...], preferred_element_type=F32))
        h = h + jnp.dot(p_ref[rows, :].astype(BF16), wple_ref[...],
                        preferred_element_type=F32) * gate
        out_ref[rows, :] = _rms(h, gfin_ref[...])

    for sub in range(STEP_TILES):
        rows = slice(sub * TOK_TILE, (sub + 1) * TOK_TILE)
        a_slot = slot_of(a_par, sub)
        rt = rt_ref[rows, :]
        lpos1, lpos2 = rt[:, 2:3], rt[:, 3:4]
        lane = lax.broadcasted_iota(I32, (TOK_TILE, LOC_ROWS), 1).astype(F32)
        perm_s[a_slot] = jnp.where((lane == lpos1) | (lane == lpos2), 1.0, 0.0).astype(BF16)
        rt_rows = rtr_ref[sub * SUBLANES:(sub + 1) * SUBLANES, :]
        row1, row2, w1_row, w2_row = (rt_rows[k:k + 1, :] for k in range(2, 6))
        grow = lax.broadcasted_iota(I32, (LOC_ROWS, TOK_TILE), 0).astype(F32)
        wrow = jnp.sum(jnp.where(grow == row1, w1_row, 0.0) + jnp.where(grow == row2, w2_row, 0.0),
                       axis=-1, keepdims=True)
        lo, hi = _load_packed(ybuf.at[a_slot], LOC_ROWS)
        yb_s[a_slot] = (jnp.concatenate([lo, hi], axis=-1) * wrow).astype(BF16)


def _final_call(tables, h, rt, rt_rows, p, ys, wple, wgate, gple, gfin, *, tile_base):
    n_rows = h.shape[0]
    n_tiles = n_rows // TOK_TILE
    n_groups = n_tiles // STEP_TILES
    blk = STEP_TILES * TOK_TILE
    b_map = lambda s, *_: (jnp.maximum(s - 1, 0), 0)
    a_map = lambda s, *_: (jnp.minimum(s, n_groups - 1), 0)
    cmap = lambda s, *_: (0, 0)
    n_slots = 2 * STEP_TILES
    return pl.pallas_call(
        functools.partial(_final_kernel, tile_base=tile_base, n_tiles=n_tiles),
        grid_spec=pltpu.PrefetchScalarGridSpec(
            num_scalar_prefetch=len(tables), grid=(n_groups + 1,),
            in_specs=[
                pl.BlockSpec((blk, D_MODEL), b_map),
                pl.BlockSpec((blk, SUBLANES), a_map),
                pl.BlockSpec((STEP_TILES * SUBLANES, TOK_TILE), a_map),
                pl.BlockSpec((blk, PLE_DIM), b_map),
                pl.BlockSpec(memory_space=pl.ANY),
                pl.BlockSpec(wple.shape, cmap),
                pl.BlockSpec(wgate.shape, cmap),
                pl.BlockSpec(gple.shape, cmap),
                pl.BlockSpec(gfin.shape, cmap),
            ],
            out_specs=pl.BlockSpec((blk, D_MODEL), b_map),
            scratch_shapes=[pltpu.VMEM((n_slots, LOC_ROWS * ROW_CHUNKS, LANES), U32),
                            pltpu.VMEM((n_slots, LOC_ROWS, D_MODEL), BF16),
                            pltpu.VMEM((n_slots, TOK_TILE, LOC_ROWS), BF16),
                            pltpu.SemaphoreType.DMA((n_slots,))]),
        out_shape=jax.ShapeDtypeStruct((n_rows, D_MODEL), F32),
        compiler_params=pltpu.CompilerParams(
            dimension_semantics=("arbitrary",), vmem_limit_bytes=VMEM_LIMIT),
        name="final",
    )(*tables, h, rt, rt_rows, p, ys, wple, wgate, gple, gfin)


def _ceil_to(x, m):
    return ((x + m - 1) // m) * m


def _excl_cumsum(a, axis):
    return jnp.cumsum(a, axis=axis) - a


def _copy_tables(n_copies, first_local, first_sorted, step, cap):
    ends = jnp.cumsum(n_copies, axis=1)
    j = jnp.arange(cap, dtype=I32)[None, :, None]
    e_of = jnp.minimum(jnp.sum(ends[:, None, :] <= j, axis=-1), N_EXPERTS - 1)
    pick = e_of[:, :, None] == jnp.arange(N_EXPERTS, dtype=I32)[None, None, :]
    take = lambda a: jnp.sum(jnp.where(pick, a[:, None, :], 0), axis=-1)
    back = step * (ends - n_copies)
    ahead = step * jnp.arange(cap, dtype=I32)[None, :]
    flat = lambda a: a.reshape(-1).astype(I32)
    return (flat(take(first_local - back) + ahead), flat(take(first_sorted - back) + ahead),
            ends[:, -1])


def kernel(x_prompt, x_sample, state_pool, p_prompt, p_sample, g_mix, w_in, w_pool, pool_scale,
           g_v, b_v, w_spatial, b_spatial, w_out, g_ffn, w_rg, b_rg, w_re, b_re, w_eg, w_eu, w_ed,
           w_ple, g_ple, w_ple_gate, g_final):
    batch, seq, _ = x_prompt.shape
    dec_batch, dec_seq, _ = x_sample.shape
    assert dec_seq == SUBLANES and g_mix.shape[0] == 1
    n_p = batch * seq
    n_s = dec_batch * dec_seq
    n_tok = n_p + n_s
    past_len = 16384

    row = lambda a: a.reshape(1, -1).astype(F32)
    wr = jnp.concatenate(
        [w_re[0], w_rg[0], jnp.zeros((D_MODEL, LANES - N_EXPERTS - N_EXPERT_GROUPS), F32)], axis=1)
    br = jnp.concatenate(
        [b_re[0], b_rg[0], jnp.zeros((LANES - N_EXPERTS - N_EXPERT_GROUPS,), F32)]).reshape(1, LANES)
    bsp_p = jnp.repeat(jnp.transpose(b_spatial[0]), GMLP_HEAD, axis=1)
    bsp_s = jnp.tile(bsp_p[:dec_seq], (CHUNK // dec_seq, 1))
    wsp_p = w_spatial[0].astype(BF16)
    wsp_s = jnp.tile(w_spatial[0][:, :dec_seq, :dec_seq].astype(BF16),
                     (1, CHUNK // dec_seq, CHUNK // dec_seq))

    def pool_band(first_chunk):
        t = jnp.arange(CHUNK, dtype=I32)[:, None]
        k = jnp.arange(POOL_SPAN, dtype=I32)[None, :]
        mats = []
        for w in POOL_WINDOWS:
            cnt = jnp.minimum(w, t + 1) if first_chunk else jnp.full_like(t, w)
            in_window = (k >= t + 17 - w) & (k <= t + 16)
            mats.append(jnp.where(in_window, 1.0 / cnt.astype(F32), 0.0) - (k == t + 16).astype(F32))
        return jnp.stack(mats).astype(BF16)

    def mixer_weights(wsp, bsp):
        return (row(g_mix[0]), w_in[0].astype(BF16), w_pool[0].astype(BF16), row(pool_scale[0]),
                row(g_v[0]), row(b_v[0]), wsp, bsp, w_out[0].astype(BF16), row(g_ffn[0]),
                jnp.transpose(wr).astype(BF16), br.reshape(LANES, 1), pool_band(False), pool_band(True))

    h_p, hn_p, rt_p, rtr_p, cnt_p, hist_p = _mixer_call(
        x_prompt, None, mixer_weights(wsp_p, bsp_p), sample=False, pos0=0)
    h_s, hn_s, rt_s, rtr_s, cnt_s, hist_s, v_s = _mixer_call(
        x_sample.reshape(n_s, D_MODEL), jnp.transpose(state_pool[0], (1, 0, 2)),
        mixer_weights(wsp_s, bsp_s),
        sample=True, pos0=past_len)

    n_run = jnp.concatenate([cnt_p, cnt_s], axis=0)[:, 0, :N_EXPERTS].astype(I32)
    n_even = _ceil_to(n_run, RUN_ALIGN)
    l_start = _excl_cumsum(n_even, 1)
    tile_rows = jnp.sum(n_even, axis=1)
    base = _excl_cumsum(n_even, 0)
    counts = jnp.sum(n_even, axis=0)
    padded = _ceil_to(counts, MOE_TILE)
    ends = jnp.cumsum(padded)
    offs = ends - padded
    g_start = offs[None, :] + base
    n8 = n_even // 8
    f4 = (n_even // 4) % 2
    f2 = (n_even // 2) % 2
    l8, g8, c8 = _copy_tables(n8, l_start, g_start, 8, COPY_CAP[0])
    l4, g4, c4 = _copy_tables(f4, l_start + 8 * n8, g_start + 8 * n8, 0, COPY_CAP[1])
    l2, g2, c2 = _copy_tables(f2, l_start + 8 * n8 + 4 * f4, g_start + 8 * n8 + 4 * f4, 0, COPY_CAP[2])
    copy_cnt = jnp.stack([c8, c4, c2], axis=1).reshape(-1).astype(I32)
    tile_rows = tile_rows.astype(I32)
    z_start = (offs + counts).astype(I32)
    z_big = ((padded - counts) // ZERO_ROWS).astype(I32)
    z_blocks = (((padded - counts) % ZERO_ROWS + 7) // 8).astype(I32)

    max_rows = 2 * n_tok + n_run.shape[0] * N_EXPERTS * (RUN_ALIGN - 1)
    n_tiles = max_rows // MOE_TILE + N_EXPERTS + 1
    n_used = (ends[-1] // MOE_TILE).astype(I32).reshape(1)
    tile_id = jnp.minimum(jnp.arange(n_tiles, dtype=I32), n_used - 1)
    tile_expert = jnp.sum((tile_id[:, None] * MOE_TILE) >= ends[None, :], axis=1).astype(I32)
    tile_expert = jnp.minimum(tile_expert, N_EXPERTS - 1)
    pick_e = tile_expert[:, None] == jnp.arange(N_EXPERTS, dtype=I32)[None, :]
    data_end = jnp.sum(jnp.where(pick_e, (offs + counts)[None, :], 0), axis=1)
    n_groups_used = jnp.clip((data_end - tile_id * MOE_TILE + ROW_GROUP - 1) // ROW_GROUP,
                             1, MOE_TILE // ROW_GROUP).astype(I32)
    e_ids = jnp.arange(N_EXPERTS, dtype=I32)
    nonempty = padded > 0
    seg_index = jnp.cumsum(nonempty.astype(I32)) - 1
    later = nonempty[None, :] & (e_ids[None, :] > e_ids[:, None])
    next_of = jnp.min(jnp.where(later, e_ids[None, :], N_EXPERTS), axis=1)
    next_of = jnp.where(next_of == N_EXPERTS, -1, next_of)
    next2_of = jnp.sum(jnp.where(next_of[:, None] == e_ids[None, :], next_of[None, :], 0), axis=1)
    next2_of = jnp.where(next_of >= 0, next2_of, -1)
    weight_buf = (jnp.sum(jnp.where(pick_e, seg_index[None, :], 0), axis=1) % WEIGHT_BUFS).astype(I32)
    next_expert = jnp.sum(jnp.where(pick_e, next_of[None, :], 0), axis=1).astype(I32)
    next2_expert = jnp.sum(jnp.where(pick_e, next2_of[None, :], 0), axis=1).astype(I32)

    copy_tabs = (l8, g8, l4, g4, l2, g2, copy_cnt, tile_rows)
    xs = _dispatch_call(copy_tabs + (z_start, z_big, z_blocks, n_used), hn_p, hn_s, rtr_p, rtr_s,
                        n_tiles * MOE_TILE)
    ys = _moe_call(tile_expert, n_used, weight_buf, next_expert, next2_expert, n_groups_used, xs, n_tiles,
                   w_eg[0], w_eu[0], w_ed[0])
    fin_w = (w_ple[0].astype(BF16), w_ple_gate[0].astype(BF16), row(g_ple[0]), row(g_final))
    y_p = _final_call(copy_tabs, h_p, rt_p, rtr_p, p_prompt[0].reshape(n_p, PLE_DIM), ys, *fin_w,
                      tile_base=0)
    y_s = _final_call(copy_tabs, h_s, rt_s, rtr_s, p_sample[0].reshape(n_s, PLE_DIM), ys, *fin_w,
                      tile_base=n_p // TOK_TILE)

    return (y_p.reshape(batch, seq, D_MODEL),
            y_s.reshape(dec_batch, dec_seq, D_MODEL),
            hist_p[None],
            jnp.transpose(hist_s, (1, 0, 2))[None],
            v_s.reshape(1, dec_batch, dec_seq, GMLP_WIDTH))
```

```python
import functools

import jax
import jax.numpy as jnp
from jax import lax
from jax.experimental import pallas as pl
from jax.experimental.pallas import tpu as pltpu

D_MODEL = 1024
POOL_WIDTH = 512
GMLP_WIDTH = 512
IN_WIDTH = POOL_WIDTH + 2 * GMLP_WIDTH
POOL_WINDOWS = (2, 4, 8, 16)
POOL_GROUP = 128
POOL_HIST = 15
CHUNK = 128
POOL_SPAN = CHUNK + 16
N_GMLP_HEADS = 4
GMLP_HEAD = 128
N_EXPERT_GROUPS = 4
EXPERTS_PER_GROUP = 8
N_EXPERTS = 32
D_EXPERT = 512
PLE_DIM = 256
EPS = 1e-6

LANES = 128
SUBLANES = 8
HALF = D_MODEL // 2
ROW_CHUNKS = HALF // LANES

TOK_TILE = 256
STEP_TILES = 2
MIX_TILE = 1024
MOE_TILE = 640
ROW_GROUP = 128
WEIGHT_BUFS = 3
X_BUFS = 3
RUN_ALIGN = 2
COPY_ROWS = (16, 8, 4, 2)
ZERO_ROWS = 64
LOC_ROWS = 2 * TOK_TILE + N_EXPERTS * (RUN_ALIGN - 1)
COPY_CAP = (LOC_ROWS // COPY_ROWS[0] + 1,) + (N_EXPERTS,) * (len(COPY_ROWS) - 1)
VMEM_LIMIT = 56 * 1024 * 1024

F32 = jnp.float32
BF16 = jnp.bfloat16
I32 = jnp.int32
U32 = jnp.uint32


def _rms(x, g=None):
    y = x * lax.rsqrt(jnp.mean(x * x, axis=-1, keepdims=True) + EPS)
    return y if g is None else y * g


def _gelu_tanh(x):
    inner = x * (0.7978845608028654 + (0.7978845608028654 * 0.044715) * (x * x))
    half = 0.5 * x
    return half + half * jnp.tanh(inner)


def _store_packed(ref, val, n_rows):
    packed = pltpu.pack_elementwise([val[:, :HALF], val[:, HALF:]], packed_dtype=BF16)
    for k in range(ROW_CHUNKS):
        ref[pl.ds(k, n_rows, stride=ROW_CHUNKS), :] = packed[:, k * LANES:(k + 1) * LANES]


def _load_packed(ref, n_rows):
    packed = jnp.concatenate(
        [ref[pl.ds(k, n_rows, stride=ROW_CHUNKS), :] for k in range(ROW_CHUNKS)], axis=-1)
    lo = pltpu.unpack_elementwise(packed, index=0, packed_dtype=BF16, unpacked_dtype=F32)
    hi = pltpu.unpack_elementwise(packed, index=1, packed_dtype=BF16, unpacked_dtype=F32)
    return lo, hi


def _rows(start, n):
    return pl.ds(pl.multiple_of(start * ROW_CHUNKS, SUBLANES), n * ROW_CHUNKS)


def _route(hn, wrt_ref, brc_ref, n_tok):
    logits = lax.dot_general(wrt_ref[...], hn, (((1,), (1,)), ((), ())),
                             preferred_element_type=F32) + brc_ref[...]
    neg = jnp.float32(-jnp.inf)
    big = jnp.float32(1e9)
    sub = lax.broadcasted_iota(I32, (SUBLANES, n_tok), 0).astype(F32)
    gl = jnp.where(sub < N_EXPERT_GROUPS, logits[N_EXPERTS:N_EXPERTS + SUBLANES, :], neg)
    gmax = jnp.max(gl, axis=0, keepdims=True)
    gidx = jnp.min(jnp.where(gl == gmax, sub, big), axis=0, keepdims=True)
    gprob = 1.0 / jnp.sum(jnp.exp(gl - gmax), axis=0, keepdims=True)
    assert EXPERTS_PER_GROUP == SUBLANES
    el = logits[0:SUBLANES, :]
    for g in range(1, N_EXPERT_GROUPS):
        el = jnp.where(gidx == g, logits[g * SUBLANES:(g + 1) * SUBLANES, :], el)
    m1 = jnp.max(el, axis=0, keepdims=True)
    o1 = jnp.min(jnp.where(el == m1, sub, big), axis=0, keepdims=True)
    el2 = jnp.where(sub == o1, neg, el)
    m2 = jnp.max(el2, axis=0, keepdims=True)
    o2 = jnp.min(jnp.where(el2 == m2, sub, big), axis=0, keepdims=True)
    i1 = gidx * EXPERTS_PER_GROUP + o1
    i2 = gidx * EXPERTS_PER_GROUP + o2
    t = jnp.exp(m2 - m1)
    w1 = gprob / (1.0 + t)
    w2 = gprob * t / (1.0 + t)

    erow = lax.broadcasted_iota(I32, (LANES, n_tok), 0).astype(F32)
    sel1 = erow == i1
    sel2 = erow == i2
    oh = jnp.where(sel1 | sel2, 1.0, 0.0).astype(BF16)
    r = lax.broadcasted_iota(I32, (n_tok, n_tok), 0)
    c = lax.broadcasted_iota(I32, (n_tok, n_tok), 1)
    earlier = jnp.where(r < c, 1.0, 0.0).astype(BF16)
    before = jnp.dot(oh, earlier, preferred_element_type=F32)
    counts_b = jnp.dot(oh, jnp.ones((n_tok, LANES), BF16), preferred_element_type=F32)
    assert RUN_ALIGN == 2
    c_even = counts_b + (counts_b - 2.0 * jnp.floor(counts_b * 0.5))
    c_hi = jnp.floor(c_even * (1.0 / 256.0))
    er = lax.broadcasted_iota(I32, (LANES, LANES), 0)
    ec = lax.broadcasted_iota(I32, (LANES, LANES), 1)
    lower = jnp.where(ec < er, 1.0, 0.0).astype(BF16)
    pieces = jnp.concatenate([c_hi, c_even - 256.0 * c_hi], axis=1).astype(BF16)
    prefix = jnp.dot(lower, pieces, preferred_element_type=F32)
    lstart = 256.0 * prefix[:, :LANES] + prefix[:, LANES:]
    placed = before + jnp.concatenate([lstart] * (n_tok // LANES), axis=1)
    lpos1 = jnp.sum(jnp.where(sel1, placed, 0.0), axis=0, keepdims=True)
    lpos2 = jnp.sum(jnp.where(sel2, placed, 0.0), axis=0, keepdims=True)

    stacked = jnp.where(erow == 0.0, i1, 0.0)
    for k, val in enumerate((i2, lpos1, lpos2, w1, w2), start=1):
        stacked = jnp.where(erow == float(k), val, stacked)
    return stacked.T[:, :SUBLANES], stacked[0:SUBLANES, :], counts_b.T[0:1, :]


def _mixer_kernel(*refs, n_tok, sample, pos0):
    if sample:
        (x_ref, hist_ref, win_ref, wpool_ref, pscale_ref, gv_ref, bv_ref, wsp_ref,
         bsp_ref, wout_ref, gffn_ref, wr_ref, br_ref, band_ref, band0_ref,
         h_ref, hn_ref, rt_ref, rtr_ref, cnt_ref, hist_out_ref, v_ref, ext_scr, pool_scr) = refs
    else:
        (x_ref, win_ref, wpool_ref, pscale_ref, gv_ref, bv_ref, wsp_ref,
         bsp_ref, wout_ref, gffn_ref, wr_ref, br_ref, band_ref, band0_ref,
         h_ref, hn_ref, rt_ref, rtr_ref, cnt_ref, hist_out_ref, ext_scr) = refs

    x = x_ref[...]
    xn = _rms(x)
    z = jnp.dot(xn.astype(BF16), win_ref[...], preferred_element_type=F32)
    a = z[:, :POOL_WIDTH]
    uv = _gelu_tanh(z[:, POOL_WIDTH:])
    u = uv[:, :GMLP_WIDTH]
    v = uv[:, GMLP_WIDTH:]
    mu = jnp.mean(v, axis=-1, keepdims=True)
    vc = v - mu
    v = vc * lax.rsqrt(jnp.mean(vc * vc, axis=-1, keepdims=True) + EPS) * gv_ref[...] + bv_ref[...]

    pooled = []
    if sample:
        n_seq = n_tok // SUBLANES
        n_grp = len(POOL_WINDOWS)
        gcols = [slice(gi * POOL_GROUP, (gi + 1) * POOL_GROUP) for gi in range(n_grp)]
        ext_scr[1:1 + POOL_HIST] = hist_ref[...]
        for gi in range(n_grp):
            pool_scr[gi] = a[:, gcols[gi]]
        for i in range(SUBLANES):
            ext_scr[1 + POOL_HIST + i] = jnp.concatenate(
                [pool_scr[gi, pl.ds(i, n_seq, stride=SUBLANES), :] for gi in range(n_grp)], axis=-1)
        hist_out_ref[...] = ext_scr[1 + SUBLANES:]
        for i in range(SUBLANES):
            back = [ext_scr[1 + POOL_HIST + i - j] for j in range(max(POOL_WINDOWS))]
            for gi, w in enumerate(POOL_WINDOWS):
                win = back[0][:, gcols[gi]]
                for j in range(1, w):
                    win = win + back[j][:, gcols[gi]]
                pool_scr[gi, pl.ds(i, n_seq, stride=SUBLANES), :] = (
                    win * (1.0 / min(w, pos0 + i + 1)) - back[0][:, gcols[gi]])
        pooled = [pool_scr[gi] for gi in range(n_grp)]
    else:
        l = pl.program_id(1)

        @pl.when(l == 0)
        def _():
            ext_scr[0:16, :] = jnp.zeros((16, POOL_WIDTH), F32)

        ext_scr[16:16 + n_tok, :] = a
        chunks = []
        for ci in range(n_tok // CHUNK):
            ext_c = ext_scr[ci * CHUNK:ci * CHUNK + POOL_SPAN, :].astype(BF16)
            groups = []
            for gi in range(len(POOL_WINDOWS)):
                band = band_ref[gi]
                if ci == 0:
                    band = jnp.where(l == 0, band0_ref[gi], band)
                groups.append(jnp.dot(band, ext_c[:, gi * POOL_GROUP:(gi + 1) * POOL_GROUP],
                                      preferred_element_type=F32))
            chunks.append(groups)
        pooled = [jnp.concatenate([chunks[ci][gi] for ci in range(n_tok // CHUNK)], axis=0)
                  for gi in range(len(POOL_WINDOWS))]
        tail = ext_scr[n_tok + 1:n_tok + 16, :]
        hist_out_ref[...] = tail
        ext_scr[1:16, :] = tail

    pool_out = jnp.concatenate(
        [jnp.dot(pooled[gi].astype(BF16), wpool_ref[gi], preferred_element_type=F32)
         for gi in range(len(POOL_WINDOWS))], axis=-1) * pscale_ref[...]

    blk = SUBLANES if sample else CHUNK
    tr = lax.broadcasted_iota(I32, (CHUNK, CHUNK), 0)
    sc = lax.broadcasted_iota(I32, (CHUNK, CHUNK), 1)
    mask = (sc <= tr) & ((sc // blk) == (tr // blk))
    wsp = [jnp.where(mask, wsp_ref[hh], jnp.zeros((), BF16)) for hh in range(N_GMLP_HEADS)]
    v_bf = v.astype(BF16)
    mixed_chunks = []
    for ci in range(n_tok // CHUNK):
        rows = slice(ci * CHUNK, (ci + 1) * CHUNK)
        heads = [jnp.dot(wsp[hh], v_bf[rows, hh * GMLP_HEAD:(hh + 1) * GMLP_HEAD],
                         preferred_element_type=F32) for hh in range(N_GMLP_HEADS)]
        mixed_chunks.append(jnp.concatenate(heads, axis=-1) + bsp_ref[...])
    mixed = jnp.concatenate(mixed_chunks, axis=0)
    gmlp_out = u * mixed

    cat = jnp.concatenate([pool_out, gmlp_out], axis=-1)
    h = x + jnp.dot(cat.astype(BF16), wout_ref[...], preferred_element_type=F32)
    hn = _rms(h, gffn_ref[...]).astype(BF16)

    h_ref[...] = h
    hn_ref[...] = hn
    if sample:
        v_ref[...] = v
    for s in range(n_tok // TOK_TILE):
        rows = slice(s * TOK_TILE, (s + 1) * TOK_TILE)
        rt, rt_rows, counts = _route(hn[rows], wr_ref, br_ref, TOK_TILE)
        rt_ref[rows, :] = rt
        rtr_ref[s * SUBLANES:(s + 1) * SUBLANES, :] = rt_rows
        cnt_ref[s] = counts


def _full(shape):
    return pl.BlockSpec(shape, lambda *_: (0,) * len(shape))


def _mixer_call(x, hist, weights, *, sample, pos0):
    w_specs = [_full(w.shape) for w in weights]
    n_tok = MIX_TILE
    if sample:
        n_rows = x.shape[0]
        n_seq = n_tok // SUBLANES
        grid = (n_rows // n_tok,)
        tok_map = lambda i: (i, 0)
        tile_map = lambda i: (i, 0, 0)
        hist_spec = pl.BlockSpec((POOL_HIST, n_seq, POOL_WIDTH), lambda i: (0, i, 0))
        in_specs = [pl.BlockSpec((n_tok, D_MODEL), tok_map), hist_spec] + w_specs
        hist_shape = jax.ShapeDtypeStruct(hist.shape, F32)
        scratch = [pltpu.VMEM((1 + POOL_HIST + SUBLANES, n_seq, POOL_WIDTH), F32),
                   pltpu.VMEM((len(POOL_WINDOWS), n_tok, POOL_GROUP), F32)]
        args = (x, hist)
    else:
        b, seq, _ = x.shape
        n_rows = b * seq
        n_l = seq // n_tok
        grid = (b, n_l)
        tok_map = lambda bi, li: (bi * n_l + li, 0)
        tile_map = lambda bi, li: (bi * n_l + li, 0, 0)
        in_specs = [pl.BlockSpec((None, n_tok, D_MODEL), lambda bi, li: (bi, li, 0))] + w_specs
        hist_shape = jax.ShapeDtypeStruct((b, POOL_HIST, POOL_WIDTH), F32)
        hist_spec = pl.BlockSpec((None, POOL_HIST, POOL_WIDTH), lambda bi, li: (bi, 0, 0))
        scratch = [pltpu.VMEM((16 + n_tok, POOL_WIDTH), F32)]
        args = (x,)
    out_shape = [jax.ShapeDtypeStruct((n_rows, D_MODEL), F32),
                 jax.ShapeDtypeStruct((n_rows, D_MODEL), BF16),
                 jax.ShapeDtypeStruct((n_rows, SUBLANES), F32),
                 jax.ShapeDtypeStruct((n_rows // TOK_TILE * SUBLANES, TOK_TILE), F32),
                 jax.ShapeDtypeStruct((n_rows // TOK_TILE, 1, LANES), F32),
                 hist_shape]
    out_specs = [pl.BlockSpec((n_tok, D_MODEL), tok_map),
                 pl.BlockSpec((n_tok, D_MODEL), tok_map),
                 pl.BlockSpec((n_tok, SUBLANES), tok_map),
                 pl.BlockSpec((n_tok // TOK_TILE * SUBLANES, TOK_TILE), tok_map),
                 pl.BlockSpec((n_tok // TOK_TILE, 1, LANES), tile_map),
                 hist_spec]
    if sample:
        out_shape.append(jax.ShapeDtypeStruct((n_rows, GMLP_WIDTH), F32))
        out_specs.append(pl.BlockSpec((n_tok, GMLP_WIDTH), tok_map))
    return pl.pallas_call(
        functools.partial(_mixer_kernel, n_tok=n_tok, sample=sample, pos0=pos0),
        grid=grid, in_specs=in_specs, out_specs=out_specs, out_shape=out_shape,
        scratch_shapes=scratch,
        compiler_params=pltpu.CompilerParams(
            dimension_semantics=("arbitrary",) * len(grid), vmem_limit_bytes=VMEM_LIMIT),
        name="mixer_sample" if sample else "mixer_prompt",
    )(*args, *weights)


def _for_each_copy(tile, cnt_ref, fn, active=True):
    for s, n_rows in enumerate(COPY_ROWS):
        def body(j, carry, s=s, n_rows=n_rows):
            fn(s, n_rows, tile * COPY_CAP[s] + j)
            return carry
        lax.fori_loop(0, jnp.where(active, cnt_ref[tile * len(COPY_ROWS) + s], 0), body, 0)


def _wait_rows(n_rows_total, make_wait):
    units = n_rows_total // RUN_ALIGN
    bit = 0
    while (RUN_ALIGN << bit) <= LOC_ROWS:
        @pl.when((units >> bit) & 1 == 1)
        def _(bit=bit):
            make_wait(RUN_ALIGN << bit).wait()
        bit += 1


def _dispatch_kernel(*refs, n_prompt_steps):
    n_tabs = 2 * len(COPY_ROWS)
    local_tabs, sorted_tabs = refs[0:n_tabs:2], refs[1:n_tabs:2]
    (cnt_ref, trow_ref, zs_ref, zbig_ref, zb_ref, nu_ref,
     hnp_ref, hns_ref, rtp_ref, rts_ref, xs_ref, loc, zbuf, sem, zsem) = refs[n_tabs:]
    i = pl.program_id(0)
    n_steps = pl.num_programs(0)
    slot = i % 2

    @pl.when(i == 0)
    def _():
        zbuf[...] = jnp.zeros(zbuf.shape, U32)
        n_tiles = xs_ref.shape[0] // (MOE_TILE * ROW_CHUNKS)

        def zero_copy(e, c, n_rows, first):
            return pltpu.make_async_copy(zbuf.at[pl.ds(0, n_rows * ROW_CHUNKS)],
                                         xs_ref.at[_rows(first + c * n_rows, n_rows)], zsem)

        def tail_copy(t):
            return pltpu.make_async_copy(zbuf, xs_ref.at[_rows(t * MOE_TILE, MOE_TILE)], zsem)

        def per_expert(e, totals):
            def big(c, carry):
                zero_copy(e, c, ZERO_ROWS, zs_ref[e]).start()
                return carry

            def small(c, carry):
                zero_copy(e, c, 8, zs_ref[e] + zbig_ref[e] * ZERO_ROWS).start()
                return carry

            lax.fori_loop(0, zbig_ref[e], big, 0)
            lax.fori_loop(0, zb_ref[e], small, 0)
            return totals[0] + zbig_ref[e], totals[1] + zb_ref[e]

        n_big, n_small = lax.fori_loop(0, N_EXPERTS, per_expert, (0, 0))

        def wait_big(_, carry):
            zero_copy(0, 0, ZERO_ROWS, 0).wait()
            return carry

        def wait_small(_, carry):
            zero_copy(0, 0, 8, 0).wait()
            return carry

        lax.fori_loop(0, n_big, wait_big, 0)
        lax.fori_loop(0, n_small, wait_small, 0)

        def tail_start(t, carry):
            tail_copy(t).start()
            return carry

        def tail_wait(t, carry):
            tail_copy(t).wait()
            return carry

        spill = pltpu.make_async_copy(zbuf.at[pl.ds(0, 8 * ROW_CHUNKS)],
                                      xs_ref.at[_rows(n_tiles * MOE_TILE, 8)], zsem)
        lax.fori_loop(nu_ref[0], n_tiles, tail_start, 0)
        spill.start()
        lax.fori_loop(nu_ref[0], n_tiles, tail_wait, 0)
        spill.wait()

    is_prompt = i < n_prompt_steps
    rt_rows = jnp.where(is_prompt, rtp_ref[...], rts_ref[...])
    row1, row2 = rt_rows[2:3, :], rt_rows[3:4, :]
    grow = lax.broadcasted_iota(I32, (LOC_ROWS, TOK_TILE), 0).astype(F32)
    perm = jnp.where((grow == row1) | (grow == row2), 1.0, 0.0).astype(BF16)
    hn = jnp.where(is_prompt, hnp_ref[...], hns_ref[...])
    grouped = jnp.dot(perm, hn, preferred_element_type=F32)
    _store_packed(loc.at[slot], grouped, LOC_ROWS)

    def run_copy(buf_slot, s, n_rows, k):
        src = loc.at[buf_slot, _rows(local_tabs[s][k], n_rows)]
        dst = xs_ref.at[_rows(sorted_tabs[s][k], n_rows)]
        return pltpu.make_async_copy(src, dst, sem.at[buf_slot])

    _for_each_copy(i, cnt_ref, lambda s, n_rows, k: run_copy(slot, s, n_rows, k).start(priority=s % 2))

    def drain(tile, buf_slot):
        def make_wait(n):
            return pltpu.make_async_copy(loc.at[buf_slot, pl.ds(0, n * ROW_CHUNKS)],
                                         xs_ref.at[pl.ds(0, n * ROW_CHUNKS)], sem.at[buf_slot])
        _wait_rows(trow_ref[tile], make_wait)

    @pl.when(i > 0)
    def _():
        drain(i - 1, 1 - slot)

    @pl.when(i == n_steps - 1)
    def _():
        drain(i, slot)


def _clamped_maps(np_steps):
    pmap = lambda i, *_: (jnp.minimum(i, np_steps - 1), 0)
    smap = lambda i, *_: (jnp.maximum(i - np_steps, 0), 0)
    return pmap, smap


def _dispatch_call(tables, hn_p, hn_s, rtr_p, rtr_s, n_sorted_rows):
    n_p, n_s = hn_p.shape[0], hn_s.shape[0]
    np_steps = n_p // TOK_TILE
    n_steps = np_steps + n_s // TOK_TILE
    pmap, smap = _clamped_maps(np_steps)
    out_rows = (n_sorted_rows + 8) * ROW_CHUNKS
    return pl.pallas_call(
        functools.partial(_dispatch_kernel, n_prompt_steps=np_steps),
        grid_spec=pltpu.PrefetchScalarGridSpec(
            num_scalar_prefetch=len(tables), grid=(n_steps,),
            in_specs=[pl.BlockSpec((TOK_TILE, D_MODEL), pmap),
                      pl.BlockSpec((TOK_TILE, D_MODEL), smap),
                      pl.BlockSpec((SUBLANES, TOK_TILE), pmap),
                      pl.BlockSpec((SUBLANES, TOK_TILE), smap)],
            out_specs=pl.BlockSpec(memory_space=pl.ANY),
            scratch_shapes=[pltpu.VMEM((2, LOC_ROWS * ROW_CHUNKS, LANES), U32),
                            pltpu.VMEM((MOE_TILE * ROW_CHUNKS, LANES), U32),
                            pltpu.SemaphoreType.DMA((2,)),
                            pltpu.SemaphoreType.DMA(())]),
        out_shape=jax.ShapeDtypeStruct((out_rows, LANES), U32),
        compiler_params=pltpu.CompilerParams(
            dimension_semantics=("arbitrary",), vmem_limit_bytes=VMEM_LIMIT),
        name="dispatch",
    )(*tables, hn_p, hn_s, rtr_p, rtr_s)


def _moe_kernel(te_ref, nu_ref, par_ref, nxt_ref, nxt2_ref, ngrp_ref, xs_hbm, wg_hbm, wu_hbm, wd_hbm, ys_ref,
                xbuf, zero_tile, wg_f32, wu_f32, wd_f32, wg_bf, wu_bf, wd_bf, xsem, wsem):
    i = pl.program_id(0)
    n_used = nu_ref[0]
    used = i < n_used
    x_slot = i % X_BUFS

    def x_copy(tile):
        slot = tile % X_BUFS
        return pltpu.make_async_copy(xs_hbm.at[_rows(tile * MOE_TILE, MOE_TILE)], xbuf.at[slot],
                                     xsem.at[slot])

    def store_zeros(ref, n_rows):
        ij = (lax.broadcasted_iota(I32, (n_rows, D_MODEL), 0)
              + lax.broadcasted_iota(I32, (n_rows, D_MODEL), 1))
        _store_packed(ref, jnp.where(ij < jnp.minimum(nu_ref[0], 0), 1.0, 0.0), n_rows)

    def weight_copies(expert, buf):
        return [pltpu.make_async_copy(hbm.at[expert], vmem.at[buf], wsem.at[buf])
                for hbm, vmem in ((wg_hbm, wg_f32), (wu_hbm, wu_f32), (wd_hbm, wd_f32))]

    @pl.when(used)
    def _():
        @pl.when(i == 0)
        def _():
            for ahead in range(X_BUFS - 1):
                @pl.when(ahead < n_used)
                def _(ahead=ahead):
                    x_copy(ahead).start()

        @pl.when(i + (X_BUFS - 1) < n_used)
        def _():
            x_copy(i + (X_BUFS - 1)).start()

        prev = te_ref[jnp.maximum(i - 1, 0)]

        @pl.when((i == 0) | (te_ref[i] != prev))
        def _():
            buf = par_ref[i]

            @pl.when(i == 0)
            def _():
                for cp in weight_copies(te_ref[0], 0):
                    cp.start()

                @pl.when(nxt_ref[0] >= 0)
                def _():
                    for cp in weight_copies(nxt_ref[0], 1):
                        cp.start()

            for cp in weight_copies(te_ref[i], buf):
                cp.wait()
            wg_bf[...] = wg_f32[buf].astype(BF16)
            wu_bf[...] = wu_f32[buf].astype(BF16)
            wd_bf[...] = wd_f32[buf].astype(BF16)

            @pl.when(nxt2_ref[i] >= 0)
            def _():
                for cp in weight_copies(nxt2_ref[i], (buf + 2) % WEIGHT_BUFS):
                    cp.start()

        x_copy(i).wait()

        def expert_rows(n_rows):
            lo, hi = _load_packed(xbuf.at[x_slot], n_rows)
            x = jnp.concatenate([lo, hi], axis=-1).astype(BF16)
            g = jnp.dot(x, wg_bf[...], preferred_element_type=F32)
            u = jnp.dot(x, wu_bf[...], preferred_element_type=F32)
            hh = (g * jax.nn.sigmoid(g)) * u
            y = jnp.dot(hh.astype(BF16), wd_bf[...], preferred_element_type=F32)
            _store_packed(ys_ref, y, n_rows)

        for n_groups in range(1, MOE_TILE // ROW_GROUP + 1):
            @pl.when(ngrp_ref[i] == n_groups)
            def _(n_rows=n_groups * ROW_GROUP):
                expert_rows(n_rows)
                if n_rows < MOE_TILE:
                    rest = MOE_TILE - n_rows
                    store_zeros(ys_ref.at[pl.ds(n_rows * ROW_CHUNKS, rest * ROW_CHUNKS)], rest)

    @pl.when(i == n_used)
    def _():
        store_zeros(zero_tile, MOE_TILE)

    @pl.when(jnp.logical_not(used))
    def _():
        ys_ref[...] = zero_tile[...]


def _moe_call(tile_expert, n_used, weight_buf, next_expert, next2_expert, n_groups_used, xs, n_tiles,
              w_eg, w_eu, w_ed):
    blk = MOE_TILE * ROW_CHUNKS
    any_spec = pl.BlockSpec(memory_space=pl.ANY)
    return pl.pallas_call(
        _moe_kernel,
        grid_spec=pltpu.PrefetchScalarGridSpec(
            num_scalar_prefetch=6, grid=(n_tiles,),
            in_specs=[
                any_spec, any_spec, any_spec, any_spec,
            ],
            out_specs=pl.BlockSpec((blk, LANES), lambda i, *_: (i, 0)),
            scratch_shapes=[pltpu.VMEM((X_BUFS, blk, LANES), U32),
                            pltpu.VMEM((blk, LANES), U32),
                            pltpu.VMEM((WEIGHT_BUFS, D_MODEL, D_EXPERT), F32),
                            pltpu.VMEM((WEIGHT_BUFS, D_MODEL, D_EXPERT), F32),
                            pltpu.VMEM((WEIGHT_BUFS, D_EXPERT, D_MODEL), F32),
                            pltpu.VMEM((D_MODEL, D_EXPERT), BF16),
                            pltpu.VMEM((D_MODEL, D_EXPERT), BF16),
                            pltpu.VMEM((D_EXPERT, D_MODEL), BF16),
                            pltpu.SemaphoreType.DMA((X_BUFS,)),
                            pltpu.SemaphoreType.DMA((WEIGHT_BUFS,))]),
        out_shape=jax.ShapeDtypeStruct((n_tiles * blk, LANES), U32),
        compiler_params=pltpu.CompilerParams(
            dimension_semantics=("arbitrary",), vmem_limit_bytes=VMEM_LIMIT),
        name="experts",
    )(tile_expert, n_used, weight_buf, next_expert, next2_expert, n_groups_used, xs, w_eg, w_eu, w_ed)


def _final_kernel(*refs, tile_base, n_tiles):
    n_tabs = 2 * len(COPY_ROWS)
    local_tabs, sorted_tabs = refs[0:n_tabs:2], refs[1:n_tabs:2]
    (cnt_ref, trow_ref, h_ref, rt_ref, rtr_ref, p_ref, ys_ref, wple_ref, wgate_ref,
     gfin_ref, out_ref, ybuf, yb_s, perm_s, sem) = refs[n_tabs:]
    s = pl.program_id(0)
    n_groups = n_tiles // STEP_TILES
    a_par = s % 2

    def slot_of(parity, sub):
        return parity * STEP_TILES + sub

    def run_copy(buf_slot, sz, n_rows, k):
        src = ys_ref.at[_rows(sorted_tabs[sz][k], n_rows)]
        dst = ybuf.at[buf_slot, _rows(local_tabs[sz][k], n_rows)]
        return pltpu.make_async_copy(src, dst, sem.at[buf_slot])

    def issue(group, parity, active):
        for sub in range(STEP_TILES):
            tile = jnp.minimum(group, n_groups - 1) * STEP_TILES + sub
            buf_slot = slot_of(parity, sub)
            _for_each_copy(tile_base + tile, cnt_ref,
                           lambda sz, n_rows, k, b=buf_slot: run_copy(b, sz, n_rows, k).start(
                               priority=sz % 2),
                           active)

    @pl.when(s == 0)
    def _():
        ybuf[...] = jnp.zeros(ybuf.shape, U32)
        yb_s[...] = jnp.zeros(yb_s.shape, BF16)
        perm_s[...] = jnp.zeros(perm_s.shape, BF16)
        issue(0, 0, True)

    issue(s + 1, 1 - a_par, s + 1 < n_groups)

    for sub in range(STEP_TILES):
        a_slot = slot_of(a_par, sub)

        def make_wait(n, a_slot=a_slot):
            return pltpu.make_async_copy(ys_ref.at[pl.ds(0, n * ROW_CHUNKS)],
                                         ybuf.at[a_slot, pl.ds(0, n * ROW_CHUNKS)], sem.at[a_slot])
        tile = tile_base + jnp.minimum(s, n_groups - 1) * STEP_TILES + sub
        _wait_rows(jnp.where(s < n_groups, trow_ref[tile], 0), make_wait)

    for sub in range(STEP_TILES):
        rows = slice(sub * TOK_TILE, (sub + 1) * TOK_TILE)
        b_slot = slot_of(1 - a_par, sub)
        h = h_ref[rows, :]
        h = h + jnp.dot(perm_s[b_slot], yb_s[b_slot], preferred_element_type=F32)
        r = _rms(h)
        gate = jax.nn.sigmoid(jnp.dot(r.astype(BF16), wgate_ref[...], preferred_element_type=F32))
        h = h + jnp.dot(p_ref[rows, :].astype(BF16), wple_ref[...],
                        preferred_element_type=F32) * gate
        out_ref[rows, :] = _rms(h, gfin_ref[...])

    for sub in range(STEP_TILES):
        rows = slice(sub * TOK_TILE, (sub + 1) * TOK_TILE)
        a_slot = slot_of(a_par, sub)
        rt = rt_ref[rows, :]
        lpos1, lpos2 = rt[:, 2:3], rt[:, 3:4]
        lane = lax.broadcasted_iota(I32, (TOK_TILE, LOC_ROWS), 1).astype(F32)
        perm_s[a_slot] = jnp.where((lane == lpos1) | (lane == lpos2), 1.0, 0.0).astype(BF16)
        rt_rows = rtr_ref[sub * SUBLANES:(sub + 1) * SUBLANES, :]
        row1, row2, w1_row, w2_row = (rt_rows[k:k + 1, :] for k in range(2, 6))
        grow = lax.broadcasted_iota(I32, (LOC_ROWS, TOK_TILE), 0).astype(F32)
        wrow = jnp.sum(jnp.where(grow == row1, w1_row, 0.0) + jnp.where(grow == row2, w2_row, 0.0),
                       axis=-1, keepdims=True)
        lo, hi = _load_packed(ybuf.at[a_slot], LOC_ROWS)
        yb_s[a_slot] = (jnp.concatenate([lo, hi], axis=-1) * wrow).astype(BF16)


def _final_call(tables, h, rt, rt_rows, p, ys, wple, wgate, gfin, *, tile_base):
    n_rows = h.shape[0]
    n_tiles = n_rows // TOK_TILE
    n_groups = n_tiles // STEP_TILES
    blk = STEP_TILES * TOK_TILE
    b_map = lambda s, *_: (jnp.maximum(s - 1, 0), 0)
    a_map = lambda s, *_: (jnp.minimum(s, n_groups - 1), 0)
    cmap = lambda s, *_: (0, 0)
    n_slots = 2 * STEP_TILES
    return pl.pallas_call(
        functools.partial(_final_kernel, tile_base=tile_base, n_tiles=n_tiles),
        grid_spec=pltpu.PrefetchScalarGridSpec(
            num_scalar_prefetch=len(tables), grid=(n_groups + 1,),
            in_specs=[
                pl.BlockSpec((blk, D_MODEL), b_map),
                pl.BlockSpec((blk, SUBLANES), a_map),
                pl.BlockSpec((STEP_TILES * SUBLANES, TOK_TILE), a_map),
                pl.BlockSpec((blk, PLE_DIM), b_map),
                pl.BlockSpec(memory_space=pl.ANY),
                pl.BlockSpec(wple.shape, cmap),
                pl.BlockSpec(wgate.shape, cmap),
                pl.BlockSpec(gfin.shape, cmap),
            ],
            out_specs=pl.BlockSpec((blk, D_MODEL), b_map),
            scratch_shapes=[pltpu.VMEM((n_slots, LOC_ROWS * ROW_CHUNKS, LANES), U32),
                            pltpu.VMEM((n_slots, LOC_ROWS, D_MODEL), BF16),
                            pltpu.VMEM((n_slots, TOK_TILE, LOC_ROWS), BF16),
                            pltpu.SemaphoreType.DMA((n_slots,))]),
        out_shape=jax.ShapeDtypeStruct((n_rows, D_MODEL), F32),
        compiler_params=pltpu.CompilerParams(
            dimension_semantics=("arbitrary",), vmem_limit_bytes=VMEM_LIMIT),
        name="final",
    )(*tables, h, rt, rt_rows, p, ys, wple, wgate, gfin)


def _ceil_to(x, m):
    return ((x + m - 1) // m) * m


def _excl_cumsum(a, axis):
    return jnp.cumsum(a, axis=axis) - a


def _copy_tables(n_copies, first_local, first_sorted, step, cap):
    ends = jnp.cumsum(n_copies, axis=1)
    j = jnp.arange(cap, dtype=I32)[None, :, None]
    e_of = jnp.minimum(jnp.sum(ends[:, None, :] <= j, axis=-1), N_EXPERTS - 1)
    pick = e_of[:, :, None] == jnp.arange(N_EXPERTS, dtype=I32)[None, None, :]
    take = lambda a: jnp.sum(jnp.where(pick, a[:, None, :], 0), axis=-1)
    back = step * (ends - n_copies)
    ahead = step * jnp.arange(cap, dtype=I32)[None, :]
    flat = lambda a: a.reshape(-1).astype(I32)
    return (flat(take(first_local - back) + ahead), flat(take(first_sorted - back) + ahead),
            ends[:, -1])


def kernel(x_prompt, x_sample, state_pool, p_prompt, p_sample, g_mix, w_in, w_pool, pool_scale,
           g_v, b_v, w_spatial, b_spatial, w_out, g_ffn, w_rg, b_rg, w_re, b_re, w_eg, w_eu, w_ed,
           w_ple, g_ple, w_ple_gate, g_final):
    batch, seq, _ = x_prompt.shape
    dec_batch, dec_seq, _ = x_sample.shape
    assert dec_seq == SUBLANES and g_mix.shape[0] == 1
    n_p = batch * seq
    n_s = dec_batch * dec_seq
    n_tok = n_p + n_s
    past_len = 16384

    row = lambda a: a.reshape(1, -1).astype(F32)
    wr = jnp.concatenate(
        [w_re[0], w_rg[0], jnp.zeros((D_MODEL, LANES - N_EXPERTS - N_EXPERT_GROUPS), F32)], axis=1)
    br = jnp.concatenate(
        [b_re[0], b_rg[0], jnp.zeros((LANES - N_EXPERTS - N_EXPERT_GROUPS,), F32)]).reshape(1, LANES)
    bsp_p = jnp.repeat(jnp.transpose(b_spatial[0]), GMLP_HEAD, axis=1)
    bsp_s = jnp.tile(bsp_p[:dec_seq], (CHUNK // dec_seq, 1))
    wsp_p = w_spatial[0].astype(BF16)
    wsp_s = jnp.tile(w_spatial[0][:, :dec_seq, :dec_seq].astype(BF16),
                     (1, CHUNK // dec_seq, CHUNK // dec_seq))

    def pool_band(first_chunk):
        t = jnp.arange(CHUNK, dtype=I32)[:, None]
        k = jnp.arange(POOL_SPAN, dtype=I32)[None, :]
        mats = []
        for w in POOL_WINDOWS:
            cnt = jnp.minimum(w, t + 1) if first_chunk else jnp.full_like(t, w)
            in_window = (k >= t + 17 - w) & (k <= t + 16)
            mats.append(jnp.where(in_window, 1.0 / cnt.astype(F32), 0.0) - (k == t + 16).astype(F32))
        return jnp.stack(mats).astype(BF16)

    def mixer_weights(wsp, bsp):
        return ((g_mix[0][:, None] * w_in[0]).astype(BF16), w_pool[0].astype(BF16), row(pool_scale[0]),
                row(g_v[0]), row(b_v[0]), wsp, bsp, w_out[0].astype(BF16), row(g_ffn[0]),
                jnp.transpose(wr).astype(BF16), br.reshape(LANES, 1), pool_band(False), pool_band(True))

    h_p, hn_p, rt_p, rtr_p, cnt_p, hist_p = _mixer_call(
        x_prompt, None, mixer_weights(wsp_p, bsp_p), sample=False, pos0=0)
    h_s, hn_s, rt_s, rtr_s, cnt_s, hist_s, v_s = _mixer_call(
        x_sample.reshape(n_s, D_MODEL), jnp.transpose(state_pool[0], (1, 0, 2)),
        mixer_weights(wsp_s, bsp_s),
        sample=True, pos0=past_len)

    n_run = jnp.concatenate([cnt_p, cnt_s], axis=0)[:, 0, :N_EXPERTS].astype(I32)
    n_even = _ceil_to(n_run, RUN_ALIGN)
    l_start = _excl_cumsum(n_even, 1)
    tile_rows = jnp.sum(n_even, axis=1)
    base = _excl_cumsum(n_even, 0)
    counts = jnp.sum(n_even, axis=0)
    padded = _ceil_to(counts, MOE_TILE)
    ends = jnp.cumsum(padded)
    offs = ends - padded
    g_start = offs[None, :] + base
    tabs, cnts = [], []
    done = jnp.zeros_like(n_even)
    for k, size in enumerate(COPY_ROWS):
        n_copies = n_even // size if k == 0 else (n_even // size) % 2
        lt, gt, ct = _copy_tables(n_copies, l_start + done, g_start + done, size if k == 0 else 0,
                                  COPY_CAP[k])
        tabs += [lt, gt]
        cnts.append(ct)
        done = done + size * n_copies
    copy_cnt = jnp.stack(cnts, axis=1).reshape(-1).astype(I32)
    tile_rows = tile_rows.astype(I32)
    z_start = (offs + counts).astype(I32)
    z_big = ((padded - counts) // ZERO_ROWS).astype(I32)
    z_blocks = (((padded - counts) % ZERO_ROWS + 7) // 8).astype(I32)

    max_rows = 2 * n_tok + n_run.shape[0] * N_EXPERTS * (RUN_ALIGN - 1)
    n_tiles = max_rows // MOE_TILE + N_EXPERTS + 1
    n_used = (ends[-1] // MOE_TILE).astype(I32).reshape(1)
    tile_id = jnp.minimum(jnp.arange(n_tiles, dtype=I32), n_used - 1)
    tile_expert = jnp.sum((tile_id[:, None] * MOE_TILE) >= ends[None, :], axis=1).astype(I32)
    tile_expert = jnp.minimum(tile_expert, N_EXPERTS - 1)
    pick_e = tile_expert[:, None] == jnp.arange(N_EXPERTS, dtype=I32)[None, :]
    data_end = jnp.sum(jnp.where(pick_e, (offs + counts)[None, :], 0), axis=1)
    n_groups_used = jnp.clip((data_end - tile_id * MOE_TILE + ROW_GROUP - 1) // ROW_GROUP,
                             1, MOE_TILE // ROW_GROUP).astype(I32)
    e_ids = jnp.arange(N_EXPERTS, dtype=I32)
    nonempty = padded > 0
    seg_index = jnp.cumsum(nonempty.astype(I32)) - 1
    later = nonempty[None, :] & (e_ids[None, :] > e_ids[:, None])
    next_of = jnp.min(jnp.where(later, e_ids[None, :], N_EXPERTS), axis=1)
    next_of = jnp.where(next_of == N_EXPERTS, -1, next_of)
    next2_of = jnp.sum(jnp.where(next_of[:, None] == e_ids[None, :], next_of[None, :], 0), axis=1)
    next2_of = jnp.where(next_of >= 0, next2_of, -1)
    weight_buf = (jnp.sum(jnp.where(pick_e, seg_index[None, :], 0), axis=1) % WEIGHT_BUFS).astype(I32)
    next_expert = jnp.sum(jnp.where(pick_e, next_of[None, :], 0), axis=1).astype(I32)
    next2_expert = jnp.sum(jnp.where(pick_e, next2_of[None, :], 0), axis=1).astype(I32)

    copy_tabs = tuple(tabs) + (copy_cnt, tile_rows)
    xs = _dispatch_call(copy_tabs + (z_start, z_big, z_blocks, n_used), hn_p, hn_s, rtr_p, rtr_s,
                        n_tiles * MOE_TILE)
    ys = _moe_call(tile_expert, n_used, weight_buf, next_expert, next2_expert, n_groups_used, xs, n_tiles,
                   w_eg[0], w_eu[0], w_ed[0])
    fin_w = (w_ple[0].astype(BF16), (g_ple[0][:, None] * w_ple_gate[0]).astype(BF16), row(g_final))
    y_p = _final_call(copy_tabs, h_p, rt_p, rtr_p, p_prompt[0].reshape(n_p, PLE_DIM), ys, *fin_w,
                      tile_base=0)
    y_s = _final_call(copy_tabs, h_s, rt_s, rtr_s, p_sample[0].reshape(n_s, PLE_DIM), ys, *fin_w,
                      tile_base=n_p // TOK_TILE)

    return (y_p.reshape(batch, seq, D_MODEL),
            y_s.reshape(dec_batch, dec_seq, D_MODEL),
            hist_p[None],
            jnp.transpose(hist_s, (1, 0, 2))[None],
            v_s.reshape(1, dec_batch, dec_seq, GMLP_WIDTH))
```
